```python
import jax, jax.numpy as jnp
from jax import lax
import numpy as np

D_MODEL = 1024
BATCH = 16
SEQ = 4096
DEPTH = 1

D_MIX = D_MODEL
D_CONF = D_MIX // 2
D_SC = D_MIX - D_CONF
CONF_HEADS = 8
SC_HEADS = 8
CONF_KERNEL = 31
SC_KERNEL = 3
D_FF = 4 * D_MODEL
D_IN = 2 * D_CONF + 3 * D_SC
N_MOD = 6
EPS = 1e-6

kernel_name = "hybrid_conformer_shortconv_adaln_block"


def rms_norm(x, gain=None):
    xf = x.astype(jnp.float32)
    y = xf * lax.rsqrt(jnp.mean(xf * xf, axis=-1, keepdims=True) + EPS)
    if gain is not None:
        y = y * gain.astype(jnp.float32)
    return y.astype(x.dtype)


def layer_norm(x, gain, bias):
    xf = x.astype(jnp.float32)
    mu = jnp.mean(xf, axis=-1, keepdims=True)
    var = jnp.mean(jnp.square(xf - mu), axis=-1, keepdims=True)
    y = (xf - mu) * lax.rsqrt(var + EPS) * gain.astype(jnp.float32) + bias.astype(jnp.float32)
    return y.astype(x.dtype)


def causal_depthwise_conv(u, w):
    k = w.shape[0]
    return lax.conv_general_dilated(
        u, w[:, None, :].astype(u.dtype), window_strides=(1,), padding=[(k - 1, 0)],
        dimension_numbers=("NWC", "WIO", "NWC"), feature_group_count=u.shape[-1])


def modulate(h, shift, scale):
    return h * (1.0 + scale[:, None, :]) + shift[:, None, :]


def _fwd_setup_inputs(seed: int = 0) -> dict:
    key = jax.random.key(seed)
    ks = jax.random.split(key, 16)
    f = jnp.float32
    n = lambda k, shape, s: (jax.random.normal(k, shape, f) * s).astype(f)
    return {
        "x": n(ks[0], (BATCH, SEQ, D_MODEL), 1.0),
        "c": n(ks[1], (BATCH, D_MODEL), 1.0),
        "w_ada": n(ks[2], (DEPTH, D_MODEL, N_MOD * D_MODEL), 0.5 * D_MODEL ** -0.5),
        "b_ada": n(ks[3], (DEPTH, N_MOD * D_MODEL), 0.02),
        "w_in": n(ks[4], (DEPTH, D_MODEL, D_IN), D_MODEL ** -0.5),
        "conf_dw_w": n(ks[5], (DEPTH, CONF_KERNEL, D_CONF), CONF_KERNEL ** -0.5),
        "conf_dw_b": n(ks[6], (DEPTH, D_CONF), 0.02),
        "conf_ln_g": 1.0 + n(ks[7], (DEPTH, D_CONF), 0.02),
        "conf_ln_b": n(ks[8], (DEPTH, D_CONF), 0.02),
        "sc_conv_w": n(ks[9], (DEPTH, SC_KERNEL, D_SC), SC_KERNEL ** -0.5),
        "w_out": n(ks[10], (DEPTH, D_MIX, D_MODEL), D_MIX ** -0.5),
        "w_mlp1": n(ks[11], (DEPTH, D_MODEL, D_FF), D_MODEL ** -0.5),
        "w_mlp2": n(ks[12], (DEPTH, D_FF, D_MODEL), D_FF ** -0.5),
        "g_final": 1.0 + n(ks[13], (D_MODEL,), 0.02),
    }


def _fwd_reference(x, c, w_ada, b_ada, w_in, conf_dw_w, conf_dw_b, conf_ln_g, conf_ln_b,
              sc_conv_w, w_out, w_mlp1, w_mlp2, g_final):
    dt = x.dtype
    c_act = jax.nn.silu(c)
    for l in range(DEPTH):
        mod = jnp.einsum("bd,de->be", c_act, w_ada[l]) + b_ada[l]
        sh1, sc1, g1, sh2, sc2, g2 = jnp.split(mod.astype(dt), N_MOD, axis=-1)

        h = modulate(rms_norm(x), sh1, sc1)
        proj = jnp.einsum("btd,de->bte", h, w_in[l])
        conf_val, conf_gate, sc_b, sc_c, sc_h = jnp.split(
            proj, np.cumsum([D_CONF, D_CONF, D_SC, D_SC])[:4].tolist(), axis=-1)

        a = conf_val * jax.nn.sigmoid(conf_gate)
        a = causal_depthwise_conv(a, conf_dw_w[l]) + conf_dw_b[l].astype(dt)
        a = jax.nn.silu(layer_norm(a, conf_ln_g[l], conf_ln_b[l]))

        s = sc_b * causal_depthwise_conv(sc_c * sc_h, sc_conv_w[l])

        mixed = jnp.concatenate([a, s], axis=-1)
        y = jnp.einsum("bte,ed->btd", mixed, w_out[l])
        x = x + g1[:, None, :] * y

        h = modulate(rms_norm(x), sh2, sc2)
        u = jnp.square(jax.nn.relu(jnp.einsum("btd,df->btf", h, w_mlp1[l])))
        y = jnp.einsum("btf,fd->btd", u, w_mlp2[l])
        x = x + g2[:, None, :] * y

    return rms_norm(x, g_final)


import jax as _jax
import jax.numpy as _jnp

TWIN_FORMAT = 'train_step'
FWD_PARAMS = ['x', 'c', 'w_ada', 'b_ada', 'w_in', 'conf_dw_w', 'conf_dw_b', 'conf_ln_g', 'conf_ln_b', 'sc_conv_w', 'w_out', 'w_mlp1', 'w_mlp2', 'g_final']
TWIN_WEIGHTS = ['w_ada', 'b_ada', 'w_in', 'conf_dw_w', 'conf_dw_b', 'conf_ln_g', 'conf_ln_b', 'sc_conv_w', 'w_out', 'w_mlp1', 'w_mlp2', 'g_final']
TWIN_DIFF_INPUT = 'x'
TWIN_INPUTS = ['x', 'c', 'w_ada', 'b_ada', 'w_in', 'conf_dw_w', 'conf_dw_b', 'conf_ln_g', 'conf_ln_b', 'sc_conv_w', 'w_out', 'w_mlp1', 'w_mlp2', 'g_final', 'loss_target', 'm_w_ada', 'm_b_ada', 'm_w_in', 'm_conf_dw_w', 'm_conf_dw_b', 'm_conf_ln_g', 'm_conf_ln_b', 'm_sc_conv_w', 'm_w_out', 'm_w_mlp1', 'm_w_mlp2', 'm_g_final', 'v_w_ada', 'v_b_ada', 'v_w_in', 'v_conf_dw_w', 'v_conf_dw_b', 'v_conf_ln_g', 'v_conf_ln_b', 'v_sc_conv_w', 'v_w_out', 'v_w_mlp1', 'v_w_mlp2', 'v_g_final']
TWIN_OUTPUTS = ['loss', 'grad_x', 'grad_w_ada', 'grad_b_ada', 'grad_w_in', 'grad_conf_dw_w', 'grad_conf_dw_b', 'grad_conf_ln_g', 'grad_conf_ln_b', 'grad_sc_conv_w', 'grad_w_out', 'grad_w_mlp1', 'grad_w_mlp2', 'grad_g_final', 'delta_w_ada', 'delta_b_ada', 'delta_w_in', 'delta_conf_dw_w', 'delta_conf_dw_b', 'delta_conf_ln_g', 'delta_conf_ln_b', 'delta_sc_conv_w', 'delta_w_out', 'delta_w_mlp1', 'delta_w_mlp2', 'delta_g_final', 'new_m_w_ada', 'new_m_b_ada', 'new_m_w_in', 'new_m_conf_dw_w', 'new_m_conf_dw_b', 'new_m_conf_ln_g', 'new_m_conf_ln_b', 'new_m_sc_conv_w', 'new_m_w_out', 'new_m_w_mlp1', 'new_m_w_mlp2', 'new_m_g_final', 'new_v_w_ada', 'new_v_b_ada', 'new_v_w_in', 'new_v_conf_dw_w', 'new_v_conf_dw_b', 'new_v_conf_ln_g', 'new_v_conf_ln_b', 'new_v_sc_conv_w', 'new_v_w_out', 'new_v_w_mlp1', 'new_v_w_mlp2', 'new_v_g_final']
TWIN_LEAF_KINDS = {'loss': 'loss', 'grad_x': 'grad_x', 'grad_w_ada': 'grad_w', 'grad_b_ada': 'grad_w', 'grad_w_in': 'grad_w', 'grad_conf_dw_w': 'grad_w', 'grad_conf_dw_b': 'grad_w', 'grad_conf_ln_g': 'grad_w', 'grad_conf_ln_b': 'grad_w', 'grad_sc_conv_w': 'grad_w', 'grad_w_out': 'grad_w', 'grad_w_mlp1': 'grad_w', 'grad_w_mlp2': 'grad_w', 'grad_g_final': 'grad_w', 'delta_w_ada': 'delta_w', 'delta_b_ada': 'delta_w', 'delta_w_in': 'delta_w', 'delta_conf_dw_w': 'delta_w', 'delta_conf_dw_b': 'delta_w', 'delta_conf_ln_g': 'delta_w', 'delta_conf_ln_b': 'delta_w', 'delta_sc_conv_w': 'delta_w', 'delta_w_out': 'delta_w', 'delta_w_mlp1': 'delta_w', 'delta_w_mlp2': 'delta_w', 'delta_g_final': 'delta_w', 'new_m_w_ada': 'new_m', 'new_m_b_ada': 'new_m', 'new_m_w_in': 'new_m', 'new_m_conf_dw_w': 'new_m', 'new_m_conf_dw_b': 'new_m', 'new_m_conf_ln_g': 'new_m', 'new_m_conf_ln_b': 'new_m', 'new_m_sc_conv_w': 'new_m', 'new_m_w_out': 'new_m', 'new_m_w_mlp1': 'new_m', 'new_m_w_mlp2': 'new_m', 'new_m_g_final': 'new_m', 'new_v_w_ada': 'new_v', 'new_v_b_ada': 'new_v', 'new_v_w_in': 'new_v', 'new_v_conf_dw_w': 'new_v', 'new_v_conf_dw_b': 'new_v', 'new_v_conf_ln_g': 'new_v', 'new_v_conf_ln_b': 'new_v', 'new_v_sc_conv_w': 'new_v', 'new_v_w_out': 'new_v', 'new_v_w_mlp1': 'new_v', 'new_v_w_mlp2': 'new_v', 'new_v_g_final': 'new_v'}


def _forward(args):
    return _fwd_reference(*[args[k] for k in FWD_PARAMS])


def _output_shape():
    out = _jax.eval_shape(lambda: _forward(_fwd_setup_inputs(0)))
    return out.shape, out.dtype

N_MICROBATCH = 1
ADAM_LR = 0.001
ADAM_B1 = 0.9
ADAM_B2 = 0.999
ADAM_EPS = 1e-08
ADAM_WD = 0.01
ADAM_STEP = 10
PER_EXAMPLE_BATCH_AXIS = {'x': 0, 'c': 0, 'loss_target': 0}
SHARED_INPUTS = []
_WEIGHT_DTYPES = {'w_ada': _jnp.float32, 'b_ada': _jnp.float32, 'w_in': _jnp.float32, 'conf_dw_w': _jnp.float32, 'conf_dw_b': _jnp.float32, 'conf_ln_g': _jnp.float32, 'conf_ln_b': _jnp.float32, 'sc_conv_w': _jnp.float32, 'w_out': _jnp.float32, 'w_mlp1': _jnp.float32, 'w_mlp2': _jnp.float32, 'g_final': _jnp.float32}
MOMENT_SCALE = {'w_ada': 1.435500e-01, 'b_ada': 2.485717e-01, 'w_in': 8.158174e-02, 'conf_dw_w': 4.658797e-02, 'conf_dw_b': 9.037072e-02, 'conf_ln_g': 5.719076e-02, 'conf_ln_b': 4.905878e-02, 'sc_conv_w': 9.849222e-02, 'w_out': 7.827133e-02, 'w_mlp1': 5.701356e-02, 'w_mlp2': 1.051526e-01, 'g_final': 6.468171e+01}


def _to_microbatches(a, axis):
    t = _jnp.moveaxis(a, axis, 0)
    t = t.reshape((N_MICROBATCH, t.shape[0] // N_MICROBATCH) + t.shape[1:])
    return _jnp.moveaxis(t, 1, axis + 1)


def setup_inputs(seed: int = 0) -> dict:
    inp = _fwd_setup_inputs(seed)
    key = _jax.random.fold_in(_jax.random.key(seed), 7919)
    shape, _ = _output_shape()
    out = dict(inp)
    out["loss_target"] = _jax.random.normal(_jax.random.fold_in(key, 0), shape, _jnp.float32)
    for i, name in enumerate(TWIN_WEIGHTS):
        w = inp[name].astype(_jnp.float32)
        if MOMENT_SCALE is None:
            s = _jnp.sqrt(_jnp.mean(_jnp.square(w)) + 1e-30)
        else:
            s = MOMENT_SCALE[name]
        km, kv = _jax.random.split(_jax.random.fold_in(key, i + 1))
        out[name] = w
        out["m_" + name] = s * _jax.random.normal(km, w.shape, _jnp.float32)
        out["v_" + name] = (s * s) * _jax.random.uniform(kv, w.shape, _jnp.float32, 0.5, 1.5)
    if N_MICROBATCH > 1:
        for name, axis in PER_EXAMPLE_BATCH_AXIS.items():
            out[name] = _to_microbatches(out[name], axis)
    return {'x': out['x'], 'c': out['c'], 'w_ada': out['w_ada'], 'b_ada': out['b_ada'], 'w_in': out['w_in'], 'conf_dw_w': out['conf_dw_w'], 'conf_dw_b': out['conf_dw_b'], 'conf_ln_g': out['conf_ln_g'], 'conf_ln_b': out['conf_ln_b'], 'sc_conv_w': out['sc_conv_w'], 'w_out': out['w_out'], 'w_mlp1': out['w_mlp1'], 'w_mlp2': out['w_mlp2'], 'g_final': out['g_final'], 'loss_target': out['loss_target'], 'm_w_ada': out['m_w_ada'], 'm_b_ada': out['m_b_ada'], 'm_w_in': out['m_w_in'], 'm_conf_dw_w': out['m_conf_dw_w'], 'm_conf_dw_b': out['m_conf_dw_b'], 'm_conf_ln_g': out['m_conf_ln_g'], 'm_conf_ln_b': out['m_conf_ln_b'], 'm_sc_conv_w': out['m_sc_conv_w'], 'm_w_out': out['m_w_out'], 'm_w_mlp1': out['m_w_mlp1'], 'm_w_mlp2': out['m_w_mlp2'], 'm_g_final': out['m_g_final'], 'v_w_ada': out['v_w_ada'], 'v_b_ada': out['v_b_ada'], 'v_w_in': out['v_w_in'], 'v_conf_dw_w': out['v_conf_dw_w'], 'v_conf_dw_b': out['v_conf_dw_b'], 'v_conf_ln_g': out['v_conf_ln_g'], 'v_conf_ln_b': out['v_conf_ln_b'], 'v_sc_conv_w': out['v_sc_conv_w'], 'v_w_out': out['v_w_out'], 'v_w_mlp1': out['v_w_mlp1'], 'v_w_mlp2': out['v_w_mlp2'], 'v_g_final': out['v_g_final']}


def _loss(weights, diff, rest, loss_target):
    with _jax.named_scope("forward"):
        args = {**rest, TWIN_DIFF_INPUT: diff, **{k: w.astype(_WEIGHT_DTYPES[k]) for k, w in weights.items()}}
        y = _forward(args)
    with _jax.named_scope("loss_head"):
        err = _jnp.square(y.astype(_jnp.float32) - loss_target)
        return 0.5 * _jnp.sum(_jnp.mean(err, axis=-1)) if err.ndim else 0.5 * err


def _adamw(w, g, m, v):
    m = ADAM_B1 * m + (1.0 - ADAM_B1) * g
    v = ADAM_B2 * v + (1.0 - ADAM_B2) * _jnp.square(g)
    m_hat = m / (1.0 - ADAM_B1 ** ADAM_STEP)
    v_hat = v / (1.0 - ADAM_B2 ** ADAM_STEP)
    delta = -ADAM_LR * (m_hat / (_jnp.sqrt(v_hat) + ADAM_EPS) + ADAM_WD * w)
    return delta, m, v


def reference(x, c, w_ada, b_ada, w_in, conf_dw_w, conf_dw_b, conf_ln_g, conf_ln_b, sc_conv_w, w_out, w_mlp1, w_mlp2, g_final, loss_target, m_w_ada, m_b_ada, m_w_in, m_conf_dw_w, m_conf_dw_b, m_conf_ln_g, m_conf_ln_b, m_sc_conv_w, m_w_out, m_w_mlp1, m_w_mlp2, m_g_final, v_w_ada, v_b_ada, v_w_in, v_conf_dw_w, v_conf_dw_b, v_conf_ln_g, v_conf_ln_b, v_sc_conv_w, v_w_out, v_w_mlp1, v_w_mlp2, v_g_final):
    given = dict(x=x, c=c, w_ada=w_ada, b_ada=b_ada, w_in=w_in, conf_dw_w=conf_dw_w, conf_dw_b=conf_dw_b, conf_ln_g=conf_ln_g, conf_ln_b=conf_ln_b, sc_conv_w=sc_conv_w, w_out=w_out, w_mlp1=w_mlp1, w_mlp2=w_mlp2, g_final=g_final, loss_target=loss_target, m_w_ada=m_w_ada, m_b_ada=m_b_ada, m_w_in=m_w_in, m_conf_dw_w=m_conf_dw_w, m_conf_dw_b=m_conf_dw_b, m_conf_ln_g=m_conf_ln_g, m_conf_ln_b=m_conf_ln_b, m_sc_conv_w=m_sc_conv_w, m_w_out=m_w_out, m_w_mlp1=m_w_mlp1, m_w_mlp2=m_w_mlp2, m_g_final=m_g_final, v_w_ada=v_w_ada, v_b_ada=v_b_ada, v_w_in=v_w_in, v_conf_dw_w=v_conf_dw_w, v_conf_dw_b=v_conf_dw_b, v_conf_ln_g=v_conf_ln_g, v_conf_ln_b=v_conf_ln_b, v_sc_conv_w=v_sc_conv_w, v_w_out=v_w_out, v_w_mlp1=v_w_mlp1, v_w_mlp2=v_w_mlp2, v_g_final=v_g_final)
    weights = {n: given[n] for n in TWIN_WEIGHTS}
    shared = {n: given[n] for n in SHARED_INPUTS}
    per_example = {n: given[n] for n in ['x', 'c']}
    grad_fn = _jax.value_and_grad(_loss, argnums=(0, 1))

    def one_microbatch(ex, loss_target):
        ex = dict(ex)
        diff = ex.pop(TWIN_DIFF_INPUT)
        return grad_fn(weights, diff, {**shared, **ex}, loss_target)

    if N_MICROBATCH == 1:
        loss, (grad_w, grad_x) = one_microbatch(per_example, given["loss_target"])
    else:
        def body(carry, xs):
            loss_sum, grad_sum = carry
            l_k, (gw_k, gx_k) = one_microbatch(xs[0], xs[1])
            with _jax.named_scope("update"):
                return (loss_sum + l_k, _jax.tree.map(_jnp.add, grad_sum, gw_k)), gx_k

        init = (_jnp.zeros((), _jnp.float32), _jax.tree.map(_jnp.zeros_like, weights))
        (loss, grad_w), grad_x = _jax.lax.scan(body, init, (per_example, given["loss_target"]))
    with _jax.named_scope("update"):
        delta_w, new_m, new_v = {}, {}, {}
        for n in TWIN_WEIGHTS:
            delta_w[n], new_m[n], new_v[n] = _adamw(weights[n], grad_w[n], given["m_" + n], given["v_" + n])
    return (loss, grad_x, *[grad_w[n] for n in TWIN_WEIGHTS], *[delta_w[n] for n in TWIN_WEIGHTS],
            *[new_m[n] for n in TWIN_WEIGHTS], *[new_v[n] for n in TWIN_WEIGHTS])
```

```python
import functools

import jax
import jax.numpy as jnp
from jax import lax
from jax.experimental import pallas as pl
from jax.experimental.pallas import tpu as pltpu

N_DEV = 8
RMS_EPS = 1e-6
ADAM_LR = 0.001
ADAM_B1 = 0.9
ADAM_B2 = 0.999
ADAM_EPS = 1e-08
ADAM_WD = 0.01
ADAM_STEP = 10

F32 = jnp.float32
BF16 = jnp.bfloat16
MESH = pl.DeviceIdType.MESH
VMEM = pltpu.VMEM
ANY = pl.ANY

HALO = 32
SHORT_HALO = 8
ROW_CHUNK = 32
SUBLANES = 8
V7X_VMEM_LIMIT = 56 * 1024 * 1024


def _coords():
    return lax.axis_index("x"), lax.axis_index("y"), lax.axis_index("c")


def _dev_index():
    x, y, c = _coords()
    return 4 * x + 2 * y + c


def _peer(k):
    x, y, c = _coords()
    px = 1 - x if (k >> 2) & 1 else x
    py = 1 - y if (k >> 1) & 1 else y
    pc = 1 - c if k & 1 else c
    return (px, py, pc), 4 * px + 2 * py + pc


def _sigmoid(v):
    return jax.nn.sigmoid(v)


def _nt_dot(a, b):
    return lax.dot_general(a, b, (((1,), (1,)), ((), ())), preferred_element_type=F32)


def _nn_dot(a, b):
    return jnp.dot(a, b, preferred_element_type=F32)


def _tn_dot(a, b):
    return lax.dot_general(a, b, (((0,), (0,)), ((), ())), preferred_element_type=F32)


def _token_tile(seq):
    return 256 if seq % 256 == 0 else 64


def _gather_call(c_pad, w_ada, b_ada_loc, small_loc, big_shards):
    n_big = len(big_shards)
    d_model = c_pad.shape[1]
    a_cols = w_ada.shape[1]

    def body(c_ref, wada_ref, bada_ref, small_ref, *rest):
        big_in = rest[:n_big]
        big_out = rest[n_big:2 * n_big]
        small_all, c_all, mod_rows = rest[2 * n_big:2 * n_big + 3]
        modcols, big_send, big_recv, loc_sem, s_send, s_recv = rest[2 * n_big + 3:]
        me = _dev_index()

        local = [pltpu.make_async_copy(big_in[a], big_out[a].at[me], loc_sem.at[a]) for a in range(n_big)]
        for cp in local:
            cp.start()
        big = []
        for a in range(n_big):
            for k in range(1, N_DEV):
                peer, _ = _peer(k)
                cp = pltpu.make_async_remote_copy(
                    src_ref=big_in[a], dst_ref=big_out[a].at[me],
                    send_sem=big_send.at[a, k - 1], recv_sem=big_recv.at[a, k - 1],
                    device_id=peer, device_id_type=MESH)
                cp.start()
                big.append(cp)

        small_all[me] = small_ref[...]
        c_all[me] = c_ref[...]
        first = []
        for k in range(1, N_DEV):
            peer, _ = _peer(k)
            for i, (src, dst) in enumerate(((small_ref, small_all), (c_ref, c_all))):
                cp = pltpu.make_async_remote_copy(
                    src_ref=src, dst_ref=dst.at[me],
                    send_sem=s_send.at[i, k - 1], recv_sem=s_recv.at[i, k - 1],
                    device_id=peer, device_id_type=MESH)
                cp.start()
                first.append(cp)
        for cp in first:
            cp.wait()

        c_rows = c_all[...].reshape(N_DEV * SUBLANES, d_model)
        c_act = c_rows * _sigmoid(c_rows)
        modcols[...] = _nn_dot(c_act.astype(BF16), wada_ref[...].astype(BF16)) + bada_ref[...]
        mod_rows[me] = modcols[pl.ds(pl.multiple_of(me * SUBLANES, SUBLANES), SUBLANES), :]
        second = []
        for k in range(1, N_DEV):
            peer, pidx = _peer(k)
            cp = pltpu.make_async_remote_copy(
                src_ref=modcols.at[pl.ds(pl.multiple_of(pidx * SUBLANES, SUBLANES), SUBLANES), :],
                dst_ref=mod_rows.at[me],
                send_sem=s_send.at[2, k - 1], recv_sem=s_recv.at[2, k - 1],
                device_id=peer, device_id_type=MESH)
            cp.start()
            second.append(cp)
        for cp in second + big + local:
            cp.wait()

    out_shape = tuple(jax.ShapeDtypeStruct((N_DEV,) + s.shape, s.dtype) for s in big_shards) + (
        jax.ShapeDtypeStruct((N_DEV,) + small_loc.shape, F32),
        jax.ShapeDtypeStruct((N_DEV, SUBLANES, d_model), F32),
        jax.ShapeDtypeStruct((N_DEV, SUBLANES, a_cols), F32),
    )
    vm = pl.BlockSpec(memory_space=VMEM)
    hbm = pl.BlockSpec(memory_space=ANY)
    outs = pl.pallas_call(
        body, name="gather_weights_mod", out_shape=out_shape,
        in_specs=[vm, vm, vm, vm] + [hbm] * n_big,
        out_specs=tuple([hbm] * n_big + [vm, vm, vm]),
        scratch_shapes=[
            pltpu.VMEM((N_DEV * SUBLANES, a_cols), F32),
            pltpu.SemaphoreType.DMA((n_big, N_DEV - 1)),
            pltpu.SemaphoreType.DMA((n_big, N_DEV - 1)),
            pltpu.SemaphoreType.DMA((n_big,)),
            pltpu.SemaphoreType.DMA((3, N_DEV - 1)),
            pltpu.SemaphoreType.DMA((3, N_DEV - 1)),
        ],
        compiler_params=pltpu.CompilerParams(vmem_limit_bytes=V7X_VMEM_LIMIT),
    )(c_pad, w_ada, b_ada_loc, small_loc, *big_shards)
    return outs[:n_big], outs[n_big], outs[n_big + 1], outs[n_big + 2]


def _layer_norm_parts(a1):
    mu = jnp.mean(a1, axis=-1, keepdims=True)
    xc = a1 - mu
    rstd = lax.rsqrt(jnp.mean(xc * xc, axis=-1, keepdims=True) + RMS_EPS)
    return xc * rstd, rstd


def _mixer_fwd_call(x, mod, win_t, wout, cw, cp):
    bsz, seq, d_model = x.shape
    c_half = cw.shape[1]
    n_taps = 31
    d_in = win_t.shape[0] * win_t.shape[1]
    tm = _token_tile(seq)
    nt = seq // tm

    def body(x_ref, mod_ref, win_ref, wout_ref, cw_ref, cp_ref,
             proj_ref, a1_ref, cv_ref, mixed_ref, y1_ref, x1_ref, aext, qext):
        t = pl.program_id(1)
        xv = x_ref[...]
        sh1, sc1, g1 = mod_ref[0:1, :], mod_ref[1:2, :], mod_ref[2:3, :]
        r1 = lax.rsqrt(jnp.mean(xv * xv, axis=-1, keepdims=True) + RMS_EPS)
        h1 = (xv * r1) * (1.0 + sc1) + sh1
        proj = _nt_dot(h1.astype(BF16), win_ref[...].reshape(d_in, d_model))
        proj_ref[...] = proj
        val, gate = proj[:, 0:c_half], proj[:, c_half:2 * c_half]
        s_b, s_c, s_h = proj[:, 2 * c_half:3 * c_half], proj[:, 3 * c_half:4 * c_half], proj[:, 4 * c_half:5 * c_half]

        @pl.when(t == 0)
        def _():
            aext[0:HALO, :] = jnp.zeros((HALO, c_half), F32)
            qext[0:SHORT_HALO, :] = jnp.zeros((SHORT_HALO, c_half), F32)

        @pl.when(t > 0)
        def _():
            aext[0:HALO, :] = aext[tm:tm + HALO, :]
            qext[0:SHORT_HALO, :] = qext[tm:tm + SHORT_HALO, :]

        aext[HALO:HALO + tm, :] = val * _sigmoid(gate)
        qext[SHORT_HALO:SHORT_HALO + tm, :] = s_c * s_h

        base = HALO - (n_taps - 1)
        for r0 in range(0, tm, ROW_CHUNK):
            acc = jnp.zeros((ROW_CHUNK, c_half), F32)
            for k in range(n_taps):
                acc = acc + cw_ref[k:k + 1, :] * aext[r0 + base + k:r0 + base + k + ROW_CHUNK, :]
            a1_ref[r0:r0 + ROW_CHUNK, :] = acc + cp_ref[0:1, :]
        sbase = SHORT_HALO - 2
        conv3 = cp_ref[3:4, :] * qext[sbase:sbase + tm, :]
        conv3 = conv3 + cp_ref[4:5, :] * qext[sbase + 1:sbase + 1 + tm, :]
        conv3 = conv3 + cp_ref[5:6, :] * qext[sbase + 2:sbase + 2 + tm, :]
        cv_ref[...] = conv3

        norm, _ = _layer_norm_parts(a1_ref[...])
        a2 = norm * cp_ref[1:2, :] + cp_ref[2:3, :]
        mixed = jnp.concatenate([a2 * _sigmoid(a2), s_b * conv3], axis=-1).astype(BF16)
        mixed_ref[...] = mixed
        y1 = _nn_dot(mixed, wout_ref[...].reshape(d_model, d_model))
        y1_ref[...] = y1
        x1_ref[...] = xv + g1 * y1

    def tok(width):
        return pl.BlockSpec((None, tm, width), lambda b, t: (b, t, 0))

    def const(shape):
        return pl.BlockSpec(shape, lambda b, t: (0,) * len(shape))

    out_shape = (
        jax.ShapeDtypeStruct((bsz, seq, d_in), F32),
        jax.ShapeDtypeStruct((bsz, seq, c_half), F32),
        jax.ShapeDtypeStruct((bsz, seq, c_half), F32),
        jax.ShapeDtypeStruct((bsz, seq, d_model), BF16),
        jax.ShapeDtypeStruct((bsz, seq, d_model), F32),
        jax.ShapeDtypeStruct((bsz, seq, d_model), F32),
    )
    return pl.pallas_call(
        body, name="mixer_fwd", out_shape=out_shape, grid=(bsz, nt),
        in_specs=[tok(d_model), pl.BlockSpec((None, 6, d_model), lambda b, t: (b, 0, 0)),
                  const(win_t.shape), const(wout.shape), const(cw.shape), const(cp.shape)],
        out_specs=(tok(d_in), tok(c_half), tok(c_half), tok(d_model), tok(d_model), tok(d_model)),
        scratch_shapes=[pltpu.VMEM((tm + HALO, c_half), F32), pltpu.VMEM((tm + SHORT_HALO, c_half), F32)],
        compiler_params=pltpu.CompilerParams(
            dimension_semantics=("arbitrary", "arbitrary"), vmem_limit_bytes=V7X_VMEM_LIMIT),
    )(x, mod, win_t, wout, cw, cp)


def _mlp_call(x1, target, mod, w1, w2, g_final):
    bsz, seq, d_model = x1.shape
    n_blk, _, f_blk = w1.shape
    d_ff = n_blk * f_blk
    tm = _token_tile(seq)
    nt = seq // tm

    def body(x1_ref, tgt_ref, mod_ref, w1_ref, w2_ref, gf_ref,
             dx1_ref, h2_ref, dy2_ref, u_ref, dz_ref, dmod_ref, head_ref, relu_scr):
        b, t = pl.program_id(0), pl.program_id(1)
        x1v = x1_ref[...]
        sh2, sc2, g2 = mod_ref[3:4, :], mod_ref[4:5, :], mod_ref[5:6, :]
        gf = gf_ref[...]
        r2 = lax.rsqrt(jnp.mean(x1v * x1v, axis=-1, keepdims=True) + RMS_EPS)
        xn2 = x1v * r2
        h2 = (xn2 * (1.0 + sc2) + sh2).astype(BF16)
        h2_ref[...] = h2
        y2 = jnp.zeros((tm, d_model), F32)
        for j in range(n_blk):
            cols = slice(j * f_blk, (j + 1) * f_blk)
            rz = jnp.maximum(_nn_dot(h2, w1_ref[j]), 0.0)
            relu_scr[:, cols] = rz
            ub = (rz * rz).astype(BF16)
            u_ref[:, cols] = ub
            y2 = y2 + _nn_dot(ub, w2_ref[j])
        x2 = x1v + g2 * y2
        r3 = lax.rsqrt(jnp.mean(x2 * x2, axis=-1, keepdims=True) + RMS_EPS)
        xn3 = x2 * r3
        diff = xn3 * gf - tgt_ref[...]
        dout = diff * (1.0 / d_model)

        @pl.when(jnp.logical_and(b == 0, t == 0))
        def _():
            head_ref[...] = jnp.zeros(head_ref.shape, F32)

        @pl.when(t == 0)
        def _():
            dmod_ref[...] = jnp.zeros(dmod_ref.shape, F32)

        head_ref[0:1, :] += jnp.sum(dout * xn3, axis=0, keepdims=True)
        head_ref[1:2, :] += jnp.sum(diff * diff, axis=0, keepdims=True)
        dxn3 = dout * gf
        dx2 = r3 * (dxn3 - xn3 * jnp.mean(dxn3 * xn3, axis=-1, keepdims=True))
        dmod_ref[2:3, :] += jnp.sum(dx2 * y2, axis=0, keepdims=True)
        dy2 = (g2 * dx2).astype(BF16)
        dy2_ref[...] = dy2
        dh2 = jnp.zeros((tm, d_model), F32)
        for j in range(n_blk):
            cols = slice(j * f_blk, (j + 1) * f_blk)
            dz = (_nt_dot(dy2, w2_ref[j]) * (2.0 * relu_scr[:, cols])).astype(BF16)
            dz_ref[:, cols] = dz
            dh2 = dh2 + _nt_dot(dz, w1_ref[j])
        dmod_ref[0:1, :] += jnp.sum(dh2, axis=0, keepdims=True)
        dmod_ref[1:2, :] += jnp.sum(dh2 * xn2, axis=0, keepdims=True)
        dxn2 = dh2 * (1.0 + sc2)
        dx1_ref[...] = dx2 + r2 * (dxn2 - xn2 * jnp.mean(dxn2 * xn2, axis=-1, keepdims=True))

    def tok(width):
        return pl.BlockSpec((None, tm, width), lambda b, t: (b, t, 0))

    def const(shape):
        return pl.BlockSpec(shape, lambda b, t: (0,) * len(shape))

    out_shape = (
        jax.ShapeDtypeStruct((bsz, seq, d_model), F32),
        jax.ShapeDtypeStruct((bsz, seq, d_model), BF16),
        jax.ShapeDtypeStruct((bsz, seq, d_model), BF16),
        jax.ShapeDtypeStruct((bsz, seq, d_ff), BF16),
        jax.ShapeDtypeStruct((bsz, seq, d_ff), BF16),
        jax.ShapeDtypeStruct((bsz, SUBLANES, d_model), F32),
        jax.ShapeDtypeStruct((SUBLANES, d_model), F32),
    )
    return pl.pallas_call(
        body, name="mlp_fwd_bwd", out_shape=out_shape, grid=(bsz, nt),
        in_specs=[tok(d_model), tok(d_model), pl.BlockSpec((None, 6, d_model), lambda b, t: (b, 0, 0)),
                  const(w1.shape), const(w2.shape), const(g_final.shape)],
        out_specs=(tok(d_model), tok(d_model), tok(d_model), tok(d_ff), tok(d_ff),
                   pl.BlockSpec((None, SUBLANES, d_model), lambda b, t: (b, 0, 0)),
                   const((SUBLANES, d_model))),
        scratch_shapes=[pltpu.VMEM((tm, d_ff), F32)],
        compiler_params=pltpu.CompilerParams(
            dimension_semantics=("arbitrary", "arbitrary"), vmem_limit_bytes=V7X_VMEM_LIMIT),
    )(x1, target, mod, w1, w2, g_final)


def _mixer_bwd_call(dx1, x, proj, a1, cv, y1, mod, win_t, wout, cw, cp):
    bsz, seq, d_model = x.shape
    c_half = cw.shape[1]
    n_taps = 31
    d_in = win_t.shape[0] * win_t.shape[1]
    tm = _token_tile(seq)
    nt = seq // tm

    def body(dx1_ref, x_ref, proj_ref, a1_ref, cv_ref, y1_ref, mod_ref, win_ref, wout_ref, cw_ref, cp_ref,
             gx_ref, dproj_ref, h1_ref, dy1_ref, dmod_ref, cgrad_ref,
             dext, cext, a0_scr, da0_scr, tap_acc, row_acc):
        b, step = pl.program_id(0), pl.program_id(1)
        first = jnp.logical_and(b == 0, step == 0)
        last = jnp.logical_and(b == bsz - 1, step == nt - 1)
        dx1v = dx1_ref[...]
        xv = x_ref[...]
        sh1, sc1, g1 = mod_ref[0:1, :], mod_ref[1:2, :], mod_ref[2:3, :]

        @pl.when(first)
        def _():
            tap_acc[...] = jnp.zeros(tap_acc.shape, F32)
            row_acc[...] = jnp.zeros(row_acc.shape, F32)

        @pl.when(step == 0)
        def _():
            dmod_ref[...] = jnp.zeros(dmod_ref.shape, F32)
            dext[tm:tm + HALO, :] = jnp.zeros((HALO, c_half), F32)
            cext[tm:tm + SHORT_HALO, :] = jnp.zeros((SHORT_HALO, c_half), F32)

        @pl.when(step > 0)
        def _():
            dext[tm:tm + HALO, :] = dext[0:HALO, :]
            cext[tm:tm + SHORT_HALO, :] = cext[0:SHORT_HALO, :]

        dmod_ref[2:3, :] += jnp.sum(dx1v * y1_ref[...], axis=0, keepdims=True)
        dy1 = (g1 * dx1v).astype(BF16)
        dy1_ref[...] = dy1
        dmixed = _nt_dot(dy1, wout_ref[...].reshape(d_model, d_model))
        d_a, d_s = dmixed[:, 0:c_half], dmixed[:, c_half:2 * c_half]

        val, gate = proj_ref[:, 0:c_half], proj_ref[:, c_half:2 * c_half]
        s_b = proj_ref[:, 2 * c_half:3 * c_half]
        s_c, s_h = proj_ref[:, 3 * c_half:4 * c_half], proj_ref[:, 4 * c_half:5 * c_half]

        d_sb = d_s * cv_ref[...]
        cext[0:tm, :] = d_s * s_b
        q = s_c * s_h
        dq = jnp.zeros((tm, c_half), F32)
        for k in range(3):
            shifted = cext[2 - k:2 - k + tm, :]
            dq = dq + cp_ref[3 + k:4 + k, :] * shifted
            row_acc[k:k + 1, :] += jnp.sum(q * shifted, axis=0, keepdims=True)
        d_sc, d_sh = dq * s_h, dq * s_c

        norm, rstd = _layer_norm_parts(a1_ref[...])
        ln_g = cp_ref[1:2, :]
        a2 = norm * ln_g + cp_ref[2:3, :]
        sg = _sigmoid(a2)
        d_a2 = d_a * (sg * (1.0 + a2 * (1.0 - sg)))
        row_acc[4:5, :] += jnp.sum(d_a2 * norm, axis=0, keepdims=True)
        row_acc[5:6, :] += jnp.sum(d_a2, axis=0, keepdims=True)
        d_n = d_a2 * ln_g
        d_a1 = rstd * (d_n - jnp.mean(d_n, axis=-1, keepdims=True)
                       - norm * jnp.mean(d_n * norm, axis=-1, keepdims=True))
        row_acc[3:4, :] += jnp.sum(d_a1, axis=0, keepdims=True)
        dext[0:tm, :] = d_a1
        sig_g = _sigmoid(gate)
        a0_scr[...] = val * sig_g

        for r0 in range(0, tm, ROW_CHUNK):
            a0c = a0_scr[r0:r0 + ROW_CHUNK, :]
            acc = jnp.zeros((ROW_CHUNK, c_half), F32)
            for k in range(n_taps):
                off = r0 + (n_taps - 1) - k
                shifted = dext[off:off + ROW_CHUNK, :]
                acc = acc + cw_ref[k:k + 1, :] * shifted
                prod = a0c * shifted
                part = prod[0:SUBLANES, :]
                for g in range(1, ROW_CHUNK // SUBLANES):
                    part = part + prod[g * SUBLANES:(g + 1) * SUBLANES, :]
                tap_acc[k * SUBLANES:(k + 1) * SUBLANES, :] += part
            da0_scr[r0:r0 + ROW_CHUNK, :] = acc
        d_a0 = da0_scr[...]
        d_val = d_a0 * sig_g
        d_gate = d_a0 * val * sig_g * (1.0 - sig_g)

        dproj = jnp.concatenate([d_val, d_gate, d_sb, d_sc, d_sh], axis=-1).astype(BF16)
        dproj_ref[...] = dproj
        dh1 = _nn_dot(dproj, win_ref[...].reshape(d_in, d_model))
        r1 = lax.rsqrt(jnp.mean(xv * xv, axis=-1, keepdims=True) + RMS_EPS)
        xn1 = xv * r1
        h1_ref[...] = (xn1 * (1.0 + sc1) + sh1).astype(BF16)
        dmod_ref[0:1, :] += jnp.sum(dh1, axis=0, keepdims=True)
        dmod_ref[1:2, :] += jnp.sum(dh1 * xn1, axis=0, keepdims=True)
        dxn1 = dh1 * (1.0 + sc1)
        gx_ref[...] = dx1v + r1 * (dxn1 - xn1 * jnp.mean(dxn1 * xn1, axis=-1, keepdims=True))

        @pl.when(last)
        def _():
            taps = jnp.sum(tap_acc[...].reshape(HALO, SUBLANES, c_half), axis=1)
            cgrad_ref[0:HALO, :] = taps
            cgrad_ref[HALO:HALO + SUBLANES, :] = row_acc[...]

    def tok(width):
        return pl.BlockSpec((None, tm, width), lambda b, s: (b, nt - 1 - s, 0))

    def const(shape):
        return pl.BlockSpec(shape, lambda b, s: (0,) * len(shape))

    mod_spec = pl.BlockSpec((None, 6, d_model), lambda b, s: (b, 0, 0))
    out_shape = (
        jax.ShapeDtypeStruct((bsz, seq, d_model), F32),
        jax.ShapeDtypeStruct((bsz, seq, d_in), BF16),
        jax.ShapeDtypeStruct((bsz, seq, d_model), BF16),
        jax.ShapeDtypeStruct((bsz, seq, d_model), BF16),
        jax.ShapeDtypeStruct((bsz, SUBLANES, d_model), F32),
        jax.ShapeDtypeStruct((HALO + SUBLANES, c_half), F32),
    )
    return pl.pallas_call(
        body, name="mixer_bwd", out_shape=out_shape, grid=(bsz, nt),
        in_specs=[tok(d_model), tok(d_model), tok(d_in), tok(c_half), tok(c_half), tok(d_model), mod_spec,
                  const(win_t.shape), const(wout.shape), const(cw.shape), const(cp.shape)],
        out_specs=(tok(d_model), tok(d_in), tok(d_model), tok(d_model),
                   pl.BlockSpec((None, SUBLANES, d_model), lambda b, s: (b, 0, 0)),
                   const((HALO + SUBLANES, c_half))),
        scratch_shapes=[
            pltpu.VMEM((tm + HALO, c_half), F32), pltpu.VMEM((tm + SHORT_HALO, c_half), F32),
            pltpu.VMEM((tm, c_half), F32), pltpu.VMEM((tm, c_half), F32),
            pltpu.VMEM((HALO * SUBLANES, c_half), F32), pltpu.VMEM((SUBLANES, c_half), F32),
        ],
        compiler_params=pltpu.CompilerParams(
            dimension_semantics=("arbitrary", "arbitrary"), vmem_limit_bytes=V7X_VMEM_LIMIT),
    )(dx1, x, proj, a1, cv, y1, mod, win_t, wout, cw, cp)


def _largest_divisor(n, cap, multiple):
    best = None
    for cand in range(multiple, min(n, cap) + 1, multiple):
        if n % cand == 0:
            best = cand
    return best if best is not None else n


def _wgrad_call(a, b, name, owner_cols=None):
    tokens, m_dim = a.shape
    n_dim = b.shape[1]
    bk = _largest_divisor(tokens, 512, 128)
    if owner_cols is None:
        bm = _largest_divisor(m_dim, 1024, m_dim // N_DEV)
        bn = n_dim
        out_shape = jax.ShapeDtypeStruct((m_dim, n_dim), F32)
        out_spec = pl.BlockSpec((bm, bn), lambda i, j, k: (i, j))
    else:
        bm = m_dim
        bn = owner_cols
        out_shape = jax.ShapeDtypeStruct((n_dim // bn, m_dim, bn), F32)
        out_spec = pl.BlockSpec((None, bm, bn), lambda i, j, k: (j, i, 0))

    def body(a_ref, b_ref, o_ref):
        @pl.when(pl.program_id(2) == 0)
        def _():
            o_ref[...] = jnp.zeros(o_ref.shape, F32)

        o_ref[...] += _tn_dot(a_ref[...], b_ref[...])

    out = pl.pallas_call(
        body, name=name, out_shape=out_shape, grid=(m_dim // bm, n_dim // bn, tokens // bk),
        in_specs=[pl.BlockSpec((bk, bm), lambda i, j, k: (k, i)), pl.BlockSpec((bk, bn), lambda i, j, k: (k, j))],
        out_specs=out_spec,
        compiler_params=pltpu.CompilerParams(
            dimension_semantics=("arbitrary", "arbitrary", "arbitrary"), vmem_limit_bytes=V7X_VMEM_LIMIT),
    )(a, b)
    if owner_cols is None:
        out = out.reshape(N_DEV, m_dim // N_DEV, n_dim)
    return out


def _scatter_call(partials, small):
    n_big = len(partials)

    def body(*refs):
        big_in = refs[:n_big]
        small_ref = refs[n_big]
        big_out = refs[n_big + 1:2 * n_big + 1]
        small_all = refs[2 * n_big + 1]
        big_send, big_recv, loc_sem, s_send, s_recv = refs[2 * n_big + 2:]
        me = _dev_index()
        copies = []
        for a in range(n_big):
            cp = pltpu.make_async_copy(big_in[a].at[me], big_out[a].at[me], loc_sem.at[a])
            cp.start()
            copies.append(cp)
        small_all[me] = small_ref[...]
        for k in range(1, N_DEV):
            peer, pidx = _peer(k)
            cp = pltpu.make_async_remote_copy(
                src_ref=small_ref, dst_ref=small_all.at[me],
                send_sem=s_send.at[k - 1], recv_sem=s_recv.at[k - 1],
                device_id=peer, device_id_type=MESH)
            cp.start()
            copies.append(cp)
            for a in range(n_big):
                cp = pltpu.make_async_remote_copy(
                    src_ref=big_in[a].at[pidx], dst_ref=big_out[a].at[me],
                    send_sem=big_send.at[a, k - 1], recv_sem=big_recv.at[a, k - 1],
                    device_id=peer, device_id_type=MESH)
                cp.start()
                copies.append(cp)
        for cp in copies:
            cp.wait()

    vm = pl.BlockSpec(memory_space=VMEM)
    hbm = pl.BlockSpec(memory_space=ANY)
    out_shape = tuple(jax.ShapeDtypeStruct(p.shape, F32) for p in partials) + (
        jax.ShapeDtypeStruct((N_DEV,) + small.shape, F32),)
    outs = pl.pallas_call(
        body, name="scatter_grads", out_shape=out_shape,
        in_specs=[hbm] * n_big + [vm], out_specs=tuple([hbm] * n_big + [vm]),
        scratch_shapes=[
            pltpu.SemaphoreType.DMA((n_big, N_DEV - 1)),
            pltpu.SemaphoreType.DMA((n_big, N_DEV - 1)),
            pltpu.SemaphoreType.DMA((n_big,)),
            pltpu.SemaphoreType.DMA((N_DEV - 1,)),
            pltpu.SemaphoreType.DMA((N_DEV - 1,)),
        ],
        compiler_params=pltpu.CompilerParams(vmem_limit_bytes=V7X_VMEM_LIMIT),
    )(*partials, small)
    return outs[:n_big], outs[n_big]


def _adamw(w, g, m, v):
    m2 = ADAM_B1 * m + (1.0 - ADAM_B1) * g
    v2 = ADAM_B2 * v + (1.0 - ADAM_B2) * (g * g)
    m_hat = m2 / (1.0 - ADAM_B1 ** ADAM_STEP)
    v_hat = v2 / (1.0 - ADAM_B2 ** ADAM_STEP)
    delta = -ADAM_LR * (m_hat / (jnp.sqrt(v_hat) + ADAM_EPS) + ADAM_WD * w)
    return delta, m2, v2


def _adam_slabs_call(slabs, w, m, v, name):
    rows, cols = w.shape
    tr = _largest_divisor(rows, 128, SUBLANES)

    def body(s_ref, w_ref, m_ref, v_ref, g_ref, d_ref, m2_ref, v2_ref):
        g = s_ref[0]
        for k in range(1, N_DEV):
            g = g + s_ref[k]
        delta, m2, v2 = _adamw(w_ref[...], g, m_ref[...], v_ref[...])
        g_ref[...] = g
        d_ref[...] = delta
        m2_ref[...] = m2
        v2_ref[...] = v2

    tile = pl.BlockSpec((tr, cols), lambda i: (i, 0))
    shp = jax.ShapeDtypeStruct((rows, cols), F32)
    return pl.pallas_call(
        body, name=name, out_shape=(shp, shp, shp, shp), grid=(rows // tr,),
        in_specs=[pl.BlockSpec((N_DEV, tr, cols), lambda i: (0, i, 0)), tile, tile, tile],
        out_specs=(tile, tile, tile, tile),
        compiler_params=pltpu.CompilerParams(dimension_semantics=("arbitrary",), vmem_limit_bytes=V7X_VMEM_LIMIT),
    )(slabs, w, m, v)


def _adam_ada_call(c_rows, dmod_cols, w, m, v):
    rows, cols = w.shape
    n_rows = c_rows.shape[0]
    tr = _largest_divisor(rows, 256, 128)

    def body(c_ref, dm_ref, w_ref, m_ref, v_ref, g_ref, d_ref, m2_ref, v2_ref):
        cv = c_ref[...]
        c_act = (cv * _sigmoid(cv)).astype(BF16)
        g = _tn_dot(c_act, dm_ref[...].astype(BF16))
        delta, m2, v2 = _adamw(w_ref[...], g, m_ref[...], v_ref[...])
        g_ref[...] = g
        d_ref[...] = delta
        m2_ref[...] = m2
        v2_ref[...] = v2

    tile = pl.BlockSpec((tr, cols), lambda i: (i, 0))
    shp = jax.ShapeDtypeStruct((rows, cols), F32)
    return pl.pallas_call(
        body, name="adam_w_ada", out_shape=(shp, shp, shp, shp), grid=(rows // tr,),
        in_specs=[pl.BlockSpec((n_rows, tr), lambda i: (0, i)), pl.BlockSpec((n_rows, cols), lambda i: (0, 0)),
                  tile, tile, tile],
        out_specs=(tile, tile, tile, tile),
        compiler_params=pltpu.CompilerParams(dimension_semantics=("arbitrary",), vmem_limit_bytes=V7X_VMEM_LIMIT),
    )(c_rows, dmod_cols, w, m, v)


def _small_sum_call(small_all, n_grad_rows, loss_rows, bias_rows, loss_scale):
    _, rows, width = small_all.shape
    lo, hi = loss_rows
    b0, b1, b2 = bias_rows
    nb = b1 - b0

    def body(s_ref, sum_ref, extra_ref):
        tot = s_ref[0]
        for k in range(1, N_DEV):
            tot = tot + s_ref[k]
        sum_ref[...] = tot[0:n_grad_rows, :]
        extra_ref[0:nb, :] = tot[b0:b1, :] + tot[b1:b2, :]
        head = tot[n_grad_rows - 2 * SUBLANES:n_grad_rows, :]
        rows_id = lax.broadcasted_iota(jnp.int32, head.shape, 0) + (n_grad_rows - 2 * SUBLANES)
        sq = jnp.where(jnp.logical_and(rows_id >= lo, rows_id < hi), head, 0.0)
        extra_ref[nb:nb + SUBLANES, :] = jnp.zeros((SUBLANES, width), F32) + jnp.sum(sq) * loss_scale

    vm = pl.BlockSpec(memory_space=VMEM)
    return pl.pallas_call(
        body, name="small_sum",
        out_shape=(jax.ShapeDtypeStruct((n_grad_rows, width), F32), jax.ShapeDtypeStruct((nb + SUBLANES, width), F32)),
        in_specs=[vm], out_specs=(vm, vm),
    )(small_all)


def _adam_packed_call(w, g, m, v):
    def body(w_ref, g_ref, m_ref, v_ref, d_ref, m2_ref, v2_ref):
        delta, m2, v2 = _adamw(w_ref[...], g_ref[...], m_ref[...], v_ref[...])
        d_ref[...] = delta
        m2_ref[...] = m2
        v2_ref[...] = v2

    vm = pl.BlockSpec(memory_space=VMEM)
    shp = jax.ShapeDtypeStruct(w.shape, F32)
    return pl.pallas_call(body, name="adam_small", out_shape=(shp, shp, shp),
                          in_specs=[vm, vm, vm, vm], out_specs=(vm, vm, vm))(w, g, m, v)


def _pack(parts):
    rows = []
    for p in parts:
        flat = p.reshape(-1)
        n = flat.shape[0]
        padded = -(-n // (SUBLANES * 128)) * SUBLANES * 128
        rows.append(jnp.pad(flat, (0, padded - n)).reshape(-1, 128))
    return jnp.concatenate(rows, axis=0)


def _unpack(packed, like):
    out, r = [], 0
    for p in like:
        n = p.size
        nrow = -(-n // (SUBLANES * 128)) * SUBLANES
        out.append(packed[r:r + nrow].reshape(-1)[:n].reshape(p.shape))
        r += nrow
    return out


def kernel(x, c, w_ada, b_ada, w_in, conf_dw_w, conf_dw_b, conf_ln_g, conf_ln_b, sc_conv_w, w_out, w_mlp1, w_mlp2, g_final, loss_target, m_w_ada, m_b_ada, m_w_in, m_conf_dw_w, m_conf_dw_b, m_conf_ln_g, m_conf_ln_b, m_sc_conv_w, m_w_out, m_w_mlp1, m_w_mlp2, m_g_final, v_w_ada, v_b_ada, v_w_in, v_conf_dw_w, v_conf_dw_b, v_conf_ln_g, v_conf_ln_b, v_sc_conv_w, v_w_out, v_w_mlp1, v_w_mlp2, v_g_final):
    bsz, seq, d_model = x.shape
    c_half = conf_dw_b.shape[-1]
    n_taps = conf_dw_w.shape[1]
    cc = conf_dw_w.shape[-1]
    a_cols = w_ada.shape[-1]
    tokens = bsz * seq
    me = _dev_index()

    c_pad = jnp.pad(c, ((0, SUBLANES - bsz), (0, 0)))
    b_ada_loc = lax.dynamic_slice(b_ada, (0, me * a_cols), (1, a_cols))
    small_loc = jnp.zeros((HALO, 128), F32)
    small_loc = small_loc.at[:n_taps, :cc].set(conf_dw_w[0]).at[:3, cc:2 * cc].set(sc_conv_w[0])
    shards = [w_in[0].T.astype(BF16), w_out[0].astype(BF16), w_mlp1[0].astype(BF16), w_mlp2[0].astype(BF16)]
    (win_t, wout_all, w1_all, w2_all), small_all, c_all, mod_rows = _gather_call(
        c_pad, w_ada[0], b_ada_loc, small_loc, shards)
    cw = small_all[:, :, :cc].transpose(1, 0, 2).reshape(HALO, c_half)
    scw = small_all[:, :3, cc:2 * cc].transpose(1, 0, 2).reshape(3, c_half)
    cp = jnp.concatenate([conf_dw_b, conf_ln_g, conf_ln_b, scw, jnp.zeros((2, c_half), F32)], axis=0)
    mod = mod_rows[:, :bsz, :].transpose(1, 0, 2).reshape(bsz, 6, d_model)

    proj, a1, cv, mixed, y1, x1 = _mixer_fwd_call(x, mod, win_t, wout_all, cw, cp)
    dx1, h2, dy2, u, dz, dmod2, head = _mlp_call(x1, loss_target, mod, w1_all, w2_all, g_final.reshape(1, d_model))
    grad_x, dproj, h1, dy1, dmod1, cgrad = _mixer_bwd_call(dx1, x, proj, a1, cv, y1, mod, win_t, wout_all, cw, cp)

    flat = lambda t: t.reshape(tokens, t.shape[-1])
    g_in_t = _wgrad_call(flat(dproj), flat(h1), "wgrad_in")
    g_out = _wgrad_call(flat(mixed), flat(dy1), "wgrad_out")
    g_w1 = _wgrad_call(flat(h2), flat(dz), "wgrad_mlp1", owner_cols=w_mlp1.shape[-1])
    g_w2 = _wgrad_call(flat(u), flat(dy2), "wgrad_mlp2")

    dmod = jnp.concatenate([dmod1[:, :3, :], dmod2[:, :3, :]], axis=1)
    n_cg = cgrad.shape[0]
    per_b = 6 * d_model // c_half
    per_b_pad = -(-per_b // SUBLANES) * SUBLANES
    dmod_rows = jnp.pad(dmod.reshape(bsz, per_b, c_half), ((0, 0), (0, per_b_pad - per_b), (0, 0)))
    small = jnp.concatenate([
        cgrad,
        head.reshape(2 * SUBLANES, c_half),
        dmod_rows.reshape(bsz * per_b_pad, c_half),
    ], axis=0)
    (s_in, s_out, s_w1, s_w2), gathered = _scatter_call([g_in_t, g_out, g_w1, g_w2], small)

    n_head = n_cg + 2 * SUBLANES
    sums, extra = _small_sum_call(
        gathered, n_head, (n_cg + 2, n_cg + 4), (n_head, n_head + per_b_pad, n_head + 2 * per_b_pad), 0.5 / d_model)
    loss = extra[per_b_pad, 0]
    g_b_ada = extra[:per_b].reshape(1, 6 * d_model)
    g_dw_w = lax.dynamic_slice(sums[:n_taps], (0, me * cc), (n_taps, cc))[None]
    g_sc_w = lax.dynamic_slice(sums[HALO:HALO + 3], (0, me * cc), (3, cc))[None]
    g_dw_b, g_ln_g, g_ln_b = sums[HALO + 3:HALO + 4], sums[HALO + 4:HALO + 5], sums[HALO + 5:HALO + 6]
    g_gf = sums[n_cg:n_cg + 2].reshape(d_model)

    dmod_all = gathered[:, n_head:, :].reshape(N_DEV, bsz, per_b_pad, c_half)[:, :, :per_b, :]
    dmod_all = dmod_all.reshape(N_DEV, bsz, 6 * d_model)
    dmod_cols = lax.dynamic_slice(dmod_all, (0, 0, me * a_cols), (N_DEV, bsz, a_cols))
    dmod_cols = jnp.pad(dmod_cols, ((0, 0), (0, SUBLANES - bsz), (0, 0))).reshape(N_DEV * SUBLANES, a_cols)
    c_rows = c_all.reshape(N_DEV * SUBLANES, d_model)
    g_ada, d_ada, m_ada, v_ada = _adam_ada_call(c_rows, dmod_cols, w_ada[0], m_w_ada[0], v_w_ada[0])

    gi, di, mi, vi = _adam_slabs_call(s_in, w_in[0].T, m_w_in[0].T, v_w_in[0].T, "adam_w_in")
    gi, di, mi, vi = gi.T, di.T, mi.T, vi.T
    go, do, mo, vo = _adam_slabs_call(s_out, w_out[0], m_w_out[0], v_w_out[0], "adam_w_out")
    g1, d1, m1, v1 = _adam_slabs_call(s_w1, w_mlp1[0], m_w_mlp1[0], v_w_mlp1[0], "adam_w_mlp1")
    g2, d2, m2, v2 = _adam_slabs_call(s_w2, w_mlp2[0], m_w_mlp2[0], v_w_mlp2[0], "adam_w_mlp2")

    small_w = [b_ada, conf_dw_w, conf_dw_b, conf_ln_g, conf_ln_b, sc_conv_w, g_final]
    small_g = [g_b_ada, g_dw_w, g_dw_b, g_ln_g, g_ln_b, g_sc_w, g_gf]
    small_m = [m_b_ada, m_conf_dw_w, m_conf_dw_b, m_conf_ln_g, m_conf_ln_b, m_sc_conv_w, m_g_final]
    small_v = [v_b_ada, v_conf_dw_w, v_conf_dw_b, v_conf_ln_g, v_conf_ln_b, v_sc_conv_w, v_g_final]
    pd, pm, pv = _adam_packed_call(_pack(small_w), _pack(small_g), _pack(small_m), _pack(small_v))
    sd, sm, sv = _unpack(pd, small_w), _unpack(pm, small_w), _unpack(pv, small_w)
    sg = [g.reshape(w.shape) for g, w in zip(small_g, small_w)]

    def ordered(ada, small_list, w_in_, w_out_, w1_, w2_):
        b_ada_, dw_w_, dw_b_, ln_g_, ln_b_, sc_w_, gf_ = small_list
        return [ada[None], b_ada_, w_in_[None], dw_w_, dw_b_, ln_g_, ln_b_, sc_w_, w_out_[None], w1_[None], w2_[None], gf_]

    grads = ordered(g_ada, sg, gi, go, g1, g2)
    deltas = ordered(d_ada, sd, di, do, d1, d2)
    new_m = ordered(m_ada, sm, mi, mo, m1, m2)
    new_v = ordered(v_ada, sv, vi, vo, v1, v2)
    return (loss, grad_x, *grads, *deltas, *new_m, *new_v)
```

```python
import functools

import jax
import jax.numpy as jnp
from jax import lax
from jax.experimental import pallas as pl
from jax.experimental.pallas import tpu as pltpu

N_DEV = 8
RMS_EPS = 1e-6
ADAM_LR = 0.001
ADAM_B1 = 0.9
ADAM_B2 = 0.999
ADAM_EPS = 1e-08
ADAM_WD = 0.01
ADAM_STEP = 10

F32 = jnp.float32
BF16 = jnp.bfloat16
MESH = pl.DeviceIdType.MESH
VMEM = pltpu.VMEM
ANY = pl.ANY

HALO = 32
SHORT_HALO = 8
ROW_CHUNK = 32
SUBLANES = 8
V7X_VMEM_LIMIT = 56 * 1024 * 1024


def _coords():
    return lax.axis_index("x"), lax.axis_index("y"), lax.axis_index("c")


def _dev_index():
    x, y, c = _coords()
    return 4 * x + 2 * y + c


def _peer(k):
    x, y, c = _coords()
    px = 1 - x if (k >> 2) & 1 else x
    py = 1 - y if (k >> 1) & 1 else y
    pc = 1 - c if k & 1 else c
    return (px, py, pc), 4 * px + 2 * py + pc


def _sigmoid(v):
    return jax.nn.sigmoid(v)


def _nt_dot(a, b):
    return lax.dot_general(a, b, (((1,), (1,)), ((), ())), preferred_element_type=F32)


def _nn_dot(a, b):
    return jnp.dot(a, b, preferred_element_type=F32)


def _tn_dot(a, b):
    return lax.dot_general(a, b, (((0,), (0,)), ((), ())), preferred_element_type=F32)


def _token_tile(seq):
    return 256 if seq % 256 == 0 else 64


def _gather_copies(srcs, dsts, send_sems, recv_sems, loc_sems):
    me = _dev_index()
    copies = []
    for a, (src, dst) in enumerate(zip(srcs, dsts)):
        copies.append(pltpu.make_async_copy(src, dst.at[me], loc_sems.at[a]))
        for k in range(1, N_DEV):
            peer, _ = _peer(k)
            copies.append(pltpu.make_async_remote_copy(
                src_ref=src, dst_ref=dst.at[me], send_sem=send_sems.at[a, k - 1], recv_sem=recv_sems.at[a, k - 1],
                device_id=peer, device_id_type=MESH))
    return copies


def _scatter_copies(srcs, dsts, send_sems, recv_sems, loc_sems):
    me = _dev_index()
    copies = []
    for a, (src, dst) in enumerate(zip(srcs, dsts)):
        copies.append(pltpu.make_async_copy(src.at[me], dst.at[me], loc_sems.at[a]))
        for k in range(1, N_DEV):
            peer, pidx = _peer(k)
            copies.append(pltpu.make_async_remote_copy(
                src_ref=src.at[pidx], dst_ref=dst.at[me], send_sem=send_sems.at[a, k - 1],
                recv_sem=recv_sems.at[a, k - 1], device_id=peer, device_id_type=MESH))
    return copies


def _exchange_sems(n_arrays):
    return [pltpu.SemaphoreType.DMA((n_arrays, N_DEV - 1)), pltpu.SemaphoreType.DMA((n_arrays, N_DEV - 1)),
            pltpu.SemaphoreType.DMA((n_arrays,))]


def _gather_call(c_pad, w_ada, b_ada_loc, small_loc, big_shards):
    n_big = len(big_shards)
    d_model = c_pad.shape[1]
    a_cols = w_ada.shape[1]

    def body(c_ref, wada_ref, bada_ref, small_ref, *rest):
        big_in = rest[:n_big]
        big_out = rest[n_big:2 * n_big]
        small_all, c_all, mod_rows = rest[2 * n_big:2 * n_big + 3]
        modcols, big_send, big_recv, loc_sem, s_send, s_recv = rest[2 * n_big + 3:]
        me = _dev_index()
        big = _gather_copies(big_in, big_out, big_send, big_recv, loc_sem)
        for cp in big:
            cp.start()

        small_all[me] = small_ref[...]
        c_all[me] = c_ref[...]
        first = []
        for k in range(1, N_DEV):
            peer, _ = _peer(k)
            for i, (src, dst) in enumerate(((small_ref, small_all), (c_ref, c_all))):
                cp = pltpu.make_async_remote_copy(
                    src_ref=src, dst_ref=dst.at[me],
                    send_sem=s_send.at[i, k - 1], recv_sem=s_recv.at[i, k - 1],
                    device_id=peer, device_id_type=MESH)
                cp.start()
                first.append(cp)
        for cp in first:
            cp.wait()

        c_rows = c_all[...].reshape(N_DEV * SUBLANES, d_model)
        c_act = c_rows * _sigmoid(c_rows)
        modcols[...] = _nn_dot(c_act.astype(BF16), wada_ref[...].astype(BF16)) + bada_ref[...]
        mod_rows[me] = modcols[pl.ds(pl.multiple_of(me * SUBLANES, SUBLANES), SUBLANES), :]
        second = []
        for k in range(1, N_DEV):
            peer, pidx = _peer(k)
            cp = pltpu.make_async_remote_copy(
                src_ref=modcols.at[pl.ds(pl.multiple_of(pidx * SUBLANES, SUBLANES), SUBLANES), :],
                dst_ref=mod_rows.at[me],
                send_sem=s_send.at[2, k - 1], recv_sem=s_recv.at[2, k - 1],
                device_id=peer, device_id_type=MESH)
            cp.start()
            second.append(cp)
        for cp in second + big:
            cp.wait()

    out_shape = tuple(jax.ShapeDtypeStruct((N_DEV,) + s.shape, s.dtype) for s in big_shards) + (
        jax.ShapeDtypeStruct((N_DEV,) + small_loc.shape, F32),
        jax.ShapeDtypeStruct((N_DEV, SUBLANES, d_model), F32),
        jax.ShapeDtypeStruct((N_DEV, SUBLANES, a_cols), F32),
    )
    vm = pl.BlockSpec(memory_space=VMEM)
    hbm = pl.BlockSpec(memory_space=ANY)
    outs = pl.pallas_call(
        body, name="gather_weights_mod", out_shape=out_shape,
        in_specs=[vm, vm, vm, vm] + [hbm] * n_big,
        out_specs=tuple([hbm] * n_big + [vm, vm, vm]),
        scratch_shapes=[
            pltpu.VMEM((N_DEV * SUBLANES, a_cols), F32),
            pltpu.SemaphoreType.DMA((n_big, N_DEV - 1)),
            pltpu.SemaphoreType.DMA((n_big, N_DEV - 1)),
            pltpu.SemaphoreType.DMA((n_big,)),
            pltpu.SemaphoreType.DMA((3, N_DEV - 1)),
            pltpu.SemaphoreType.DMA((3, N_DEV - 1)),
        ],
        compiler_params=pltpu.CompilerParams(vmem_limit_bytes=V7X_VMEM_LIMIT),
    )(c_pad, w_ada, b_ada_loc, small_loc, *big_shards)
    return outs[:n_big], outs[n_big], outs[n_big + 1], outs[n_big + 2]


def _layer_norm_parts(a1):
    mu = jnp.mean(a1, axis=-1, keepdims=True)
    xc = a1 - mu
    rstd = lax.rsqrt(jnp.mean(xc * xc, axis=-1, keepdims=True) + RMS_EPS)
    return xc * rstd, rstd


def _mixer_fwd_call(x, mod, win_t, wout, cw, cp, later_shards):
    n_later = len(later_shards)
    bsz, seq, d_model = x.shape
    c_half = cw.shape[1]
    n_taps = 31
    d_in = win_t.shape[0] * win_t.shape[1]
    tm = _token_tile(seq)
    nt = seq // tm

    def body(x_ref, mod_ref, win_ref, wout_ref, cw_ref, cp_ref, *rest):
        shard_refs, rest = rest[:n_later], rest[n_later:]
        proj_ref, a1_ref, cv_ref, mixed_ref, y1_ref, x1_ref = rest[:6]
        gathered_refs, rest = rest[6:6 + n_later], rest[6 + n_later:]
        aext, qext, send_sems, recv_sems, loc_sems = rest
        b, t = pl.program_id(0), pl.program_id(1)

        @pl.when(jnp.logical_and(b == 0, t == 0))
        def _():
            for copy in _gather_copies(shard_refs, gathered_refs, send_sems, recv_sems, loc_sems):
                copy.start()

        xv = x_ref[...]
        sh1, sc1, g1 = mod_ref[0:1, :], mod_ref[1:2, :], mod_ref[2:3, :]
        r1 = lax.rsqrt(jnp.mean(xv * xv, axis=-1, keepdims=True) + RMS_EPS)
        h1 = (xv * r1) * (1.0 + sc1) + sh1
        proj = _nt_dot(h1.astype(BF16), win_ref[...].reshape(d_in, d_model))
        proj_ref[...] = proj
        val, gate = proj[:, 0:c_half], proj[:, c_half:2 * c_half]
        s_b, s_c, s_h = proj[:, 2 * c_half:3 * c_half], proj[:, 3 * c_half:4 * c_half], proj[:, 4 * c_half:5 * c_half]

        @pl.when(t == 0)
        def _():
            aext[0:HALO, :] = jnp.zeros((HALO, c_half), F32)
            qext[0:SHORT_HALO, :] = jnp.zeros((SHORT_HALO, c_half), F32)

        @pl.when(t > 0)
        def _():
            aext[0:HALO, :] = aext[tm:tm + HALO, :]
            qext[0:SHORT_HALO, :] = qext[tm:tm + SHORT_HALO, :]

        aext[HALO:HALO + tm, :] = val * _sigmoid(gate)
        qext[SHORT_HALO:SHORT_HALO + tm, :] = s_c * s_h

        base = HALO - (n_taps - 1)
        for r0 in range(0, tm, ROW_CHUNK):
            acc = jnp.zeros((ROW_CHUNK, c_half), F32)
            for k in range(n_taps):
                acc = acc + cw_ref[k:k + 1, :] * aext[r0 + base + k:r0 + base + k + ROW_CHUNK, :]
            a1_ref[r0:r0 + ROW_CHUNK, :] = acc + cp_ref[0:1, :]
        sbase = SHORT_HALO - 2
        conv3 = cp_ref[3:4, :] * qext[sbase:sbase + tm, :]
        conv3 = conv3 + cp_ref[4:5, :] * qext[sbase + 1:sbase + 1 + tm, :]
        conv3 = conv3 + cp_ref[5:6, :] * qext[sbase + 2:sbase + 2 + tm, :]
        cv_ref[...] = conv3

        norm, _ = _layer_norm_parts(a1_ref[...])
        a2 = norm * cp_ref[1:2, :] + cp_ref[2:3, :]
        mixed = jnp.concatenate([a2 * _sigmoid(a2), s_b * conv3], axis=-1).astype(BF16)
        mixed_ref[...] = mixed
        y1 = _nn_dot(mixed, wout_ref[...].reshape(d_model, d_model))
        y1_ref[...] = y1
        x1_ref[...] = xv + g1 * y1

        @pl.when(jnp.logical_and(b == bsz - 1, t == nt - 1))
        def _():
            for copy in _gather_copies(shard_refs, gathered_refs, send_sems, recv_sems, loc_sems):
                copy.wait()

    hbm = pl.BlockSpec(memory_space=ANY)

    def tok(width):
        return pl.BlockSpec((None, tm, width), lambda b, t: (b, t, 0))

    def const(shape):
        return pl.BlockSpec(shape, lambda b, t: (0,) * len(shape))

    out_shape = (
        jax.ShapeDtypeStruct((bsz, seq, d_in), F32),
        jax.ShapeDtypeStruct((bsz, seq, c_half), F32),
        jax.ShapeDtypeStruct((bsz, seq, c_half), F32),
        jax.ShapeDtypeStruct((bsz, seq, d_model), BF16),
        jax.ShapeDtypeStruct((bsz, seq, d_model), F32),
        jax.ShapeDtypeStruct((bsz, seq, d_model), F32),
    ) + tuple(jax.ShapeDtypeStruct((N_DEV,) + s.shape, s.dtype) for s in later_shards)
    outs = pl.pallas_call(
        body, name="mixer_fwd", out_shape=out_shape, grid=(bsz, nt),
        in_specs=[tok(d_model), pl.BlockSpec((None, 6, d_model), lambda b, t: (b, 0, 0)),
                  const(win_t.shape), const(wout.shape), const(cw.shape), const(cp.shape)] + [hbm] * n_later,
        out_specs=(tok(d_in), tok(c_half), tok(c_half), tok(d_model), tok(d_model), tok(d_model)) + (hbm,) * n_later,
        scratch_shapes=[pltpu.VMEM((tm + HALO, c_half), F32), pltpu.VMEM((tm + SHORT_HALO, c_half), F32)]
        + _exchange_sems(n_later),
        compiler_params=pltpu.CompilerParams(
            dimension_semantics=("arbitrary", "arbitrary"), vmem_limit_bytes=V7X_VMEM_LIMIT),
    )(x, mod, win_t, wout, cw, cp, *later_shards)
    return outs[:6], outs[6:]


def _mlp_call(x1, target, mod, w1, w2, g_final):
    bsz, seq, d_model = x1.shape
    n_blk, _, f_blk = w1.shape
    d_ff = n_blk * f_blk
    tm = _token_tile(seq)
    nt = seq // tm

    def body(x1_ref, tgt_ref, mod_ref, w1_ref, w2_ref, gf_ref,
             dx1_ref, h2_ref, dy2_ref, u_ref, dz_ref, dmod_ref, head_ref, relu_scr):
        b, t = pl.program_id(0), pl.program_id(1)
        x1v = x1_ref[...]
        sh2, sc2, g2 = mod_ref[3:4, :], mod_ref[4:5, :], mod_ref[5:6, :]
        gf = gf_ref[...]
        r2 = lax.rsqrt(jnp.mean(x1v * x1v, axis=-1, keepdims=True) + RMS_EPS)
        xn2 = x1v * r2
        h2 = (xn2 * (1.0 + sc2) + sh2).astype(BF16)
        h2_ref[...] = h2
        y2 = jnp.zeros((tm, d_model), F32)
        for j in range(n_blk):
            cols = slice(j * f_blk, (j + 1) * f_blk)
            rz = jnp.maximum(_nn_dot(h2, w1_ref[j]), 0.0)
            relu_scr[:, cols] = rz
            ub = (rz * rz).astype(BF16)
            u_ref[:, cols] = ub
            y2 = y2 + _nn_dot(ub, w2_ref[j])
        x2 = x1v + g2 * y2
        r3 = lax.rsqrt(jnp.mean(x2 * x2, axis=-1, keepdims=True) + RMS_EPS)
        xn3 = x2 * r3
        diff = xn3 * gf - tgt_ref[...]
        dout = diff * (1.0 / d_model)

        @pl.when(jnp.logical_and(b == 0, t == 0))
        def _():
            head_ref[...] = jnp.zeros(head_ref.shape, F32)

        @pl.when(t == 0)
        def _():
            dmod_ref[...] = jnp.zeros(dmod_ref.shape, F32)

        head_ref[0:1, :] += jnp.sum(dout * xn3, axis=0, keepdims=True)
        head_ref[1:2, :] += jnp.sum(diff * diff, axis=0, keepdims=True)
        dxn3 = dout * gf
        dx2 = r3 * (dxn3 - xn3 * jnp.mean(dxn3 * xn3, axis=-1, keepdims=True))
        dmod_ref[2:3, :] += jnp.sum(dx2 * y2, axis=0, keepdims=True)
        dy2 = (g2 * dx2).astype(BF16)
        dy2_ref[...] = dy2
        dh2 = jnp.zeros((tm, d_model), F32)
        for j in range(n_blk):
            cols = slice(j * f_blk, (j + 1) * f_blk)
            dz = (_nt_dot(dy2, w2_ref[j]) * (2.0 * relu_scr[:, cols])).astype(BF16)
            dz_ref[:, cols] = dz
            dh2 = dh2 + _nt_dot(dz, w1_ref[j])
        dmod_ref[0:1, :] += jnp.sum(dh2, axis=0, keepdims=True)
        dmod_ref[1:2, :] += jnp.sum(dh2 * xn2, axis=0, keepdims=True)
        dxn2 = dh2 * (1.0 + sc2)
        dx1_ref[...] = dx2 + r2 * (dxn2 - xn2 * jnp.mean(dxn2 * xn2, axis=-1, keepdims=True))

    def tok(width):
        return pl.BlockSpec((None, tm, width), lambda b, t: (b, t, 0))

    def const(shape):
        return pl.BlockSpec(shape, lambda b, t: (0,) * len(shape))

    out_shape = (
        jax.ShapeDtypeStruct((bsz, seq, d_model), F32),
        jax.ShapeDtypeStruct((bsz, seq, d_model), BF16),
        jax.ShapeDtypeStruct((bsz, seq, d_model), BF16),
        jax.ShapeDtypeStruct((bsz, seq, d_ff), BF16),
        jax.ShapeDtypeStruct((bsz, seq, d_ff), BF16),
        jax.ShapeDtypeStruct((bsz, SUBLANES, d_model), F32),
        jax.ShapeDtypeStruct((SUBLANES, d_model), F32),
    )
    return pl.pallas_call(
        body, name="mlp_fwd_bwd", out_shape=out_shape, grid=(bsz, nt),
        in_specs=[tok(d_model), tok(d_model), pl.BlockSpec((None, 6, d_model), lambda b, t: (b, 0, 0)),
                  const(w1.shape), const(w2.shape), const(g_final.shape)],
        out_specs=(tok(d_model), tok(d_model), tok(d_model), tok(d_ff), tok(d_ff),
                   pl.BlockSpec((None, SUBLANES, d_model), lambda b, t: (b, 0, 0)),
                   const((SUBLANES, d_model))),
        scratch_shapes=[pltpu.VMEM((tm, d_ff), F32)],
        compiler_params=pltpu.CompilerParams(
            dimension_semantics=("arbitrary", "arbitrary"), vmem_limit_bytes=V7X_VMEM_LIMIT),
    )(x1, target, mod, w1, w2, g_final)


def _mixer_bwd_call(dx1, x, proj, a1, cv, y1, mod, win_t, wout, cw, cp, partials):
    n_part = len(partials)
    bsz, seq, d_model = x.shape
    c_half = cw.shape[1]
    n_taps = 31
    d_in = win_t.shape[0] * win_t.shape[1]
    tm = _token_tile(seq)
    nt = seq // tm

    def body(dx1_ref, x_ref, proj_ref, a1_ref, cv_ref, y1_ref, mod_ref, win_ref, wout_ref, cw_ref, cp_ref, *rest):
        part_refs, rest = rest[:n_part], rest[n_part:]
        gx_ref, dproj_ref, h1_ref, dy1_ref, dmod_ref, cgrad_ref = rest[:6]
        slab_refs, rest = rest[6:6 + n_part], rest[6 + n_part:]
        dext, cext, a0_scr, da0_scr, tap_acc, row_acc, send_sems, recv_sems, loc_sems = rest
        b, step = pl.program_id(0), pl.program_id(1)
        first = jnp.logical_and(b == 0, step == 0)
        last = jnp.logical_and(b == bsz - 1, step == nt - 1)

        @pl.when(first)
        def _():
            for copy in _scatter_copies(part_refs, slab_refs, send_sems, recv_sems, loc_sems):
                copy.start()

        dx1v = dx1_ref[...]
        xv = x_ref[...]
        sh1, sc1, g1 = mod_ref[0:1, :], mod_ref[1:2, :], mod_ref[2:3, :]

        @pl.when(first)
        def _():
            tap_acc[...] = jnp.zeros(tap_acc.shape, F32)
            row_acc[...] = jnp.zeros(row_acc.shape, F32)

        @pl.when(step == 0)
        def _():
            dmod_ref[...] = jnp.zeros(dmod_ref.shape, F32)
            dext[tm:tm + HALO, :] = jnp.zeros((HALO, c_half), F32)
            cext[tm:tm + SHORT_HALO, :] = jnp.zeros((SHORT_HALO, c_half), F32)

        @pl.when(step > 0)
        def _():
            dext[tm:tm + HALO, :] = dext[0:HALO, :]
            cext[tm:tm + SHORT_HALO, :] = cext[0:SHORT_HALO, :]

        dmod_ref[2:3, :] += jnp.sum(dx1v * y1_ref[...], axis=0, keepdims=True)
        dy1 = (g1 * dx1v).astype(BF16)
        dy1_ref[...] = dy1
        dmixed = _nt_dot(dy1, wout_ref[...].reshape(d_model, d_model))
        d_a, d_s = dmixed[:, 0:c_half], dmixed[:, c_half:2 * c_half]

        val, gate = proj_ref[:, 0:c_half], proj_ref[:, c_half:2 * c_half]
        s_b = proj_ref[:, 2 * c_half:3 * c_half]
        s_c, s_h = proj_ref[:, 3 * c_half:4 * c_half], proj_ref[:, 4 * c_half:5 * c_half]

        d_sb = d_s * cv_ref[...]
        cext[0:tm, :] = d_s * s_b
        q = s_c * s_h
        dq = jnp.zeros((tm, c_half), F32)
        for k in range(3):
            shifted = cext[2 - k:2 - k + tm, :]
            dq = dq + cp_ref[3 + k:4 + k, :] * shifted
            row_acc[k:k + 1, :] += jnp.sum(q * shifted, axis=0, keepdims=True)
        d_sc, d_sh = dq * s_h, dq * s_c

        norm, rstd = _layer_norm_parts(a1_ref[...])
        ln_g = cp_ref[1:2, :]
        a2 = norm * ln_g + cp_ref[2:3, :]
        sg = _sigmoid(a2)
        d_a2 = d_a * (sg * (1.0 + a2 * (1.0 - sg)))
        row_acc[4:5, :] += jnp.sum(d_a2 * norm, axis=0, keepdims=True)
        row_acc[5:6, :] += jnp.sum(d_a2, axis=0, keepdims=True)
        d_n = d_a2 * ln_g
        d_a1 = rstd * (d_n - jnp.mean(d_n, axis=-1, keepdims=True)
                       - norm * jnp.mean(d_n * norm, axis=-1, keepdims=True))
        row_acc[3:4, :] += jnp.sum(d_a1, axis=0, keepdims=True)
        dext[0:tm, :] = d_a1
        sig_g = _sigmoid(gate)
        a0_scr[...] = val * sig_g

        for r0 in range(0, tm, ROW_CHUNK):
            a0c = a0_scr[r0:r0 + ROW_CHUNK, :]
            acc = jnp.zeros((ROW_CHUNK, c_half), F32)
            for k in range(n_taps):
                off = r0 + (n_taps - 1) - k
                shifted = dext[off:off + ROW_CHUNK, :]
                acc = acc + cw_ref[k:k + 1, :] * shifted
                prod = a0c * shifted
                part = prod[0:SUBLANES, :]
                for g in range(1, ROW_CHUNK // SUBLANES):
                    part = part + prod[g * SUBLANES:(g + 1) * SUBLANES, :]
                tap_acc[k * SUBLANES:(k + 1) * SUBLANES, :] += part
            da0_scr[r0:r0 + ROW_CHUNK, :] = acc
        d_a0 = da0_scr[...]
        d_val = d_a0 * sig_g
        d_gate = d_a0 * val * sig_g * (1.0 - sig_g)

        dproj = jnp.concatenate([d_val, d_gate, d_sb, d_sc, d_sh], axis=-1).astype(BF16)
        dproj_ref[...] = dproj
        dh1 = _nn_dot(dproj, win_ref[...].reshape(d_in, d_model))
        r1 = lax.rsqrt(jnp.mean(xv * xv, axis=-1, keepdims=True) + RMS_EPS)
        xn1 = xv * r1
        h1_ref[...] = (xn1 * (1.0 + sc1) + sh1).astype(BF16)
        dmod_ref[0:1, :] += jnp.sum(dh1, axis=0, keepdims=True)
        dmod_ref[1:2, :] += jnp.sum(dh1 * xn1, axis=0, keepdims=True)
        dxn1 = dh1 * (1.0 + sc1)
        gx_ref[...] = dx1v + r1 * (dxn1 - xn1 * jnp.mean(dxn1 * xn1, axis=-1, keepdims=True))

        @pl.when(last)
        def _():
            taps = jnp.sum(tap_acc[...].reshape(HALO, SUBLANES, c_half), axis=1)
            cgrad_ref[0:HALO, :] = taps
            cgrad_ref[HALO:HALO + SUBLANES, :] = row_acc[...]
            for copy in _scatter_copies(part_refs, slab_refs, send_sems, recv_sems, loc_sems):
                copy.wait()

    hbm = pl.BlockSpec(memory_space=ANY)

    def tok(width):
        return pl.BlockSpec((None, tm, width), lambda b, s: (b, nt - 1 - s, 0))

    def const(shape):
        return pl.BlockSpec(shape, lambda b, s: (0,) * len(shape))

    mod_spec = pl.BlockSpec((None, 6, d_model), lambda b, s: (b, 0, 0))
    out_shape = (
        jax.ShapeDtypeStruct((bsz, seq, d_model), F32),
        jax.ShapeDtypeStruct((bsz, seq, d_in), BF16),
        jax.ShapeDtypeStruct((bsz, seq, d_model), BF16),
        jax.ShapeDtypeStruct((bsz, seq, d_model), BF16),
        jax.ShapeDtypeStruct((bsz, SUBLANES, d_model), F32),
        jax.ShapeDtypeStruct((HALO + SUBLANES, c_half), F32),
    ) + tuple(jax.ShapeDtypeStruct(p.shape, p.dtype) for p in partials)
    outs = pl.pallas_call(
        body, name="mixer_bwd", out_shape=out_shape, grid=(bsz, nt),
        in_specs=[tok(d_model), tok(d_model), tok(d_in), tok(c_half), tok(c_half), tok(d_model), mod_spec,
                  const(win_t.shape), const(wout.shape), const(cw.shape), const(cp.shape)] + [hbm] * n_part,
        out_specs=(tok(d_model), tok(d_in), tok(d_model), tok(d_model),
                   pl.BlockSpec((None, SUBLANES, d_model), lambda b, s: (b, 0, 0)),
                   const((HALO + SUBLANES, c_half))) + (hbm,) * n_part,
        scratch_shapes=[
            pltpu.VMEM((tm + HALO, c_half), F32), pltpu.VMEM((tm + SHORT_HALO, c_half), F32),
            pltpu.VMEM((tm, c_half), F32), pltpu.VMEM((tm, c_half), F32),
            pltpu.VMEM((HALO * SUBLANES, c_half), F32), pltpu.VMEM((SUBLANES, c_half), F32),
        ] + _exchange_sems(n_part),
        compiler_params=pltpu.CompilerParams(
            dimension_semantics=("arbitrary", "arbitrary"), vmem_limit_bytes=V7X_VMEM_LIMIT),
    )(dx1, x, proj, a1, cv, y1, mod, win_t, wout, cw, cp, *partials)
    return outs[:6], outs[6:]


def _largest_divisor(n, cap, multiple):
    best = None
    for cand in range(multiple, min(n, cap) + 1, multiple):
        if n % cand == 0:
            best = cand
    return best if best is not None else n


def _wgrad_call(a, b, name, owner_cols=None):
    tokens, m_dim = a.shape
    n_dim = b.shape[1]
    bk = _largest_divisor(tokens, 512, 128)
    n_k = tokens // bk
    if owner_cols is None:
        bm = _largest_divisor(m_dim, 1024, m_dim // N_DEV)
        bn = n_dim
        out_shape = jax.ShapeDtypeStruct((m_dim, n_dim), BF16)
        out_spec = pl.BlockSpec((bm, bn), lambda i, j, k: (i, j))
    else:
        bm = m_dim
        bn = owner_cols
        out_shape = jax.ShapeDtypeStruct((n_dim // bn, m_dim, bn), BF16)
        out_spec = pl.BlockSpec((None, bm, bn), lambda i, j, k: (j, i, 0))

    def body(a_ref, b_ref, o_ref, acc):
        k = pl.program_id(2)

        @pl.when(k == 0)
        def _():
            acc[...] = jnp.zeros(acc.shape, F32)

        acc[...] += _tn_dot(a_ref[...], b_ref[...])

        @pl.when(k == n_k - 1)
        def _():
            o_ref[...] = acc[...].astype(BF16)

    out = pl.pallas_call(
        body, name=name, out_shape=out_shape, grid=(m_dim // bm, n_dim // bn, n_k),
        in_specs=[pl.BlockSpec((bk, bm), lambda i, j, k: (k, i)), pl.BlockSpec((bk, bn), lambda i, j, k: (k, j))],
        out_specs=out_spec, scratch_shapes=[pltpu.VMEM((bm, bn), F32)],
        compiler_params=pltpu.CompilerParams(
            dimension_semantics=("arbitrary", "arbitrary", "arbitrary"), vmem_limit_bytes=V7X_VMEM_LIMIT),
    )(a, b)
    if owner_cols is None:
        out = out.reshape(N_DEV, m_dim // N_DEV, n_dim)
    return out


def _scatter_call(partials, small):
    n_big = len(partials)

    def body(*refs):
        big_in = refs[:n_big]
        small_ref = refs[n_big]
        big_out = refs[n_big + 1:2 * n_big + 1]
        small_all = refs[2 * n_big + 1]
        big_send, big_recv, loc_sem, s_send, s_recv = refs[2 * n_big + 2:]
        me = _dev_index()
        small_all[me] = small_ref[...]
        copies = []
        for k in range(1, N_DEV):
            peer, _ = _peer(k)
            copies.append(pltpu.make_async_remote_copy(
                src_ref=small_ref, dst_ref=small_all.at[me],
                send_sem=s_send.at[k - 1], recv_sem=s_recv.at[k - 1],
                device_id=peer, device_id_type=MESH))
        copies += _scatter_copies(big_in, big_out, big_send, big_recv, loc_sem)
        for cp in copies:
            cp.start()
        for cp in copies:
            cp.wait()

    vm = pl.BlockSpec(memory_space=VMEM)
    hbm = pl.BlockSpec(memory_space=ANY)
    out_shape = tuple(jax.ShapeDtypeStruct(p.shape, p.dtype) for p in partials) + (
        jax.ShapeDtypeStruct((N_DEV,) + small.shape, F32),)
    outs = pl.pallas_call(
        body, name="scatter_grads", out_shape=out_shape,
        in_specs=[hbm] * n_big + [vm], out_specs=tuple([hbm] * n_big + [vm]),
        scratch_shapes=[
            pltpu.SemaphoreType.DMA((n_big, N_DEV - 1)),
            pltpu.SemaphoreType.DMA((n_big, N_DEV - 1)),
            pltpu.SemaphoreType.DMA((n_big,)),
            pltpu.SemaphoreType.DMA((N_DEV - 1,)),
            pltpu.SemaphoreType.DMA((N_DEV - 1,)),
        ],
        compiler_params=pltpu.CompilerParams(vmem_limit_bytes=V7X_VMEM_LIMIT),
    )(*partials, small)
    return outs[:n_big], outs[n_big]


def _adamw(w, g, m, v):
    m2 = ADAM_B1 * m + (1.0 - ADAM_B1) * g
    v2 = ADAM_B2 * v + (1.0 - ADAM_B2) * (g * g)
    m_hat = m2 / (1.0 - ADAM_B1 ** ADAM_STEP)
    v_hat = v2 / (1.0 - ADAM_B2 ** ADAM_STEP)
    delta = -ADAM_LR * (m_hat / (jnp.sqrt(v_hat) + ADAM_EPS) + ADAM_WD * w)
    return delta, m2, v2


def _adam_slabs_call(slabs, w, m, v, name):
    rows, cols = w.shape
    tr = _largest_divisor(rows, 128, 2 * SUBLANES)

    def body(s_ref, w_ref, m_ref, v_ref, g_ref, d_ref, m2_ref, v2_ref):
        g = s_ref[0].astype(F32)
        for k in range(1, N_DEV):
            g = g + s_ref[k].astype(F32)
        delta, m2, v2 = _adamw(w_ref[...], g, m_ref[...], v_ref[...])
        g_ref[...] = g
        d_ref[...] = delta
        m2_ref[...] = m2
        v2_ref[...] = v2

    tile = pl.BlockSpec((tr, cols), lambda i: (i, 0))
    shp = jax.ShapeDtypeStruct((rows, cols), F32)
    return pl.pallas_call(
        body, name=name, out_shape=(shp, shp, shp, shp), grid=(rows // tr,),
        in_specs=[pl.BlockSpec((N_DEV, tr, cols), lambda i: (0, i, 0)), tile, tile, tile],
        out_specs=(tile, tile, tile, tile),
        compiler_params=pltpu.CompilerParams(dimension_semantics=("arbitrary",), vmem_limit_bytes=V7X_VMEM_LIMIT),
    )(slabs, w, m, v)


def _adam_ada_call(c_rows, dmod_cols, w, m, v):
    rows, cols = w.shape
    n_rows = c_rows.shape[0]
    tr = _largest_divisor(rows, 256, 128)

    def body(c_ref, dm_ref, w_ref, m_ref, v_ref, g_ref, d_ref, m2_ref, v2_ref):
        cv = c_ref[...]
        c_act = (cv * _sigmoid(cv)).astype(BF16)
        g = _tn_dot(c_act, dm_ref[...].astype(BF16))
        delta, m2, v2 = _adamw(w_ref[...], g, m_ref[...], v_ref[...])
        g_ref[...] = g
        d_ref[...] = delta
        m2_ref[...] = m2
        v2_ref[...] = v2

    tile = pl.BlockSpec((tr, cols), lambda i: (i, 0))
    shp = jax.ShapeDtypeStruct((rows, cols), F32)
    return pl.pallas_call(
        body, name="adam_w_ada", out_shape=(shp, shp, shp, shp), grid=(rows // tr,),
        in_specs=[pl.BlockSpec((n_rows, tr), lambda i: (0, i)), pl.BlockSpec((n_rows, cols), lambda i: (0, 0)),
                  tile, tile, tile],
        out_specs=(tile, tile, tile, tile),
        compiler_params=pltpu.CompilerParams(dimension_semantics=("arbitrary",), vmem_limit_bytes=V7X_VMEM_LIMIT),
    )(c_rows, dmod_cols, w, m, v)


def _small_sum_call(small_all, n_grad_rows, loss_rows, bias_rows, loss_scale):
    _, rows, width = small_all.shape
    lo, hi = loss_rows
    b0, b1, b2 = bias_rows
    nb = b1 - b0

    def body(s_ref, sum_ref, extra_ref):
        tot = s_ref[0]
        for k in range(1, N_DEV):
            tot = tot + s_ref[k]
        sum_ref[...] = tot[0:n_grad_rows, :]
        extra_ref[0:nb, :] = tot[b0:b1, :] + tot[b1:b2, :]
        head = tot[n_grad_rows - 2 * SUBLANES:n_grad_rows, :]
        rows_id = lax.broadcasted_iota(jnp.int32, head.shape, 0) + (n_grad_rows - 2 * SUBLANES)
        sq = jnp.where(jnp.logical_and(rows_id >= lo, rows_id < hi), head, 0.0)
        extra_ref[nb:nb + SUBLANES, :] = jnp.zeros((SUBLANES, width), F32) + jnp.sum(sq) * loss_scale

    vm = pl.BlockSpec(memory_space=VMEM)
    return pl.pallas_call(
        body, name="small_sum",
        out_shape=(jax.ShapeDtypeStruct((n_grad_rows, width), F32), jax.ShapeDtypeStruct((nb + SUBLANES, width), F32)),
        in_specs=[vm], out_specs=(vm, vm),
    )(small_all)


def _adam_packed_call(w, g, m, v):
    def body(w_ref, g_ref, m_ref, v_ref, d_ref, m2_ref, v2_ref):
        delta, m2, v2 = _adamw(w_ref[...], g_ref[...], m_ref[...], v_ref[...])
        d_ref[...] = delta
        m2_ref[...] = m2
        v2_ref[...] = v2

    vm = pl.BlockSpec(memory_space=VMEM)
    shp = jax.ShapeDtypeStruct(w.shape, F32)
    return pl.pallas_call(body, name="adam_small", out_shape=(shp, shp, shp),
                          in_specs=[vm, vm, vm, vm], out_specs=(vm, vm, vm))(w, g, m, v)


def _pack(parts):
    rows = []
    for p in parts:
        flat = p.reshape(-1)
        n = flat.shape[0]
        padded = -(-n // (SUBLANES * 128)) * SUBLANES * 128
        rows.append(jnp.pad(flat, (0, padded - n)).reshape(-1, 128))
    return jnp.concatenate(rows, axis=0)


def _unpack(packed, like):
    out, r = [], 0
    for p in like:
        n = p.size
        nrow = -(-n // (SUBLANES * 128)) * SUBLANES
        out.append(packed[r:r + nrow].reshape(-1)[:n].reshape(p.shape))
        r += nrow
    return out


def kernel(x, c, w_ada, b_ada, w_in, conf_dw_w, conf_dw_b, conf_ln_g, conf_ln_b, sc_conv_w, w_out, w_mlp1, w_mlp2, g_final, loss_target, m_w_ada, m_b_ada, m_w_in, m_conf_dw_w, m_conf_dw_b, m_conf_ln_g, m_conf_ln_b, m_sc_conv_w, m_w_out, m_w_mlp1, m_w_mlp2, m_g_final, v_w_ada, v_b_ada, v_w_in, v_conf_dw_w, v_conf_dw_b, v_conf_ln_g, v_conf_ln_b, v_sc_conv_w, v_w_out, v_w_mlp1, v_w_mlp2, v_g_final):
    bsz, seq, d_model = x.shape
    c_half = conf_dw_b.shape[-1]
    n_taps = conf_dw_w.shape[1]
    cc = conf_dw_w.shape[-1]
    a_cols = w_ada.shape[-1]
    tokens = bsz * seq
    me = _dev_index()

    c_pad = jnp.pad(c, ((0, SUBLANES - bsz), (0, 0)))
    b_ada_loc = lax.dynamic_slice(b_ada, (0, me * a_cols), (1, a_cols))
    small_loc = jnp.zeros((HALO, 128), F32)
    small_loc = small_loc.at[:n_taps, :cc].set(conf_dw_w[0]).at[:3, cc:2 * cc].set(sc_conv_w[0])
    (win_t, wout_all), small_all, c_all, mod_rows = _gather_call(
        c_pad, w_ada[0], b_ada_loc, small_loc, [w_in[0].T.astype(BF16), w_out[0].astype(BF16)])
    cw = small_all[:, :, :cc].transpose(1, 0, 2).reshape(HALO, c_half)
    scw = small_all[:, :3, cc:2 * cc].transpose(1, 0, 2).reshape(3, c_half)
    cp = jnp.concatenate([conf_dw_b, conf_ln_g, conf_ln_b, scw, jnp.zeros((2, c_half), F32)], axis=0)
    mod = mod_rows[:, :bsz, :].transpose(1, 0, 2).reshape(bsz, 6, d_model)

    flat = lambda t: t.reshape(tokens, t.shape[-1])
    (proj, a1, cv, mixed, y1, x1), (w1_all, w2_all) = _mixer_fwd_call(
        x, mod, win_t, wout_all, cw, cp, [w_mlp1[0].astype(BF16), w_mlp2[0].astype(BF16)])
    dx1, h2, dy2, u, dz, dmod2, head = _mlp_call(x1, loss_target, mod, w1_all, w2_all, g_final.reshape(1, d_model))
    g_w1 = _wgrad_call(flat(h2), flat(dz), "wgrad_mlp1", owner_cols=w_mlp1.shape[-1])
    g_w2 = _wgrad_call(flat(u), flat(dy2), "wgrad_mlp2")
    (grad_x, dproj, h1, dy1, dmod1, cgrad), (s_w1, s_w2) = _mixer_bwd_call(
        dx1, x, proj, a1, cv, y1, mod, win_t, wout_all, cw, cp, [g_w1, g_w2])
    g_in_t = _wgrad_call(flat(dproj), flat(h1), "wgrad_in")
    g_out = _wgrad_call(flat(mixed), flat(dy1), "wgrad_out")

    dmod = jnp.concatenate([dmod1[:, :3, :], dmod2[:, :3, :]], axis=1)
    n_cg = cgrad.shape[0]
    per_b = 6 * d_model // c_half
    per_b_pad = -(-per_b // SUBLANES) * SUBLANES
    dmod_rows = jnp.pad(dmod.reshape(bsz, per_b, c_half), ((0, 0), (0, per_b_pad - per_b), (0, 0)))
    small = jnp.concatenate([
        cgrad,
        head.reshape(2 * SUBLANES, c_half),
        dmod_rows.reshape(bsz * per_b_pad, c_half),
    ], axis=0)
    (s_in, s_out), gathered = _scatter_call([g_in_t, g_out], small)

    n_head = n_cg + 2 * SUBLANES
    sums, extra = _small_sum_call(
        gathered, n_head, (n_cg + 2, n_cg + 4), (n_head, n_head + per_b_pad, n_head + 2 * per_b_pad), 0.5 / d_model)
    loss = extra[per_b_pad, 0]
    g_b_ada = extra[:per_b].reshape(1, 6 * d_model)
    g_dw_w = lax.dynamic_slice(sums[:n_taps], (0, me * cc), (n_taps, cc))[None]
    g_sc_w = lax.dynamic_slice(sums[HALO:HALO + 3], (0, me * cc), (3, cc))[None]
    g_dw_b, g_ln_g, g_ln_b = sums[HALO + 3:HALO + 4], sums[HALO + 4:HALO + 5], sums[HALO + 5:HALO + 6]
    g_gf = sums[n_cg:n_cg + 2].reshape(d_model)

    dmod_all = gathered[:, n_head:, :].reshape(N_DEV, bsz, per_b_pad, c_half)[:, :, :per_b, :]
    dmod_all = dmod_all.reshape(N_DEV, bsz, 6 * d_model)
    dmod_cols = lax.dynamic_slice(dmod_all, (0, 0, me * a_cols), (N_DEV, bsz, a_cols))
    dmod_cols = jnp.pad(dmod_cols, ((0, 0), (0, SUBLANES - bsz), (0, 0))).reshape(N_DEV * SUBLANES, a_cols)
    c_rows = c_all.reshape(N_DEV * SUBLANES, d_model)
    g_ada, d_ada, m_ada, v_ada = _adam_ada_call(c_rows, dmod_cols, w_ada[0], m_w_ada[0], v_w_ada[0])

    gi, di, mi, vi = _adam_slabs_call(s_in, w_in[0].T, m_w_in[0].T, v_w_in[0].T, "adam_w_in")
    gi, di, mi, vi = gi.T, di.T, mi.T, vi.T
    go, do, mo, vo = _adam_slabs_call(s_out, w_out[0], m_w_out[0], v_w_out[0], "adam_w_out")
    g1, d1, m1, v1 = _adam_slabs_call(s_w1, w_mlp1[0], m_w_mlp1[0], v_w_mlp1[0], "adam_w_mlp1")
    g2, d2, m2, v2 = _adam_slabs_call(s_w2, w_mlp2[0], m_w_mlp2[0], v_w_mlp2[0], "adam_w_mlp2")

    small_w = [b_ada, conf_dw_w, conf_dw_b, conf_ln_g, conf_ln_b, sc_conv_w, g_final]
    small_g = [g_b_ada, g_dw_w, g_dw_b, g_ln_g, g_ln_b, g_sc_w, g_gf]
    small_m = [m_b_ada, m_conf_dw_w, m_conf_dw_b, m_conf_ln_g, m_conf_ln_b, m_sc_conv_w, m_g_final]
    small_v = [v_b_ada, v_conf_dw_w, v_conf_dw_b, v_conf_ln_g, v_conf_ln_b, v_sc_conv_w, v_g_final]
    pd, pm, pv = _adam_packed_call(_pack(small_w), _pack(small_g), _pack(small_m), _pack(small_v))
    sd, sm, sv = _unpack(pd, small_w), _unpack(pm, small_w), _unpack(pv, small_w)
    sg = [g.reshape(w.shape) for g, w in zip(small_g, small_w)]

    def ordered(ada, small_list, w_in_, w_out_, w1_, w2_):
        b_ada_, dw_w_, dw_b_, ln_g_, ln_b_, sc_w_, gf_ = small_list
        return [ada[None], b_ada_, w_in_[None], dw_w_, dw_b_, ln_g_, ln_b_, sc_w_, w_out_[None], w1_[None], w2_[None], gf_]

    grads = ordered(g_ada, sg, gi, go, g1, g2)
    deltas = ordered(d_ada, sd, di, do, d1, d2)
    new_m = ordered(m_ada, sm, mi, mo, m1, m2)
    new_v = ordered(v_ada, sv, vi, vo, v1, v2)
    return (loss, grad_x, *grads, *deltas, *new_m, *new_v)
```

```python
import functools

import jax
import jax.numpy as jnp
from jax import lax
from jax.experimental import pallas as pl
from jax.experimental.pallas import tpu as pltpu

N_DEV = 8
RMS_EPS = 1e-6
ADAM_LR = 0.001
ADAM_B1 = 0.9
ADAM_B2 = 0.999
ADAM_EPS = 1e-08
ADAM_WD = 0.01
ADAM_STEP = 10

F32 = jnp.float32
BF16 = jnp.bfloat16
MESH = pl.DeviceIdType.MESH
VMEM = pltpu.VMEM
ANY = pl.ANY

HALO = 32
SHORT_HALO = 8
ROW_CHUNK = 32
SUBLANES = 8
V7X_VMEM_LIMIT = 56 * 1024 * 1024


def _coords():
    return lax.axis_index("x"), lax.axis_index("y"), lax.axis_index("c")


def _dev_index():
    x, y, c = _coords()
    return 4 * x + 2 * y + c


def _peer(k):
    x, y, c = _coords()
    px = 1 - x if (k >> 2) & 1 else x
    py = 1 - y if (k >> 1) & 1 else y
    pc = 1 - c if k & 1 else c
    return (px, py, pc), 4 * px + 2 * py + pc


def _sigmoid(v):
    return jax.nn.sigmoid(v)


def _nt_dot(a, b):
    return lax.dot_general(a, b, (((1,), (1,)), ((), ())), preferred_element_type=F32)


def _nn_dot(a, b):
    return jnp.dot(a, b, preferred_element_type=F32)


def _tn_dot(a, b):
    return lax.dot_general(a, b, (((0,), (0,)), ((), ())), preferred_element_type=F32)


def _token_tile(seq):
    return 256 if seq % 256 == 0 else 64


def _gather_copies(srcs, dsts, send_sems, recv_sems, loc_sems):
    me = _dev_index()
    copies = []
    for a, (src, dst) in enumerate(zip(srcs, dsts)):
        copies.append(pltpu.make_async_copy(src, dst.at[me], loc_sems.at[a]))
        for k in range(1, N_DEV):
            peer, _ = _peer(k)
            copies.append(pltpu.make_async_remote_copy(
                src_ref=src, dst_ref=dst.at[me], send_sem=send_sems.at[a, k - 1], recv_sem=recv_sems.at[a, k - 1],
                device_id=peer, device_id_type=MESH))
    return copies


def _scatter_copies(srcs, dsts, send_sems, recv_sems, loc_sems):
    me = _dev_index()
    copies = []
    for a, (src, dst) in enumerate(zip(srcs, dsts)):
        copies.append(pltpu.make_async_copy(src.at[me], dst.at[me], loc_sems.at[a]))
        for k in range(1, N_DEV):
            peer, pidx = _peer(k)
            copies.append(pltpu.make_async_remote_copy(
                src_ref=src.at[pidx], dst_ref=dst.at[me], send_sem=send_sems.at[a, k - 1],
                recv_sem=recv_sems.at[a, k - 1], device_id=peer, device_id_type=MESH))
    return copies


def _exchange_sems(n_arrays):
    return [pltpu.SemaphoreType.DMA((n_arrays, N_DEV - 1)), pltpu.SemaphoreType.DMA((n_arrays, N_DEV - 1)),
            pltpu.SemaphoreType.DMA((n_arrays,))]


def _gather_call(c_pad, w_ada, b_ada_loc, small_loc, big_shards):
    n_big = len(big_shards)
    d_model = c_pad.shape[1]
    a_cols = w_ada.shape[1]

    def body(c_ref, wada_ref, bada_ref, small_ref, *rest):
        big_in = rest[:n_big]
        big_out = rest[n_big:2 * n_big]
        small_all, c_all, mod_rows = rest[2 * n_big:2 * n_big + 3]
        modcols, big_send, big_recv, loc_sem, s_send, s_recv = rest[2 * n_big + 3:]
        me = _dev_index()
        big = _gather_copies(big_in, big_out, big_send, big_recv, loc_sem)
        for cp in big:
            cp.start()

        small_all[me] = small_ref[...]
        c_all[me] = c_ref[...]
        first = []
        for k in range(1, N_DEV):
            peer, _ = _peer(k)
            for i, (src, dst) in enumerate(((small_ref, small_all), (c_ref, c_all))):
                cp = pltpu.make_async_remote_copy(
                    src_ref=src, dst_ref=dst.at[me],
                    send_sem=s_send.at[i, k - 1], recv_sem=s_recv.at[i, k - 1],
                    device_id=peer, device_id_type=MESH)
                cp.start()
                first.append(cp)
        for cp in first:
            cp.wait()

        c_rows = c_all[...].reshape(N_DEV * SUBLANES, d_model)
        c_act = c_rows * _sigmoid(c_rows)
        modcols[...] = _nn_dot(c_act.astype(BF16), wada_ref[...].astype(BF16)) + bada_ref[...]
        mod_rows[me] = modcols[pl.ds(pl.multiple_of(me * SUBLANES, SUBLANES), SUBLANES), :]
        second = []
        for k in range(1, N_DEV):
            peer, pidx = _peer(k)
            cp = pltpu.make_async_remote_copy(
                src_ref=modcols.at[pl.ds(pl.multiple_of(pidx * SUBLANES, SUBLANES), SUBLANES), :],
                dst_ref=mod_rows.at[me],
                send_sem=s_send.at[2, k - 1], recv_sem=s_recv.at[2, k - 1],
                device_id=peer, device_id_type=MESH)
            cp.start()
            second.append(cp)
        for cp in second + big:
            cp.wait()

    out_shape = tuple(jax.ShapeDtypeStruct((N_DEV,) + s.shape, s.dtype) for s in big_shards) + (
        jax.ShapeDtypeStruct((N_DEV,) + small_loc.shape, F32),
        jax.ShapeDtypeStruct((N_DEV, SUBLANES, d_model), F32),
        jax.ShapeDtypeStruct((N_DEV, SUBLANES, a_cols), F32),
    )
    vm = pl.BlockSpec(memory_space=VMEM)
    hbm = pl.BlockSpec(memory_space=ANY)
    outs = pl.pallas_call(
        body, name="gather_weights_mod", out_shape=out_shape,
        in_specs=[vm, vm, vm, vm] + [hbm] * n_big,
        out_specs=tuple([hbm] * n_big + [vm, vm, vm]),
        scratch_shapes=[
            pltpu.VMEM((N_DEV * SUBLANES, a_cols), F32),
            pltpu.SemaphoreType.DMA((n_big, N_DEV - 1)),
            pltpu.SemaphoreType.DMA((n_big, N_DEV - 1)),
            pltpu.SemaphoreType.DMA((n_big,)),
            pltpu.SemaphoreType.DMA((3, N_DEV - 1)),
            pltpu.SemaphoreType.DMA((3, N_DEV - 1)),
        ],
        compiler_params=pltpu.CompilerParams(vmem_limit_bytes=V7X_VMEM_LIMIT),
    )(c_pad, w_ada, b_ada_loc, small_loc, *big_shards)
    return outs[:n_big], outs[n_big], outs[n_big + 1], outs[n_big + 2]


def _shifted_rows_count(tm):
    return tm + HALO - SUBLANES


def _fill_shifted(ext, shifted, tm):
    for s in range(1, SUBLANES):
        shifted[s - 1] = ext[s:s + _shifted_rows_count(tm), :]


def _shifted_rows(ext, shifted, start, rows):
    phase = start % SUBLANES
    aligned = start - phase
    if phase == 0:
        return ext[aligned:aligned + rows, :]
    return shifted[phase - 1, aligned:aligned + rows, :]


def _layer_norm_parts(a1):
    mu = jnp.mean(a1, axis=-1, keepdims=True)
    xc = a1 - mu
    rstd = lax.rsqrt(jnp.mean(xc * xc, axis=-1, keepdims=True) + RMS_EPS)
    return xc * rstd, rstd


def _mixer_fwd_call(x, mod, win_t, wout, cw, cp, later_shards):
    n_later = len(later_shards)
    bsz, seq, d_model = x.shape
    c_half = cw.shape[1]
    n_taps = 31
    d_in = win_t.shape[0] * win_t.shape[1]
    tm = _token_tile(seq)
    nt = seq // tm

    def body(x_ref, mod_ref, win_ref, wout_ref, cw_ref, cp_ref, *rest):
        shard_refs, rest = rest[:n_later], rest[n_later:]
        proj_ref, a1_ref, cv_ref, mixed_ref, y1_ref, x1_ref = rest[:6]
        gathered_refs, rest = rest[6:6 + n_later], rest[6 + n_later:]
        aext, qext, ashift, send_sems, recv_sems, loc_sems = rest
        b, t = pl.program_id(0), pl.program_id(1)

        @pl.when(jnp.logical_and(b == 0, t == 0))
        def _():
            for copy in _gather_copies(shard_refs, gathered_refs, send_sems, recv_sems, loc_sems):
                copy.start()

        xv = x_ref[...]
        sh1, sc1, g1 = mod_ref[0:1, :], mod_ref[1:2, :], mod_ref[2:3, :]
        r1 = lax.rsqrt(jnp.mean(xv * xv, axis=-1, keepdims=True) + RMS_EPS)
        h1 = (xv * r1) * (1.0 + sc1) + sh1
        proj = _nt_dot(h1.astype(BF16), win_ref[...].reshape(d_in, d_model))
        proj_ref[...] = proj
        val, gate = proj[:, 0:c_half], proj[:, c_half:2 * c_half]
        s_b, s_c, s_h = proj[:, 2 * c_half:3 * c_half], proj[:, 3 * c_half:4 * c_half], proj[:, 4 * c_half:5 * c_half]

        @pl.when(t == 0)
        def _():
            aext[0:HALO, :] = jnp.zeros((HALO, c_half), F32)
            qext[0:SHORT_HALO, :] = jnp.zeros((SHORT_HALO, c_half), F32)

        @pl.when(t > 0)
        def _():
            aext[0:HALO, :] = aext[tm:tm + HALO, :]
            qext[0:SHORT_HALO, :] = qext[tm:tm + SHORT_HALO, :]

        aext[HALO:HALO + tm, :] = val * _sigmoid(gate)
        qext[SHORT_HALO:SHORT_HALO + tm, :] = s_c * s_h

        base = HALO - (n_taps - 1)
        _fill_shifted(aext, ashift, tm)
        for r0 in range(0, tm, ROW_CHUNK):
            acc = jnp.zeros((ROW_CHUNK, c_half), F32)
            for k in range(n_taps):
                acc = acc + cw_ref[k:k + 1, :] * _shifted_rows(aext, ashift, r0 + base + k, ROW_CHUNK)
            a1_ref[r0:r0 + ROW_CHUNK, :] = acc + cp_ref[0:1, :]
        sbase = SHORT_HALO - 2
        conv3 = cp_ref[3:4, :] * qext[sbase:sbase + tm, :]
        conv3 = conv3 + cp_ref[4:5, :] * qext[sbase + 1:sbase + 1 + tm, :]
        conv3 = conv3 + cp_ref[5:6, :] * qext[sbase + 2:sbase + 2 + tm, :]
        cv_ref[...] = conv3

        norm, _ = _layer_norm_parts(a1_ref[...])
        a2 = norm * cp_ref[1:2, :] + cp_ref[2:3, :]
        mixed = jnp.concatenate([a2 * _sigmoid(a2), s_b * conv3], axis=-1).astype(BF16)
        mixed_ref[...] = mixed
        y1 = _nn_dot(mixed, wout_ref[...].reshape(d_model, d_model))
        y1_ref[...] = y1
        x1_ref[...] = xv + g1 * y1

        @pl.when(jnp.logical_and(b == bsz - 1, t == nt - 1))
        def _():
            for copy in _gather_copies(shard_refs, gathered_refs, send_sems, recv_sems, loc_sems):
                copy.wait()

    hbm = pl.BlockSpec(memory_space=ANY)

    def tok(width):
        return pl.BlockSpec((None, tm, width), lambda b, t: (b, t, 0))

    def const(shape):
        return pl.BlockSpec(shape, lambda b, t: (0,) * len(shape))

    out_shape = (
        jax.ShapeDtypeStruct((bsz, seq, d_in), F32),
        jax.ShapeDtypeStruct((bsz, seq, c_half), F32),
        jax.ShapeDtypeStruct((bsz, seq, c_half), F32),
        jax.ShapeDtypeStruct((bsz, seq, d_model), BF16),
        jax.ShapeDtypeStruct((bsz, seq, d_model), F32),
        jax.ShapeDtypeStruct((bsz, seq, d_model), F32),
    ) + tuple(jax.ShapeDtypeStruct((N_DEV,) + s.shape, s.dtype) for s in later_shards)
    outs = pl.pallas_call(
        body, name="mixer_fwd", out_shape=out_shape, grid=(bsz, nt),
        in_specs=[tok(d_model), pl.BlockSpec((None, 6, d_model), lambda b, t: (b, 0, 0)),
                  const(win_t.shape), const(wout.shape), const(cw.shape), const(cp.shape)] + [hbm] * n_later,
        out_specs=(tok(d_in), tok(c_half), tok(c_half), tok(d_model), tok(d_model), tok(d_model)) + (hbm,) * n_later,
        scratch_shapes=[pltpu.VMEM((tm + HALO, c_half), F32), pltpu.VMEM((tm + SHORT_HALO, c_half), F32),
                        pltpu.VMEM((SUBLANES - 1, _shifted_rows_count(tm), c_half), F32)]
        + _exchange_sems(n_later),
        compiler_params=pltpu.CompilerParams(
            dimension_semantics=("arbitrary", "arbitrary"), vmem_limit_bytes=V7X_VMEM_LIMIT),
    )(x, mod, win_t, wout, cw, cp, *later_shards)
    return outs[:6], outs[6:]


def _mlp_call(x1, target, mod, w1, w2, g_final):
    bsz, seq, d_model = x1.shape
    n_blk, _, f_blk = w1.shape
    d_ff = n_blk * f_blk
    tm = _token_tile(seq)
    nt = seq // tm

    def body(x1_ref, tgt_ref, mod_ref, w1_ref, w2_ref, gf_ref,
             dx1_ref, h2_ref, dy2_ref, u_ref, dz_ref, dmod_ref, head_ref, relu_scr):
        b, t = pl.program_id(0), pl.program_id(1)
        x1v = x1_ref[...]
        sh2, sc2, g2 = mod_ref[3:4, :], mod_ref[4:5, :], mod_ref[5:6, :]
        gf = gf_ref[...]
        r2 = lax.rsqrt(jnp.mean(x1v * x1v, axis=-1, keepdims=True) + RMS_EPS)
        xn2 = x1v * r2
        h2 = (xn2 * (1.0 + sc2) + sh2).astype(BF16)
        h2_ref[...] = h2
        y2 = jnp.zeros((tm, d_model), F32)
        for j in range(n_blk):
            cols = slice(j * f_blk, (j + 1) * f_blk)
            rz = jnp.maximum(_nn_dot(h2, w1_ref[j]), 0.0)
            relu_scr[:, cols] = rz
            ub = (rz * rz).astype(BF16)
            u_ref[:, cols] = ub
            y2 = y2 + _nn_dot(ub, w2_ref[j])
        x2 = x1v + g2 * y2
        r3 = lax.rsqrt(jnp.mean(x2 * x2, axis=-1, keepdims=True) + RMS_EPS)
        xn3 = x2 * r3
        diff = xn3 * gf - tgt_ref[...]
        dout = diff * (1.0 / d_model)

        @pl.when(jnp.logical_and(b == 0, t == 0))
        def _():
            head_ref[...] = jnp.zeros(head_ref.shape, F32)

        @pl.when(t == 0)
        def _():
            dmod_ref[...] = jnp.zeros(dmod_ref.shape, F32)

        head_ref[0:1, :] += jnp.sum(dout * xn3, axis=0, keepdims=True)
        head_ref[1:2, :] += jnp.sum(diff * diff, axis=0, keepdims=True)
        dxn3 = dout * gf
        dx2 = r3 * (dxn3 - xn3 * jnp.mean(dxn3 * xn3, axis=-1, keepdims=True))
        dmod_ref[2:3, :] += jnp.sum(dx2 * y2, axis=0, keepdims=True)
        dy2 = (g2 * dx2).astype(BF16)
        dy2_ref[...] = dy2
        dh2 = jnp.zeros((tm, d_model), F32)
        for j in range(n_blk):
            cols = slice(j * f_blk, (j + 1) * f_blk)
            dz = (_nt_dot(dy2, w2_ref[j]) * (2.0 * relu_scr[:, cols])).astype(BF16)
            dz_ref[:, cols] = dz
            dh2 = dh2 + _nt_dot(dz, w1_ref[j])
        dmod_ref[0:1, :] += jnp.sum(dh2, axis=0, keepdims=True)
        dmod_ref[1:2, :] += jnp.sum(dh2 * xn2, axis=0, keepdims=True)
        dxn2 = dh2 * (1.0 + sc2)
        dx1_ref[...] = dx2 + r2 * (dxn2 - xn2 * jnp.mean(dxn2 * xn2, axis=-1, keepdims=True))

    def tok(width):
        return pl.BlockSpec((None, tm, width), lambda b, t: (b, t, 0))

    def const(shape):
        return pl.BlockSpec(shape, lambda b, t: (0,) * len(shape))

    out_shape = (
        jax.ShapeDtypeStruct((bsz, seq, d_model), F32),
        jax.ShapeDtypeStruct((bsz, seq, d_model), BF16),
        jax.ShapeDtypeStruct((bsz, seq, d_model), BF16),
        jax.ShapeDtypeStruct((bsz, seq, d_ff), BF16),
        jax.ShapeDtypeStruct((bsz, seq, d_ff), BF16),
        jax.ShapeDtypeStruct((bsz, SUBLANES, d_model), F32),
        jax.ShapeDtypeStruct((SUBLANES, d_model), F32),
    )
    return pl.pallas_call(
        body, name="mlp_fwd_bwd", out_shape=out_shape, grid=(bsz, nt),
        in_specs=[tok(d_model), tok(d_model), pl.BlockSpec((None, 6, d_model), lambda b, t: (b, 0, 0)),
                  const(w1.shape), const(w2.shape), const(g_final.shape)],
        out_specs=(tok(d_model), tok(d_model), tok(d_model), tok(d_ff), tok(d_ff),
                   pl.BlockSpec((None, SUBLANES, d_model), lambda b, t: (b, 0, 0)),
                   const((SUBLANES, d_model))),
        scratch_shapes=[pltpu.VMEM((tm, d_ff), F32)],
        compiler_params=pltpu.CompilerParams(
            dimension_semantics=("arbitrary", "arbitrary"), vmem_limit_bytes=V7X_VMEM_LIMIT),
    )(x1, target, mod, w1, w2, g_final)


def _mixer_bwd_call(dx1, x, proj, a1, cv, y1, mod, win_t, wout, cw, cp, partials):
    n_part = len(partials)
    bsz, seq, d_model = x.shape
    c_half = cw.shape[1]
    n_taps = 31
    d_in = win_t.shape[0] * win_t.shape[1]
    tm = _token_tile(seq)
    nt = seq // tm

    def body(dx1_ref, x_ref, proj_ref, a1_ref, cv_ref, y1_ref, mod_ref, win_ref, wout_ref, cw_ref, cp_ref, *rest):
        part_refs, rest = rest[:n_part], rest[n_part:]
        gx_ref, dproj_ref, h1_ref, dy1_ref, dmod_ref, cgrad_ref = rest[:6]
        slab_refs, rest = rest[6:6 + n_part], rest[6 + n_part:]
        dext, cext, a0_scr, da0_scr, tap_acc, row_acc, dshift, send_sems, recv_sems, loc_sems = rest
        b, step = pl.program_id(0), pl.program_id(1)
        first = jnp.logical_and(b == 0, step == 0)
        last = jnp.logical_and(b == bsz - 1, step == nt - 1)

        @pl.when(first)
        def _():
            for copy in _scatter_copies(part_refs, slab_refs, send_sems, recv_sems, loc_sems):
                copy.start()

        dx1v = dx1_ref[...]
        xv = x_ref[...]
        sh1, sc1, g1 = mod_ref[0:1, :], mod_ref[1:2, :], mod_ref[2:3, :]

        @pl.when(first)
        def _():
            tap_acc[...] = jnp.zeros(tap_acc.shape, F32)
            row_acc[...] = jnp.zeros(row_acc.shape, F32)

        @pl.when(step == 0)
        def _():
            dmod_ref[...] = jnp.zeros(dmod_ref.shape, F32)
            dext[tm:tm + HALO, :] = jnp.zeros((HALO, c_half), F32)
            cext[tm:tm + SHORT_HALO, :] = jnp.zeros((SHORT_HALO, c_half), F32)

        @pl.when(step > 0)
        def _():
            dext[tm:tm + HALO, :] = dext[0:HALO, :]
            cext[tm:tm + SHORT_HALO, :] = cext[0:SHORT_HALO, :]

        dmod_ref[2:3, :] += jnp.sum(dx1v * y1_ref[...], axis=0, keepdims=True)
        dy1 = (g1 * dx1v).astype(BF16)
        dy1_ref[...] = dy1
        dmixed = _nt_dot(dy1, wout_ref[...].reshape(d_model, d_model))
        d_a, d_s = dmixed[:, 0:c_half], dmixed[:, c_half:2 * c_half]

        val, gate = proj_ref[:, 0:c_half], proj_ref[:, c_half:2 * c_half]
        s_b = proj_ref[:, 2 * c_half:3 * c_half]
        s_c, s_h = proj_ref[:, 3 * c_half:4 * c_half], proj_ref[:, 4 * c_half:5 * c_half]

        d_sb = d_s * cv_ref[...]
        cext[0:tm, :] = d_s * s_b
        q = s_c * s_h
        dq = jnp.zeros((tm, c_half), F32)
        for k in range(3):
            shifted = cext[2 - k:2 - k + tm, :]
            dq = dq + cp_ref[3 + k:4 + k, :] * shifted
            row_acc[k:k + 1, :] += jnp.sum(q * shifted, axis=0, keepdims=True)
        d_sc, d_sh = dq * s_h, dq * s_c

        norm, rstd = _layer_norm_parts(a1_ref[...])
        ln_g = cp_ref[1:2, :]
        a2 = norm * ln_g + cp_ref[2:3, :]
        sg = _sigmoid(a2)
        d_a2 = d_a * (sg * (1.0 + a2 * (1.0 - sg)))
        row_acc[4:5, :] += jnp.sum(d_a2 * norm, axis=0, keepdims=True)
        row_acc[5:6, :] += jnp.sum(d_a2, axis=0, keepdims=True)
        d_n = d_a2 * ln_g
        d_a1 = rstd * (d_n - jnp.mean(d_n, axis=-1, keepdims=True)
                       - norm * jnp.mean(d_n * norm, axis=-1, keepdims=True))
        row_acc[3:4, :] += jnp.sum(d_a1, axis=0, keepdims=True)
        dext[0:tm, :] = d_a1
        sig_g = _sigmoid(gate)
        a0_scr[...] = val * sig_g

        _fill_shifted(dext, dshift, tm)
        for r0 in range(0, tm, ROW_CHUNK):
            a0c = a0_scr[r0:r0 + ROW_CHUNK, :]
            acc = jnp.zeros((ROW_CHUNK, c_half), F32)
            for k in range(n_taps):
                shifted = _shifted_rows(dext, dshift, r0 + (n_taps - 1) - k, ROW_CHUNK)
                acc = acc + cw_ref[k:k + 1, :] * shifted
                prod = a0c * shifted
                part = prod[0:SUBLANES, :]
                for g in range(1, ROW_CHUNK // SUBLANES):
                    part = part + prod[g * SUBLANES:(g + 1) * SUBLANES, :]
                tap_acc[k * SUBLANES:(k + 1) * SUBLANES, :] += part
            da0_scr[r0:r0 + ROW_CHUNK, :] = acc
        d_a0 = da0_scr[...]
        d_val = d_a0 * sig_g
        d_gate = d_a0 * val * sig_g * (1.0 - sig_g)

        dproj = jnp.concatenate([d_val, d_gate, d_sb, d_sc, d_sh], axis=-1).astype(BF16)
        dproj_ref[...] = dproj
        dh1 = _nn_dot(dproj, win_ref[...].reshape(d_in, d_model))
        r1 = lax.rsqrt(jnp.mean(xv * xv, axis=-1, keepdims=True) + RMS_EPS)
        xn1 = xv * r1
        h1_ref[...] = (xn1 * (1.0 + sc1) + sh1).astype(BF16)
        dmod_ref[0:1, :] += jnp.sum(dh1, axis=0, keepdims=True)
        dmod_ref[1:2, :] += jnp.sum(dh1 * xn1, axis=0, keepdims=True)
        dxn1 = dh1 * (1.0 + sc1)
        gx_ref[...] = dx1v + r1 * (dxn1 - xn1 * jnp.mean(dxn1 * xn1, axis=-1, keepdims=True))

        @pl.when(last)
        def _():
            taps = jnp.sum(tap_acc[...].reshape(HALO, SUBLANES, c_half), axis=1)
            cgrad_ref[0:HALO, :] = taps
            cgrad_ref[HALO:HALO + SUBLANES, :] = row_acc[...]
            for copy in _scatter_copies(part_refs, slab_refs, send_sems, recv_sems, loc_sems):
                copy.wait()

    hbm = pl.BlockSpec(memory_space=ANY)

    def tok(width):
        return pl.BlockSpec((None, tm, width), lambda b, s: (b, nt - 1 - s, 0))

    def const(shape):
        return pl.BlockSpec(shape, lambda b, s: (0,) * len(shape))

    mod_spec = pl.BlockSpec((None, 6, d_model), lambda b, s: (b, 0, 0))
    out_shape = (
        jax.ShapeDtypeStruct((bsz, seq, d_model), F32),
        jax.ShapeDtypeStruct((bsz, seq, d_in), BF16),
        jax.ShapeDtypeStruct((bsz, seq, d_model), BF16),
        jax.ShapeDtypeStruct((bsz, seq, d_model), BF16),
        jax.ShapeDtypeStruct((bsz, SUBLANES, d_model), F32),
        jax.ShapeDtypeStruct((HALO + SUBLANES, c_half), F32),
    ) + tuple(jax.ShapeDtypeStruct(p.shape, p.dtype) for p in partials)
    outs = pl.pallas_call(
        body, name="mixer_bwd", out_shape=out_shape, grid=(bsz, nt),
        in_specs=[tok(d_model), tok(d_model), tok(d_in), tok(c_half), tok(c_half), tok(d_model), mod_spec,
                  const(win_t.shape), const(wout.shape), const(cw.shape), const(cp.shape)] + [hbm] * n_part,
        out_specs=(tok(d_model), tok(d_in), tok(d_model), tok(d_model),
                   pl.BlockSpec((None, SUBLANES, d_model), lambda b, s: (b, 0, 0)),
                   const((HALO + SUBLANES, c_half))) + (hbm,) * n_part,
        scratch_shapes=[
            pltpu.VMEM((tm + HALO, c_half), F32), pltpu.VMEM((tm + SHORT_HALO, c_half), F32),
            pltpu.VMEM((tm, c_half), F32), pltpu.VMEM((tm, c_half), F32),
            pltpu.VMEM((HALO * SUBLANES, c_half), F32), pltpu.VMEM((SUBLANES, c_half), F32),
            pltpu.VMEM((SUBLANES - 1, _shifted_rows_count(tm), c_half), F32),
        ] + _exchange_sems(n_part),
        compiler_params=pltpu.CompilerParams(
            dimension_semantics=("arbitrary", "arbitrary"), vmem_limit_bytes=V7X_VMEM_LIMIT),
    )(dx1, x, proj, a1, cv, y1, mod, win_t, wout, cw, cp, *partials)
    return outs[:6], outs[6:]


def _largest_divisor(n, cap, multiple):
    best = None
    for cand in range(multiple, min(n, cap) + 1, multiple):
        if n % cand == 0:
            best = cand
    return best if best is not None else n


WGRAD_TOKENS_PER_STEP = 2048
WGRAD_COLS_PER_STEP = 1024


def _wgrad_call(a, b, name, owner_cols=None):
    tokens, m_dim = a.shape
    n_dim = b.shape[1]
    bk = _largest_divisor(tokens, WGRAD_TOKENS_PER_STEP, 128)
    n_k = tokens // bk
    if owner_cols is None:
        bm = _largest_divisor(m_dim, 1024, m_dim // N_DEV)
        bn = n_dim
        owners = 1
        out_shape = jax.ShapeDtypeStruct((m_dim, n_dim), BF16)
        out_spec = pl.BlockSpec((bm, bn), lambda i, j, k: (i, j))
    else:
        bm = m_dim
        bn = _largest_divisor(n_dim, WGRAD_COLS_PER_STEP, owner_cols)
        owners = bn // owner_cols
        out_shape = jax.ShapeDtypeStruct((n_dim // owner_cols, m_dim, owner_cols), BF16)
        out_spec = pl.BlockSpec((owners, bm, owner_cols), lambda i, j, k: (j, i, 0))

    def body(a_ref, b_ref, o_ref, acc):
        k = pl.program_id(2)

        @pl.when(k == 0)
        def _():
            acc[...] = jnp.zeros(acc.shape, F32)

        acc[...] += _tn_dot(a_ref[...], b_ref[...])

        @pl.when(k == n_k - 1)
        def _():
            if owner_cols is None:
                o_ref[...] = acc[...].astype(BF16)
            else:
                for q in range(owners):
                    o_ref[q] = acc[:, q * owner_cols:(q + 1) * owner_cols].astype(BF16)

    out = pl.pallas_call(
        body, name=name, out_shape=out_shape, grid=(m_dim // bm, n_dim // bn, n_k),
        in_specs=[pl.BlockSpec((bk, bm), lambda i, j, k: (k, i)), pl.BlockSpec((bk, bn), lambda i, j, k: (k, j))],
        out_specs=out_spec, scratch_shapes=[pltpu.VMEM((bm, bn), F32)],
        compiler_params=pltpu.CompilerParams(
            dimension_semantics=("arbitrary", "arbitrary", "arbitrary"), vmem_limit_bytes=V7X_VMEM_LIMIT),
    )(a, b)
    if owner_cols is None:
        out = out.reshape(N_DEV, m_dim // N_DEV, n_dim)
    return out


def _scatter_call(partials, small):
    n_big = len(partials)

    def body(*refs):
        big_in = refs[:n_big]
        small_ref = refs[n_big]
        big_out = refs[n_big + 1:2 * n_big + 1]
        small_all = refs[2 * n_big + 1]
        big_send, big_recv, loc_sem, s_send, s_recv = refs[2 * n_big + 2:]
        me = _dev_index()
        small_all[me] = small_ref[...]
        copies = []
        for k in range(1, N_DEV):
            peer, _ = _peer(k)
            copies.append(pltpu.make_async_remote_copy(
                src_ref=small_ref, dst_ref=small_all.at[me],
                send_sem=s_send.at[k - 1], recv_sem=s_recv.at[k - 1],
                device_id=peer, device_id_type=MESH))
        copies += _scatter_copies(big_in, big_out, big_send, big_recv, loc_sem)
        for cp in copies:
            cp.start()
        for cp in copies:
            cp.wait()

    vm = pl.BlockSpec(memory_space=VMEM)
    hbm = pl.BlockSpec(memory_space=ANY)
    out_shape = tuple(jax.ShapeDtypeStruct(p.shape, p.dtype) for p in partials) + (
        jax.ShapeDtypeStruct((N_DEV,) + small.shape, F32),)
    outs = pl.pallas_call(
        body, name="scatter_grads", out_shape=out_shape,
        in_specs=[hbm] * n_big + [vm], out_specs=tuple([hbm] * n_big + [vm]),
        scratch_shapes=[
            pltpu.SemaphoreType.DMA((n_big, N_DEV - 1)),
            pltpu.SemaphoreType.DMA((n_big, N_DEV - 1)),
            pltpu.SemaphoreType.DMA((n_big,)),
            pltpu.SemaphoreType.DMA((N_DEV - 1,)),
            pltpu.SemaphoreType.DMA((N_DEV - 1,)),
        ],
        compiler_params=pltpu.CompilerParams(vmem_limit_bytes=V7X_VMEM_LIMIT),
    )(*partials, small)
    return outs[:n_big], outs[n_big]


def _adamw(w, g, m, v):
    m2 = ADAM_B1 * m + (1.0 - ADAM_B1) * g
    v2 = ADAM_B2 * v + (1.0 - ADAM_B2) * (g * g)
    m_hat = m2 / (1.0 - ADAM_B1 ** ADAM_STEP)
    v_hat = v2 / (1.0 - ADAM_B2 ** ADAM_STEP)
    delta = -ADAM_LR * (m_hat / (jnp.sqrt(v_hat) + ADAM_EPS) + ADAM_WD * w)
    return delta, m2, v2


def _adam_slabs_call(slabs, w, m, v, name):
    rows, cols = w.shape
    tr = _largest_divisor(rows, 128, 2 * SUBLANES)

    def body(s_ref, w_ref, m_ref, v_ref, g_ref, d_ref, m2_ref, v2_ref):
        g = s_ref[0].astype(F32)
        for k in range(1, N_DEV):
            g = g + s_ref[k].astype(F32)
        delta, m2, v2 = _adamw(w_ref[...], g, m_ref[...], v_ref[...])
        g_ref[...] = g
        d_ref[...] = delta
        m2_ref[...] = m2
        v2_ref[...] = v2

    tile = pl.BlockSpec((tr, cols), lambda i: (i, 0))
    shp = jax.ShapeDtypeStruct((rows, cols), F32)
    return pl.pallas_call(
        body, name=name, out_shape=(shp, shp, shp, shp), grid=(rows // tr,),
        in_specs=[pl.BlockSpec((N_DEV, tr, cols), lambda i: (0, i, 0)), tile, tile, tile],
        out_specs=(tile, tile, tile, tile),
        compiler_params=pltpu.CompilerParams(dimension_semantics=("arbitrary",), vmem_limit_bytes=V7X_VMEM_LIMIT),
    )(slabs, w, m, v)


def _adam_ada_call(c_rows, dmod_cols, w, m, v):
    rows, cols = w.shape
    n_rows = c_rows.shape[0]
    tr = _largest_divisor(rows, 256, 128)

    def body(c_ref, dm_ref, w_ref, m_ref, v_ref, g_ref, d_ref, m2_ref, v2_ref):
        cv = c_ref[...]
        c_act = (cv * _sigmoid(cv)).astype(BF16)
        g = _tn_dot(c_act, dm_ref[...].astype(BF16))
        delta, m2, v2 = _adamw(w_ref[...], g, m_ref[...], v_ref[...])
        g_ref[...] = g
        d_ref[...] = delta
        m2_ref[...] = m2
        v2_ref[...] = v2

    tile = pl.BlockSpec((tr, cols), lambda i: (i, 0))
    shp = jax.ShapeDtypeStruct((rows, cols), F32)
    return pl.pallas_call(
        body, name="adam_w_ada", out_shape=(shp, shp, shp, shp), grid=(rows // tr,),
        in_specs=[pl.BlockSpec((n_rows, tr), lambda i: (0, i)), pl.BlockSpec((n_rows, cols), lambda i: (0, 0)),
                  tile, tile, tile],
        out_specs=(tile, tile, tile, tile),
        compiler_params=pltpu.CompilerParams(dimension_semantics=("arbitrary",), vmem_limit_bytes=V7X_VMEM_LIMIT),
    )(c_rows, dmod_cols, w, m, v)


def _small_sum_call(small_all, n_grad_rows, loss_rows, bias_rows, loss_scale):
    _, rows, width = small_all.shape
    lo, hi = loss_rows
    b0, b1, b2 = bias_rows
    nb = b1 - b0

    def body(s_ref, sum_ref, extra_ref):
        tot = s_ref[0]
        for k in range(1, N_DEV):
            tot = tot + s_ref[k]
        sum_ref[...] = tot[0:n_grad_rows, :]
        extra_ref[0:nb, :] = tot[b0:b1, :] + tot[b1:b2, :]
        head = tot[n_grad_rows - 2 * SUBLANES:n_grad_rows, :]
        rows_id = lax.broadcasted_iota(jnp.int32, head.shape, 0) + (n_grad_rows - 2 * SUBLANES)
        sq = jnp.where(jnp.logical_and(rows_id >= lo, rows_id < hi), head, 0.0)
        extra_ref[nb:nb + SUBLANES, :] = jnp.zeros((SUBLANES, width), F32) + jnp.sum(sq) * loss_scale

    vm = pl.BlockSpec(memory_space=VMEM)
    return pl.pallas_call(
        body, name="small_sum",
        out_shape=(jax.ShapeDtypeStruct((n_grad_rows, width), F32), jax.ShapeDtypeStruct((nb + SUBLANES, width), F32)),
        in_specs=[vm], out_specs=(vm, vm),
    )(small_all)


def _adam_packed_call(w, g, m, v):
    def body(w_ref, g_ref, m_ref, v_ref, d_ref, m2_ref, v2_ref):
        delta, m2, v2 = _adamw(w_ref[...], g_ref[...], m_ref[...], v_ref[...])
        d_ref[...] = delta
        m2_ref[...] = m2
        v2_ref[...] = v2

    vm = pl.BlockSpec(memory_space=VMEM)
    shp = jax.ShapeDtypeStruct(w.shape, F32)
    return pl.pallas_call(body, name="adam_small", out_shape=(shp, shp, shp),
                          in_specs=[vm, vm, vm, vm], out_specs=(vm, vm, vm))(w, g, m, v)


def _pack(parts):
    rows = []
    for p in parts:
        flat = p.reshape(-1)
        n = flat.shape[0]
        padded = -(-n // (SUBLANES * 128)) * SUBLANES * 128
        rows.append(jnp.pad(flat, (0, padded - n)).reshape(-1, 128))
    return jnp.concatenate(rows, axis=0)


def _unpack(packed, like):
    out, r = [], 0
    for p in like:
        n = p.size
        nrow = -(-n // (SUBLANES * 128)) * SUBLANES
        out.append(packed[r:r + nrow].reshape(-1)[:n].reshape(p.shape))
        r += nrow
    return out


def kernel(x, c, w_ada, b_ada, w_in, conf_dw_w, conf_dw_b, conf_ln_g, conf_ln_b, sc_conv_w, w_out, w_mlp1, w_mlp2, g_final, loss_target, m_w_ada, m_b_ada, m_w_in, m_conf_dw_w, m_conf_dw_b, m_conf_ln_g, m_conf_ln_b, m_sc_conv_w, m_w_out, m_w_mlp1, m_w_mlp2, m_g_final, v_w_ada, v_b_ada, v_w_in, v_conf_dw_w, v_conf_dw_b, v_conf_ln_g, v_conf_ln_b, v_sc_conv_w, v_w_out, v_w_mlp1, v_w_mlp2, v_g_final):
    bsz, seq, d_model = x.shape
    c_half = conf_dw_b.shape[-1]
    n_taps = conf_dw_w.shape[1]
    cc = conf_dw_w.shape[-1]
    a_cols = w_ada.shape[-1]
    tokens = bsz * seq
    me = _dev_index()

    c_pad = jnp.pad(c, ((0, SUBLANES - bsz), (0, 0)))
    b_ada_loc = lax.dynamic_slice(b_ada, (0, me * a_cols), (1, a_cols))
    small_loc = jnp.zeros((HALO, 128), F32)
    small_loc = small_loc.at[:n_taps, :cc].set(conf_dw_w[0]).at[:3, cc:2 * cc].set(sc_conv_w[0])
    (win_t, wout_all), small_all, c_all, mod_rows = _gather_call(
        c_pad, w_ada[0], b_ada_loc, small_loc, [w_in[0].T.astype(BF16), w_out[0].astype(BF16)])
    cw = small_all[:, :, :cc].transpose(1, 0, 2).reshape(HALO, c_half)
    scw = small_all[:, :3, cc:2 * cc].transpose(1, 0, 2).reshape(3, c_half)
    cp = jnp.concatenate([conf_dw_b, conf_ln_g, conf_ln_b, scw, jnp.zeros((2, c_half), F32)], axis=0)
    mod = mod_rows[:, :bsz, :].transpose(1, 0, 2).reshape(bsz, 6, d_model)

    flat = lambda t: t.reshape(tokens, t.shape[-1])
    (proj, a1, cv, mixed, y1, x1), (w1_all, w2_all) = _mixer_fwd_call(
        x, mod, win_t, wout_all, cw, cp, [w_mlp1[0].astype(BF16), w_mlp2[0].astype(BF16)])
    dx1, h2, dy2, u, dz, dmod2, head = _mlp_call(x1, loss_target, mod, w1_all, w2_all, g_final.reshape(1, d_model))
    g_w1 = _wgrad_call(flat(h2), flat(dz), "wgrad_mlp1", owner_cols=w_mlp1.shape[-1])
    g_w2 = _wgrad_call(flat(u), flat(dy2), "wgrad_mlp2")
    (grad_x, dproj, h1, dy1, dmod1, cgrad), (s_w1, s_w2) = _mixer_bwd_call(
        dx1, x, proj, a1, cv, y1, mod, win_t, wout_all, cw, cp, [g_w1, g_w2])
    g_in_t = _wgrad_call(flat(dproj), flat(h1), "wgrad_in")
    g_out = _wgrad_call(flat(mixed), flat(dy1), "wgrad_out")

    dmod = jnp.concatenate([dmod1[:, :3, :], dmod2[:, :3, :]], axis=1)
    n_cg = cgrad.shape[0]
    per_b = 6 * d_model // c_half
    per_b_pad = -(-per_b // SUBLANES) * SUBLANES
    dmod_rows = jnp.pad(dmod.reshape(bsz, per_b, c_half), ((0, 0), (0, per_b_pad - per_b), (0, 0)))
    small = jnp.concatenate([
        cgrad,
        head.reshape(2 * SUBLANES, c_half),
        dmod_rows.reshape(bsz * per_b_pad, c_half),
    ], axis=0)
    (s_in, s_out), gathered = _scatter_call([g_in_t, g_out], small)

    n_head = n_cg + 2 * SUBLANES
    sums, extra = _small_sum_call(
        gathered, n_head, (n_cg + 2, n_cg + 4), (n_head, n_head + per_b_pad, n_head + 2 * per_b_pad), 0.5 / d_model)
    loss = extra[per_b_pad, 0]
    g_b_ada = extra[:per_b].reshape(1, 6 * d_model)
    g_dw_w = lax.dynamic_slice(sums[:n_taps], (0, me * cc), (n_taps, cc))[None]
    g_sc_w = lax.dynamic_slice(sums[HALO:HALO + 3], (0, me * cc), (3, cc))[None]
    g_dw_b, g_ln_g, g_ln_b = sums[HALO + 3:HALO + 4], sums[HALO + 4:HALO + 5], sums[HALO + 5:HALO + 6]
    g_gf = sums[n_cg:n_cg + 2].reshape(d_model)

    dmod_all = gathered[:, n_head:, :].reshape(N_DEV, bsz, per_b_pad, c_half)[:, :, :per_b, :]
    dmod_all = dmod_all.reshape(N_DEV, bsz, 6 * d_model)
    dmod_cols = lax.dynamic_slice(dmod_all, (0, 0, me * a_cols), (N_DEV, bsz, a_cols))
    dmod_cols = jnp.pad(dmod_cols, ((0, 0), (0, SUBLANES - bsz), (0, 0))).reshape(N_DEV * SUBLANES, a_cols)
    c_rows = c_all.reshape(N_DEV * SUBLANES, d_model)
    g_ada, d_ada, m_ada, v_ada = _adam_ada_call(c_rows, dmod_cols, w_ada[0], m_w_ada[0], v_w_ada[0])

    gi, di, mi, vi = _adam_slabs_call(s_in, w_in[0].T, m_w_in[0].T, v_w_in[0].T, "adam_w_in")
    gi, di, mi, vi = gi.T, di.T, mi.T, vi.T
    go, do, mo, vo = _adam_slabs_call(s_out, w_out[0], m_w_out[0], v_w_out[0], "adam_w_out")
    g1, d1, m1, v1 = _adam_slabs_call(s_w1, w_mlp1[0], m_w_mlp1[0], v_w_mlp1[0], "adam_w_mlp1")
    g2, d2, m2, v2 = _adam_slabs_call(s_w2, w_mlp2[0], m_w_mlp2[0], v_w_mlp2[0], "adam_w_mlp2")

    small_w = [b_ada, conf_dw_w, conf_dw_b, conf_ln_g, conf_ln_b, sc_conv_w, g_final]
    small_g = [g_b_ada, g_dw_w, g_dw_b, g_ln_g, g_ln_b, g_sc_w, g_gf]
    small_m = [m_b_ada, m_conf_dw_w, m_conf_dw_b, m_conf_ln_g, m_conf_ln_b, m_sc_conv_w, m_g_final]
    small_v = [v_b_ada, v_conf_dw_w, v_conf_dw_b, v_conf_ln_g, v_conf_ln_b, v_sc_conv_w, v_g_final]
    pd, pm, pv = _adam_packed_call(_pack(small_w), _pack(small_g), _pack(small_m), _pack(small_v))
    sd, sm, sv = _unpack(pd, small_w), _unpack(pm, small_w), _unpack(pv, small_w)
    sg = [g.reshape(w.shape) for g, w in zip(small_g, small_w)]

    def ordered(ada, small_list, w_in_, w_out_, w1_, w2_):
        b_ada_, dw_w_, dw_b_, ln_g_, ln_b_, sc_w_, gf_ = small_list
        return [ada[None], b_ada_, w_in_[None], dw_w_, dw_b_, ln_g_, ln_b_, sc_w_, w_out_[None], w1_[None], w2_[None], gf_]

    grads = ordered(g_ada, sg, gi, go, g1, g2)
    deltas = ordered(d_ada, sd, di, do, d1, d2)
    new_m = ordered(m_ada, sm, mi, mo, m1, m2)
    new_v = ordered(v_ada, sv, vi, vo, v1, v2)
    return (loss, grad_x, *grads, *deltas, *new_m, *new_v)
```

```python
import functools

import jax
import jax.numpy as jnp
from jax import lax
from jax.experimental import pallas as pl
from jax.experimental.pallas import tpu as pltpu

N_DEV = 8
RMS_EPS = 1e-6
ADAM_LR = 0.001
ADAM_B1 = 0.9
ADAM_B2 = 0.999
ADAM_EPS = 1e-08
ADAM_WD = 0.01
ADAM_STEP = 10

F32 = jnp.float32
BF16 = jnp.bfloat16
MESH = pl.DeviceIdType.MESH
VMEM = pltpu.VMEM
ANY = pl.ANY

HALO = 32
SHORT_HALO = 8
ROW_CHUNK = 32
SUBLANES = 8
V7X_VMEM_LIMIT = 56 * 1024 * 1024


def _coords():
    return lax.axis_index("x"), lax.axis_index("y"), lax.axis_index("c")


def _dev_index():
    x, y, c = _coords()
    return 4 * x + 2 * y + c


def _peer(k):
    x, y, c = _coords()
    px = 1 - x if (k >> 2) & 1 else x
    py = 1 - y if (k >> 1) & 1 else y
    pc = 1 - c if k & 1 else c
    return (px, py, pc), 4 * px + 2 * py + pc


def _sigmoid(v):
    return jax.nn.sigmoid(v)


def _nt_dot(a, b):
    return lax.dot_general(a, b, (((1,), (1,)), ((), ())), preferred_element_type=F32)


def _nn_dot(a, b):
    return jnp.dot(a, b, preferred_element_type=F32)


def _tn_dot(a, b):
    return lax.dot_general(a, b, (((0,), (0,)), ((), ())), preferred_element_type=F32)


def _token_tile(seq):
    return 256 if seq % 256 == 0 else 64


class _TwoLevelGather:
    def __init__(self, srcs, dsts, send_sems, recv_sems, loc_sems):
        x, y, c = _coords()
        me = 4 * x + 2 * y + c
        sibling = (x, y, 1 - c)
        chips = [(1 - x, y), (x, 1 - y), (1 - x, 1 - y)]

        def remote(src, dst, a, col, to):
            return pltpu.make_async_remote_copy(
                src_ref=src, dst_ref=dst, send_sem=send_sems.at[a, col], recv_sem=recv_sems.at[a, col],
                device_id=to, device_id_type=MESH)

        self.local, self.to_sibling, self.ici, self.forwards = [], [], [], []
        for a, (src, dst) in enumerate(zip(srcs, dsts)):
            self.local.append(pltpu.make_async_copy(src, dst.at[me], loc_sems.at[a]))
            self.to_sibling.append(remote(src, dst.at[me], a, 0, sibling))
            for j, (cx, cy) in enumerate(chips):
                self.ici.append(remote(src, dst.at[me], a, 1 + j, (cx, cy, c)))
                landed = dst.at[4 * cx + 2 * cy + c]
                self.forwards.append(remote(landed, landed, a, 4 + j, sibling))

    def start(self):
        for cp in self.local + self.to_sibling + self.ici:
            cp.start()

    def forward(self):
        for arrival, onward in zip(self.ici, self.forwards):
            arrival.wait_recv()
            onward.start()

    def finish(self):
        for cp in self.ici:
            cp.wait_send()
        for cp in self.local + self.to_sibling + self.forwards:
            cp.wait()


def _scatter_copies(srcs, dsts, send_sems, recv_sems, loc_sems):
    me = _dev_index()
    copies = []
    for a, (src, dst) in enumerate(zip(srcs, dsts)):
        copies.append(pltpu.make_async_copy(src.at[me], dst.at[me], loc_sems.at[a]))
        for k in range(1, N_DEV):
            peer, pidx = _peer(k)
            copies.append(pltpu.make_async_remote_copy(
                src_ref=src.at[pidx], dst_ref=dst.at[me], send_sem=send_sems.at[a, k - 1],
                recv_sem=recv_sems.at[a, k - 1], device_id=peer, device_id_type=MESH))
    return copies


def _exchange_sems(n_arrays):
    return [pltpu.SemaphoreType.DMA((n_arrays, N_DEV - 1)), pltpu.SemaphoreType.DMA((n_arrays, N_DEV - 1)),
            pltpu.SemaphoreType.DMA((n_arrays,))]


def _gather_call(c_pad, w_ada, b_ada_loc, small_loc, big_shards):
    n_big = len(big_shards)
    d_model = c_pad.shape[1]
    a_cols = w_ada.shape[1]

    def body(c_ref, wada_ref, bada_ref, small_ref, *rest):
        big_in = rest[:n_big]
        big_out = rest[n_big:2 * n_big]
        small_all, c_all, mod_rows = rest[2 * n_big:2 * n_big + 3]
        modcols, big_send, big_recv, loc_sem, s_send, s_recv = rest[2 * n_big + 3:]
        me = _dev_index()
        big = _TwoLevelGather(big_in, big_out, big_send, big_recv, loc_sem)
        big.start()

        small_all[me] = small_ref[...]
        c_all[me] = c_ref[...]
        first = []
        for k in range(1, N_DEV):
            peer, _ = _peer(k)
            for i, (src, dst) in enumerate(((small_ref, small_all), (c_ref, c_all))):
                cp = pltpu.make_async_remote_copy(
                    src_ref=src, dst_ref=dst.at[me],
                    send_sem=s_send.at[i, k - 1], recv_sem=s_recv.at[i, k - 1],
                    device_id=peer, device_id_type=MESH)
                cp.start()
                first.append(cp)
        for cp in first:
            cp.wait()

        c_rows = c_all[...].reshape(N_DEV * SUBLANES, d_model)
        c_act = c_rows * _sigmoid(c_rows)
        modcols[...] = _nn_dot(c_act.astype(BF16), wada_ref[...].astype(BF16)) + bada_ref[...]
        mod_rows[me] = modcols[pl.ds(pl.multiple_of(me * SUBLANES, SUBLANES), SUBLANES), :]
        second = []
        for k in range(1, N_DEV):
            peer, pidx = _peer(k)
            cp = pltpu.make_async_remote_copy(
                src_ref=modcols.at[pl.ds(pl.multiple_of(pidx * SUBLANES, SUBLANES), SUBLANES), :],
                dst_ref=mod_rows.at[me],
                send_sem=s_send.at[2, k - 1], recv_sem=s_recv.at[2, k - 1],
                device_id=peer, device_id_type=MESH)
            cp.start()
            second.append(cp)
        big.forward()
        for cp in second:
            cp.wait()
        big.finish()

    out_shape = tuple(jax.ShapeDtypeStruct((N_DEV,) + s.shape, s.dtype) for s in big_shards) + (
        jax.ShapeDtypeStruct((N_DEV,) + small_loc.shape, F32),
        jax.ShapeDtypeStruct((N_DEV, SUBLANES, d_model), F32),
        jax.ShapeDtypeStruct((N_DEV, SUBLANES, a_cols), F32),
    )
    vm = pl.BlockSpec(memory_space=VMEM)
    hbm = pl.BlockSpec(memory_space=ANY)
    outs = pl.pallas_call(
        body, name="gather_weights_mod", out_shape=out_shape,
        in_specs=[vm, vm, vm, vm] + [hbm] * n_big,
        out_specs=tuple([hbm] * n_big + [vm, vm, vm]),
        scratch_shapes=[
            pltpu.VMEM((N_DEV * SUBLANES, a_cols), F32),
            pltpu.SemaphoreType.DMA((n_big, N_DEV - 1)),
            pltpu.SemaphoreType.DMA((n_big, N_DEV - 1)),
            pltpu.SemaphoreType.DMA((n_big,)),
            pltpu.SemaphoreType.DMA((3, N_DEV - 1)),
            pltpu.SemaphoreType.DMA((3, N_DEV - 1)),
        ],
        compiler_params=pltpu.CompilerParams(vmem_limit_bytes=V7X_VMEM_LIMIT),
    )(c_pad, w_ada, b_ada_loc, small_loc, *big_shards)
    return outs[:n_big], outs[n_big], outs[n_big + 1], outs[n_big + 2]


def _shifted_rows_count(tm):
    return tm + HALO - SUBLANES


def _fill_shifted(ext, shifted, tm):
    for s in range(1, SUBLANES):
        shifted[s - 1] = ext[s:s + _shifted_rows_count(tm), :]


def _shifted_rows(ext, shifted, start, rows):
    phase = start % SUBLANES
    aligned = start - phase
    if phase == 0:
        return ext[aligned:aligned + rows, :]
    return shifted[phase - 1, aligned:aligned + rows, :]


def _layer_norm_parts(a1):
    mu = jnp.mean(a1, axis=-1, keepdims=True)
    xc = a1 - mu
    rstd = lax.rsqrt(jnp.mean(xc * xc, axis=-1, keepdims=True) + RMS_EPS)
    return xc * rstd, rstd


def _mixer_fwd_call(x, mod, win_t, wout, cw, cp, later_shards):
    n_later = len(later_shards)
    bsz, seq, d_model = x.shape
    forward_step = (2 * bsz * (seq // _token_tile(seq))) // 3
    c_half = cw.shape[1]
    n_taps = 31
    d_in = win_t.shape[0] * win_t.shape[1]
    tm = _token_tile(seq)
    nt = seq // tm

    def body(x_ref, mod_ref, win_ref, wout_ref, cw_ref, cp_ref, *rest):
        shard_refs, rest = rest[:n_later], rest[n_later:]
        proj_ref, a1_ref, cv_ref, mixed_ref, y1_ref, x1_ref = rest[:6]
        gathered_refs, rest = rest[6:6 + n_later], rest[6 + n_later:]
        aext, qext, ashift, send_sems, recv_sems, loc_sems = rest
        b, t = pl.program_id(0), pl.program_id(1)

        step = b * nt + t

        @pl.when(step == 0)
        def _():
            _TwoLevelGather(shard_refs, gathered_refs, send_sems, recv_sems, loc_sems).start()

        @pl.when(step == forward_step)
        def _():
            _TwoLevelGather(shard_refs, gathered_refs, send_sems, recv_sems, loc_sems).forward()

        xv = x_ref[...]
        sh1, sc1, g1 = mod_ref[0:1, :], mod_ref[1:2, :], mod_ref[2:3, :]
        r1 = lax.rsqrt(jnp.mean(xv * xv, axis=-1, keepdims=True) + RMS_EPS)
        h1 = (xv * r1) * (1.0 + sc1) + sh1
        proj = _nt_dot(h1.astype(BF16), win_ref[...].reshape(d_in, d_model))
        proj_ref[...] = proj
        val, gate = proj[:, 0:c_half], proj[:, c_half:2 * c_half]
        s_b, s_c, s_h = proj[:, 2 * c_half:3 * c_half], proj[:, 3 * c_half:4 * c_half], proj[:, 4 * c_half:5 * c_half]

        @pl.when(t == 0)
        def _():
            aext[0:HALO, :] = jnp.zeros((HALO, c_half), F32)
            qext[0:SHORT_HALO, :] = jnp.zeros((SHORT_HALO, c_half), F32)

        @pl.when(t > 0)
        def _():
            aext[0:HALO, :] = aext[tm:tm + HALO, :]
            qext[0:SHORT_HALO, :] = qext[tm:tm + SHORT_HALO, :]

        aext[HALO:HALO + tm, :] = val * _sigmoid(gate)
        qext[SHORT_HALO:SHORT_HALO + tm, :] = s_c * s_h

        base = HALO - (n_taps - 1)
        _fill_shifted(aext, ashift, tm)
        for r0 in range(0, tm, ROW_CHUNK):
            acc = jnp.zeros((ROW_CHUNK, c_half), F32)
            for k in range(n_taps):
                acc = acc + cw_ref[k:k + 1, :] * _shifted_rows(aext, ashift, r0 + base + k, ROW_CHUNK)
            a1_ref[r0:r0 + ROW_CHUNK, :] = acc + cp_ref[0:1, :]
        sbase = SHORT_HALO - 2
        conv3 = cp_ref[3:4, :] * qext[sbase:sbase + tm, :]
        conv3 = conv3 + cp_ref[4:5, :] * qext[sbase + 1:sbase + 1 + tm, :]
        conv3 = conv3 + cp_ref[5:6, :] * qext[sbase + 2:sbase + 2 + tm, :]
        cv_ref[...] = conv3

        norm, _ = _layer_norm_parts(a1_ref[...])
        a2 = norm * cp_ref[1:2, :] + cp_ref[2:3, :]
        mixed = jnp.concatenate([a2 * _sigmoid(a2), s_b * conv3], axis=-1).astype(BF16)
        mixed_ref[...] = mixed
        y1 = _nn_dot(mixed, wout_ref[...].reshape(d_model, d_model))
        y1_ref[...] = y1
        x1_ref[...] = xv + g1 * y1

        @pl.when(step == bsz * nt - 1)
        def _():
            _TwoLevelGather(shard_refs, gathered_refs, send_sems, recv_sems, loc_sems).finish()

    hbm = pl.BlockSpec(memory_space=ANY)

    def tok(width):
        return pl.BlockSpec((None, tm, width), lambda b, t: (b, t, 0))

    def const(shape):
        return pl.BlockSpec(shape, lambda b, t: (0,) * len(shape))

    out_shape = (
        jax.ShapeDtypeStruct((bsz, seq, d_in), F32),
        jax.ShapeDtypeStruct((bsz, seq, c_half), F32),
        jax.ShapeDtypeStruct((bsz, seq, c_half), F32),
        jax.ShapeDtypeStruct((bsz, seq, d_model), BF16),
        jax.ShapeDtypeStruct((bsz, seq, d_model), F32),
        jax.ShapeDtypeStruct((bsz, seq, d_model), F32),
    ) + tuple(jax.ShapeDtypeStruct((N_DEV,) + s.shape, s.dtype) for s in later_shards)
    outs = pl.pallas_call(
        body, name="mixer_fwd", out_shape=out_shape, grid=(bsz, nt),
        in_specs=[tok(d_model), pl.BlockSpec((None, 6, d_model), lambda b, t: (b, 0, 0)),
                  const(win_t.shape), const(wout.shape), const(cw.shape), const(cp.shape)] + [hbm] * n_later,
        out_specs=(tok(d_in), tok(c_half), tok(c_half), tok(d_model), tok(d_model), tok(d_model)) + (hbm,) * n_later,
        scratch_shapes=[pltpu.VMEM((tm + HALO, c_half), F32), pltpu.VMEM((tm + SHORT_HALO, c_half), F32),
                        pltpu.VMEM((SUBLANES - 1, _shifted_rows_count(tm), c_half), F32)]
        + _exchange_sems(n_later),
        compiler_params=pltpu.CompilerParams(
            dimension_semantics=("arbitrary", "arbitrary"), vmem_limit_bytes=V7X_VMEM_LIMIT),
    )(x, mod, win_t, wout, cw, cp, *later_shards)
    return outs[:6], outs[6:]


def _mlp_call(x1, target, mod, w1, w2, g_final):
    bsz, seq, d_model = x1.shape
    n_blk, _, f_blk = w1.shape
    d_ff = n_blk * f_blk
    tm = _token_tile(seq)
    nt = seq // tm

    def body(x1_ref, tgt_ref, mod_ref, w1_ref, w2_ref, gf_ref,
             dx1_ref, h2_ref, dy2_ref, u_ref, dz_ref, dmod_ref, head_ref, dy1_ref, relu_scr):
        b, t = pl.program_id(0), pl.program_id(1)
        x1v = x1_ref[...]
        sh2, sc2, g2 = mod_ref[3:4, :], mod_ref[4:5, :], mod_ref[5:6, :]
        gf = gf_ref[...]
        r2 = lax.rsqrt(jnp.mean(x1v * x1v, axis=-1, keepdims=True) + RMS_EPS)
        xn2 = x1v * r2
        h2 = (xn2 * (1.0 + sc2) + sh2).astype(BF16)
        h2_ref[...] = h2
        y2 = jnp.zeros((tm, d_model), F32)
        for j in range(n_blk):
            cols = slice(j * f_blk, (j + 1) * f_blk)
            rz = jnp.maximum(_nn_dot(h2, w1_ref[j]), 0.0)
            relu_scr[:, cols] = rz
            ub = (rz * rz).astype(BF16)
            u_ref[:, cols] = ub
            y2 = y2 + _nn_dot(ub, w2_ref[j])
        x2 = x1v + g2 * y2
        r3 = lax.rsqrt(jnp.mean(x2 * x2, axis=-1, keepdims=True) + RMS_EPS)
        xn3 = x2 * r3
        diff = xn3 * gf - tgt_ref[...]
        dout = diff * (1.0 / d_model)

        @pl.when(jnp.logical_and(b == 0, t == 0))
        def _():
            head_ref[...] = jnp.zeros(head_ref.shape, F32)

        @pl.when(t == 0)
        def _():
            dmod_ref[...] = jnp.zeros(dmod_ref.shape, F32)

        head_ref[0:1, :] += jnp.sum(dout * xn3, axis=0, keepdims=True)
        head_ref[1:2, :] += jnp.sum(diff * diff, axis=0, keepdims=True)
        dxn3 = dout * gf
        dx2 = r3 * (dxn3 - xn3 * jnp.mean(dxn3 * xn3, axis=-1, keepdims=True))
        dmod_ref[2:3, :] += jnp.sum(dx2 * y2, axis=0, keepdims=True)
        dy2 = (g2 * dx2).astype(BF16)
        dy2_ref[...] = dy2
        dh2 = jnp.zeros((tm, d_model), F32)
        for j in range(n_blk):
            cols = slice(j * f_blk, (j + 1) * f_blk)
            dz = (_nt_dot(dy2, w2_ref[j]) * (2.0 * relu_scr[:, cols])).astype(BF16)
            dz_ref[:, cols] = dz
            dh2 = dh2 + _nt_dot(dz, w1_ref[j])
        dmod_ref[0:1, :] += jnp.sum(dh2, axis=0, keepdims=True)
        dmod_ref[1:2, :] += jnp.sum(dh2 * xn2, axis=0, keepdims=True)
        dxn2 = dh2 * (1.0 + sc2)
        dx1 = dx2 + r2 * (dxn2 - xn2 * jnp.mean(dxn2 * xn2, axis=-1, keepdims=True))
        dx1_ref[...] = dx1
        dy1_ref[...] = (mod_ref[2:3, :] * dx1).astype(BF16)

    def tok(width):
        return pl.BlockSpec((None, tm, width), lambda b, t: (b, t, 0))

    def const(shape):
        return pl.BlockSpec(shape, lambda b, t: (0,) * len(shape))

    out_shape = (
        jax.ShapeDtypeStruct((bsz, seq, d_model), F32),
        jax.ShapeDtypeStruct((bsz, seq, d_model), BF16),
        jax.ShapeDtypeStruct((bsz, seq, d_model), BF16),
        jax.ShapeDtypeStruct((bsz, seq, d_ff), BF16),
        jax.ShapeDtypeStruct((bsz, seq, d_ff), BF16),
        jax.ShapeDtypeStruct((bsz, SUBLANES, d_model), F32),
        jax.ShapeDtypeStruct((SUBLANES, d_model), F32),
        jax.ShapeDtypeStruct((bsz, seq, d_model), BF16),
    )
    return pl.pallas_call(
        body, name="mlp_fwd_bwd", out_shape=out_shape, grid=(bsz, nt),
        in_specs=[tok(d_model), tok(d_model), pl.BlockSpec((None, 6, d_model), lambda b, t: (b, 0, 0)),
                  const(w1.shape), const(w2.shape), const(g_final.shape)],
        out_specs=(tok(d_model), tok(d_model), tok(d_model), tok(d_ff), tok(d_ff),
                   pl.BlockSpec((None, SUBLANES, d_model), lambda b, t: (b, 0, 0)),
                   const((SUBLANES, d_model)), tok(d_model)),
        scratch_shapes=[pltpu.VMEM((tm, d_ff), F32)],
        compiler_params=pltpu.CompilerParams(
            dimension_semantics=("arbitrary", "arbitrary"), vmem_limit_bytes=V7X_VMEM_LIMIT),
    )(x1, target, mod, w1, w2, g_final)


def _mixer_bwd_call(dx1, x, proj, a1, cv, y1, mod, win_t, wout, cw, cp, partials):
    n_part = len(partials)
    bsz, seq, d_model = x.shape
    c_half = cw.shape[1]
    n_taps = 31
    d_in = win_t.shape[0] * win_t.shape[1]
    tm = _token_tile(seq)
    nt = seq // tm

    def body(dx1_ref, x_ref, proj_ref, a1_ref, cv_ref, y1_ref, mod_ref, win_ref, wout_ref, cw_ref, cp_ref, *rest):
        part_refs, rest = rest[:n_part], rest[n_part:]
        gx_ref, dproj_ref, h1_ref, dmod_ref, cgrad_ref = rest[:5]
        slab_refs, rest = rest[5:5 + n_part], rest[5 + n_part:]
        dext, cext, a0_scr, da0_scr, tap_acc, row_acc, dshift, send_sems, recv_sems, loc_sems = rest
        b, step = pl.program_id(0), pl.program_id(1)
        first = jnp.logical_and(b == 0, step == 0)
        last = jnp.logical_and(b == bsz - 1, step == nt - 1)

        @pl.when(first)
        def _():
            for copy in _scatter_copies(part_refs, slab_refs, send_sems, recv_sems, loc_sems):
                copy.start()

        dx1v = dx1_ref[...]
        xv = x_ref[...]
        sh1, sc1, g1 = mod_ref[0:1, :], mod_ref[1:2, :], mod_ref[2:3, :]

        @pl.when(first)
        def _():
            tap_acc[...] = jnp.zeros(tap_acc.shape, F32)
            row_acc[...] = jnp.zeros(row_acc.shape, F32)

        @pl.when(step == 0)
        def _():
            dmod_ref[...] = jnp.zeros(dmod_ref.shape, F32)
            dext[tm:tm + HALO, :] = jnp.zeros((HALO, c_half), F32)
            cext[tm:tm + SHORT_HALO, :] = jnp.zeros((SHORT_HALO, c_half), F32)

        @pl.when(step > 0)
        def _():
            dext[tm:tm + HALO, :] = dext[0:HALO, :]
            cext[tm:tm + SHORT_HALO, :] = cext[0:SHORT_HALO, :]

        dmod_ref[2:3, :] += jnp.sum(dx1v * y1_ref[...], axis=0, keepdims=True)
        dy1 = (g1 * dx1v).astype(BF16)
        dmixed = _nt_dot(dy1, wout_ref[...].reshape(d_model, d_model))
        d_a, d_s = dmixed[:, 0:c_half], dmixed[:, c_half:2 * c_half]

        val, gate = proj_ref[:, 0:c_half], proj_ref[:, c_half:2 * c_half]
        s_b = proj_ref[:, 2 * c_half:3 * c_half]
        s_c, s_h = proj_ref[:, 3 * c_half:4 * c_half], proj_ref[:, 4 * c_half:5 * c_half]

        d_sb = d_s * cv_ref[...]
        cext[0:tm, :] = d_s * s_b
        q = s_c * s_h
        dq = jnp.zeros((tm, c_half), F32)
        for k in range(3):
            shifted = cext[2 - k:2 - k + tm, :]
            dq = dq + cp_ref[3 + k:4 + k, :] * shifted
            row_acc[k:k + 1, :] += jnp.sum(q * shifted, axis=0, keepdims=True)
        d_sc, d_sh = dq * s_h, dq * s_c

        norm, rstd = _layer_norm_parts(a1_ref[...])
        ln_g = cp_ref[1:2, :]
        a2 = norm * ln_g + cp_ref[2:3, :]
        sg = _sigmoid(a2)
        d_a2 = d_a * (sg * (1.0 + a2 * (1.0 - sg)))
        row_acc[4:5, :] += jnp.sum(d_a2 * norm, axis=0, keepdims=True)
        row_acc[5:6, :] += jnp.sum(d_a2, axis=0, keepdims=True)
        d_n = d_a2 * ln_g
        d_a1 = rstd * (d_n - jnp.mean(d_n, axis=-1, keepdims=True)
                       - norm * jnp.mean(d_n * norm, axis=-1, keepdims=True))
        row_acc[3:4, :] += jnp.sum(d_a1, axis=0, keepdims=True)
        dext[0:tm, :] = d_a1
        sig_g = _sigmoid(gate)
        a0_scr[...] = val * sig_g

        _fill_shifted(dext, dshift, tm)
        for r0 in range(0, tm, ROW_CHUNK):
            a0c = a0_scr[r0:r0 + ROW_CHUNK, :]
            acc = jnp.zeros((ROW_CHUNK, c_half), F32)
            for k in range(n_taps):
                shifted = _shifted_rows(dext, dshift, r0 + (n_taps - 1) - k, ROW_CHUNK)
                acc = acc + cw_ref[k:k + 1, :] * shifted
                prod = a0c * shifted
                part = prod[0:SUBLANES, :]
                for g in range(1, ROW_CHUNK // SUBLANES):
                    part = part + prod[g * SUBLANES:(g + 1) * SUBLANES, :]
                tap_acc[k * SUBLANES:(k + 1) * SUBLANES, :] += part
            da0_scr[r0:r0 + ROW_CHUNK, :] = acc
        d_a0 = da0_scr[...]
        d_val = d_a0 * sig_g
        d_gate = d_a0 * val * sig_g * (1.0 - sig_g)

        dproj = jnp.concatenate([d_val, d_gate, d_sb, d_sc, d_sh], axis=-1).astype(BF16)
        dproj_ref[...] = dproj
        dh1 = _nn_dot(dproj, win_ref[...].reshape(d_in, d_model))
        r1 = lax.rsqrt(jnp.mean(xv * xv, axis=-1, keepdims=True) + RMS_EPS)
        xn1 = xv * r1
        h1_ref[...] = (xn1 * (1.0 + sc1) + sh1).astype(BF16)
        dmod_ref[0:1, :] += jnp.sum(dh1, axis=0, keepdims=True)
        dmod_ref[1:2, :] += jnp.sum(dh1 * xn1, axis=0, keepdims=True)
        dxn1 = dh1 * (1.0 + sc1)
        gx_ref[...] = dx1v + r1 * (dxn1 - xn1 * jnp.mean(dxn1 * xn1, axis=-1, keepdims=True))

        @pl.when(last)
        def _():
            taps = jnp.sum(tap_acc[...].reshape(HALO, SUBLANES, c_half), axis=1)
            cgrad_ref[0:HALO, :] = taps
            cgrad_ref[HALO:HALO + SUBLANES, :] = row_acc[...]
            for copy in _scatter_copies(part_refs, slab_refs, send_sems, recv_sems, loc_sems):
                copy.wait()

    hbm = pl.BlockSpec(memory_space=ANY)

    def tok(width):
        return pl.BlockSpec((None, tm, width), lambda b, s: (b, nt - 1 - s, 0))

    def const(shape):
        return pl.BlockSpec(shape, lambda b, s: (0,) * len(shape))

    mod_spec = pl.BlockSpec((None, 6, d_model), lambda b, s: (b, 0, 0))
    out_shape = (
        jax.ShapeDtypeStruct((bsz, seq, d_model), F32),
        jax.ShapeDtypeStruct((bsz, seq, d_in), BF16),
        jax.ShapeDtypeStruct((bsz, seq, d_model), BF16),
        jax.ShapeDtypeStruct((bsz, SUBLANES, d_model), F32),
        jax.ShapeDtypeStruct((HALO + SUBLANES, c_half), F32),
    ) + tuple(jax.ShapeDtypeStruct(p.shape, p.dtype) for p in partials)
    outs = pl.pallas_call(
        body, name="mixer_bwd", out_shape=out_shape, grid=(bsz, nt),
        in_specs=[tok(d_model), tok(d_model), tok(d_in), tok(c_half), tok(c_half), tok(d_model), mod_spec,
                  const(win_t.shape), const(wout.shape), const(cw.shape), const(cp.shape)] + [hbm] * n_part,
        out_specs=(tok(d_model), tok(d_in), tok(d_model),
                   pl.BlockSpec((None, SUBLANES, d_model), lambda b, s: (b, 0, 0)),
                   const((HALO + SUBLANES, c_half))) + (hbm,) * n_part,
        scratch_shapes=[
            pltpu.VMEM((tm + HALO, c_half), F32), pltpu.VMEM((tm + SHORT_HALO, c_half), F32),
            pltpu.VMEM((tm, c_half), F32), pltpu.VMEM((tm, c_half), F32),
            pltpu.VMEM((HALO * SUBLANES, c_half), F32), pltpu.VMEM((SUBLANES, c_half), F32),
            pltpu.VMEM((SUBLANES - 1, _shifted_rows_count(tm), c_half), F32),
        ] + _exchange_sems(n_part),
        compiler_params=pltpu.CompilerParams(
            dimension_semantics=("arbitrary", "arbitrary"), vmem_limit_bytes=V7X_VMEM_LIMIT),
    )(dx1, x, proj, a1, cv, y1, mod, win_t, wout, cw, cp, *partials)
    return outs[:5], outs[5:]


def _largest_divisor(n, cap, multiple):
    best = None
    for cand in range(multiple, min(n, cap) + 1, multiple):
        if n % cand == 0:
            best = cand
    return best if best is not None else n


WGRAD_TOKENS_PER_STEP = 2048
WGRAD_COLS_PER_STEP = 1024


def _wgrad_call(a, b, name, owner_cols=None):
    tokens, m_dim = a.shape
    n_dim = b.shape[1]
    bk = _largest_divisor(tokens, WGRAD_TOKENS_PER_STEP, 128)
    n_k = tokens // bk
    if owner_cols is None:
        bm = _largest_divisor(m_dim, 1024, m_dim // N_DEV)
        bn = n_dim
        owners = 1
        out_shape = jax.ShapeDtypeStruct((m_dim, n_dim), BF16)
        out_spec = pl.BlockSpec((bm, bn), lambda i, j, k: (i, j))
    else:
        bm = m_dim
        bn = _largest_divisor(n_dim, WGRAD_COLS_PER_STEP, owner_cols)
        owners = bn // owner_cols
        out_shape = jax.ShapeDtypeStruct((n_dim // owner_cols, m_dim, owner_cols), BF16)
        out_spec = pl.BlockSpec((owners, bm, owner_cols), lambda i, j, k: (j, i, 0))

    def body(a_ref, b_ref, o_ref, acc):
        k = pl.program_id(2)

        @pl.when(k == 0)
        def _():
            acc[...] = jnp.zeros(acc.shape, F32)

        acc[...] += _tn_dot(a_ref[...], b_ref[...])

        @pl.when(k == n_k - 1)
        def _():
            if owner_cols is None:
                o_ref[...] = acc[...].astype(BF16)
            else:
                for q in range(owners):
                    o_ref[q] = acc[:, q * owner_cols:(q + 1) * owner_cols].astype(BF16)

    out = pl.pallas_call(
        body, name=name, out_shape=out_shape, grid=(m_dim // bm, n_dim // bn, n_k),
        in_specs=[pl.BlockSpec((bk, bm), lambda i, j, k: (k, i)), pl.BlockSpec((bk, bn), lambda i, j, k: (k, j))],
        out_specs=out_spec, scratch_shapes=[pltpu.VMEM((bm, bn), F32)],
        compiler_params=pltpu.CompilerParams(
            dimension_semantics=("arbitrary", "arbitrary", "arbitrary"), vmem_limit_bytes=V7X_VMEM_LIMIT),
    )(a, b)
    if owner_cols is None:
        out = out.reshape(N_DEV, m_dim // N_DEV, n_dim)
    return out


def _scatter_call(partials, small):
    n_big = len(partials)

    def body(*refs):
        big_in = refs[:n_big]
        small_ref = refs[n_big]
        big_out = refs[n_big + 1:2 * n_big + 1]
        small_all = refs[2 * n_big + 1]
        big_send, big_recv, loc_sem, s_send, s_recv = refs[2 * n_big + 2:]
        me = _dev_index()
        small_all[me] = small_ref[...]
        copies = []
        for k in range(1, N_DEV):
            peer, _ = _peer(k)
            copies.append(pltpu.make_async_remote_copy(
                src_ref=small_ref, dst_ref=small_all.at[me],
                send_sem=s_send.at[k - 1], recv_sem=s_recv.at[k - 1],
                device_id=peer, device_id_type=MESH))
        copies += _scatter_copies(big_in, big_out, big_send, big_recv, loc_sem)
        for cp in copies:
            cp.start()
        for cp in copies:
            cp.wait()

    vm = pl.BlockSpec(memory_space=VMEM)
    hbm = pl.BlockSpec(memory_space=ANY)
    out_shape = tuple(jax.ShapeDtypeStruct(p.shape, p.dtype) for p in partials) + (
        jax.ShapeDtypeStruct((N_DEV,) + small.shape, F32),)
    outs = pl.pallas_call(
        body, name="scatter_grads", out_shape=out_shape,
        in_specs=[hbm] * n_big + [vm], out_specs=tuple([hbm] * n_big + [vm]),
        scratch_shapes=[
            pltpu.SemaphoreType.DMA((n_big, N_DEV - 1)),
            pltpu.SemaphoreType.DMA((n_big, N_DEV - 1)),
            pltpu.SemaphoreType.DMA((n_big,)),
            pltpu.SemaphoreType.DMA((N_DEV - 1,)),
            pltpu.SemaphoreType.DMA((N_DEV - 1,)),
        ],
        compiler_params=pltpu.CompilerParams(vmem_limit_bytes=V7X_VMEM_LIMIT),
    )(*partials, small)
    return outs[:n_big], outs[n_big]


def _adamw(w, g, m, v):
    m2 = ADAM_B1 * m + (1.0 - ADAM_B1) * g
    v2 = ADAM_B2 * v + (1.0 - ADAM_B2) * (g * g)
    m_hat = m2 / (1.0 - ADAM_B1 ** ADAM_STEP)
    v_hat = v2 / (1.0 - ADAM_B2 ** ADAM_STEP)
    delta = -ADAM_LR * (m_hat / (jnp.sqrt(v_hat) + ADAM_EPS) + ADAM_WD * w)
    return delta, m2, v2


def _adam_slabs_call(slabs, w, m, v, name):
    rows, cols = w.shape
    tr = _largest_divisor(rows, 128, 2 * SUBLANES)

    def body(s_ref, w_ref, m_ref, v_ref, g_ref, d_ref, m2_ref, v2_ref):
        g = s_ref[0].astype(F32)
        for k in range(1, N_DEV):
            g = g + s_ref[k].astype(F32)
        delta, m2, v2 = _adamw(w_ref[...], g, m_ref[...], v_ref[...])
        g_ref[...] = g
        d_ref[...] = delta
        m2_ref[...] = m2
        v2_ref[...] = v2

    tile = pl.BlockSpec((tr, cols), lambda i: (i, 0))
    shp = jax.ShapeDtypeStruct((rows, cols), F32)
    return pl.pallas_call(
        body, name=name, out_shape=(shp, shp, shp, shp), grid=(rows // tr,),
        in_specs=[pl.BlockSpec((N_DEV, tr, cols), lambda i: (0, i, 0)), tile, tile, tile],
        out_specs=(tile, tile, tile, tile),
        compiler_params=pltpu.CompilerParams(dimension_semantics=("arbitrary",), vmem_limit_bytes=V7X_VMEM_LIMIT),
    )(slabs, w, m, v)


def _adam_ada_call(c_rows, dmod_cols, w, m, v):
    rows, cols = w.shape
    n_rows = c_rows.shape[0]
    tr = _largest_divisor(rows, 256, 128)

    def body(c_ref, dm_ref, w_ref, m_ref, v_ref, g_ref, d_ref, m2_ref, v2_ref):
        cv = c_ref[...]
        c_act = (cv * _sigmoid(cv)).astype(BF16)
        g = _tn_dot(c_act, dm_ref[...].astype(BF16))
        delta, m2, v2 = _adamw(w_ref[...], g, m_ref[...], v_ref[...])
        g_ref[...] = g
        d_ref[...] = delta
        m2_ref[...] = m2
        v2_ref[...] = v2

    tile = pl.BlockSpec((tr, cols), lambda i: (i, 0))
    shp = jax.ShapeDtypeStruct((rows, cols), F32)
    return pl.pallas_call(
        body, name="adam_w_ada", out_shape=(shp, shp, shp, shp), grid=(rows // tr,),
        in_specs=[pl.BlockSpec((n_rows, tr), lambda i: (0, i)), pl.BlockSpec((n_rows, cols), lambda i: (0, 0)),
                  tile, tile, tile],
        out_specs=(tile, tile, tile, tile),
        compiler_params=pltpu.CompilerParams(dimension_semantics=("arbitrary",), vmem_limit_bytes=V7X_VMEM_LIMIT),
    )(c_rows, dmod_cols, w, m, v)


def _small_sum_call(small_all, n_grad_rows, loss_rows, bias_rows, loss_scale):
    _, rows, width = small_all.shape
    lo, hi = loss_rows
    b0, b1, b2 = bias_rows
    nb = b1 - b0

    def body(s_ref, sum_ref, extra_ref):
        tot = s_ref[0]
        for k in range(1, N_DEV):
            tot = tot + s_ref[k]
        sum_ref[...] = tot[0:n_grad_rows, :]
        extra_ref[0:nb, :] = tot[b0:b1, :] + tot[b1:b2, :]
        head = tot[n_grad_rows - 2 * SUBLANES:n_grad_rows, :]
        rows_id = lax.broadcasted_iota(jnp.int32, head.shape, 0) + (n_grad_rows - 2 * SUBLANES)
        sq = jnp.where(jnp.logical_and(rows_id >= lo, rows_id < hi), head, 0.0)
        extra_ref[nb:nb + SUBLANES, :] = jnp.zeros((SUBLANES, width), F32) + jnp.sum(sq) * loss_scale

    vm = pl.BlockSpec(memory_space=VMEM)
    return pl.pallas_call(
        body, name="small_sum",
        out_shape=(jax.ShapeDtypeStruct((n_grad_rows, width), F32), jax.ShapeDtypeStruct((nb + SUBLANES, width), F32)),
        in_specs=[vm], out_specs=(vm, vm),
    )(small_all)


def _adam_packed_call(w, g, m, v):
    def body(w_ref, g_ref, m_ref, v_ref, d_ref, m2_ref, v2_ref):
        delta, m2, v2 = _adamw(w_ref[...], g_ref[...], m_ref[...], v_ref[...])
        d_ref[...] = delta
        m2_ref[...] = m2
        v2_ref[...] = v2

    vm = pl.BlockSpec(memory_space=VMEM)
    shp = jax.ShapeDtypeStruct(w.shape, F32)
    return pl.pallas_call(body, name="adam_small", out_shape=(shp, shp, shp),
                          in_specs=[vm, vm, vm, vm], out_specs=(vm, vm, vm))(w, g, m, v)


def _pack(parts):
    rows = []
    for p in parts:
        flat = p.reshape(-1)
        n = flat.shape[0]
        padded = -(-n // (SUBLANES * 128)) * SUBLANES * 128
        rows.append(jnp.pad(flat, (0, padded - n)).reshape(-1, 128))
    return jnp.concatenate(rows, axis=0)


def _unpack(packed, like):
    out, r = [], 0
    for p in like:
        n = p.size
        nrow = -(-n // (SUBLANES * 128)) * SUBLANES
        out.append(packed[r:r + nrow].reshape(-1)[:n].reshape(p.shape))
        r += nrow
    return out


def kernel(x, c, w_ada, b_ada, w_in, conf_dw_w, conf_dw_b, conf_ln_g, conf_ln_b, sc_conv_w, w_out, w_mlp1, w_mlp2, g_final, loss_target, m_w_ada, m_b_ada, m_w_in, m_conf_dw_w, m_conf_dw_b, m_conf_ln_g, m_conf_ln_b, m_sc_conv_w, m_w_out, m_w_mlp1, m_w_mlp2, m_g_final, v_w_ada, v_b_ada, v_w_in, v_conf_dw_w, v_conf_dw_b, v_conf_ln_g, v_conf_ln_b, v_sc_conv_w, v_w_out, v_w_mlp1, v_w_mlp2, v_g_final):
    bsz, seq, d_model = x.shape
    c_half = conf_dw_b.shape[-1]
    n_taps = conf_dw_w.shape[1]
    cc = conf_dw_w.shape[-1]
    a_cols = w_ada.shape[-1]
    tokens = bsz * seq
    me = _dev_index()

    c_pad = jnp.pad(c, ((0, SUBLANES - bsz), (0, 0)))
    b_ada_loc = lax.dynamic_slice(b_ada, (0, me * a_cols), (1, a_cols))
    small_loc = jnp.zeros((HALO, 128), F32)
    small_loc = small_loc.at[:n_taps, :cc].set(conf_dw_w[0]).at[:3, cc:2 * cc].set(sc_conv_w[0])
    (win_t, wout_all), small_all, c_all, mod_rows = _gather_call(
        c_pad, w_ada[0], b_ada_loc, small_loc, [w_in[0].T.astype(BF16), w_out[0].astype(BF16)])
    cw = small_all[:, :, :cc].transpose(1, 0, 2).reshape(HALO, c_half)
    scw = small_all[:, :3, cc:2 * cc].transpose(1, 0, 2).reshape(3, c_half)
    cp = jnp.concatenate([conf_dw_b, conf_ln_g, conf_ln_b, scw, jnp.zeros((2, c_half), F32)], axis=0)
    mod = mod_rows[:, :bsz, :].transpose(1, 0, 2).reshape(bsz, 6, d_model)

    flat = lambda t: t.reshape(tokens, t.shape[-1])
    (proj, a1, cv, mixed, y1, x1), (w1_all, w2_all) = _mixer_fwd_call(
        x, mod, win_t, wout_all, cw, cp, [w_mlp1[0].astype(BF16), w_mlp2[0].astype(BF16)])
    dx1, h2, dy2, u, dz, dmod2, head, dy1 = _mlp_call(
        x1, loss_target, mod, w1_all, w2_all, g_final.reshape(1, d_model))
    g_w1 = _wgrad_call(flat(h2), flat(dz), "wgrad_mlp1", owner_cols=w_mlp1.shape[-1])
    g_w2 = _wgrad_call(flat(u), flat(dy2), "wgrad_mlp2")
    g_out = _wgrad_call(flat(mixed), flat(dy1), "wgrad_out")
    (grad_x, dproj, h1, dmod1, cgrad), (s_w1, s_w2, s_out) = _mixer_bwd_call(
        dx1, x, proj, a1, cv, y1, mod, win_t, wout_all, cw, cp, [g_w1, g_w2, g_out])
    g_in_t = _wgrad_call(flat(dproj), flat(h1), "wgrad_in")

    dmod = jnp.concatenate([dmod1[:, :3, :], dmod2[:, :3, :]], axis=1)
    n_cg = cgrad.shape[0]
    per_b = 6 * d_model // c_half
    per_b_pad = -(-per_b // SUBLANES) * SUBLANES
    dmod_rows = jnp.pad(dmod.reshape(bsz, per_b, c_half), ((0, 0), (0, per_b_pad - per_b), (0, 0)))
    small = jnp.concatenate([
        cgrad,
        head.reshape(2 * SUBLANES, c_half),
        dmod_rows.reshape(bsz * per_b_pad, c_half),
    ], axis=0)
    (s_in,), gathered = _scatter_call([g_in_t], small)

    n_head = n_cg + 2 * SUBLANES
    sums, extra = _small_sum_call(
        gathered, n_head, (n_cg + 2, n_cg + 4), (n_head, n_head + per_b_pad, n_head + 2 * per_b_pad), 0.5 / d_model)
    loss = extra[per_b_pad, 0]
    g_b_ada = extra[:per_b].reshape(1, 6 * d_model)
    g_dw_w = lax.dynamic_slice(sums[:n_taps], (0, me * cc), (n_taps, cc))[None]
    g_sc_w = lax.dynamic_slice(sums[HALO:HALO + 3], (0, me * cc), (3, cc))[None]
    g_dw_b, g_ln_g, g_ln_b = sums[HALO + 3:HALO + 4], sums[HALO + 4:HALO + 5], sums[HALO + 5:HALO + 6]
    g_gf = sums[n_cg:n_cg + 2].reshape(d_model)

    dmod_all = gathered[:, n_head:, :].reshape(N_DEV, bsz, per_b_pad, c_half)[:, :, :per_b, :]
    dmod_all = dmod_all.reshape(N_DEV, bsz, 6 * d_model)
    dmod_cols = lax.dynamic_slice(dmod_all, (0, 0, me * a_cols), (N_DEV, bsz, a_cols))
    dmod_cols = jnp.pad(dmod_cols, ((0, 0), (0, SUBLANES - bsz), (0, 0))).reshape(N_DEV * SUBLANES, a_cols)
    c_rows = c_all.reshape(N_DEV * SUBLANES, d_model)
    g_ada, d_ada, m_ada, v_ada = _adam_ada_call(c_rows, dmod_cols, w_ada[0], m_w_ada[0], v_w_ada[0])

    gi, di, mi, vi = _adam_slabs_call(s_in, w_in[0].T, m_w_in[0].T, v_w_in[0].T, "adam_w_in")
    gi, di, mi, vi = gi.T, di.T, mi.T, vi.T
    go, do, mo, vo = _adam_slabs_call(s_out, w_out[0], m_w_out[0], v_w_out[0], "adam_w_out")
    g1, d1, m1, v1 = _adam_slabs_call(s_w1, w_mlp1[0], m_w_mlp1[0], v_w_mlp1[0], "adam_w_mlp1")
    g2, d2, m2, v2 = _adam_slabs_call(s_w2, w_mlp2[0], m_w_mlp2[0], v_w_mlp2[0], "adam_w_mlp2")

    small_w = [b_ada, conf_dw_w, conf_dw_b, conf_ln_g, conf_ln_b, sc_conv_w, g_final]
    small_g = [g_b_ada, g_dw_w, g_dw_b, g_ln_g, g_ln_b, g_sc_w, g_gf]
    small_m = [m_b_ada, m_conf_dw_w, m_conf_dw_b, m_conf_ln_g, m_conf_ln_b, m_sc_conv_w, m_g_final]
    small_v = [v_b_ada, v_conf_dw_w, v_conf_dw_b, v_conf_ln_g, v_conf_ln_b, v_sc_conv_w, v_g_final]
    pd, pm, pv = _adam_packed_call(_pack(small_w), _pack(small_g), _pack(small_m), _pack(small_v))
    sd, sm, sv = _unpack(pd, small_w), _unpack(pm, small_w), _unpack(pv, small_w)
    sg = [g.reshape(w.shape) for g, w in zip(small_g, small_w)]

    def ordered(ada, small_list, w_in_, w_out_, w1_, w2_):
        b_ada_, dw_w_, dw_b_, ln_g_, ln_b_, sc_w_, gf_ = small_list
        return [ada[None], b_ada_, w_in_[None], dw_w_, dw_b_, ln_g_, ln_b_, sc_w_, w_out_[None], w1_[None], w2_[None], gf_]

    grads = ordered(g_ada, sg, gi, go, g1, g2)
    deltas = ordered(d_ada, sd, di, do, d1, d2)
    new_m = ordered(m_ada, sm, mi, mo, m1, m2)
    new_v = ordered(v_ada, sv, vi, vo, v1, v2)
    return (loss, grad_x, *grads, *deltas, *new_m, *new_v)
```

```python
import functools

import jax
import jax.numpy as jnp
from jax import lax
from jax.experimental import pallas as pl
from jax.experimental.pallas import tpu as pltpu

N_DEV = 8
RMS_EPS = 1e-6
ADAM_LR = 0.001
ADAM_B1 = 0.9
ADAM_B2 = 0.999
ADAM_EPS = 1e-08
ADAM_WD = 0.01
ADAM_STEP = 10

F32 = jnp.float32
BF16 = jnp.bfloat16
MESH = pl.DeviceIdType.MESH
VMEM = pltpu.VMEM
ANY = pl.ANY

HALO = 32
SHORT_HALO = 8
ROW_CHUNK = 32
SUBLANES = 8
V7X_VMEM_LIMIT = 56 * 1024 * 1024
MLP_VMEM_LIMIT = 48 * 1024 * 1024
MLP_TOKEN_TILE = 256


def _coords():
    return lax.axis_index("x"), lax.axis_index("y"), lax.axis_index("c")


def _dev_index():
    x, y, c = _coords()
    return 4 * x + 2 * y + c


def _peer(k):
    x, y, c = _coords()
    px = 1 - x if (k >> 2) & 1 else x
    py = 1 - y if (k >> 1) & 1 else y
    pc = 1 - c if k & 1 else c
    return (px, py, pc), 4 * px + 2 * py + pc


def _sigmoid(v):
    return jax.nn.sigmoid(v)


def _nt_dot(a, b):
    return lax.dot_general(a, b, (((1,), (1,)), ((), ())), preferred_element_type=F32)


def _nn_dot(a, b):
    return jnp.dot(a, b, preferred_element_type=F32)


def _tn_dot(a, b):
    return lax.dot_general(a, b, (((0,), (0,)), ((), ())), preferred_element_type=F32)


def _token_tile(seq):
    return 256 if seq % 256 == 0 else 64


class _TwoLevelGather:
    def __init__(self, srcs, dsts, send_sems, recv_sems, loc_sems):
        x, y, c = _coords()
        me = 4 * x + 2 * y + c
        sibling = (x, y, 1 - c)
        chips = [(1 - x, y), (x, 1 - y), (1 - x, 1 - y)]

        def remote(src, dst, a, col, to):
            return pltpu.make_async_remote_copy(
                src_ref=src, dst_ref=dst, send_sem=send_sems.at[a, col], recv_sem=recv_sems.at[a, col],
                device_id=to, device_id_type=MESH)

        self.local, self.to_sibling, self.ici, self.forwards = [], [], [], []
        for a, (src, dst) in enumerate(zip(srcs, dsts)):
            self.local.append(pltpu.make_async_copy(src, dst.at[me], loc_sems.at[a]))
            self.to_sibling.append(remote(src, dst.at[me], a, 0, sibling))
            for j, (cx, cy) in enumerate(chips):
                self.ici.append(remote(src, dst.at[me], a, 1 + j, (cx, cy, c)))
                landed = dst.at[4 * cx + 2 * cy + c]
                self.forwards.append(remote(landed, landed, a, 4 + j, sibling))

    def start(self):
        for cp in self.local + self.to_sibling + self.ici:
            cp.start()

    def forward(self):
        for arrival, onward in zip(self.ici, self.forwards):
            arrival.wait_recv()
            onward.start()

    def finish(self):
        for cp in self.ici:
            cp.wait_send()
        for cp in self.local + self.to_sibling + self.forwards:
            cp.wait()


def _scatter_copies(srcs, dsts, send_sems, recv_sems, loc_sems):
    me = _dev_index()
    copies = []
    for a, (src, dst) in enumerate(zip(srcs, dsts)):
        copies.append(pltpu.make_async_copy(src.at[me], dst.at[me], loc_sems.at[a]))
        for k in range(1, N_DEV):
            peer, pidx = _peer(k)
            copies.append(pltpu.make_async_remote_copy(
                src_ref=src.at[pidx], dst_ref=dst.at[me], send_sem=send_sems.at[a, k - 1],
                recv_sem=recv_sems.at[a, k - 1], device_id=peer, device_id_type=MESH))
    return copies


def _exchange_sems(n_arrays):
    return [pltpu.SemaphoreType.DMA((n_arrays, N_DEV - 1)), pltpu.SemaphoreType.DMA((n_arrays, N_DEV - 1)),
            pltpu.SemaphoreType.DMA((n_arrays,))]


def _gather_call(c_pad, w_ada, b_ada_loc, small_loc, big_shards):
    n_big = len(big_shards)
    d_model = c_pad.shape[1]
    a_cols = w_ada.shape[1]

    def body(c_ref, wada_ref, bada_ref, small_ref, *rest):
        big_in = rest[:n_big]
        big_out = rest[n_big:2 * n_big]
        small_all, c_all, mod_rows = rest[2 * n_big:2 * n_big + 3]
        modcols, big_send, big_recv, loc_sem, s_send, s_recv = rest[2 * n_big + 3:]
        me = _dev_index()
        big = _TwoLevelGather(big_in, big_out, big_send, big_recv, loc_sem)
        big.start()

        small_all[me] = small_ref[...]
        c_all[me] = c_ref[...]
        first = []
        for k in range(1, N_DEV):
            peer, _ = _peer(k)
            for i, (src, dst) in enumerate(((small_ref, small_all), (c_ref, c_all))):
                cp = pltpu.make_async_remote_copy(
                    src_ref=src, dst_ref=dst.at[me],
                    send_sem=s_send.at[i, k - 1], recv_sem=s_recv.at[i, k - 1],
                    device_id=peer, device_id_type=MESH)
                cp.start()
                first.append(cp)
        for cp in first:
            cp.wait()

        c_rows = c_all[...].reshape(N_DEV * SUBLANES, d_model)
        c_act = c_rows * _sigmoid(c_rows)
        modcols[...] = _nn_dot(c_act.astype(BF16), wada_ref[...].astype(BF16)) + bada_ref[...]
        mod_rows[me] = modcols[pl.ds(pl.multiple_of(me * SUBLANES, SUBLANES), SUBLANES), :]
        second = []
        for k in range(1, N_DEV):
            peer, pidx = _peer(k)
            cp = pltpu.make_async_remote_copy(
                src_ref=modcols.at[pl.ds(pl.multiple_of(pidx * SUBLANES, SUBLANES), SUBLANES), :],
                dst_ref=mod_rows.at[me],
                send_sem=s_send.at[2, k - 1], recv_sem=s_recv.at[2, k - 1],
                device_id=peer, device_id_type=MESH)
            cp.start()
            second.append(cp)
        big.forward()
        for cp in second:
            cp.wait()
        big.finish()

    out_shape = tuple(jax.ShapeDtypeStruct((N_DEV,) + s.shape, s.dtype) for s in big_shards) + (
        jax.ShapeDtypeStruct((N_DEV,) + small_loc.shape, F32),
        jax.ShapeDtypeStruct((N_DEV, SUBLANES, d_model), F32),
        jax.ShapeDtypeStruct((N_DEV, SUBLANES, a_cols), F32),
    )
    vm = pl.BlockSpec(memory_space=VMEM)
    hbm = pl.BlockSpec(memory_space=ANY)
    outs = pl.pallas_call(
        body, name="gather_weights_mod", out_shape=out_shape,
        in_specs=[vm, vm, vm, vm] + [hbm] * n_big,
        out_specs=tuple([hbm] * n_big + [vm, vm, vm]),
        scratch_shapes=[
            pltpu.VMEM((N_DEV * SUBLANES, a_cols), F32),
            pltpu.SemaphoreType.DMA((n_big, N_DEV - 1)),
            pltpu.SemaphoreType.DMA((n_big, N_DEV - 1)),
            pltpu.SemaphoreType.DMA((n_big,)),
            pltpu.SemaphoreType.DMA((3, N_DEV - 1)),
            pltpu.SemaphoreType.DMA((3, N_DEV - 1)),
        ],
        compiler_params=pltpu.CompilerParams(vmem_limit_bytes=V7X_VMEM_LIMIT),
    )(c_pad, w_ada, b_ada_loc, small_loc, *big_shards)
    return outs[:n_big], outs[n_big], outs[n_big + 1], outs[n_big + 2]


def _shifted_rows_count(tm):
    return tm + HALO - SUBLANES


def _fill_shifted(ext, shifted, tm):
    for s in range(1, SUBLANES):
        shifted[s - 1] = ext[s:s + _shifted_rows_count(tm), :]


def _shifted_rows(ext, shifted, start, rows):
    phase = start % SUBLANES
    aligned = start - phase
    if phase == 0:
        return ext[aligned:aligned + rows, :]
    return shifted[phase - 1, aligned:aligned + rows, :]


def _layer_norm_parts(a1):
    mu = jnp.mean(a1, axis=-1, keepdims=True)
    xc = a1 - mu
    rstd = lax.rsqrt(jnp.mean(xc * xc, axis=-1, keepdims=True) + RMS_EPS)
    return xc * rstd, rstd


def _mixer_fwd_call(x, mod, win_t, wout, cw, cp, later_shards):
    n_later = len(later_shards)
    bsz, seq, d_model = x.shape
    forward_step = (2 * bsz * (seq // _token_tile(seq))) // 3
    c_half = cw.shape[1]
    n_taps = 31
    d_in = win_t.shape[0] * win_t.shape[1]
    tm = _token_tile(seq)
    nt = seq // tm

    def body(x_ref, mod_ref, win_ref, wout_ref, cw_ref, cp_ref, *rest):
        shard_refs, rest = rest[:n_later], rest[n_later:]
        proj_ref, a1_ref, cv_ref, mixed_ref, y1_ref, x1_ref = rest[:6]
        gathered_refs, rest = rest[6:6 + n_later], rest[6 + n_later:]
        aext, qext, ashift, send_sems, recv_sems, loc_sems = rest
        b, t = pl.program_id(0), pl.program_id(1)

        step = b * nt + t

        @pl.when(step == 0)
        def _():
            _TwoLevelGather(shard_refs, gathered_refs, send_sems, recv_sems, loc_sems).start()

        @pl.when(step == forward_step)
        def _():
            _TwoLevelGather(shard_refs, gathered_refs, send_sems, recv_sems, loc_sems).forward()

        xv = x_ref[...]
        sh1, sc1, g1 = mod_ref[0:1, :], mod_ref[1:2, :], mod_ref[2:3, :]
        r1 = lax.rsqrt(jnp.mean(xv * xv, axis=-1, keepdims=True) + RMS_EPS)
        h1 = (xv * r1) * (1.0 + sc1) + sh1
        proj = _nt_dot(h1.astype(BF16), win_ref[...].reshape(d_in, d_model))
        proj_ref[...] = proj
        val, gate = proj[:, 0:c_half], proj[:, c_half:2 * c_half]
        s_b, s_c, s_h = proj[:, 2 * c_half:3 * c_half], proj[:, 3 * c_half:4 * c_half], proj[:, 4 * c_half:5 * c_half]

        @pl.when(t == 0)
        def _():
            aext[0:HALO, :] = jnp.zeros((HALO, c_half), F32)
            qext[0:SHORT_HALO, :] = jnp.zeros((SHORT_HALO, c_half), F32)

        @pl.when(t > 0)
        def _():
            aext[0:HALO, :] = aext[tm:tm + HALO, :]
            qext[0:SHORT_HALO, :] = qext[tm:tm + SHORT_HALO, :]

        aext[HALO:HALO + tm, :] = val * _sigmoid(gate)
        qext[SHORT_HALO:SHORT_HALO + tm, :] = s_c * s_h

        base = HALO - (n_taps - 1)
        _fill_shifted(aext, ashift, tm)
        for r0 in range(0, tm, ROW_CHUNK):
            acc = jnp.zeros((ROW_CHUNK, c_half), F32)
            for k in range(n_taps):
                acc = acc + cw_ref[k:k + 1, :] * _shifted_rows(aext, ashift, r0 + base + k, ROW_CHUNK)
            a1_ref[r0:r0 + ROW_CHUNK, :] = acc + cp_ref[0:1, :]
        sbase = SHORT_HALO - 2
        conv3 = cp_ref[3:4, :] * qext[sbase:sbase + tm, :]
        conv3 = conv3 + cp_ref[4:5, :] * qext[sbase + 1:sbase + 1 + tm, :]
        conv3 = conv3 + cp_ref[5:6, :] * qext[sbase + 2:sbase + 2 + tm, :]
        cv_ref[...] = conv3

        norm, _ = _layer_norm_parts(a1_ref[...])
        a2 = norm * cp_ref[1:2, :] + cp_ref[2:3, :]
        mixed = jnp.concatenate([a2 * _sigmoid(a2), s_b * conv3], axis=-1).astype(BF16)
        mixed_ref[...] = mixed
        y1 = _nn_dot(mixed, wout_ref[...].reshape(d_model, d_model))
        y1_ref[...] = y1
        x1_ref[...] = xv + g1 * y1

        @pl.when(step == bsz * nt - 1)
        def _():
            _TwoLevelGather(shard_refs, gathered_refs, send_sems, recv_sems, loc_sems).finish()

    hbm = pl.BlockSpec(memory_space=ANY)

    def tok(width):
        return pl.BlockSpec((None, tm, width), lambda b, t: (b, t, 0))

    def const(shape):
        return pl.BlockSpec(shape, lambda b, t: (0,) * len(shape))

    out_shape = (
        jax.ShapeDtypeStruct((bsz, seq, d_in), F32),
        jax.ShapeDtypeStruct((bsz, seq, c_half), F32),
        jax.ShapeDtypeStruct((bsz, seq, c_half), F32),
        jax.ShapeDtypeStruct((bsz, seq, d_model), BF16),
        jax.ShapeDtypeStruct((bsz, seq, d_model), F32),
        jax.ShapeDtypeStruct((bsz, seq, d_model), F32),
    ) + tuple(jax.ShapeDtypeStruct((N_DEV,) + s.shape, s.dtype) for s in later_shards)
    outs = pl.pallas_call(
        body, name="mixer_fwd", out_shape=out_shape, grid=(bsz, nt),
        in_specs=[tok(d_model), pl.BlockSpec((None, 6, d_model), lambda b, t: (b, 0, 0)),
                  const(win_t.shape), const(wout.shape), const(cw.shape), const(cp.shape)] + [hbm] * n_later,
        out_specs=(tok(d_in), tok(c_half), tok(c_half), tok(d_model), tok(d_model), tok(d_model)) + (hbm,) * n_later,
        scratch_shapes=[pltpu.VMEM((tm + HALO, c_half), F32), pltpu.VMEM((tm + SHORT_HALO, c_half), F32),
                        pltpu.VMEM((SUBLANES - 1, _shifted_rows_count(tm), c_half), F32)]
        + _exchange_sems(n_later),
        compiler_params=pltpu.CompilerParams(
            dimension_semantics=("arbitrary", "arbitrary"), vmem_limit_bytes=V7X_VMEM_LIMIT),
    )(x, mod, win_t, wout, cw, cp, *later_shards)
    return outs[:6], outs[6:]


def _mlp_call(x1, target, mod, w1, w2, g_final):
    bsz, seq, d_model = x1.shape
    n_blk, _, f_blk = w1.shape
    d_ff = n_blk * f_blk
    tm = MLP_TOKEN_TILE if seq % MLP_TOKEN_TILE == 0 else _token_tile(seq)
    nt = seq // tm

    def body(x1_ref, tgt_ref, mod_ref, w1_ref, w2_ref, gf_ref,
             dx1_ref, h2_ref, dy2_ref, u_ref, dz_ref, dmod_ref, head_ref, dy1_ref, relu_scr):
        b, t = pl.program_id(0), pl.program_id(1)
        x1v = x1_ref[...]
        sh2, sc2, g2 = mod_ref[3:4, :], mod_ref[4:5, :], mod_ref[5:6, :]
        gf = gf_ref[...]
        r2 = lax.rsqrt(jnp.mean(x1v * x1v, axis=-1, keepdims=True) + RMS_EPS)
        xn2 = x1v * r2
        h2 = (xn2 * (1.0 + sc2) + sh2).astype(BF16)
        h2_ref[...] = h2
        y2 = jnp.zeros((tm, d_model), F32)
        for j in range(n_blk):
            cols = slice(j * f_blk, (j + 1) * f_blk)
            rz = jnp.maximum(_nn_dot(h2, w1_ref[j]), 0.0)
            relu_scr[:, cols] = rz
            ub = (rz * rz).astype(BF16)
            u_ref[:, cols] = ub
            y2 = y2 + _nn_dot(ub, w2_ref[j])
        x2 = x1v + g2 * y2
        r3 = lax.rsqrt(jnp.mean(x2 * x2, axis=-1, keepdims=True) + RMS_EPS)
        xn3 = x2 * r3
        diff = xn3 * gf - tgt_ref[...]
        dout = diff * (1.0 / d_model)

        @pl.when(jnp.logical_and(b == 0, t == 0))
        def _():
            head_ref[...] = jnp.zeros(head_ref.shape, F32)

        @pl.when(t == 0)
        def _():
            dmod_ref[...] = jnp.zeros(dmod_ref.shape, F32)

        head_ref[0:1, :] += jnp.sum(dout * xn3, axis=0, keepdims=True)
        head_ref[1:2, :] += jnp.sum(diff * diff, axis=0, keepdims=True)
        dxn3 = dout * gf
        dx2 = r3 * (dxn3 - xn3 * jnp.mean(dxn3 * xn3, axis=-1, keepdims=True))
        dmod_ref[2:3, :] += jnp.sum(dx2 * y2, axis=0, keepdims=True)
        dy2 = (g2 * dx2).astype(BF16)
        dy2_ref[...] = dy2
        dh2 = jnp.zeros((tm, d_model), F32)
        for j in range(n_blk):
            cols = slice(j * f_blk, (j + 1) * f_blk)
            dz = (_nt_dot(dy2, w2_ref[j]) * (2.0 * relu_scr[:, cols])).astype(BF16)
            dz_ref[:, cols] = dz
            dh2 = dh2 + _nt_dot(dz, w1_ref[j])
        dmod_ref[0:1, :] += jnp.sum(dh2, axis=0, keepdims=True)
        dmod_ref[1:2, :] += jnp.sum(dh2 * xn2, axis=0, keepdims=True)
        dxn2 = dh2 * (1.0 + sc2)
        dx1 = dx2 + r2 * (dxn2 - xn2 * jnp.mean(dxn2 * xn2, axis=-1, keepdims=True))
        dx1_ref[...] = dx1
        dy1_ref[...] = (mod_ref[2:3, :] * dx1).astype(BF16)

    def tok(width):
        return pl.BlockSpec((None, tm, width), lambda b, t: (b, t, 0))

    def const(shape):
        return pl.BlockSpec(shape, lambda b, t: (0,) * len(shape))

    def resident(shape):
        return pl.BlockSpec(shape, lambda b, t: (0,) * len(shape), pipeline_mode=pl.Buffered(1))

    out_shape = (
        jax.ShapeDtypeStruct((bsz, seq, d_model), F32),
        jax.ShapeDtypeStruct((bsz, seq, d_model), BF16),
        jax.ShapeDtypeStruct((bsz, seq, d_model), BF16),
        jax.ShapeDtypeStruct((bsz, seq, d_ff), BF16),
        jax.ShapeDtypeStruct((bsz, seq, d_ff), BF16),
        jax.ShapeDtypeStruct((bsz, SUBLANES, d_model), F32),
        jax.ShapeDtypeStruct((SUBLANES, d_model), F32),
        jax.ShapeDtypeStruct((bsz, seq, d_model), BF16),
    )
    return pl.pallas_call(
        body, name="mlp_fwd_bwd", out_shape=out_shape, grid=(bsz, nt),
        in_specs=[tok(d_model), tok(d_model), pl.BlockSpec((None, 6, d_model), lambda b, t: (b, 0, 0)),
                  resident(w1.shape), resident(w2.shape), const(g_final.shape)],
        out_specs=(tok(d_model), tok(d_model), tok(d_model), tok(d_ff), tok(d_ff),
                   pl.BlockSpec((None, SUBLANES, d_model), lambda b, t: (b, 0, 0)),
                   const((SUBLANES, d_model)), tok(d_model)),
        scratch_shapes=[pltpu.VMEM((tm, d_ff), F32)],
        compiler_params=pltpu.CompilerParams(
            dimension_semantics=("arbitrary", "arbitrary"), vmem_limit_bytes=MLP_VMEM_LIMIT),
    )(x1, target, mod, w1, w2, g_final)


def _mixer_bwd_call(dx1, x, proj, a1, cv, y1, mod, win_t, wout, cw, cp, partials):
    n_part = len(partials)
    bsz, seq, d_model = x.shape
    c_half = cw.shape[1]
    n_taps = 31
    d_in = win_t.shape[0] * win_t.shape[1]
    tm = _token_tile(seq)
    nt = seq // tm

    def body(dx1_ref, x_ref, proj_ref, a1_ref, cv_ref, y1_ref, mod_ref, win_ref, wout_ref, cw_ref, cp_ref, *rest):
        part_refs, rest = rest[:n_part], rest[n_part:]
        gx_ref, dproj_ref, h1_ref, dmod_ref, cgrad_ref = rest[:5]
        slab_refs, rest = rest[5:5 + n_part], rest[5 + n_part:]
        dext, cext, a0_scr, da0_scr, tap_acc, row_acc, dshift, send_sems, recv_sems, loc_sems = rest
        b, step = pl.program_id(0), pl.program_id(1)
        first = jnp.logical_and(b == 0, step == 0)
        last = jnp.logical_and(b == bsz - 1, step == nt - 1)

        @pl.when(first)
        def _():
            for copy in _scatter_copies(part_refs, slab_refs, send_sems, recv_sems, loc_sems):
                copy.start()

        dx1v = dx1_ref[...]
        xv = x_ref[...]
        sh1, sc1, g1 = mod_ref[0:1, :], mod_ref[1:2, :], mod_ref[2:3, :]

        @pl.when(first)
        def _():
            tap_acc[...] = jnp.zeros(tap_acc.shape, F32)
            row_acc[...] = jnp.zeros(row_acc.shape, F32)

        @pl.when(step == 0)
        def _():
            dmod_ref[...] = jnp.zeros(dmod_ref.shape, F32)
            dext[tm:tm + HALO, :] = jnp.zeros((HALO, c_half), F32)
            cext[tm:tm + SHORT_HALO, :] = jnp.zeros((SHORT_HALO, c_half), F32)

        @pl.when(step > 0)
        def _():
            dext[tm:tm + HALO, :] = dext[0:HALO, :]
            cext[tm:tm + SHORT_HALO, :] = cext[0:SHORT_HALO, :]

        dmod_ref[2:3, :] += jnp.sum(dx1v * y1_ref[...], axis=0, keepdims=True)
        dy1 = (g1 * dx1v).astype(BF16)
        dmixed = _nt_dot(dy1, wout_ref[...].reshape(d_model, d_model))
        d_a, d_s = dmixed[:, 0:c_half], dmixed[:, c_half:2 * c_half]

        val, gate = proj_ref[:, 0:c_half], proj_ref[:, c_half:2 * c_half]
        s_b = proj_ref[:, 2 * c_half:3 * c_half]
        s_c, s_h = proj_ref[:, 3 * c_half:4 * c_half], proj_ref[:, 4 * c_half:5 * c_half]

        d_sb = d_s * cv_ref[...]
        cext[0:tm, :] = d_s * s_b
        q = s_c * s_h
        dq = jnp.zeros((tm, c_half), F32)
        for k in range(3):
            shifted = cext[2 - k:2 - k + tm, :]
            dq = dq + cp_ref[3 + k:4 + k, :] * shifted
            row_acc[k:k + 1, :] += jnp.sum(q * shifted, axis=0, keepdims=True)
        d_sc, d_sh = dq * s_h, dq * s_c

        norm, rstd = _layer_norm_parts(a1_ref[...])
        ln_g = cp_ref[1:2, :]
        a2 = norm * ln_g + cp_ref[2:3, :]
        sg = _sigmoid(a2)
        d_a2 = d_a * (sg * (1.0 + a2 * (1.0 - sg)))
        row_acc[4:5, :] += jnp.sum(d_a2 * norm, axis=0, keepdims=True)
        row_acc[5:6, :] += jnp.sum(d_a2, axis=0, keepdims=True)
        d_n = d_a2 * ln_g
        d_a1 = rstd * (d_n - jnp.mean(d_n, axis=-1, keepdims=True)
                       - norm * jnp.mean(d_n * norm, axis=-1, keepdims=True))
        row_acc[3:4, :] += jnp.sum(d_a1, axis=0, keepdims=True)
        dext[0:tm, :] = d_a1
        sig_g = _sigmoid(gate)
        a0_scr[...] = val * sig_g

        _fill_shifted(dext, dshift, tm)
        for r0 in range(0, tm, ROW_CHUNK):
            a0c = a0_scr[r0:r0 + ROW_CHUNK, :]
            acc = jnp.zeros((ROW_CHUNK, c_half), F32)
            for k in range(n_taps):
                shifted = _shifted_rows(dext, dshift, r0 + (n_taps - 1) - k, ROW_CHUNK)
                acc = acc + cw_ref[k:k + 1, :] * shifted
                prod = a0c * shifted
                part = prod[0:SUBLANES, :]
                for g in range(1, ROW_CHUNK // SUBLANES):
                    part = part + prod[g * SUBLANES:(g + 1) * SUBLANES, :]
                tap_acc[k * SUBLANES:(k + 1) * SUBLANES, :] += part
            da0_scr[r0:r0 + ROW_CHUNK, :] = acc
        d_a0 = da0_scr[...]
        d_val = d_a0 * sig_g
        d_gate = d_a0 * val * sig_g * (1.0 - sig_g)

        dproj = jnp.concatenate([d_val, d_gate, d_sb, d_sc, d_sh], axis=-1).astype(BF16)
        dproj_ref[...] = dproj
        dh1 = _nn_dot(dproj, win_ref[...].reshape(d_in, d_model))
        r1 = lax.rsqrt(jnp.mean(xv * xv, axis=-1, keepdims=True) + RMS_EPS)
        xn1 = xv * r1
        h1_ref[...] = (xn1 * (1.0 + sc1) + sh1).astype(BF16)
        dmod_ref[0:1, :] += jnp.sum(dh1, axis=0, keepdims=True)
        dmod_ref[1:2, :] += jnp.sum(dh1 * xn1, axis=0, keepdims=True)
        dxn1 = dh1 * (1.0 + sc1)
        gx_ref[...] = dx1v + r1 * (dxn1 - xn1 * jnp.mean(dxn1 * xn1, axis=-1, keepdims=True))

        @pl.when(last)
        def _():
            taps = jnp.sum(tap_acc[...].reshape(HALO, SUBLANES, c_half), axis=1)
            cgrad_ref[0:HALO, :] = taps
            cgrad_ref[HALO:HALO + SUBLANES, :] = row_acc[...]
            for copy in _scatter_copies(part_refs, slab_refs, send_sems, recv_sems, loc_sems):
                copy.wait()

    hbm = pl.BlockSpec(memory_space=ANY)

    def tok(width):
        return pl.BlockSpec((None, tm, width), lambda b, s: (b, nt - 1 - s, 0))

    def const(shape):
        return pl.BlockSpec(shape, lambda b, s: (0,) * len(shape))

    mod_spec = pl.BlockSpec((None, 6, d_model), lambda b, s: (b, 0, 0))
    out_shape = (
        jax.ShapeDtypeStruct((bsz, seq, d_model), F32),
        jax.ShapeDtypeStruct((bsz, seq, d_in), BF16),
        jax.ShapeDtypeStruct((bsz, seq, d_model), BF16),
        jax.ShapeDtypeStruct((bsz, SUBLANES, d_model), F32),
        jax.ShapeDtypeStruct((HALO + SUBLANES, c_half), F32),
    ) + tuple(jax.ShapeDtypeStruct(p.shape, p.dtype) for p in partials)
    outs = pl.pallas_call(
        body, name="mixer_bwd", out_shape=out_shape, grid=(bsz, nt),
        in_specs=[tok(d_model), tok(d_model), tok(d_in), tok(c_half), tok(c_half), tok(d_model), mod_spec,
                  const(win_t.shape), const(wout.shape), const(cw.shape), const(cp.shape)] + [hbm] * n_part,
        out_specs=(tok(d_model), tok(d_in), tok(d_model),
                   pl.BlockSpec((None, SUBLANES, d_model), lambda b, s: (b, 0, 0)),
                   const((HALO + SUBLANES, c_half))) + (hbm,) * n_part,
        scratch_shapes=[
            pltpu.VMEM((tm + HALO, c_half), F32), pltpu.VMEM((tm + SHORT_HALO, c_half), F32),
            pltpu.VMEM((tm, c_half), F32), pltpu.VMEM((tm, c_half), F32),
            pltpu.VMEM((HALO * SUBLANES, c_half), F32), pltpu.VMEM((SUBLANES, c_half), F32),
            pltpu.VMEM((SUBLANES - 1, _shifted_rows_count(tm), c_half), F32),
        ] + _exchange_sems(n_part),
        compiler_params=pltpu.CompilerParams(
            dimension_semantics=("arbitrary", "arbitrary"), vmem_limit_bytes=V7X_VMEM_LIMIT),
    )(dx1, x, proj, a1, cv, y1, mod, win_t, wout, cw, cp, *partials)
    return outs[:5], outs[5:]


def _largest_divisor(n, cap, multiple):
    best = None
    for cand in range(multiple, min(n, cap) + 1, multiple):
        if n % cand == 0:
            best = cand
    return best if best is not None else n


WGRAD_TOKENS_PER_STEP = 2048
WGRAD_COLS_PER_STEP = 1024


def _wgrad_call(a, b, name, owner_cols=None):
    tokens, m_dim = a.shape
    n_dim = b.shape[1]
    bk = _largest_divisor(tokens, WGRAD_TOKENS_PER_STEP, 128)
    n_k = tokens // bk
    if owner_cols is None:
        bm = _largest_divisor(m_dim, 1024, m_dim // N_DEV)
        bn = n_dim
        owners = 1
        out_shape = jax.ShapeDtypeStruct((m_dim, n_dim), BF16)
        out_spec = pl.BlockSpec((bm, bn), lambda i, j, k: (i, j))
    else:
        bm = m_dim
        bn = _largest_divisor(n_dim, WGRAD_COLS_PER_STEP, owner_cols)
        owners = bn // owner_cols
        out_shape = jax.ShapeDtypeStruct((n_dim // owner_cols, m_dim, owner_cols), BF16)
        out_spec = pl.BlockSpec((owners, bm, owner_cols), lambda i, j, k: (j, i, 0))

    def body(a_ref, b_ref, o_ref, acc):
        k = pl.program_id(2)

        @pl.when(k == 0)
        def _():
            acc[...] = jnp.zeros(acc.shape, F32)

        acc[...] += _tn_dot(a_ref[...], b_ref[...])

        @pl.when(k == n_k - 1)
        def _():
            if owner_cols is None:
                o_ref[...] = acc[...].astype(BF16)
            else:
                for q in range(owners):
                    o_ref[q] = acc[:, q * owner_cols:(q + 1) * owner_cols].astype(BF16)

    out = pl.pallas_call(
        body, name=name, out_shape=out_shape, grid=(m_dim // bm, n_dim // bn, n_k),
        in_specs=[pl.BlockSpec((bk, bm), lambda i, j, k: (k, i)), pl.BlockSpec((bk, bn), lambda i, j, k: (k, j))],
        out_specs=out_spec, scratch_shapes=[pltpu.VMEM((bm, bn), F32)],
        compiler_params=pltpu.CompilerParams(
            dimension_semantics=("arbitrary", "arbitrary", "arbitrary"), vmem_limit_bytes=V7X_VMEM_LIMIT),
    )(a, b)
    if owner_cols is None:
        out = out.reshape(N_DEV, m_dim // N_DEV, n_dim)
    return out


def _tail_scatter_call(partial, small):
    _, rows, cols = partial.shape
    n_chips = N_DEV // 2

    def body(g_ref, small_ref, out_ref, small_all, from_sibling, pair,
             p1_send, p1_recv, p2_send, p2_recv, s_send, s_recv, s_loc):
        x, y, c = _coords()
        sibling = (x, y, 1 - c)
        my_chip = 2 * x + y
        chips = [(1 - x, y), (x, 1 - y), (1 - x, 1 - y)]

        def remote(src, dst, send_sem, recv_sem, to):
            return pltpu.make_async_remote_copy(src_ref=src, dst_ref=dst, send_sem=send_sem, recv_sem=recv_sem,
                                                device_id=to, device_id_type=MESH)

        gather = _TwoLevelGather([small_ref], [small_all], s_send, s_recv, s_loc)
        gather.start()
        first = [remote(g_ref.at[2 * q + (1 - c)], from_sibling.at[q], p1_send.at[q], p1_recv.at[q], sibling)
                 for q in range(n_chips)]
        for cp in first:
            cp.start()
        for cp in first:
            cp.wait()
        for q in range(n_chips):
            pair[q] = (g_ref[2 * q + c].astype(F32) + from_sibling[q].astype(F32)).astype(BF16)
        second = [remote(pair.at[2 * cx + cy], out_ref.at[my_chip], p2_send.at[j], p2_recv.at[j], (cx, cy, c))
                  for j, (cx, cy) in enumerate(chips)]
        for cp in second:
            cp.start()
        out_ref[my_chip] = pair[my_chip]
        gather.forward()
        for cp in second:
            cp.wait()
        gather.finish()

    vm = pl.BlockSpec(memory_space=VMEM)
    slabs = pltpu.VMEM((n_chips, rows, cols), BF16)
    return pl.pallas_call(
        body, name="scatter_tail",
        out_shape=(jax.ShapeDtypeStruct((n_chips, rows, cols), BF16), jax.ShapeDtypeStruct((N_DEV,) + small.shape, F32)),
        in_specs=[vm, vm], out_specs=(vm, vm),
        scratch_shapes=[slabs, slabs,
                        pltpu.SemaphoreType.DMA((n_chips,)), pltpu.SemaphoreType.DMA((n_chips,)),
                        pltpu.SemaphoreType.DMA((n_chips - 1,)), pltpu.SemaphoreType.DMA((n_chips - 1,))]
        + _exchange_sems(1),
        compiler_params=pltpu.CompilerParams(vmem_limit_bytes=V7X_VMEM_LIMIT),
    )(partial, small)


def _adamw(w, g, m, v):
    m2 = ADAM_B1 * m + (1.0 - ADAM_B1) * g
    v2 = ADAM_B2 * v + (1.0 - ADAM_B2) * (g * g)
    m_hat = m2 / (1.0 - ADAM_B1 ** ADAM_STEP)
    v_hat = v2 / (1.0 - ADAM_B2 ** ADAM_STEP)
    delta = -ADAM_LR * (m_hat / (jnp.sqrt(v_hat) + ADAM_EPS) + ADAM_WD * w)
    return delta, m2, v2


def _adam_slabs_call(slabs, w, m, v, name):
    rows, cols = w.shape
    n_slabs = slabs.shape[0]
    tr = _largest_divisor(rows, 256, 2 * SUBLANES)

    def body(s_ref, w_ref, m_ref, v_ref, g_ref, d_ref, m2_ref, v2_ref):
        g = s_ref[0].astype(F32)
        for k in range(1, n_slabs):
            g = g + s_ref[k].astype(F32)
        delta, m2, v2 = _adamw(w_ref[...], g, m_ref[...], v_ref[...])
        g_ref[...] = g
        d_ref[...] = delta
        m2_ref[...] = m2
        v2_ref[...] = v2

    tile = pl.BlockSpec((tr, cols), lambda i: (i, 0))
    shp = jax.ShapeDtypeStruct((rows, cols), F32)
    return pl.pallas_call(
        body, name=name, out_shape=(shp, shp, shp, shp), grid=(rows // tr,),
        in_specs=[pl.BlockSpec((n_slabs, tr, cols), lambda i: (0, i, 0)), tile, tile, tile],
        out_specs=(tile, tile, tile, tile),
        compiler_params=pltpu.CompilerParams(dimension_semantics=("arbitrary",), vmem_limit_bytes=V7X_VMEM_LIMIT),
    )(slabs, w, m, v)


def _adam_ada_call(c_rows, dmod_cols, w, m, v):
    rows, cols = w.shape
    n_rows = c_rows.shape[0]
    tr = _largest_divisor(rows, 256, 128)

    def body(c_ref, dm_ref, w_ref, m_ref, v_ref, g_ref, d_ref, m2_ref, v2_ref):
        cv = c_ref[...]
        c_act = (cv * _sigmoid(cv)).astype(BF16)
        g = _tn_dot(c_act, dm_ref[...].astype(BF16))
        delta, m2, v2 = _adamw(w_ref[...], g, m_ref[...], v_ref[...])
        g_ref[...] = g
        d_ref[...] = delta
        m2_ref[...] = m2
        v2_ref[...] = v2

    tile = pl.BlockSpec((tr, cols), lambda i: (i, 0))
    shp = jax.ShapeDtypeStruct((rows, cols), F32)
    return pl.pallas_call(
        body, name="adam_w_ada", out_shape=(shp, shp, shp, shp), grid=(rows // tr,),
        in_specs=[pl.BlockSpec((n_rows, tr), lambda i: (0, i)), pl.BlockSpec((n_rows, cols), lambda i: (0, 0)),
                  tile, tile, tile],
        out_specs=(tile, tile, tile, tile),
        compiler_params=pltpu.CompilerParams(dimension_semantics=("arbitrary",), vmem_limit_bytes=V7X_VMEM_LIMIT),
    )(c_rows, dmod_cols, w, m, v)


def _small_sum_call(small_all, n_grad_rows, loss_rows, bias_rows, loss_scale):
    _, rows, width = small_all.shape
    lo, hi = loss_rows
    b0, b1, b2 = bias_rows
    nb = b1 - b0

    def body(s_ref, sum_ref, extra_ref):
        tot = s_ref[0]
        for k in range(1, N_DEV):
            tot = tot + s_ref[k]
        sum_ref[...] = tot[0:n_grad_rows, :]
        extra_ref[0:nb, :] = tot[b0:b1, :] + tot[b1:b2, :]
        head = tot[n_grad_rows - 2 * SUBLANES:n_grad_rows, :]
        rows_id = lax.broadcasted_iota(jnp.int32, head.shape, 0) + (n_grad_rows - 2 * SUBLANES)
        sq = jnp.where(jnp.logical_and(rows_id >= lo, rows_id < hi), head, 0.0)
        extra_ref[nb:nb + SUBLANES, :] = jnp.zeros((SUBLANES, width), F32) + jnp.sum(sq) * loss_scale

    vm = pl.BlockSpec(memory_space=VMEM)
    return pl.pallas_call(
        body, name="small_sum",
        out_shape=(jax.ShapeDtypeStruct((n_grad_rows, width), F32), jax.ShapeDtypeStruct((nb + SUBLANES, width), F32)),
        in_specs=[vm], out_specs=(vm, vm),
    )(small_all)


def _adam_small_call(ws, gs, ms, vs):
    n = len(ws)

    def body(*refs):
        w_refs, g_refs, m_refs, v_refs = refs[:n], refs[n:2 * n], refs[2 * n:3 * n], refs[3 * n:4 * n]
        d_refs, m2_refs, v2_refs = refs[4 * n:5 * n], refs[5 * n:6 * n], refs[6 * n:7 * n]
        for i in range(n):
            delta, m2, v2 = _adamw(w_refs[i][...], g_refs[i][...], m_refs[i][...], v_refs[i][...])
            d_refs[i][...] = delta
            m2_refs[i][...] = m2
            v2_refs[i][...] = v2

    vm = pl.BlockSpec(memory_space=VMEM)
    shapes = tuple(jax.ShapeDtypeStruct(w.shape, F32) for w in ws)
    outs = pl.pallas_call(body, name="adam_small", out_shape=shapes * 3,
                          in_specs=[vm] * (4 * n), out_specs=(vm,) * (3 * n))(*ws, *gs, *ms, *vs)
    return outs[:n], outs[n:2 * n], outs[2 * n:]


def kernel(x, c, w_ada, b_ada, w_in, conf_dw_w, conf_dw_b, conf_ln_g, conf_ln_b, sc_conv_w, w_out, w_mlp1, w_mlp2, g_final, loss_target, m_w_ada, m_b_ada, m_w_in, m_conf_dw_w, m_conf_dw_b, m_conf_ln_g, m_conf_ln_b, m_sc_conv_w, m_w_out, m_w_mlp1, m_w_mlp2, m_g_final, v_w_ada, v_b_ada, v_w_in, v_conf_dw_w, v_conf_dw_b, v_conf_ln_g, v_conf_ln_b, v_sc_conv_w, v_w_out, v_w_mlp1, v_w_mlp2, v_g_final):
    bsz, seq, d_model = x.shape
    c_half = conf_dw_b.shape[-1]
    n_taps = conf_dw_w.shape[1]
    cc = conf_dw_w.shape[-1]
    a_cols = w_ada.shape[-1]
    tokens = bsz * seq
    me = _dev_index()

    c_pad = jnp.pad(c, ((0, SUBLANES - bsz), (0, 0)))
    b_ada_loc = lax.dynamic_slice(b_ada, (0, me * a_cols), (1, a_cols))
    small_loc = jnp.zeros((HALO, 128), F32)
    small_loc = small_loc.at[:n_taps, :cc].set(conf_dw_w[0]).at[:3, cc:2 * cc].set(sc_conv_w[0])
    (win_t, wout_all), small_all, c_all, mod_rows = _gather_call(
        c_pad, w_ada[0], b_ada_loc, small_loc, [w_in[0].T.astype(BF16), w_out[0].astype(BF16)])
    cw = small_all[:, :, :cc].transpose(1, 0, 2).reshape(HALO, c_half)
    scw = small_all[:, :3, cc:2 * cc].transpose(1, 0, 2).reshape(3, c_half)
    cp = jnp.concatenate([conf_dw_b, conf_ln_g, conf_ln_b, scw, jnp.zeros((2, c_half), F32)], axis=0)
    mod = mod_rows[:, :bsz, :].transpose(1, 0, 2).reshape(bsz, 6, d_model)

    flat = lambda t: t.reshape(tokens, t.shape[-1])
    (proj, a1, cv, mixed, y1, x1), (w1_all, w2_all) = _mixer_fwd_call(
        x, mod, win_t, wout_all, cw, cp, [w_mlp1[0].astype(BF16), w_mlp2[0].astype(BF16)])
    dx1, h2, dy2, u, dz, dmod2, head, dy1 = _mlp_call(
        x1, loss_target, mod, w1_all, w2_all, g_final.reshape(1, d_model))
    g_w1 = _wgrad_call(flat(h2), flat(dz), "wgrad_mlp1", owner_cols=w_mlp1.shape[-1])
    g_w2 = _wgrad_call(flat(u), flat(dy2), "wgrad_mlp2")
    g_out = _wgrad_call(flat(mixed), flat(dy1), "wgrad_out")
    (grad_x, dproj, h1, dmod1, cgrad), (s_w1, s_w2, s_out) = _mixer_bwd_call(
        dx1, x, proj, a1, cv, y1, mod, win_t, wout_all, cw, cp, [g_w1, g_w2, g_out])
    g_in_t = _wgrad_call(flat(dproj), flat(h1), "wgrad_in")

    dmod = jnp.concatenate([dmod1[:, :3, :], dmod2[:, :3, :]], axis=1)
    n_cg = cgrad.shape[0]
    per_b = 6 * d_model // c_half
    per_b_pad = -(-per_b // SUBLANES) * SUBLANES
    dmod_rows = jnp.pad(dmod.reshape(bsz, per_b, c_half), ((0, 0), (0, per_b_pad - per_b), (0, 0)))
    small = jnp.concatenate([
        cgrad,
        head.reshape(2 * SUBLANES, c_half),
        dmod_rows.reshape(bsz * per_b_pad, c_half),
    ], axis=0)
    s_in, gathered = _tail_scatter_call(g_in_t, small)

    n_head = n_cg + 2 * SUBLANES
    sums, extra = _small_sum_call(
        gathered, n_head, (n_cg + 2, n_cg + 4), (n_head, n_head + per_b_pad, n_head + 2 * per_b_pad), 0.5 / d_model)
    loss = extra[per_b_pad, 0]
    g_b_ada = extra[:per_b].reshape(1, 6 * d_model)
    g_dw_w = lax.dynamic_slice(sums[:n_taps], (0, me * cc), (n_taps, cc))
    g_sc_w = lax.dynamic_slice(sums[HALO:HALO + 3], (0, me * cc), (3, cc))
    g_dw_b, g_ln_g, g_ln_b = sums[HALO + 3:HALO + 4], sums[HALO + 4:HALO + 5], sums[HALO + 5:HALO + 6]
    g_gf = sums[n_cg:n_cg + 2].reshape(1, d_model)

    dmod_all = gathered[:, n_head:, :].reshape(N_DEV, bsz, per_b_pad, c_half)[:, :, :per_b, :]
    dmod_all = dmod_all.reshape(N_DEV, bsz, 6 * d_model)
    dmod_cols = lax.dynamic_slice(dmod_all, (0, 0, me * a_cols), (N_DEV, bsz, a_cols))
    dmod_cols = jnp.pad(dmod_cols, ((0, 0), (0, SUBLANES - bsz), (0, 0))).reshape(N_DEV * SUBLANES, a_cols)
    c_rows = c_all.reshape(N_DEV * SUBLANES, d_model)
    g_ada, d_ada, m_ada, v_ada = _adam_ada_call(c_rows, dmod_cols, w_ada[0], m_w_ada[0], v_w_ada[0])

    gi, di, mi, vi = _adam_slabs_call(s_in, w_in[0].T, m_w_in[0].T, v_w_in[0].T, "adam_w_in")
    gi, di, mi, vi = gi.T, di.T, mi.T, vi.T
    go, do, mo, vo = _adam_slabs_call(s_out, w_out[0], m_w_out[0], v_w_out[0], "adam_w_out")
    g1, d1, m1, v1 = _adam_slabs_call(s_w1, w_mlp1[0], m_w_mlp1[0], v_w_mlp1[0], "adam_w_mlp1")
    g2, d2, m2, v2 = _adam_slabs_call(s_w2, w_mlp2[0], m_w_mlp2[0], v_w_mlp2[0], "adam_w_mlp2")

    small_like = [b_ada, conf_dw_w, conf_dw_b, conf_ln_g, conf_ln_b, sc_conv_w, g_final]
    two_d = lambda t: t.reshape(-1, t.shape[-1])
    small_g = [g_b_ada, g_dw_w, g_dw_b, g_ln_g, g_ln_b, g_sc_w, g_gf]
    sd, sm, sv = _adam_small_call(
        [two_d(t) for t in small_like], small_g,
        [two_d(t) for t in (m_b_ada, m_conf_dw_w, m_conf_dw_b, m_conf_ln_g, m_conf_ln_b, m_sc_conv_w, m_g_final)],
        [two_d(t) for t in (v_b_ada, v_conf_dw_w, v_conf_dw_b, v_conf_ln_g, v_conf_ln_b, v_sc_conv_w, v_g_final)])
    like = lambda parts: [p.reshape(w.shape) for p, w in zip(parts, small_like)]
    sg, sd, sm, sv = like(small_g), like(sd), like(sm), like(sv)

    def ordered(ada, small_list, w_in_, w_out_, w1_, w2_):
        b_ada_, dw_w_, dw_b_, ln_g_, ln_b_, sc_w_, gf_ = small_list
        return [ada[None], b_ada_, w_in_[None], dw_w_, dw_b_, ln_g_, ln_b_, sc_w_, w_out_[None], w1_[None], w2_[None], gf_]

    grads = ordered(g_ada, sg, gi, go, g1, g2)
    deltas = ordered(d_ada, sd, di, do, d1, d2)
    new_m = ordered(m_ada, sm, mi, mo, m1, m2)
    new_v = ordered(v_ada, sv, vi, vo, v1, v2)
    return (loss, grad_x, *grads, *deltas, *new_m, *new_v)
```

```python
import functools

import jax
import jax.numpy as jnp
from jax import lax
from jax.experimental import pallas as pl
from jax.experimental.pallas import tpu as pltpu

N_DEV = 8
RMS_EPS = 1e-6
ADAM_LR = 0.001
ADAM_B1 = 0.9
ADAM_B2 = 0.999
ADAM_EPS = 1e-08
ADAM_WD = 0.01
ADAM_STEP = 10

F32 = jnp.float32
BF16 = jnp.bfloat16
MESH = pl.DeviceIdType.MESH
VMEM = pltpu.VMEM
ANY = pl.ANY

HALO = 32
SHORT_HALO = 8
ROW_CHUNK = 32
SUBLANES = 8
V7X_VMEM_LIMIT = 56 * 1024 * 1024
MLP_VMEM_LIMIT = 48 * 1024 * 1024
MLP_TOKEN_TILE = 256


def _coords():
    return lax.axis_index("x"), lax.axis_index("y"), lax.axis_index("c")


def _dev_index():
    x, y, c = _coords()
    return 4 * x + 2 * y + c


def _peer(k):
    x, y, c = _coords()
    px = 1 - x if (k >> 2) & 1 else x
    py = 1 - y if (k >> 1) & 1 else y
    pc = 1 - c if k & 1 else c
    return (px, py, pc), 4 * px + 2 * py + pc


def _sigmoid(v):
    return jax.nn.sigmoid(v)


def _nt_dot(a, b):
    return lax.dot_general(a, b, (((1,), (1,)), ((), ())), preferred_element_type=F32)


def _nn_dot(a, b):
    return jnp.dot(a, b, preferred_element_type=F32)


def _tn_dot(a, b):
    return lax.dot_general(a, b, (((0,), (0,)), ((), ())), preferred_element_type=F32)


def _token_tile(seq):
    return 256 if seq % 256 == 0 else 64


class _TwoLevelGather:
    def __init__(self, srcs, dsts, send_sems, recv_sems, loc_sems):
        x, y, c = _coords()
        me = 4 * x + 2 * y + c
        sibling = (x, y, 1 - c)
        chips = [(1 - x, y), (x, 1 - y), (1 - x, 1 - y)]

        def remote(src, dst, a, col, to):
            return pltpu.make_async_remote_copy(
                src_ref=src, dst_ref=dst, send_sem=send_sems.at[a, col], recv_sem=recv_sems.at[a, col],
                device_id=to, device_id_type=MESH)

        self.local, self.to_sibling, self.ici, self.forwards = [], [], [], []
        for a, (src, dst) in enumerate(zip(srcs, dsts)):
            self.local.append(pltpu.make_async_copy(src, dst.at[me], loc_sems.at[a]))
            self.to_sibling.append(remote(src, dst.at[me], a, 0, sibling))
            for j, (cx, cy) in enumerate(chips):
                self.ici.append(remote(src, dst.at[me], a, 1 + j, (cx, cy, c)))
                landed = dst.at[4 * cx + 2 * cy + c]
                self.forwards.append(remote(landed, landed, a, 4 + j, sibling))

    def start(self):
        for cp in self.local + self.to_sibling + self.ici:
            cp.start()

    def forward(self):
        for arrival, onward in zip(self.ici, self.forwards):
            arrival.wait_recv()
            onward.start()

    def finish(self):
        for cp in self.ici:
            cp.wait_send()
        for cp in self.local + self.to_sibling + self.forwards:
            cp.wait()


def _scatter_copies(srcs, dsts, send_sems, recv_sems, loc_sems):
    me = _dev_index()
    copies = []
    for a, (src, dst) in enumerate(zip(srcs, dsts)):
        copies.append(pltpu.make_async_copy(src.at[me], dst.at[me], loc_sems.at[a]))
        for k in range(1, N_DEV):
            peer, pidx = _peer(k)
            copies.append(pltpu.make_async_remote_copy(
                src_ref=src.at[pidx], dst_ref=dst.at[me], send_sem=send_sems.at[a, k - 1],
                recv_sem=recv_sems.at[a, k - 1], device_id=peer, device_id_type=MESH))
    return copies


def _exchange_sems(n_arrays):
    return [pltpu.SemaphoreType.DMA((n_arrays, N_DEV - 1)), pltpu.SemaphoreType.DMA((n_arrays, N_DEV - 1)),
            pltpu.SemaphoreType.DMA((n_arrays,))]


def _gather_call(c_pad, w_ada, b_ada_loc, small_loc, big_shards):
    n_big = len(big_shards)
    d_model = c_pad.shape[1]
    a_cols = w_ada.shape[1]

    def body(c_ref, wada_ref, bada_ref, small_ref, *rest):
        big_in = rest[:n_big]
        big_out = rest[n_big:2 * n_big]
        small_all, c_all, mod_rows = rest[2 * n_big:2 * n_big + 3]
        modcols, big_send, big_recv, loc_sem, s_send, s_recv = rest[2 * n_big + 3:]
        me = _dev_index()
        big = _TwoLevelGather(big_in, big_out, big_send, big_recv, loc_sem)
        big.start()

        small_all[me] = small_ref[...]
        c_all[me] = c_ref[...]
        first = []
        for k in range(1, N_DEV):
            peer, _ = _peer(k)
            for i, (src, dst) in enumerate(((small_ref, small_all), (c_ref, c_all))):
                cp = pltpu.make_async_remote_copy(
                    src_ref=src, dst_ref=dst.at[me],
                    send_sem=s_send.at[i, k - 1], recv_sem=s_recv.at[i, k - 1],
                    device_id=peer, device_id_type=MESH)
                cp.start()
                first.append(cp)
        for cp in first:
            cp.wait()

        c_rows = c_all[...].reshape(N_DEV * SUBLANES, d_model)
        c_act = c_rows * _sigmoid(c_rows)
        modcols[...] = _nn_dot(c_act.astype(BF16), wada_ref[...].astype(BF16)) + bada_ref[...]
        mod_rows[me] = modcols[pl.ds(pl.multiple_of(me * SUBLANES, SUBLANES), SUBLANES), :]
        second = []
        for k in range(1, N_DEV):
            peer, pidx = _peer(k)
            cp = pltpu.make_async_remote_copy(
                src_ref=modcols.at[pl.ds(pl.multiple_of(pidx * SUBLANES, SUBLANES), SUBLANES), :],
                dst_ref=mod_rows.at[me],
                send_sem=s_send.at[2, k - 1], recv_sem=s_recv.at[2, k - 1],
                device_id=peer, device_id_type=MESH)
            cp.start()
            second.append(cp)
        big.forward()
        for cp in second:
            cp.wait()
        big.finish()

    out_shape = tuple(jax.ShapeDtypeStruct((N_DEV,) + s.shape, s.dtype) for s in big_shards) + (
        jax.ShapeDtypeStruct((N_DEV,) + small_loc.shape, F32),
        jax.ShapeDtypeStruct((N_DEV, SUBLANES, d_model), F32),
        jax.ShapeDtypeStruct((N_DEV, SUBLANES, a_cols), F32),
    )
    vm = pl.BlockSpec(memory_space=VMEM)
    hbm = pl.BlockSpec(memory_space=ANY)
    outs = pl.pallas_call(
        body, name="gather_weights_mod", out_shape=out_shape,
        in_specs=[vm, vm, vm, vm] + [hbm] * n_big,
        out_specs=tuple([hbm] * n_big + [vm, vm, vm]),
        scratch_shapes=[
            pltpu.VMEM((N_DEV * SUBLANES, a_cols), F32),
            pltpu.SemaphoreType.DMA((n_big, N_DEV - 1)),
            pltpu.SemaphoreType.DMA((n_big, N_DEV - 1)),
            pltpu.SemaphoreType.DMA((n_big,)),
            pltpu.SemaphoreType.DMA((3, N_DEV - 1)),
            pltpu.SemaphoreType.DMA((3, N_DEV - 1)),
        ],
        compiler_params=pltpu.CompilerParams(vmem_limit_bytes=V7X_VMEM_LIMIT),
    )(c_pad, w_ada, b_ada_loc, small_loc, *big_shards)
    return outs[:n_big], outs[n_big], outs[n_big + 1], outs[n_big + 2]


def _shifted_rows_count(tm):
    return tm + HALO - SUBLANES


def _fill_shifted(ext, shifted, tm):
    for s in range(1, SUBLANES):
        shifted[s - 1] = ext[s:s + _shifted_rows_count(tm), :]


def _shifted_rows(ext, shifted, start, rows):
    phase = start % SUBLANES
    aligned = start - phase
    if phase == 0:
        return ext[aligned:aligned + rows, :]
    return shifted[phase - 1, aligned:aligned + rows, :]


def _layer_norm_parts(a1):
    mu = jnp.mean(a1, axis=-1, keepdims=True)
    xc = a1 - mu
    rstd = lax.rsqrt(jnp.mean(xc * xc, axis=-1, keepdims=True) + RMS_EPS)
    return xc * rstd, rstd


def _mixer_fwd_call(x, mod, win_t, wout, cw, cp, later_shards):
    n_later = len(later_shards)
    bsz, seq, d_model = x.shape
    forward_step = (2 * bsz * (seq // _token_tile(seq))) // 3
    c_half = cw.shape[1]
    n_taps = 31
    d_in = win_t.shape[0] * win_t.shape[1]
    tm = _token_tile(seq)
    nt = seq // tm

    def body(x_ref, mod_ref, win_ref, wout_ref, cw_ref, cp_ref, *rest):
        shard_refs, rest = rest[:n_later], rest[n_later:]
        proj_ref, a1_ref, cv_ref, mixed_ref, y1_ref, x1_ref = rest[:6]
        gathered_refs, rest = rest[6:6 + n_later], rest[6 + n_later:]
        aext, qext, ashift, send_sems, recv_sems, loc_sems = rest
        b, t = pl.program_id(0), pl.program_id(1)

        step = b * nt + t

        @pl.when(step == 0)
        def _():
            _TwoLevelGather(shard_refs, gathered_refs, send_sems, recv_sems, loc_sems).start()

        @pl.when(step == forward_step)
        def _():
            _TwoLevelGather(shard_refs, gathered_refs, send_sems, recv_sems, loc_sems).forward()

        xv = x_ref[...]
        sh1, sc1, g1 = mod_ref[0:1, :], mod_ref[1:2, :], mod_ref[2:3, :]
        r1 = lax.rsqrt(jnp.mean(xv * xv, axis=-1, keepdims=True) + RMS_EPS)
        h1 = (xv * r1) * (1.0 + sc1) + sh1
        proj = _nt_dot(h1.astype(BF16), win_ref[...].reshape(d_in, d_model))
        proj_ref[...] = proj
        val, gate = proj[:, 0:c_half], proj[:, c_half:2 * c_half]
        s_b, s_c, s_h = proj[:, 2 * c_half:3 * c_half], proj[:, 3 * c_half:4 * c_half], proj[:, 4 * c_half:5 * c_half]

        @pl.when(t == 0)
        def _():
            aext[0:HALO, :] = jnp.zeros((HALO, c_half), F32)
            qext[0:SHORT_HALO, :] = jnp.zeros((SHORT_HALO, c_half), F32)

        @pl.when(t > 0)
        def _():
            aext[0:HALO, :] = aext[tm:tm + HALO, :]
            qext[0:SHORT_HALO, :] = qext[tm:tm + SHORT_HALO, :]
        aext[HALO:HALO + tm, :] = val * _sigmoid(gate)
        qext[SHORT_HALO:SHORT_HALO + tm, :] = s_c * s_h

        base = HALO - (n_taps - 1)
        _fill_shifted(aext, ashift, tm)
        for r0 in range(0, tm, ROW_CHUNK):
            acc = jnp.zeros((ROW_CHUNK, c_half), F32)
            for k in range(n_taps):
                acc = acc + cw_ref[k:k + 1, :] * _shifted_rows(aext, ashift, r0 + base + k, ROW_CHUNK)
            a1_ref[r0:r0 + ROW_CHUNK, :] = acc + cp_ref[0:1, :]
        sbase = SHORT_HALO - 2
        conv3 = cp_ref[3:4, :] * qext[sbase:sbase + tm, :]
        conv3 = conv3 + cp_ref[4:5, :] * qext[sbase + 1:sbase + 1 + tm, :]
        conv3 = conv3 + cp_ref[5:6, :] * qext[sbase + 2:sbase + 2 + tm, :]
        cv_ref[...] = conv3

        norm, _ = _layer_norm_parts(a1_ref[...])
        a2 = norm * cp_ref[1:2, :] + cp_ref[2:3, :]
        mixed = jnp.concatenate([a2 * _sigmoid(a2), s_b * conv3], axis=-1).astype(BF16)
        mixed_ref[...] = mixed
        y1 = _nn_dot(mixed, wout_ref[...].reshape(d_model, d_model))
        y1_ref[...] = y1
        x1_ref[...] = xv + g1 * y1

        @pl.when(step == bsz * nt - 1)
        def _():
            _TwoLevelGather(shard_refs, gathered_refs, send_sems, recv_sems, loc_sems).finish()

    hbm = pl.BlockSpec(memory_space=ANY)

    def tok(width):
        return pl.BlockSpec((None, tm, width), lambda b, t: (b, t, 0))

    def const(shape):
        return pl.BlockSpec(shape, lambda b, t: (0,) * len(shape))

    out_shape = (
        jax.ShapeDtypeStruct((bsz, seq, d_in), F32),
        jax.ShapeDtypeStruct((bsz, seq, c_half), F32),
        jax.ShapeDtypeStruct((bsz, seq, c_half), F32),
        jax.ShapeDtypeStruct((bsz, seq, d_model), BF16),
        jax.ShapeDtypeStruct((bsz, seq, d_model), F32),
        jax.ShapeDtypeStruct((bsz, seq, d_model), F32),
    ) + tuple(jax.ShapeDtypeStruct((N_DEV,) + s.shape, s.dtype) for s in later_shards)
    outs = pl.pallas_call(
        body, name="mixer_fwd", out_shape=out_shape, grid=(bsz, nt),
        in_specs=[tok(d_model), pl.BlockSpec((None, 6, d_model), lambda b, t: (b, 0, 0)),
                  const(win_t.shape), const(wout.shape), const(cw.shape), const(cp.shape)] + [hbm] * n_later,
        out_specs=(tok(d_in), tok(c_half), tok(c_half), tok(d_model), tok(d_model), tok(d_model)) + (hbm,) * n_later,
        scratch_shapes=[pltpu.VMEM((tm + HALO, c_half), F32), pltpu.VMEM((tm + SHORT_HALO, c_half), F32),
                        pltpu.VMEM((SUBLANES - 1, _shifted_rows_count(tm), c_half), F32)]
        + _exchange_sems(n_later),
        compiler_params=pltpu.CompilerParams(
            dimension_semantics=("arbitrary", "arbitrary"), vmem_limit_bytes=V7X_VMEM_LIMIT),
    )(x, mod, win_t, wout, cw, cp, *later_shards)
    return outs[:6], outs[6:]


def _mlp_call(x1, target, mod, w1, w2, g_final):
    bsz, seq, d_model = x1.shape
    n_blk, _, f_blk = w1.shape
    d_ff = n_blk * f_blk
    tm = MLP_TOKEN_TILE if seq % MLP_TOKEN_TILE == 0 else _token_tile(seq)
    nt = seq // tm

    def body(x1_ref, tgt_ref, mod_ref, w1_ref, w2_ref, gf_ref,
             dx1_ref, h2_ref, dy2_ref, u_ref, dz_ref, dmod_ref, head_ref, dy1_ref, relu_scr):
        b, t = pl.program_id(0), pl.program_id(1)
        x1v = x1_ref[...]
        sh2, sc2, g2 = mod_ref[3:4, :], mod_ref[4:5, :], mod_ref[5:6, :]
        gf = gf_ref[...]
        r2 = lax.rsqrt(jnp.mean(x1v * x1v, axis=-1, keepdims=True) + RMS_EPS)
        xn2 = x1v * r2
        h2 = (xn2 * (1.0 + sc2) + sh2).astype(BF16)
        h2_ref[...] = h2
        y2 = jnp.zeros((tm, d_model), F32)
        for j in range(n_blk):
            cols = slice(j * f_blk, (j + 1) * f_blk)
            rz = jnp.maximum(_nn_dot(h2, w1_ref[j]), 0.0)
            relu_scr[:, cols] = rz
            ub = (rz * rz).astype(BF16)
            u_ref[:, cols] = ub
            y2 = y2 + _nn_dot(ub, w2_ref[j])
        x2 = x1v + g2 * y2
        r3 = lax.rsqrt(jnp.mean(x2 * x2, axis=-1, keepdims=True) + RMS_EPS)
        xn3 = x2 * r3
        diff = xn3 * gf - tgt_ref[...]
        dout = diff * (1.0 / d_model)

        @pl.when(jnp.logical_and(b == 0, t == 0))
        def _():
            head_ref[...] = jnp.zeros(head_ref.shape, F32)

        @pl.when(t == 0)
        def _():
            dmod_ref[...] = jnp.zeros(dmod_ref.shape, F32)

        head_ref[0:1, :] += jnp.sum(dout * xn3, axis=0, keepdims=True)
        head_ref[1:2, :] += jnp.sum(diff * diff, axis=0, keepdims=True)
        dxn3 = dout * gf
        dx2 = r3 * (dxn3 - xn3 * jnp.mean(dxn3 * xn3, axis=-1, keepdims=True))
        dmod_ref[2:3, :] += jnp.sum(dx2 * y2, axis=0, keepdims=True)
        dy2 = (g2 * dx2).astype(BF16)
        dy2_ref[...] = dy2
        dh2 = jnp.zeros((tm, d_model), F32)
        for j in range(n_blk):
            cols = slice(j * f_blk, (j + 1) * f_blk)
            dz = (_nt_dot(dy2, w2_ref[j]) * (2.0 * relu_scr[:, cols])).astype(BF16)
            dz_ref[:, cols] = dz
            dh2 = dh2 + _nt_dot(dz, w1_ref[j])
        dmod_ref[0:1, :] += jnp.sum(dh2, axis=0, keepdims=True)
        dmod_ref[1:2, :] += jnp.sum(dh2 * xn2, axis=0, keepdims=True)
        dxn2 = dh2 * (1.0 + sc2)
        dx1 = dx2 + r2 * (dxn2 - xn2 * jnp.mean(dxn2 * xn2, axis=-1, keepdims=True))
        dx1_ref[...] = dx1
        dy1_ref[...] = (mod_ref[2:3, :] * dx1).astype(BF16)

    def tok(width):
        return pl.BlockSpec((None, tm, width), lambda b, t: (b, t, 0))

    def const(shape):
        return pl.BlockSpec(shape, lambda b, t: (0,) * len(shape))

    def resident(shape):
        return pl.BlockSpec(shape, lambda b, t: (0,) * len(shape), pipeline_mode=pl.Buffered(1))

    out_shape = (
        jax.ShapeDtypeStruct((bsz, seq, d_model), F32),
        jax.ShapeDtypeStruct((bsz, seq, d_model), BF16),
        jax.ShapeDtypeStruct((bsz, seq, d_model), BF16),
        jax.ShapeDtypeStruct((bsz, seq, d_ff), BF16),
        jax.ShapeDtypeStruct((bsz, seq, d_ff), BF16),
        jax.ShapeDtypeStruct((bsz, SUBLANES, d_model), F32),
        jax.ShapeDtypeStruct((SUBLANES, d_model), F32),
        jax.ShapeDtypeStruct((bsz, seq, d_model), BF16),
    )
    return pl.pallas_call(
        body, name="mlp_fwd_bwd", out_shape=out_shape, grid=(bsz, nt),
        in_specs=[tok(d_model), tok(d_model), pl.BlockSpec((None, 6, d_model), lambda b, t: (b, 0, 0)),
                  resident(w1.shape), resident(w2.shape), const(g_final.shape)],
        out_specs=(tok(d_model), tok(d_model), tok(d_model), tok(d_ff), tok(d_ff),
                   pl.BlockSpec((None, SUBLANES, d_model), lambda b, t: (b, 0, 0)),
                   const((SUBLANES, d_model)), tok(d_model)),
        scratch_shapes=[pltpu.VMEM((tm, d_ff), F32)],
        compiler_params=pltpu.CompilerParams(
            dimension_semantics=("arbitrary", "arbitrary"), vmem_limit_bytes=MLP_VMEM_LIMIT),
    )(x1, target, mod, w1, w2, g_final)


def _mixer_bwd_call(dx1, x, proj, a1, cv, y1, mod, win_t, wout, cw, cp, partials):
    n_part = len(partials)
    bsz, seq, d_model = x.shape
    c_half = cw.shape[1]
    n_taps = 31
    d_in = win_t.shape[0] * win_t.shape[1]
    tm = _token_tile(seq)
    nt = seq // tm

    def body(dx1_ref, x_ref, proj_ref, a1_ref, cv_ref, y1_ref, mod_ref, win_ref, wout_ref, cw_ref, cp_ref, *rest):
        part_refs, rest = rest[:n_part], rest[n_part:]
        gx_ref, dproj_ref, h1_ref, dmod_ref, cgrad_ref = rest[:5]
        slab_refs, rest = rest[5:5 + n_part], rest[5 + n_part:]
        dext, cext, a0_scr, da0_scr, tap_acc, row_acc, dshift, send_sems, recv_sems, loc_sems = rest
        b, step = pl.program_id(0), pl.program_id(1)
        first = jnp.logical_and(b == 0, step == 0)
        last = jnp.logical_and(b == bsz - 1, step == nt - 1)

        @pl.when(first)
        def _():
            for copy in _scatter_copies(part_refs, slab_refs, send_sems, recv_sems, loc_sems):
                copy.start()

        dx1v = dx1_ref[...]
        xv = x_ref[...]
        sh1, sc1, g1 = mod_ref[0:1, :], mod_ref[1:2, :], mod_ref[2:3, :]

        @pl.when(first)
        def _():
            tap_acc[...] = jnp.zeros(tap_acc.shape, F32)
            row_acc[...] = jnp.zeros(row_acc.shape, F32)

        @pl.when(step == 0)
        def _():
            dmod_ref[...] = jnp.zeros(dmod_ref.shape, F32)
            dext[tm:tm + HALO, :] = jnp.zeros((HALO, c_half), F32)
            cext[tm:tm + SHORT_HALO, :] = jnp.zeros((SHORT_HALO, c_half), F32)

        @pl.when(step > 0)
        def _():
            dext[tm:tm + HALO, :] = dext[0:HALO, :]
            cext[tm:tm + SHORT_HALO, :] = cext[0:SHORT_HALO, :]

        dmod_ref[2:3, :] += jnp.sum(dx1v * y1_ref[...], axis=0, keepdims=True)
        dy1 = (g1 * dx1v).astype(BF16)
        dmixed = _nt_dot(dy1, wout_ref[...].reshape(d_model, d_model))
        d_a, d_s = dmixed[:, 0:c_half], dmixed[:, c_half:2 * c_half]

        val, gate = proj_ref[:, 0:c_half], proj_ref[:, c_half:2 * c_half]
        s_b = proj_ref[:, 2 * c_half:3 * c_half]
        s_c, s_h = proj_ref[:, 3 * c_half:4 * c_half], proj_ref[:, 4 * c_half:5 * c_half]

        d_sb = d_s * cv_ref[...]
        cext[0:tm, :] = d_s * s_b
        q = s_c * s_h
        dq = jnp.zeros((tm, c_half), F32)
        for k in range(3):
            shifted = cext[2 - k:2 - k + tm, :]
            dq = dq + cp_ref[3 + k:4 + k, :] * shifted
            row_acc[k:k + 1, :] += jnp.sum(q * shifted, axis=0, keepdims=True)
        d_sc, d_sh = dq * s_h, dq * s_c

        norm, rstd = _layer_norm_parts(a1_ref[...])
        ln_g = cp_ref[1:2, :]
        a2 = norm * ln_g + cp_ref[2:3, :]
        sg = _sigmoid(a2)
        d_a2 = d_a * (sg * (1.0 + a2 * (1.0 - sg)))
        row_acc[4:5, :] += jnp.sum(d_a2 * norm, axis=0, keepdims=True)
        row_acc[5:6, :] += jnp.sum(d_a2, axis=0, keepdims=True)
        d_n = d_a2 * ln_g
        d_a1 = rstd * (d_n - jnp.mean(d_n, axis=-1, keepdims=True)
                       - norm * jnp.mean(d_n * norm, axis=-1, keepdims=True))
        row_acc[3:4, :] += jnp.sum(d_a1, axis=0, keepdims=True)
        dext[0:tm, :] = d_a1
        sig_g = _sigmoid(gate)
        a0_scr[...] = val * sig_g

        _fill_shifted(dext, dshift, tm)
        for r0 in range(0, tm, ROW_CHUNK):
            a0c = a0_scr[r0:r0 + ROW_CHUNK, :]
            acc = jnp.zeros((ROW_CHUNK, c_half), F32)
            for k in range(n_taps):
                shifted = _shifted_rows(dext, dshift, r0 + (n_taps - 1) - k, ROW_CHUNK)
                acc = acc + cw_ref[k:k + 1, :] * shifted
                prod = a0c * shifted
                part = prod[0:SUBLANES, :]
                for g in range(1, ROW_CHUNK // SUBLANES):
                    part = part + prod[g * SUBLANES:(g + 1) * SUBLANES, :]
                tap_acc[k * SUBLANES:(k + 1) * SUBLANES, :] += part
            da0_scr[r0:r0 + ROW_CHUNK, :] = acc
        d_a0 = da0_scr[...]
        d_val = d_a0 * sig_g
        d_gate = d_a0 * val * sig_g * (1.0 - sig_g)

        dproj = jnp.concatenate([d_val, d_gate, d_sb, d_sc, d_sh], axis=-1).astype(BF16)
        dproj_ref[...] = dproj
        dh1 = _nn_dot(dproj, win_ref[...].reshape(d_in, d_model))
        r1 = lax.rsqrt(jnp.mean(xv * xv, axis=-1, keepdims=True) + RMS_EPS)
        xn1 = xv * r1
        h1_ref[...] = (xn1 * (1.0 + sc1) + sh1).astype(BF16)
        dmod_ref[0:1, :] += jnp.sum(dh1, axis=0, keepdims=True)
        dmod_ref[1:2, :] += jnp.sum(dh1 * xn1, axis=0, keepdims=True)
        dxn1 = dh1 * (1.0 + sc1)
        gx_ref[...] = dx1v + r1 * (dxn1 - xn1 * jnp.mean(dxn1 * xn1, axis=-1, keepdims=True))

        @pl.when(last)
        def _():
            taps = jnp.sum(tap_acc[...].reshape(HALO, SUBLANES, c_half), axis=1)
            cgrad_ref[0:HALO, :] = taps
            cgrad_ref[HALO:HALO + SUBLANES, :] = row_acc[...]
            for copy in _scatter_copies(part_refs, slab_refs, send_sems, recv_sems, loc_sems):
                copy.wait()

    hbm = pl.BlockSpec(memory_space=ANY)

    def tok(width):
        return pl.BlockSpec((None, tm, width), lambda b, s: (b, nt - 1 - s, 0))

    def const(shape):
        return pl.BlockSpec(shape, lambda b, s: (0,) * len(shape))

    mod_spec = pl.BlockSpec((None, 6, d_model), lambda b, s: (b, 0, 0))
    out_shape = (
        jax.ShapeDtypeStruct((bsz, seq, d_model), F32),
        jax.ShapeDtypeStruct((bsz, seq, d_in), BF16),
        jax.ShapeDtypeStruct((bsz, seq, d_model), BF16),
        jax.ShapeDtypeStruct((bsz, SUBLANES, d_model), F32),
        jax.ShapeDtypeStruct((HALO + SUBLANES, c_half), F32),
    ) + tuple(jax.ShapeDtypeStruct(p.shape, p.dtype) for p in partials)
    outs = pl.pallas_call(
        body, name="mixer_bwd", out_shape=out_shape, grid=(bsz, nt),
        in_specs=[tok(d_model), tok(d_model), tok(d_in), tok(c_half), tok(c_half), tok(d_model), mod_spec,
                  const(win_t.shape), const(wout.shape), const(cw.shape), const(cp.shape)] + [hbm] * n_part,
        out_specs=(tok(d_model), tok(d_in), tok(d_model),
                   pl.BlockSpec((None, SUBLANES, d_model), lambda b, s: (b, 0, 0)),
                   const((HALO + SUBLANES, c_half))) + (hbm,) * n_part,
        scratch_shapes=[
            pltpu.VMEM((tm + HALO, c_half), F32), pltpu.VMEM((tm + SHORT_HALO, c_half), F32),
            pltpu.VMEM((tm, c_half), F32), pltpu.VMEM((tm, c_half), F32),
            pltpu.VMEM((HALO * SUBLANES, c_half), F32), pltpu.VMEM((SUBLANES, c_half), F32),
            pltpu.VMEM((SUBLANES - 1, _shifted_rows_count(tm), c_half), F32),
        ] + _exchange_sems(n_part),
        compiler_params=pltpu.CompilerParams(
            dimension_semantics=("arbitrary", "arbitrary"), vmem_limit_bytes=V7X_VMEM_LIMIT),
    )(dx1, x, proj, a1, cv, y1, mod, win_t, wout, cw, cp, *partials)
    return outs[:5], outs[5:]


def _largest_divisor(n, cap, multiple):
    best = None
    for cand in range(multiple, min(n, cap) + 1, multiple):
        if n % cand == 0:
            best = cand
    return best if best is not None else n


WGRAD_TOKENS_PER_STEP = 2048
WGRAD_COLS_PER_STEP = 1024


def _wgrad_call(a, b, name, owner_cols=None, carried=None):
    tokens, m_dim = a.shape
    n_dim = b.shape[1]
    bk = _largest_divisor(tokens, WGRAD_TOKENS_PER_STEP, 128)
    n_k = tokens // bk
    if owner_cols is None:
        bm = _largest_divisor(m_dim, 1024, m_dim // N_DEV)
        bn = n_dim
        owners = 1
        out_shape = jax.ShapeDtypeStruct((m_dim, n_dim), BF16)
        out_spec = pl.BlockSpec((bm, bn), lambda i, j, k: (i, j))
    else:
        bm = m_dim
        bn = _largest_divisor(n_dim, WGRAD_COLS_PER_STEP, owner_cols)
        owners = bn // owner_cols
        out_shape = jax.ShapeDtypeStruct((n_dim // owner_cols, m_dim, owner_cols), BF16)
        out_spec = pl.BlockSpec((owners, bm, owner_cols), lambda i, j, k: (j, i, 0))

    def body(a_ref, b_ref, *rest):
        o_ref, acc = (rest[0], rest[1]) if carried is None else (rest[1], rest[3])
        k = pl.program_id(2)

        @pl.when(k == 0)
        def _():
            acc[...] = jnp.zeros(acc.shape, F32)

        acc[...] += _tn_dot(a_ref[...], b_ref[...])

        @pl.when(k == n_k - 1)
        def _():
            if owner_cols is None:
                o_ref[...] = acc[...].astype(BF16)
            else:
                for q in range(owners):
                    o_ref[q] = acc[:, q * owner_cols:(q + 1) * owner_cols].astype(BF16)

    in_specs = [pl.BlockSpec((bk, bm), lambda i, j, k: (k, i)), pl.BlockSpec((bk, bn), lambda i, j, k: (k, j))]
    operands, out_shapes, out_specs, aliases = [a, b], [out_shape], [out_spec], {}
    if carried is not None:
        hbm = pl.BlockSpec(memory_space=ANY)
        in_specs.append(hbm)
        operands.append(carried)
        out_shapes.append(jax.ShapeDtypeStruct(carried.shape, carried.dtype))
        out_specs.append(hbm)
        aliases = {2: 1}
    outs = pl.pallas_call(
        body, name=name, out_shape=tuple(out_shapes), grid=(m_dim // bm, n_dim // bn, n_k),
        in_specs=in_specs, out_specs=tuple(out_specs), scratch_shapes=[pltpu.VMEM((bm, bn), F32)],
        input_output_aliases=aliases,
        compiler_params=pltpu.CompilerParams(
            dimension_semantics=("arbitrary", "arbitrary", "arbitrary"), vmem_limit_bytes=V7X_VMEM_LIMIT),
    )(*operands)
    out = outs[0]
    if owner_cols is None:
        out = out.reshape(N_DEV, m_dim // N_DEV, n_dim)
    return out if carried is None else (out, outs[1])


def _tail_scatter_call(partial, small):
    _, rows, cols = partial.shape
    n_chips = N_DEV // 2

    def body(g_ref, small_ref, out_ref, small_all, from_sibling, pair,
             p1_send, p1_recv, p2_send, p2_recv, s_send, s_recv, s_loc):
        x, y, c = _coords()
        sibling = (x, y, 1 - c)
        my_chip = 2 * x + y
        chips = [(1 - x, y), (x, 1 - y), (1 - x, 1 - y)]

        def remote(src, dst, send_sem, recv_sem, to):
            return pltpu.make_async_remote_copy(src_ref=src, dst_ref=dst, send_sem=send_sem, recv_sem=recv_sem,
                                                device_id=to, device_id_type=MESH)

        gather = _TwoLevelGather([small_ref], [small_all], s_send, s_recv, s_loc)
        gather.start()
        first = [remote(g_ref.at[2 * q + (1 - c)], from_sibling.at[q], p1_send.at[q], p1_recv.at[q], sibling)
                 for q in range(n_chips)]
        for cp in first:
            cp.start()
        for cp in first:
            cp.wait()
        for q in range(n_chips):
            pair[q] = (g_ref[2 * q + c].astype(F32) + from_sibling[q].astype(F32)).astype(BF16)
        second = [remote(pair.at[2 * cx + cy], out_ref.at[my_chip], p2_send.at[j], p2_recv.at[j], (cx, cy, c))
                  for j, (cx, cy) in enumerate(chips)]
        for cp in second:
            cp.start()
        out_ref[my_chip] = pair[my_chip]
        gather.forward()
        for cp in second:
            cp.wait()
        gather.finish()

    vm = pl.BlockSpec(memory_space=VMEM)
    slabs = pltpu.VMEM((n_chips, rows, cols), BF16)
    return pl.pallas_call(
        body, name="scatter_tail",
        out_shape=(jax.ShapeDtypeStruct((n_chips, rows, cols), BF16), jax.ShapeDtypeStruct((N_DEV,) + small.shape, F32)),
        in_specs=[vm, vm], out_specs=(vm, vm),
        scratch_shapes=[slabs, slabs,
                        pltpu.SemaphoreType.DMA((n_chips,)), pltpu.SemaphoreType.DMA((n_chips,)),
                        pltpu.SemaphoreType.DMA((n_chips - 1,)), pltpu.SemaphoreType.DMA((n_chips - 1,))]
        + _exchange_sems(1),
        compiler_params=pltpu.CompilerParams(vmem_limit_bytes=V7X_VMEM_LIMIT),
    )(partial, small)


def _adamw(w, g, m, v):
    m2 = ADAM_B1 * m + (1.0 - ADAM_B1) * g
    v2 = ADAM_B2 * v + (1.0 - ADAM_B2) * (g * g)
    m_hat = m2 / (1.0 - ADAM_B1 ** ADAM_STEP)
    v_hat = v2 / (1.0 - ADAM_B2 ** ADAM_STEP)
    delta = -ADAM_LR * (m_hat / (jnp.sqrt(v_hat) + ADAM_EPS) + ADAM_WD * w)
    return delta, m2, v2


def _adam_slabs_call(slabs, w, m, v, name):
    rows, cols = w.shape
    n_slabs = slabs.shape[0]
    tr = _largest_divisor(rows, 256, 2 * SUBLANES)

    def body(s_ref, w_ref, m_ref, v_ref, g_ref, d_ref, m2_ref, v2_ref):
        g = s_ref[0].astype(F32)
        for k in range(1, n_slabs):
            g = g + s_ref[k].astype(F32)
        delta, m2, v2 = _adamw(w_ref[...], g, m_ref[...], v_ref[...])
        g_ref[...] = g
        d_ref[...] = delta
        m2_ref[...] = m2
        v2_ref[...] = v2

    tile = pl.BlockSpec((tr, cols), lambda i: (i, 0))
    shp = jax.ShapeDtypeStruct((rows, cols), F32)
    return pl.pallas_call(
        body, name=name, out_shape=(shp, shp, shp, shp), grid=(rows // tr,),
        in_specs=[pl.BlockSpec((n_slabs, tr, cols), lambda i: (0, i, 0)), tile, tile, tile],
        out_specs=(tile, tile, tile, tile),
        compiler_params=pltpu.CompilerParams(dimension_semantics=("arbitrary",), vmem_limit_bytes=V7X_VMEM_LIMIT),
    )(slabs, w, m, v)


def _adam_ada_call(c_rows, dmod_cols, w, m, v):
    rows, cols = w.shape
    n_rows = c_rows.shape[0]
    tr = _largest_divisor(rows, 256, 128)

    def body(c_ref, dm_ref, w_ref, m_ref, v_ref, g_ref, d_ref, m2_ref, v2_ref):
        cv = c_ref[...]
        c_act = (cv * _sigmoid(cv)).astype(BF16)
        g = _tn_dot(c_act, dm_ref[...].astype(BF16))
        delta, m2, v2 = _adamw(w_ref[...], g, m_ref[...], v_ref[...])
        g_ref[...] = g
        d_ref[...] = delta
        m2_ref[...] = m2
        v2_ref[...] = v2

    tile = pl.BlockSpec((tr, cols), lambda i: (i, 0))
    shp = jax.ShapeDtypeStruct((rows, cols), F32)
    return pl.pallas_call(
        body, name="adam_w_ada", out_shape=(shp, shp, shp, shp), grid=(rows // tr,),
        in_specs=[pl.BlockSpec((n_rows, tr), lambda i: (0, i)), pl.BlockSpec((n_rows, cols), lambda i: (0, 0)),
                  tile, tile, tile],
        out_specs=(tile, tile, tile, tile),
        compiler_params=pltpu.CompilerParams(dimension_semantics=("arbitrary",), vmem_limit_bytes=V7X_VMEM_LIMIT),
    )(c_rows, dmod_cols, w, m, v)


def _small_sum_call(small_all, n_grad_rows, loss_rows, bias_rows, loss_scale):
    _, rows, width = small_all.shape
    lo, hi = loss_rows
    b0, b1, b2 = bias_rows
    nb = b1 - b0

    def body(s_ref, sum_ref, extra_ref):
        tot = s_ref[0]
        for k in range(1, N_DEV):
            tot = tot + s_ref[k]
        sum_ref[...] = tot[0:n_grad_rows, :]
        extra_ref[0:nb, :] = tot[b0:b1, :] + tot[b1:b2, :]
        head = tot[n_grad_rows - 2 * SUBLANES:n_grad_rows, :]
        rows_id = lax.broadcasted_iota(jnp.int32, head.shape, 0) + (n_grad_rows - 2 * SUBLANES)
        sq = jnp.where(jnp.logical_and(rows_id >= lo, rows_id < hi), head, 0.0)
        extra_ref[nb:nb + SUBLANES, :] = jnp.zeros((SUBLANES, width), F32) + jnp.sum(sq) * loss_scale

    vm = pl.BlockSpec(memory_space=VMEM)
    return pl.pallas_call(
        body, name="small_sum",
        out_shape=(jax.ShapeDtypeStruct((n_grad_rows, width), F32), jax.ShapeDtypeStruct((nb + SUBLANES, width), F32)),
        in_specs=[vm], out_specs=(vm, vm),
    )(small_all)


def _adam_small_call(ws, gs, ms, vs):
    n = len(ws)

    def body(*refs):
        w_refs, g_refs, m_refs, v_refs = refs[:n], refs[n:2 * n], refs[2 * n:3 * n], refs[3 * n:4 * n]
        d_refs, m2_refs, v2_refs = refs[4 * n:5 * n], refs[5 * n:6 * n], refs[6 * n:7 * n]
        for i in range(n):
            delta, m2, v2 = _adamw(w_refs[i][...], g_refs[i][...], m_refs[i][...], v_refs[i][...])
            d_refs[i][...] = delta
            m2_refs[i][...] = m2
            v2_refs[i][...] = v2

    vm = pl.BlockSpec(memory_space=VMEM)
    shapes = tuple(jax.ShapeDtypeStruct(w.shape, F32) for w in ws)
    outs = pl.pallas_call(body, name="adam_small", out_shape=shapes * 3,
                          in_specs=[vm] * (4 * n), out_specs=(vm,) * (3 * n))(*ws, *gs, *ms, *vs)
    return outs[:n], outs[n:2 * n], outs[2 * n:]


def kernel(x, c, w_ada, b_ada, w_in, conf_dw_w, conf_dw_b, conf_ln_g, conf_ln_b, sc_conv_w, w_out, w_mlp1, w_mlp2, g_final, loss_target, m_w_ada, m_b_ada, m_w_in, m_conf_dw_w, m_conf_dw_b, m_conf_ln_g, m_conf_ln_b, m_sc_conv_w, m_w_out, m_w_mlp1, m_w_mlp2, m_g_final, v_w_ada, v_b_ada, v_w_in, v_conf_dw_w, v_conf_dw_b, v_conf_ln_g, v_conf_ln_b, v_sc_conv_w, v_w_out, v_w_mlp1, v_w_mlp2, v_g_final):
    bsz, seq, d_model = x.shape
    c_half = conf_dw_b.shape[-1]
    n_taps = conf_dw_w.shape[1]
    cc = conf_dw_w.shape[-1]
    a_cols = w_ada.shape[-1]
    tokens = bsz * seq
    me = _dev_index()

    c_pad = jnp.pad(c, ((0, SUBLANES - bsz), (0, 0)))
    b_ada_loc = lax.dynamic_slice(b_ada, (0, me * a_cols), (1, a_cols))
    small_loc = jnp.zeros((HALO, 128), F32)
    small_loc = small_loc.at[:n_taps, :cc].set(conf_dw_w[0]).at[:3, cc:2 * cc].set(sc_conv_w[0])
    (win_t, wout_all), small_all, c_all, mod_rows = _gather_call(
        c_pad, w_ada[0], b_ada_loc, small_loc, [w_in[0].T.astype(BF16), w_out[0].astype(BF16)])
    cw = small_all[:, :, :cc].transpose(1, 0, 2).reshape(HALO, c_half)
    scw = small_all[:, :3, cc:2 * cc].transpose(1, 0, 2).reshape(3, c_half)
    cp = jnp.concatenate([conf_dw_b, conf_ln_g, conf_ln_b, scw, jnp.zeros((2, c_half), F32)], axis=0)
    mod = mod_rows[:, :bsz, :].transpose(1, 0, 2).reshape(bsz, 6, d_model)

    flat = lambda t: t.reshape(tokens, t.shape[-1])
    (proj, a1, cv, mixed, y1, x1), (w1_all, w2_all) = _mixer_fwd_call(
        x, mod, win_t, wout_all, cw, cp, [w_mlp1[0].astype(BF16), w_mlp2[0].astype(BF16)])
    dx1, h2, dy2, u, dz, dmod2, head, dy1 = _mlp_call(
        x1, loss_target, mod, w1_all, w2_all, g_final.reshape(1, d_model))
    g_w1 = _wgrad_call(flat(h2), flat(dz), "wgrad_mlp1", owner_cols=w_mlp1.shape[-1])
    g_w2 = _wgrad_call(flat(u), flat(dy2), "wgrad_mlp2")
    g_out = _wgrad_call(flat(mixed), flat(dy1), "wgrad_out")
    (grad_x, dproj, h1, dmod1, cgrad), (s_w1, s_w2, s_out) = _mixer_bwd_call(
        dx1, x, proj, a1, cv, y1, mod, win_t, wout_all, cw, cp, [g_w1, g_w2, g_out])
    g_in_t, grad_x = _wgrad_call(flat(dproj), flat(h1), "wgrad_in", carried=grad_x)

    dmod = jnp.concatenate([dmod1[:, :3, :], dmod2[:, :3, :]], axis=1)
    n_cg = cgrad.shape[0]
    per_b = 6 * d_model // c_half
    per_b_pad = -(-per_b // SUBLANES) * SUBLANES
    dmod_rows = jnp.pad(dmod.reshape(bsz, per_b, c_half), ((0, 0), (0, per_b_pad - per_b), (0, 0)))
    small = jnp.concatenate([
        cgrad,
        head.reshape(2 * SUBLANES, c_half),
        dmod_rows.reshape(bsz * per_b_pad, c_half),
    ], axis=0)
    s_in, gathered = _tail_scatter_call(g_in_t, small)

    n_head = n_cg + 2 * SUBLANES
    sums, extra = _small_sum_call(
        gathered, n_head, (n_cg + 2, n_cg + 4), (n_head, n_head + per_b_pad, n_head + 2 * per_b_pad), 0.5 / d_model)
    loss = extra[per_b_pad, 0]
    g_b_ada = extra[:per_b].reshape(1, 6 * d_model)
    g_dw_w = lax.dynamic_slice(sums[:n_taps], (0, me * cc), (n_taps, cc))
    g_sc_w = lax.dynamic_slice(sums[HALO:HALO + 3], (0, me * cc), (3, cc))
    g_dw_b, g_ln_g, g_ln_b = sums[HALO + 3:HALO + 4], sums[HALO + 4:HALO + 5], sums[HALO + 5:HALO + 6]
    g_gf = sums[n_cg:n_cg + 2].reshape(1, d_model)

    dmod_all = gathered[:, n_head:, :].reshape(N_DEV, bsz, per_b_pad, c_half)[:, :, :per_b, :]
    dmod_all = dmod_all.reshape(N_DEV, bsz, 6 * d_model)
    dmod_cols = lax.dynamic_slice(dmod_all, (0, 0, me * a_cols), (N_DEV, bsz, a_cols))
    dmod_cols = jnp.pad(dmod_cols, ((0, 0), (0, SUBLANES - bsz), (0, 0))).reshape(N_DEV * SUBLANES, a_cols)
    c_rows = c_all.reshape(N_DEV * SUBLANES, d_model)
    g_ada, d_ada, m_ada, v_ada = _adam_ada_call(c_rows, dmod_cols, w_ada[0], m_w_ada[0], v_w_ada[0])

    gi, di, mi, vi = _adam_slabs_call(s_in, w_in[0].T, m_w_in[0].T, v_w_in[0].T, "adam_w_in")
    gi, di, mi, vi = gi.T, di.T, mi.T, vi.T
    go, do, mo, vo = _adam_slabs_call(s_out, w_out[0], m_w_out[0], v_w_out[0], "adam_w_out")
    g1, d1, m1, v1 = _adam_slabs_call(s_w1, w_mlp1[0], m_w_mlp1[0], v_w_mlp1[0], "adam_w_mlp1")
    g2, d2, m2, v2 = _adam_slabs_call(s_w2, w_mlp2[0], m_w_mlp2[0], v_w_mlp2[0], "adam_w_mlp2")

    small_like = [b_ada, conf_dw_w, conf_dw_b, conf_ln_g, conf_ln_b, sc_conv_w, g_final]
    two_d = lambda t: t.reshape(-1, t.shape[-1])
    small_g = [g_b_ada, g_dw_w, g_dw_b, g_ln_g, g_ln_b, g_sc_w, g_gf]
    sd, sm, sv = _adam_small_call(
        [two_d(t) for t in small_like], small_g,
        [two_d(t) for t in (m_b_ada, m_conf_dw_w, m_conf_dw_b, m_conf_ln_g, m_conf_ln_b, m_sc_conv_w, m_g_final)],
        [two_d(t) for t in (v_b_ada, v_conf_dw_w, v_conf_dw_b, v_conf_ln_g, v_conf_ln_b, v_sc_conv_w, v_g_final)])
    like = lambda parts: [p.reshape(w.shape) for p, w in zip(parts, small_like)]
    sg, sd, sm, sv = like(small_g), like(sd), like(sm), like(sv)

    def ordered(ada, small_list, w_in_, w_out_, w1_, w2_):
        b_ada_, dw_w_, dw_b_, ln_g_, ln_b_, sc_w_, gf_ = small_list
        return [ada[None], b_ada_, w_in_[None], dw_w_, dw_b_, ln_g_, ln_b_, sc_w_, w_out_[None], w1_[None], w2_[None], gf_]

    grads = ordered(g_ada, sg, gi, go, g1, g2)
    deltas = ordered(d_ada, sd, di, do, d1, d2)
    new_m = ordered(m_ada, sm, mi, mo, m1, m2)
    new_v = ordered(v_ada, sv, vi, vo, v1, v2)
    return (loss, grad_x, *grads, *deltas, *new_m, *new_v)
```

```python
import functools

import jax
import jax.numpy as jnp
from jax import lax
from jax.experimental import pallas as pl
from jax.experimental.pallas import tpu as pltpu

N_DEV = 8
RMS_EPS = 1e-6
ADAM_LR = 0.001
ADAM_B1 = 0.9
ADAM_B2 = 0.999
ADAM_EPS = 1e-08
ADAM_WD = 0.01
ADAM_STEP = 10

F32 = jnp.float32
BF16 = jnp.bfloat16
MESH = pl.DeviceIdType.MESH
VMEM = pltpu.VMEM
ANY = pl.ANY

HALO = 32
SHORT_HALO = 8
ROW_CHUNK = 32
SUBLANES = 8
V7X_VMEM_LIMIT = 56 * 1024 * 1024
MLP_VMEM_LIMIT = 48 * 1024 * 1024
MLP_TOKEN_TILE = 256
MIXER_FWD_TOKEN_TILE = 512


def _coords():
    return lax.axis_index("x"), lax.axis_index("y"), lax.axis_index("c")


def _dev_index():
    x, y, c = _coords()
    return 4 * x + 2 * y + c


def _peer(k):
    x, y, c = _coords()
    px = 1 - x if (k >> 2) & 1 else x
    py = 1 - y if (k >> 1) & 1 else y
    pc = 1 - c if k & 1 else c
    return (px, py, pc), 4 * px + 2 * py + pc


def _sigmoid(v):
    return jax.nn.sigmoid(v)


def _nt_dot(a, b):
    return lax.dot_general(a, b, (((1,), (1,)), ((), ())), preferred_element_type=F32)


def _nn_dot(a, b):
    return jnp.dot(a, b, preferred_element_type=F32)


def _tn_dot(a, b):
    return lax.dot_general(a, b, (((0,), (0,)), ((), ())), preferred_element_type=F32)


def _token_tile(seq):
    return 256 if seq % 256 == 0 else 64


class _TwoLevelGather:
    def __init__(self, srcs, dsts, send_sems, recv_sems, loc_sems):
        x, y, c = _coords()
        me = 4 * x + 2 * y + c
        sibling = (x, y, 1 - c)
        chips = [(1 - x, y), (x, 1 - y), (1 - x, 1 - y)]

        def remote(src, dst, a, col, to):
            return pltpu.make_async_remote_copy(
                src_ref=src, dst_ref=dst, send_sem=send_sems.at[a, col], recv_sem=recv_sems.at[a, col],
                device_id=to, device_id_type=MESH)

        self.local, self.to_sibling, self.ici, self.forwards = [], [], [], []
        for a, (src, dst) in enumerate(zip(srcs, dsts)):
            self.local.append(pltpu.make_async_copy(src, dst.at[me], loc_sems.at[a]))
            self.to_sibling.append(remote(src, dst.at[me], a, 0, sibling))
            for j, (cx, cy) in enumerate(chips):
                self.ici.append(remote(src, dst.at[me], a, 1 + j, (cx, cy, c)))
                landed = dst.at[4 * cx + 2 * cy + c]
                self.forwards.append(remote(landed, landed, a, 4 + j, sibling))

    def start(self):
        for cp in self.local + self.to_sibling + self.ici:
            cp.start()

    def forward(self):
        for arrival, onward in zip(self.ici, self.forwards):
            arrival.wait_recv()
            onward.start()

    def finish(self):
        for cp in self.ici:
            cp.wait_send()
        for cp in self.local + self.to_sibling + self.forwards:
            cp.wait()


def _scatter_copies(srcs, dsts, send_sems, recv_sems, loc_sems):
    me = _dev_index()
    copies = []
    for a, (src, dst) in enumerate(zip(srcs, dsts)):
        copies.append(pltpu.make_async_copy(src.at[me], dst.at[me], loc_sems.at[a]))
        for k in range(1, N_DEV):
            peer, pidx = _peer(k)
            copies.append(pltpu.make_async_remote_copy(
                src_ref=src.at[pidx], dst_ref=dst.at[me], send_sem=send_sems.at[a, k - 1],
                recv_sem=recv_sems.at[a, k - 1], device_id=peer, device_id_type=MESH))
    return copies


def _exchange_sems(n_arrays):
    return [pltpu.SemaphoreType.DMA((n_arrays, N_DEV - 1)), pltpu.SemaphoreType.DMA((n_arrays, N_DEV - 1)),
            pltpu.SemaphoreType.DMA((n_arrays,))]


def _gather_call(c_pad, w_ada, b_ada_loc, small_loc, big_shards):
    n_big = len(big_shards)
    d_model = c_pad.shape[1]
    a_cols = w_ada.shape[1]

    def body(c_ref, wada_ref, bada_ref, small_ref, *rest):
        big_in = rest[:n_big]
        big_out = rest[n_big:2 * n_big]
        small_all, c_all, mod_rows = rest[2 * n_big:2 * n_big + 3]
        modcols, big_send, big_recv, loc_sem, s_send, s_recv = rest[2 * n_big + 3:]
        me = _dev_index()
        big = _TwoLevelGather(big_in, big_out, big_send, big_recv, loc_sem)
        big.start()

        small_all[me] = small_ref[...]
        c_all[me] = c_ref[...]
        first = []
        for k in range(1, N_DEV):
            peer, _ = _peer(k)
            for i, (src, dst) in enumerate(((small_ref, small_all), (c_ref, c_all))):
                cp = pltpu.make_async_remote_copy(
                    src_ref=src, dst_ref=dst.at[me],
                    send_sem=s_send.at[i, k - 1], recv_sem=s_recv.at[i, k - 1],
                    device_id=peer, device_id_type=MESH)
                cp.start()
                first.append(cp)
        for cp in first:
            cp.wait()

        c_rows = c_all[...].reshape(N_DEV * SUBLANES, d_model)
        c_act = c_rows * _sigmoid(c_rows)
        modcols[...] = _nn_dot(c_act.astype(BF16), wada_ref[...].astype(BF16)) + bada_ref[...]
        mod_rows[me] = modcols[pl.ds(pl.multiple_of(me * SUBLANES, SUBLANES), SUBLANES), :]
        second = []
        for k in range(1, N_DEV):
            peer, pidx = _peer(k)
            cp = pltpu.make_async_remote_copy(
                src_ref=modcols.at[pl.ds(pl.multiple_of(pidx * SUBLANES, SUBLANES), SUBLANES), :],
                dst_ref=mod_rows.at[me],
                send_sem=s_send.at[2, k - 1], recv_sem=s_recv.at[2, k - 1],
                device_id=peer, device_id_type=MESH)
            cp.start()
            second.append(cp)
        big.forward()
        for cp in second:
            cp.wait()
        big.finish()

    out_shape = tuple(jax.ShapeDtypeStruct((N_DEV,) + s.shape, s.dtype) for s in big_shards) + (
        jax.ShapeDtypeStruct((N_DEV,) + small_loc.shape, F32),
        jax.ShapeDtypeStruct((N_DEV, SUBLANES, d_model), F32),
        jax.ShapeDtypeStruct((N_DEV, SUBLANES, a_cols), F32),
    )
    vm = pl.BlockSpec(memory_space=VMEM)
    hbm = pl.BlockSpec(memory_space=ANY)
    outs = pl.pallas_call(
        body, name="gather_weights_mod", out_shape=out_shape,
        in_specs=[vm, vm, vm, vm] + [hbm] * n_big,
        out_specs=tuple([hbm] * n_big + [vm, vm, vm]),
        scratch_shapes=[
            pltpu.VMEM((N_DEV * SUBLANES, a_cols), F32),
            pltpu.SemaphoreType.DMA((n_big, N_DEV - 1)),
            pltpu.SemaphoreType.DMA((n_big, N_DEV - 1)),
            pltpu.SemaphoreType.DMA((n_big,)),
            pltpu.SemaphoreType.DMA((3, N_DEV - 1)),
            pltpu.SemaphoreType.DMA((3, N_DEV - 1)),
        ],
        compiler_params=pltpu.CompilerParams(vmem_limit_bytes=V7X_VMEM_LIMIT),
    )(c_pad, w_ada, b_ada_loc, small_loc, *big_shards)
    return outs[:n_big], outs[n_big], outs[n_big + 1], outs[n_big + 2]


def _shifted_rows_count(tm):
    return tm + HALO - SUBLANES


def _fill_shifted(ext, shifted, tm):
    for s in range(1, SUBLANES):
        shifted[s - 1] = ext[s:s + _shifted_rows_count(tm), :]


def _shifted_rows(ext, shifted, start, rows):
    phase = start % SUBLANES
    aligned = start - phase
    if phase == 0:
        return ext[aligned:aligned + rows, :]
    return shifted[phase - 1, aligned:aligned + rows, :]


def _layer_norm_parts(a1):
    mu = jnp.mean(a1, axis=-1, keepdims=True)
    xc = a1 - mu
    rstd = lax.rsqrt(jnp.mean(xc * xc, axis=-1, keepdims=True) + RMS_EPS)
    return xc * rstd, rstd


def _mixer_fwd_call(x, mod, win_t, wout, cw, cp, later_shards):
    n_later = len(later_shards)
    bsz, seq, d_model = x.shape
    c_half = cw.shape[1]
    n_taps = 31
    d_in = win_t.shape[0] * win_t.shape[1]
    tm = MIXER_FWD_TOKEN_TILE if seq % MIXER_FWD_TOKEN_TILE == 0 else _token_tile(seq)
    nt = seq // tm
    forward_step = (2 * bsz * nt) // 3

    def body(x_ref, mod_ref, win_ref, wout_ref, cw_ref, cp_ref, *rest):
        shard_refs, rest = rest[:n_later], rest[n_later:]
        proj_ref, a1_ref, cv_ref, mixed_ref, y1_ref, x1_ref = rest[:6]
        gathered_refs, rest = rest[6:6 + n_later], rest[6 + n_later:]
        aext, qext, ashift, send_sems, recv_sems, loc_sems = rest
        b, t = pl.program_id(0), pl.program_id(1)

        step = b * nt + t

        @pl.when(step == 0)
        def _():
            _TwoLevelGather(shard_refs, gathered_refs, send_sems, recv_sems, loc_sems).start()

        @pl.when(step == forward_step)
        def _():
            _TwoLevelGather(shard_refs, gathered_refs, send_sems, recv_sems, loc_sems).forward()

        xv = x_ref[...]
        sh1, sc1, g1 = mod_ref[0:1, :], mod_ref[1:2, :], mod_ref[2:3, :]
        r1 = lax.rsqrt(jnp.mean(xv * xv, axis=-1, keepdims=True) + RMS_EPS)
        h1 = (xv * r1) * (1.0 + sc1) + sh1
        proj = _nt_dot(h1.astype(BF16), win_ref[...].reshape(d_in, d_model))
        proj_ref[...] = proj
        val, gate = proj[:, 0:c_half], proj[:, c_half:2 * c_half]
        s_b, s_c, s_h = proj[:, 2 * c_half:3 * c_half], proj[:, 3 * c_half:4 * c_half], proj[:, 4 * c_half:5 * c_half]

        @pl.when(t == 0)
        def _():
            aext[0:HALO, :] = jnp.zeros((HALO, c_half), F32)
            qext[0:SHORT_HALO, :] = jnp.zeros((SHORT_HALO, c_half), F32)

        @pl.when(t > 0)
        def _():
            aext[0:HALO, :] = aext[tm:tm + HALO, :]
            qext[0:SHORT_HALO, :] = qext[tm:tm + SHORT_HALO, :]
        aext[HALO:HALO + tm, :] = val * _sigmoid(gate)
        qext[SHORT_HALO:SHORT_HALO + tm, :] = s_c * s_h

        base = HALO - (n_taps - 1)
        _fill_shifted(aext, ashift, tm)
        for r0 in range(0, tm, ROW_CHUNK):
            acc = jnp.zeros((ROW_CHUNK, c_half), F32)
            for k in range(n_taps):
                acc = acc + cw_ref[k:k + 1, :] * _shifted_rows(aext, ashift, r0 + base + k, ROW_CHUNK)
            a1_ref[r0:r0 + ROW_CHUNK, :] = acc + cp_ref[0:1, :]
        sbase = SHORT_HALO - 2
        conv3 = cp_ref[3:4, :] * qext[sbase:sbase + tm, :]
        conv3 = conv3 + cp_ref[4:5, :] * qext[sbase + 1:sbase + 1 + tm, :]
        conv3 = conv3 + cp_ref[5:6, :] * qext[sbase + 2:sbase + 2 + tm, :]
        cv_ref[...] = conv3

        norm, _ = _layer_norm_parts(a1_ref[...])
        a2 = norm * cp_ref[1:2, :] + cp_ref[2:3, :]
        mixed = jnp.concatenate([a2 * _sigmoid(a2), s_b * conv3], axis=-1).astype(BF16)
        mixed_ref[...] = mixed
        y1 = _nn_dot(mixed, wout_ref[...].reshape(d_model, d_model))
        y1_ref[...] = y1
        x1_ref[...] = xv + g1 * y1

        @pl.when(step == bsz * nt - 1)
        def _():
            _TwoLevelGather(shard_refs, gathered_refs, send_sems, recv_sems, loc_sems).finish()

    hbm = pl.BlockSpec(memory_space=ANY)

    def tok(width):
        return pl.BlockSpec((None, tm, width), lambda b, t: (b, t, 0))

    def const(shape):
        return pl.BlockSpec(shape, lambda b, t: (0,) * len(shape))

    def resident(shape):
        return pl.BlockSpec(shape, lambda b, t: (0,) * len(shape), pipeline_mode=pl.Buffered(1))

    out_shape = (
        jax.ShapeDtypeStruct((bsz, seq, d_in), F32),
        jax.ShapeDtypeStruct((bsz, seq, c_half), F32),
        jax.ShapeDtypeStruct((bsz, seq, c_half), F32),
        jax.ShapeDtypeStruct((bsz, seq, d_model), BF16),
        jax.ShapeDtypeStruct((bsz, seq, d_model), F32),
        jax.ShapeDtypeStruct((bsz, seq, d_model), F32),
    ) + tuple(jax.ShapeDtypeStruct((N_DEV,) + s.shape, s.dtype) for s in later_shards)
    outs = pl.pallas_call(
        body, name="mixer_fwd", out_shape=out_shape, grid=(bsz, nt),
        in_specs=[tok(d_model), pl.BlockSpec((None, 6, d_model), lambda b, t: (b, 0, 0)),
                  resident(win_t.shape), resident(wout.shape), const(cw.shape), const(cp.shape)] + [hbm] * n_later,
        out_specs=(tok(d_in), tok(c_half), tok(c_half), tok(d_model), tok(d_model), tok(d_model)) + (hbm,) * n_later,
        scratch_shapes=[pltpu.VMEM((tm + HALO, c_half), F32), pltpu.VMEM((tm + SHORT_HALO, c_half), F32),
                        pltpu.VMEM((SUBLANES - 1, _shifted_rows_count(tm), c_half), F32)]
        + _exchange_sems(n_later),
        compiler_params=pltpu.CompilerParams(
            dimension_semantics=("arbitrary", "arbitrary"), vmem_limit_bytes=V7X_VMEM_LIMIT),
    )(x, mod, win_t, wout, cw, cp, *later_shards)
    return outs[:6], outs[6:]


def _mlp_call(x1, target, mod, w1, w2, g_final):
    bsz, seq, d_model = x1.shape
    n_blk, _, f_blk = w1.shape
    d_ff = n_blk * f_blk
    tm = MLP_TOKEN_TILE if seq % MLP_TOKEN_TILE == 0 else _token_tile(seq)
    nt = seq // tm

    def body(x1_ref, tgt_ref, mod_ref, w1_ref, w2_ref, gf_ref,
             dx1_ref, h2_ref, dy2_ref, u_ref, dz_ref, dmod_ref, head_ref, dy1_ref, relu_scr):
        b, t = pl.program_id(0), pl.program_id(1)
        x1v = x1_ref[...]
        sh2, sc2, g2 = mod_ref[3:4, :], mod_ref[4:5, :], mod_ref[5:6, :]
        gf = gf_ref[...]
        r2 = lax.rsqrt(jnp.mean(x1v * x1v, axis=-1, keepdims=True) + RMS_EPS)
        xn2 = x1v * r2
        h2 = (xn2 * (1.0 + sc2) + sh2).astype(BF16)
        h2_ref[...] = h2
        y2 = jnp.zeros((tm, d_model), F32)
        for j in range(n_blk):
            cols = slice(j * f_blk, (j + 1) * f_blk)
            rz = jnp.maximum(_nn_dot(h2, w1_ref[j]), 0.0)
            relu_scr[:, cols] = rz
            ub = (rz * rz).astype(BF16)
            u_ref[:, cols] = ub
            y2 = y2 + _nn_dot(ub, w2_ref[j])
        x2 = x1v + g2 * y2
        r3 = lax.rsqrt(jnp.mean(x2 * x2, axis=-1, keepdims=True) + RMS_EPS)
        xn3 = x2 * r3
        diff = xn3 * gf - tgt_ref[...]
        dout = diff * (1.0 / d_model)

        @pl.when(jnp.logical_and(b == 0, t == 0))
        def _():
            head_ref[...] = jnp.zeros(head_ref.shape, F32)

        @pl.when(t == 0)
        def _():
            dmod_ref[...] = jnp.zeros(dmod_ref.shape, F32)

        head_ref[0:1, :] += jnp.sum(dout * xn3, axis=0, keepdims=True)
        head_ref[1:2, :] += jnp.sum(diff * diff, axis=0, keepdims=True)
        dxn3 = dout * gf
        dx2 = r3 * (dxn3 - xn3 * jnp.mean(dxn3 * xn3, axis=-1, keepdims=True))
        dmod_ref[2:3, :] += jnp.sum(dx2 * y2, axis=0, keepdims=True)
        dy2 = (g2 * dx2).astype(BF16)
        dy2_ref[...] = dy2
        dh2 = jnp.zeros((tm, d_model), F32)
        for j in range(n_blk):
            cols = slice(j * f_blk, (j + 1) * f_blk)
            dz = (_nt_dot(dy2, w2_ref[j]) * (2.0 * relu_scr[:, cols])).astype(BF16)
            dz_ref[:, cols] = dz
            dh2 = dh2 + _nt_dot(dz, w1_ref[j])
        dmod_ref[0:1, :] += jnp.sum(dh2, axis=0, keepdims=True)
        dmod_ref[1:2, :] += jnp.sum(dh2 * xn2, axis=0, keepdims=True)
        dxn2 = dh2 * (1.0 + sc2)
        dx1 = dx2 + r2 * (dxn2 - xn2 * jnp.mean(dxn2 * xn2, axis=-1, keepdims=True))
        dx1_ref[...] = dx1
        dy1_ref[...] = (mod_ref[2:3, :] * dx1).astype(BF16)

    def tok(width):
        return pl.BlockSpec((None, tm, width), lambda b, t: (b, t, 0))

    def const(shape):
        return pl.BlockSpec(shape, lambda b, t: (0,) * len(shape))

    def resident(shape):
        return pl.BlockSpec(shape, lambda b, t: (0,) * len(shape), pipeline_mode=pl.Buffered(1))

    out_shape = (
        jax.ShapeDtypeStruct((bsz, seq, d_model), F32),
        jax.ShapeDtypeStruct((bsz, seq, d_model), BF16),
        jax.ShapeDtypeStruct((bsz, seq, d_model), BF16),
        jax.ShapeDtypeStruct((bsz, seq, d_ff), BF16),
        jax.ShapeDtypeStruct((bsz, seq, d_ff), BF16),
        jax.ShapeDtypeStruct((bsz, SUBLANES, d_model), F32),
        jax.ShapeDtypeStruct((SUBLANES, d_model), F32),
        jax.ShapeDtypeStruct((bsz, seq, d_model), BF16),
    )
    return pl.pallas_call(
        body, name="mlp_fwd_bwd", out_shape=out_shape, grid=(bsz, nt),
        in_specs=[tok(d_model), tok(d_model), pl.BlockSpec((None, 6, d_model), lambda b, t: (b, 0, 0)),
                  resident(w1.shape), resident(w2.shape), const(g_final.shape)],
        out_specs=(tok(d_model), tok(d_model), tok(d_model), tok(d_ff), tok(d_ff),
                   pl.BlockSpec((None, SUBLANES, d_model), lambda b, t: (b, 0, 0)),
                   const((SUBLANES, d_model)), tok(d_model)),
        scratch_shapes=[pltpu.VMEM((tm, d_ff), F32)],
        compiler_params=pltpu.CompilerParams(
            dimension_semantics=("arbitrary", "arbitrary"), vmem_limit_bytes=MLP_VMEM_LIMIT),
    )(x1, target, mod, w1, w2, g_final)


def _mixer_bwd_call(dx1, x, proj, a1, cv, y1, mod, win_t, wout, cw, cp, partials):
    n_part = len(partials)
    bsz, seq, d_model = x.shape
    c_half = cw.shape[1]
    n_taps = 31
    d_in = win_t.shape[0] * win_t.shape[1]
    tm = _token_tile(seq)
    nt = seq // tm

    def body(dx1_ref, x_ref, proj_ref, a1_ref, cv_ref, y1_ref, mod_ref, win_ref, wout_ref, cw_ref, cp_ref, *rest):
        part_refs, rest = rest[:n_part], rest[n_part:]
        gx_ref, dproj_ref, h1_ref, dmod_ref, cgrad_ref = rest[:5]
        slab_refs, rest = rest[5:5 + n_part], rest[5 + n_part:]
        dext, cext, a0_scr, da0_scr, tap_acc, row_acc, dshift, send_sems, recv_sems, loc_sems = rest
        b, step = pl.program_id(0), pl.program_id(1)
        first = jnp.logical_and(b == 0, step == 0)
        last = jnp.logical_and(b == bsz - 1, step == nt - 1)

        @pl.when(first)
        def _():
            for copy in _scatter_copies(part_refs, slab_refs, send_sems, recv_sems, loc_sems):
                copy.start()

        dx1v = dx1_ref[...]
        xv = x_ref[...]
        sh1, sc1, g1 = mod_ref[0:1, :], mod_ref[1:2, :], mod_ref[2:3, :]

        @pl.when(first)
        def _():
            tap_acc[...] = jnp.zeros(tap_acc.shape, F32)
            row_acc[...] = jnp.zeros(row_acc.shape, F32)

        @pl.when(step == 0)
        def _():
            dmod_ref[...] = jnp.zeros(dmod_ref.shape, F32)
            dext[tm:tm + HALO, :] = jnp.zeros((HALO, c_half), F32)
            cext[tm:tm + SHORT_HALO, :] = jnp.zeros((SHORT_HALO, c_half), F32)

        @pl.when(step > 0)
        def _():
            dext[tm:tm + HALO, :] = dext[0:HALO, :]
            cext[tm:tm + SHORT_HALO, :] = cext[0:SHORT_HALO, :]

        dmod_ref[2:3, :] += jnp.sum(dx1v * y1_ref[...], axis=0, keepdims=True)
        dy1 = (g1 * dx1v).astype(BF16)
        dmixed = _nt_dot(dy1, wout_ref[...].reshape(d_model, d_model))
        d_a, d_s = dmixed[:, 0:c_half], dmixed[:, c_half:2 * c_half]

        val, gate = proj_ref[:, 0:c_half], proj_ref[:, c_half:2 * c_half]
        s_b = proj_ref[:, 2 * c_half:3 * c_half]
        s_c, s_h = proj_ref[:, 3 * c_half:4 * c_half], proj_ref[:, 4 * c_half:5 * c_half]

        d_sb = d_s * cv_ref[...]
        cext[0:tm, :] = d_s * s_b
        q = s_c * s_h
        dq = jnp.zeros((tm, c_half), F32)
        for k in range(3):
            shifted = cext[2 - k:2 - k + tm, :]
            dq = dq + cp_ref[3 + k:4 + k, :] * shifted
            row_acc[k:k + 1, :] += jnp.sum(q * shifted, axis=0, keepdims=True)
        d_sc, d_sh = dq * s_h, dq * s_c

        norm, rstd = _layer_norm_parts(a1_ref[...])
        ln_g = cp_ref[1:2, :]
        a2 = norm * ln_g + cp_ref[2:3, :]
        sg = _sigmoid(a2)
        d_a2 = d_a * (sg * (1.0 + a2 * (1.0 - sg)))
        row_acc[4:5, :] += jnp.sum(d_a2 * norm, axis=0, keepdims=True)
        row_acc[5:6, :] += jnp.sum(d_a2, axis=0, keepdims=True)
        d_n = d_a2 * ln_g
        d_a1 = rstd * (d_n - jnp.mean(d_n, axis=-1, keepdims=True)
                       - norm * jnp.mean(d_n * norm, axis=-1, keepdims=True))
        row_acc[3:4, :] += jnp.sum(d_a1, axis=0, keepdims=True)
        dext[0:tm, :] = d_a1
        sig_g = _sigmoid(gate)
        a0_scr[...] = val * sig_g

        _fill_shifted(dext, dshift, tm)
        for r0 in range(0, tm, ROW_CHUNK):
            a0c = a0_scr[r0:r0 + ROW_CHUNK, :]
            acc = jnp.zeros((ROW_CHUNK, c_half), F32)
            for k in range(n_taps):
                shifted = _shifted_rows(dext, dshift, r0 + (n_taps - 1) - k, ROW_CHUNK)
                acc = acc + cw_ref[k:k + 1, :] * shifted
                prod = a0c * shifted
                part = prod[0:SUBLANES, :]
                for g in range(1, ROW_CHUNK // SUBLANES):
                    part = part + prod[g * SUBLANES:(g + 1) * SUBLANES, :]
                tap_acc[k * SUBLANES:(k + 1) * SUBLANES, :] += part
            da0_scr[r0:r0 + ROW_CHUNK, :] = acc
        d_a0 = da0_scr[...]
        d_val = d_a0 * sig_g
        d_gate = d_a0 * val * sig_g * (1.0 - sig_g)

        dproj = jnp.concatenate([d_val, d_gate, d_sb, d_sc, d_sh], axis=-1).astype(BF16)
        dproj_ref[...] = dproj
        dh1 = _nn_dot(dproj, win_ref[...].reshape(d_in, d_model))
        r1 = lax.rsqrt(jnp.mean(xv * xv, axis=-1, keepdims=True) + RMS_EPS)
        xn1 = xv * r1
        h1_ref[...] = (xn1 * (1.0 + sc1) + sh1).astype(BF16)
        dmod_ref[0:1, :] += jnp.sum(dh1, axis=0, keepdims=True)
        dmod_ref[1:2, :] += jnp.sum(dh1 * xn1, axis=0, keepdims=True)
        dxn1 = dh1 * (1.0 + sc1)
        gx_ref[...] = dx1v + r1 * (dxn1 - xn1 * jnp.mean(dxn1 * xn1, axis=-1, keepdims=True))

        @pl.when(last)
        def _():
            taps = jnp.sum(tap_acc[...].reshape(HALO, SUBLANES, c_half), axis=1)
            cgrad_ref[0:HALO, :] = taps
            cgrad_ref[HALO:HALO + SUBLANES, :] = row_acc[...]
            for copy in _scatter_copies(part_refs, slab_refs, send_sems, recv_sems, loc_sems):
                copy.wait()

    hbm = pl.BlockSpec(memory_space=ANY)

    def tok(width):
        return pl.BlockSpec((None, tm, width), lambda b, s: (b, nt - 1 - s, 0))

    def const(shape):
        return pl.BlockSpec(shape, lambda b, s: (0,) * len(shape))

    mod_spec = pl.BlockSpec((None, 6, d_model), lambda b, s: (b, 0, 0))
    out_shape = (
        jax.ShapeDtypeStruct((bsz, seq, d_model), F32),
        jax.ShapeDtypeStruct((bsz, seq, d_in), BF16),
        jax.ShapeDtypeStruct((bsz, seq, d_model), BF16),
        jax.ShapeDtypeStruct((bsz, SUBLANES, d_model), F32),
        jax.ShapeDtypeStruct((HALO + SUBLANES, c_half), F32),
    ) + tuple(jax.ShapeDtypeStruct(p.shape, p.dtype) for p in partials)
    outs = pl.pallas_call(
        body, name="mixer_bwd", out_shape=out_shape, grid=(bsz, nt),
        in_specs=[tok(d_model), tok(d_model), tok(d_in), tok(c_half), tok(c_half), tok(d_model), mod_spec,
                  const(win_t.shape), const(wout.shape), const(cw.shape), const(cp.shape)] + [hbm] * n_part,
        out_specs=(tok(d_model), tok(d_in), tok(d_model),
                   pl.BlockSpec((None, SUBLANES, d_model), lambda b, s: (b, 0, 0)),
                   const((HALO + SUBLANES, c_half))) + (hbm,) * n_part,
        scratch_shapes=[
            pltpu.VMEM((tm + HALO, c_half), F32), pltpu.VMEM((tm + SHORT_HALO, c_half), F32),
            pltpu.VMEM((tm, c_half), F32), pltpu.VMEM((tm, c_half), F32),
            pltpu.VMEM((HALO * SUBLANES, c_half), F32), pltpu.VMEM((SUBLANES, c_half), F32),
            pltpu.VMEM((SUBLANES - 1, _shifted_rows_count(tm), c_half), F32),
        ] + _exchange_sems(n_part),
        compiler_params=pltpu.CompilerParams(
            dimension_semantics=("arbitrary", "arbitrary"), vmem_limit_bytes=V7X_VMEM_LIMIT),
    )(dx1, x, proj, a1, cv, y1, mod, win_t, wout, cw, cp, *partials)
    return outs[:5], outs[5:]


def _largest_divisor(n, cap, multiple):
    best = None
    for cand in range(multiple, min(n, cap) + 1, multiple):
        if n % cand == 0:
            best = cand
    return best if best is not None else n


WGRAD_TOKENS_PER_STEP = 4096
WGRAD_COLS_PER_STEP = 1024


def _wgrad_call(a, b, name, owner_cols=None):
    tokens, m_dim = a.shape
    n_dim = b.shape[1]
    bk = _largest_divisor(tokens, WGRAD_TOKENS_PER_STEP, 128)
    n_k = tokens // bk
    if owner_cols is None:
        bm = _largest_divisor(m_dim, 1024, m_dim // N_DEV)
        bn = n_dim
        owners = 1
        out_shape = jax.ShapeDtypeStruct((m_dim, n_dim), BF16)
        out_spec = pl.BlockSpec((bm, bn), lambda i, j, k: (i, j))
    else:
        bm = m_dim
        bn = _largest_divisor(n_dim, WGRAD_COLS_PER_STEP, owner_cols)
        owners = bn // owner_cols
        out_shape = jax.ShapeDtypeStruct((n_dim // owner_cols, m_dim, owner_cols), BF16)
        out_spec = pl.BlockSpec((owners, bm, owner_cols), lambda i, j, k: (j, i, 0))

    def body(a_ref, b_ref, o_ref, acc):
        k = pl.program_id(2)

        @pl.when(k == 0)
        def _():
            acc[...] = jnp.zeros(acc.shape, F32)

        acc[...] += _tn_dot(a_ref[...], b_ref[...])

        @pl.when(k == n_k - 1)
        def _():
            if owner_cols is None:
                o_ref[...] = acc[...].astype(BF16)
            else:
                for q in range(owners):
                    o_ref[q] = acc[:, q * owner_cols:(q + 1) * owner_cols].astype(BF16)

    out = pl.pallas_call(
        body, name=name, out_shape=out_shape, grid=(m_dim // bm, n_dim // bn, n_k),
        in_specs=[pl.BlockSpec((bk, bm), lambda i, j, k: (k, i)), pl.BlockSpec((bk, bn), lambda i, j, k: (k, j))],
        out_specs=out_spec, scratch_shapes=[pltpu.VMEM((bm, bn), F32)],
        compiler_params=pltpu.CompilerParams(
            dimension_semantics=("arbitrary", "arbitrary", "arbitrary"), vmem_limit_bytes=V7X_VMEM_LIMIT),
    )(a, b)
    if owner_cols is None:
        out = out.reshape(N_DEV, m_dim // N_DEV, n_dim)
    return out


def _tail_scatter_call(partial, small):
    _, rows, cols = partial.shape
    n_chips = N_DEV // 2

    def body(g_ref, small_ref, out_ref, small_all, from_sibling, pair,
             p1_send, p1_recv, p2_send, p2_recv, s_send, s_recv, s_loc):
        x, y, c = _coords()
        sibling = (x, y, 1 - c)
        my_chip = 2 * x + y
        chips = [(1 - x, y), (x, 1 - y), (1 - x, 1 - y)]

        def remote(src, dst, send_sem, recv_sem, to):
            return pltpu.make_async_remote_copy(src_ref=src, dst_ref=dst, send_sem=send_sem, recv_sem=recv_sem,
                                                device_id=to, device_id_type=MESH)

        gather = _TwoLevelGather([small_ref], [small_all], s_send, s_recv, s_loc)
        gather.start()
        first = [remote(g_ref.at[2 * q + (1 - c)], from_sibling.at[q], p1_send.at[q], p1_recv.at[q], sibling)
                 for q in range(n_chips)]
        for cp in first:
            cp.start()
        for cp in first:
            cp.wait()
        for q in range(n_chips):
            pair[q] = (g_ref[2 * q + c].astype(F32) + from_sibling[q].astype(F32)).astype(BF16)
        second = [remote(pair.at[2 * cx + cy], out_ref.at[my_chip], p2_send.at[j], p2_recv.at[j], (cx, cy, c))
                  for j, (cx, cy) in enumerate(chips)]
        for cp in second:
            cp.start()
        out_ref[my_chip] = pair[my_chip]
        gather.forward()
        for cp in second:
            cp.wait()
        gather.finish()

    vm = pl.BlockSpec(memory_space=VMEM)
    slabs = pltpu.VMEM((n_chips, rows, cols), BF16)
    return pl.pallas_call(
        body, name="scatter_tail",
        out_shape=(jax.ShapeDtypeStruct((n_chips, rows, cols), BF16), jax.ShapeDtypeStruct((N_DEV,) + small.shape, F32)),
        in_specs=[vm, vm], out_specs=(vm, vm),
        scratch_shapes=[slabs, slabs,
                        pltpu.SemaphoreType.DMA((n_chips,)), pltpu.SemaphoreType.DMA((n_chips,)),
                        pltpu.SemaphoreType.DMA((n_chips - 1,)), pltpu.SemaphoreType.DMA((n_chips - 1,))]
        + _exchange_sems(1),
        compiler_params=pltpu.CompilerParams(vmem_limit_bytes=V7X_VMEM_LIMIT),
    )(partial, small)


def _adamw(w, g, m, v):
    m2 = ADAM_B1 * m + (1.0 - ADAM_B1) * g
    v2 = ADAM_B2 * v + (1.0 - ADAM_B2) * (g * g)
    m_hat = m2 / (1.0 - ADAM_B1 ** ADAM_STEP)
    v_hat = v2 / (1.0 - ADAM_B2 ** ADAM_STEP)
    delta = -ADAM_LR * (m_hat / (jnp.sqrt(v_hat) + ADAM_EPS) + ADAM_WD * w)
    return delta, m2, v2


def _adam_slabs_call(slabs, w, m, v, name):
    rows, cols = w.shape
    n_slabs = slabs.shape[0]
    tr = _largest_divisor(rows, 256, 2 * SUBLANES)

    def body(s_ref, w_ref, m_ref, v_ref, g_ref, d_ref, m2_ref, v2_ref):
        g = s_ref[0].astype(F32)
        for k in range(1, n_slabs):
            g = g + s_ref[k].astype(F32)
        delta, m2, v2 = _adamw(w_ref[...], g, m_ref[...], v_ref[...])
        g_ref[...] = g
        d_ref[...] = delta
        m2_ref[...] = m2
        v2_ref[...] = v2

    tile = pl.BlockSpec((tr, cols), lambda i: (i, 0))
    shp = jax.ShapeDtypeStruct((rows, cols), F32)
    return pl.pallas_call(
        body, name=name, out_shape=(shp, shp, shp, shp), grid=(rows // tr,),
        in_specs=[pl.BlockSpec((n_slabs, tr, cols), lambda i: (0, i, 0)), tile, tile, tile],
        out_specs=(tile, tile, tile, tile),
        compiler_params=pltpu.CompilerParams(dimension_semantics=("arbitrary",), vmem_limit_bytes=V7X_VMEM_LIMIT),
    )(slabs, w, m, v)


def _adam_ada_call(c_rows, dmod_cols, w, m, v):
    rows, cols = w.shape
    n_rows = c_rows.shape[0]
    tr = _largest_divisor(rows, 256, 128)

    def body(c_ref, dm_ref, w_ref, m_ref, v_ref, g_ref, d_ref, m2_ref, v2_ref):
        cv = c_ref[...]
        c_act = (cv * _sigmoid(cv)).astype(BF16)
        g = _tn_dot(c_act, dm_ref[...].astype(BF16))
        delta, m2, v2 = _adamw(w_ref[...], g, m_ref[...], v_ref[...])
        g_ref[...] = g
        d_ref[...] = delta
        m2_ref[...] = m2
        v2_ref[...] = v2

    tile = pl.BlockSpec((tr, cols), lambda i: (i, 0))
    shp = jax.ShapeDtypeStruct((rows, cols), F32)
    return pl.pallas_call(
        body, name="adam_w_ada", out_shape=(shp, shp, shp, shp), grid=(rows // tr,),
        in_specs=[pl.BlockSpec((n_rows, tr), lambda i: (0, i)), pl.BlockSpec((n_rows, cols), lambda i: (0, 0)),
                  tile, tile, tile],
        out_specs=(tile, tile, tile, tile),
        compiler_params=pltpu.CompilerParams(dimension_semantics=("arbitrary",), vmem_limit_bytes=V7X_VMEM_LIMIT),
    )(c_rows, dmod_cols, w, m, v)


def _small_sum_call(small_all, n_grad_rows, loss_rows, bias_rows, loss_scale):
    _, rows, width = small_all.shape
    lo, hi = loss_rows
    b0, b1, b2 = bias_rows
    nb = b1 - b0

    def body(s_ref, sum_ref, extra_ref):
        tot = s_ref[0]
        for k in range(1, N_DEV):
            tot = tot + s_ref[k]
        sum_ref[...] = tot[0:n_grad_rows, :]
        extra_ref[0:nb, :] = tot[b0:b1, :] + tot[b1:b2, :]
        head = tot[n_grad_rows - 2 * SUBLANES:n_grad_rows, :]
        rows_id = lax.broadcasted_iota(jnp.int32, head.shape, 0) + (n_grad_rows - 2 * SUBLANES)
        sq = jnp.where(jnp.logical_and(rows_id >= lo, rows_id < hi), head, 0.0)
        extra_ref[nb:nb + SUBLANES, :] = jnp.zeros((SUBLANES, width), F32) + jnp.sum(sq) * loss_scale

    vm = pl.BlockSpec(memory_space=VMEM)
    return pl.pallas_call(
        body, name="small_sum",
        out_shape=(jax.ShapeDtypeStruct((n_grad_rows, width), F32), jax.ShapeDtypeStruct((nb + SUBLANES, width), F32)),
        in_specs=[vm], out_specs=(vm, vm),
    )(small_all)


def _adam_small_call(ws, gs, ms, vs):
    n = len(ws)

    def body(*refs):
        w_refs, g_refs, m_refs, v_refs = refs[:n], refs[n:2 * n], refs[2 * n:3 * n], refs[3 * n:4 * n]
        d_refs, m2_refs, v2_refs = refs[4 * n:5 * n], refs[5 * n:6 * n], refs[6 * n:7 * n]
        for i in range(n):
            delta, m2, v2 = _adamw(w_refs[i][...], g_refs[i][...], m_refs[i][...], v_refs[i][...])
            d_refs[i][...] = delta
            m2_refs[i][...] = m2
            v2_refs[i][...] = v2

    vm = pl.BlockSpec(memory_space=VMEM)
    shapes = tuple(jax.ShapeDtypeStruct(w.shape, F32) for w in ws)
    outs = pl.pallas_call(body, name="adam_small", out_shape=shapes * 3,
                          in_specs=[vm] * (4 * n), out_specs=(vm,) * (3 * n))(*ws, *gs, *ms, *vs)
    return outs[:n], outs[n:2 * n], outs[2 * n:]


def kernel(x, c, w_ada, b_ada, w_in, conf_dw_w, conf_dw_b, conf_ln_g, conf_ln_b, sc_conv_w, w_out, w_mlp1, w_mlp2, g_final, loss_target, m_w_ada, m_b_ada, m_w_in, m_conf_dw_w, m_conf_dw_b, m_conf_ln_g, m_conf_ln_b, m_sc_conv_w, m_w_out, m_w_mlp1, m_w_mlp2, m_g_final, v_w_ada, v_b_ada, v_w_in, v_conf_dw_w, v_conf_dw_b, v_conf_ln_g, v_conf_ln_b, v_sc_conv_w, v_w_out, v_w_mlp1, v_w_mlp2, v_g_final):
    bsz, seq, d_model = x.shape
    c_half = conf_dw_b.shape[-1]
    n_taps = conf_dw_w.shape[1]
    cc = conf_dw_w.shape[-1]
    a_cols = w_ada.shape[-1]
    tokens = bsz * seq
    me = _dev_index()

    c_pad = jnp.pad(c, ((0, SUBLANES - bsz), (0, 0)))
    b_ada_loc = lax.dynamic_slice(b_ada, (0, me * a_cols), (1, a_cols))
    small_loc = jnp.zeros((HALO, 128), F32)
    small_loc = small_loc.at[:n_taps, :cc].set(conf_dw_w[0]).at[:3, cc:2 * cc].set(sc_conv_w[0])
    (win_t, wout_all), small_all, c_all, mod_rows = _gather_call(
        c_pad, w_ada[0], b_ada_loc, small_loc, [w_in[0].T.astype(BF16), w_out[0].astype(BF16)])
    cw = small_all[:, :, :cc].transpose(1, 0, 2).reshape(HALO, c_half)
    scw = small_all[:, :3, cc:2 * cc].transpose(1, 0, 2).reshape(3, c_half)
    cp = jnp.concatenate([conf_dw_b, conf_ln_g, conf_ln_b, scw, jnp.zeros((2, c_half), F32)], axis=0)
    mod = mod_rows[:, :bsz, :].transpose(1, 0, 2).reshape(bsz, 6, d_model)

    flat = lambda t: t.reshape(tokens, t.shape[-1])
    (proj, a1, cv, mixed, y1, x1), (w1_all, w2_all) = _mixer_fwd_call(
        x, mod, win_t, wout_all, cw, cp, [w_mlp1[0].astype(BF16), w_mlp2[0].astype(BF16)])
    dx1, h2, dy2, u, dz, dmod2, head, dy1 = _mlp_call(
        x1, loss_target, mod, w1_all, w2_all, g_final.reshape(1, d_model))
    g_w1 = _wgrad_call(flat(h2), flat(dz), "wgrad_mlp1", owner_cols=w_mlp1.shape[-1])
    g_w2 = _wgrad_call(flat(u), flat(dy2), "wgrad_mlp2")
    g_out = _wgrad_call(flat(mixed), flat(dy1), "wgrad_out")
    (grad_x, dproj, h1, dmod1, cgrad), (s_w1, s_w2, s_out) = _mixer_bwd_call(
        dx1, x, proj, a1, cv, y1, mod, win_t, wout_all, cw, cp, [g_w1, g_w2, g_out])
    g_in_t = _wgrad_call(flat(dproj), flat(h1), "wgrad_in")

    dmod = jnp.concatenate([dmod1[:, :3, :], dmod2[:, :3, :]], axis=1)
    n_cg = cgrad.shape[0]
    per_b = 6 * d_model // c_half
    per_b_pad = -(-per_b // SUBLANES) * SUBLANES
    dmod_rows = jnp.pad(dmod.reshape(bsz, per_b, c_half), ((0, 0), (0, per_b_pad - per_b), (0, 0)))
    small = jnp.concatenate([
        cgrad,
        head.reshape(2 * SUBLANES, c_half),
        dmod_rows.reshape(bsz * per_b_pad, c_half),
    ], axis=0)
    s_in, gathered = _tail_scatter_call(g_in_t, small)

    n_head = n_cg + 2 * SUBLANES
    sums, extra = _small_sum_call(
        gathered, n_head, (n_cg + 2, n_cg + 4), (n_head, n_head + per_b_pad, n_head + 2 * per_b_pad), 0.5 / d_model)
    loss = extra[per_b_pad, 0]
    g_b_ada = extra[:per_b].reshape(1, 6 * d_model)
    g_dw_w = lax.dynamic_slice(sums[:n_taps], (0, me * cc), (n_taps, cc))
    g_sc_w = lax.dynamic_slice(sums[HALO:HALO + 3], (0, me * cc), (3, cc))
    g_dw_b, g_ln_g, g_ln_b = sums[HALO + 3:HALO + 4], sums[HALO + 4:HALO + 5], sums[HALO + 5:HALO + 6]
    g_gf = sums[n_cg:n_cg + 2].reshape(1, d_model)

    dmod_all = gathered[:, n_head:, :].reshape(N_DEV, bsz, per_b_pad, c_half)[:, :, :per_b, :]
    dmod_all = dmod_all.reshape(N_DEV, bsz, 6 * d_model)
    dmod_cols = lax.dynamic_slice(dmod_all, (0, 0, me * a_cols), (N_DEV, bsz, a_cols))
    dmod_cols = jnp.pad(dmod_cols, ((0, 0), (0, SUBLANES - bsz), (0, 0))).reshape(N_DEV * SUBLANES, a_cols)
    c_rows = c_all.reshape(N_DEV * SUBLANES, d_model)
    g_ada, d_ada, m_ada, v_ada = _adam_ada_call(c_rows, dmod_cols, w_ada[0], m_w_ada[0], v_w_ada[0])

    gi, di, mi, vi = _adam_slabs_call(s_in, w_in[0].T, m_w_in[0].T, v_w_in[0].T, "adam_w_in")
    gi, di, mi, vi = gi.T, di.T, mi.T, vi.T
    go, do, mo, vo = _adam_slabs_call(s_out, w_out[0], m_w_out[0], v_w_out[0], "adam_w_out")
    g1, d1, m1, v1 = _adam_slabs_call(s_w1, w_mlp1[0], m_w_mlp1[0], v_w_mlp1[0], "adam_w_mlp1")
    g2, d2, m2, v2 = _adam_slabs_call(s_w2, w_mlp2[0], m_w_mlp2[0], v_w_mlp2[0], "adam_w_mlp2")

    small_like = [b_ada, conf_dw_w, conf_dw_b, conf_ln_g, conf_ln_b, sc_conv_w, g_final]
    two_d = lambda t: t.reshape(-1, t.shape[-1])
    small_g = [g_b_ada, g_dw_w, g_dw_b, g_ln_g, g_ln_b, g_sc_w, g_gf]
    sd, sm, sv = _adam_small_call(
        [two_d(t) for t in small_like], small_g,
        [two_d(t) for t in (m_b_ada, m_conf_dw_w, m_conf_dw_b, m_conf_ln_g, m_conf_ln_b, m_sc_conv_w, m_g_final)],
        [two_d(t) for t in (v_b_ada, v_conf_dw_w, v_conf_dw_b, v_conf_ln_g, v_conf_ln_b, v_sc_conv_w, v_g_final)])
    like = lambda parts: [p.reshape(w.shape) for p, w in zip(parts, small_like)]
    sg, sd, sm, sv = like(small_g), like(sd), like(sm), like(sv)

    def ordered(ada, small_list, w_in_, w_out_, w1_, w2_):
        b_ada_, dw_w_, dw_b_, ln_g_, ln_b_, sc_w_, gf_ = small_list
        return [ada[None], b_ada_, w_in_[None], dw_w_, dw_b_, ln_g_, ln_b_, sc_w_, w_out_[None], w1_[None], w2_[None], gf_]

    grads = ordered(g_ada, sg, gi, go, g1, g2)
    deltas = ordered(d_ada, sd, di, do, d1, d2)
    new_m = ordered(m_ada, sm, mi, mo, m1, m2)
    new_v = ordered(v_ada, sv, vi, vo, v1, v2)
    return (loss, grad_x, *grads, *deltas, *new_m, *new_v)
```

```python
import functools

import jax
import jax.numpy as jnp
from jax import lax
from jax.experimental import pallas as pl
from jax.experimental.pallas import tpu as pltpu

N_DEV = 8
RMS_EPS = 1e-6
ADAM_LR = 0.001
ADAM_B1 = 0.9
ADAM_B2 = 0.999
ADAM_EPS = 1e-08
ADAM_WD = 0.01
ADAM_STEP = 10

F32 = jnp.float32
BF16 = jnp.bfloat16
MESH = pl.DeviceIdType.MESH
VMEM = pltpu.VMEM
ANY = pl.ANY

HALO = 32
SHORT_HALO = 8
ROW_CHUNK = 32
SUBLANES = 8
V7X_VMEM_LIMIT = 56 * 1024 * 1024
MLP_VMEM_LIMIT = 48 * 1024 * 1024
MLP_TOKEN_TILE = 256
MIXER_FWD_TOKEN_TILE = 512


def _coords():
    return lax.axis_index("x"), lax.axis_index("y"), lax.axis_index("c")


def _dev_index():
    x, y, c = _coords()
    return 4 * x + 2 * y + c


def _peer(k):
    x, y, c = _coords()
    px = 1 - x if (k >> 2) & 1 else x
    py = 1 - y if (k >> 1) & 1 else y
    pc = 1 - c if k & 1 else c
    return (px, py, pc), 4 * px + 2 * py + pc


def _sigmoid(v):
    return jax.nn.sigmoid(v)


def _nt_dot(a, b):
    return lax.dot_general(a, b, (((1,), (1,)), ((), ())), preferred_element_type=F32)


def _nn_dot(a, b):
    return jnp.dot(a, b, preferred_element_type=F32)


def _tn_dot(a, b):
    return lax.dot_general(a, b, (((0,), (0,)), ((), ())), preferred_element_type=F32)


def _token_tile(seq):
    return 256 if seq % 256 == 0 else 64


class _TwoLevelGather:
    def __init__(self, srcs, dsts, send_sems, recv_sems, loc_sems):
        x, y, c = _coords()
        me = 4 * x + 2 * y + c
        sibling = (x, y, 1 - c)
        chips = [(1 - x, y), (x, 1 - y), (1 - x, 1 - y)]

        def remote(src, dst, a, col, to):
            return pltpu.make_async_remote_copy(
                src_ref=src, dst_ref=dst, send_sem=send_sems.at[a, col], recv_sem=recv_sems.at[a, col],
                device_id=to, device_id_type=MESH)

        self.local, self.to_sibling, self.ici, self.forwards = [], [], [], []
        for a, (src, dst) in enumerate(zip(srcs, dsts)):
            self.local.append(pltpu.make_async_copy(src, dst.at[me], loc_sems.at[a]))
            self.to_sibling.append(remote(src, dst.at[me], a, 0, sibling))
            for j, (cx, cy) in enumerate(chips):
                self.ici.append(remote(src, dst.at[me], a, 1 + j, (cx, cy, c)))
                landed = dst.at[4 * cx + 2 * cy + c]
                self.forwards.append(remote(landed, landed, a, 4 + j, sibling))

    def start(self):
        for cp in self.local + self.to_sibling + self.ici:
            cp.start()

    def forward(self):
        for arrival, onward in zip(self.ici, self.forwards):
            arrival.wait_recv()
            onward.start()

    def finish(self):
        for cp in self.ici:
            cp.wait_send()
        for cp in self.local + self.to_sibling + self.forwards:
            cp.wait()


def _scatter_copies(srcs, dsts, send_sems, recv_sems, loc_sems):
    me = _dev_index()
    copies = []
    for a, (src, dst) in enumerate(zip(srcs, dsts)):
        copies.append(pltpu.make_async_copy(src.at[me], dst.at[me], loc_sems.at[a]))
        for k in range(1, N_DEV):
            peer, pidx = _peer(k)
            copies.append(pltpu.make_async_remote_copy(
                src_ref=src.at[pidx], dst_ref=dst.at[me], send_sem=send_sems.at[a, k - 1],
                recv_sem=recv_sems.at[a, k - 1], device_id=peer, device_id_type=MESH))
    return copies


def _exchange_sems(n_arrays):
    return [pltpu.SemaphoreType.DMA((n_arrays, N_DEV - 1)), pltpu.SemaphoreType.DMA((n_arrays, N_DEV - 1)),
            pltpu.SemaphoreType.DMA((n_arrays,))]


def _gather_call(c_pad, w_ada, b_ada_loc, small_loc, big_shards):
    n_big = len(big_shards)
    d_model = c_pad.shape[1]
    a_cols = w_ada.shape[1]

    def body(c_ref, wada_ref, bada_ref, small_ref, *rest):
        big_in = rest[:n_big]
        big_out = rest[n_big:2 * n_big]
        small_all, c_all, mod_rows = rest[2 * n_big:2 * n_big + 3]
        modcols, big_send, big_recv, loc_sem, s_send, s_recv = rest[2 * n_big + 3:]
        me = _dev_index()
        big = _TwoLevelGather(big_in, big_out, big_send, big_recv, loc_sem)
        big.start()

        small_all[me] = small_ref[...]
        c_all[me] = c_ref[...]
        first = []
        for k in range(1, N_DEV):
            peer, _ = _peer(k)
            for i, (src, dst) in enumerate(((small_ref, small_all), (c_ref, c_all))):
                cp = pltpu.make_async_remote_copy(
                    src_ref=src, dst_ref=dst.at[me],
                    send_sem=s_send.at[i, k - 1], recv_sem=s_recv.at[i, k - 1],
                    device_id=peer, device_id_type=MESH)
                cp.start()
                first.append(cp)
        for cp in first:
            cp.wait()

        c_rows = c_all[...].reshape(N_DEV * SUBLANES, d_model)
        c_act = c_rows * _sigmoid(c_rows)
        modcols[...] = _nn_dot(c_act.astype(BF16), wada_ref[...].astype(BF16)) + bada_ref[...]
        mod_rows[me] = modcols[pl.ds(pl.multiple_of(me * SUBLANES, SUBLANES), SUBLANES), :]
        second = []
        for k in range(1, N_DEV):
            peer, pidx = _peer(k)
            cp = pltpu.make_async_remote_copy(
                src_ref=modcols.at[pl.ds(pl.multiple_of(pidx * SUBLANES, SUBLANES), SUBLANES), :],
                dst_ref=mod_rows.at[me],
                send_sem=s_send.at[2, k - 1], recv_sem=s_recv.at[2, k - 1],
                device_id=peer, device_id_type=MESH)
            cp.start()
            second.append(cp)
        big.forward()
        for cp in second:
            cp.wait()
        big.finish()

    out_shape = tuple(jax.ShapeDtypeStruct((N_DEV,) + s.shape, s.dtype) for s in big_shards) + (
        jax.ShapeDtypeStruct((N_DEV,) + small_loc.shape, F32),
        jax.ShapeDtypeStruct((N_DEV, SUBLANES, d_model), F32),
        jax.ShapeDtypeStruct((N_DEV, SUBLANES, a_cols), F32),
    )
    vm = pl.BlockSpec(memory_space=VMEM)
    hbm = pl.BlockSpec(memory_space=ANY)
    outs = pl.pallas_call(
        body, name="gather_weights_mod", out_shape=out_shape,
        in_specs=[vm, vm, vm, vm] + [hbm] * n_big,
        out_specs=tuple([hbm] * n_big + [vm, vm, vm]),
        scratch_shapes=[
            pltpu.VMEM((N_DEV * SUBLANES, a_cols), F32),
            pltpu.SemaphoreType.DMA((n_big, N_DEV - 1)),
            pltpu.SemaphoreType.DMA((n_big, N_DEV - 1)),
            pltpu.SemaphoreType.DMA((n_big,)),
            pltpu.SemaphoreType.DMA((3, N_DEV - 1)),
            pltpu.SemaphoreType.DMA((3, N_DEV - 1)),
        ],
        compiler_params=pltpu.CompilerParams(vmem_limit_bytes=V7X_VMEM_LIMIT),
    )(c_pad, w_ada, b_ada_loc, small_loc, *big_shards)
    return outs[:n_big], outs[n_big], outs[n_big + 1], outs[n_big + 2]


def _shifted_rows_count(tm):
    return tm + HALO - SUBLANES


def _fill_shifted(ext, shifted, tm):
    for s in range(1, SUBLANES):
        shifted[s - 1] = ext[s:s + _shifted_rows_count(tm), :]


def _shifted_rows(ext, shifted, start, rows):
    phase = start % SUBLANES
    aligned = start - phase
    if phase == 0:
        return ext[aligned:aligned + rows, :]
    return shifted[phase - 1, aligned:aligned + rows, :]


def _layer_norm_parts(a1):
    mu = jnp.mean(a1, axis=-1, keepdims=True)
    xc = a1 - mu
    rstd = lax.rsqrt(jnp.mean(xc * xc, axis=-1, keepdims=True) + RMS_EPS)
    return xc * rstd, rstd


def _mixer_fwd_call(x, mod, win_t, wout, cw, cp, later_shards):
    n_later = len(later_shards)
    bsz, seq, d_model = x.shape
    c_half = cw.shape[1]
    n_taps = 31
    d_in = win_t.shape[0] * win_t.shape[1]
    tm = MIXER_FWD_TOKEN_TILE if seq % MIXER_FWD_TOKEN_TILE == 0 else _token_tile(seq)
    nt = seq // tm
    forward_step = (2 * bsz * nt) // 3

    def body(x_ref, mod_ref, win_ref, wout_ref, cw_ref, cp_ref, *rest):
        shard_refs, rest = rest[:n_later], rest[n_later:]
        proj_ref, a1_ref, cv_ref, mixed_ref, y1_ref = rest[:5]
        gathered_refs, rest = rest[5:5 + n_later], rest[5 + n_later:]
        aext, qext, ashift, send_sems, recv_sems, loc_sems = rest
        b, t = pl.program_id(0), pl.program_id(1)

        step = b * nt + t

        @pl.when(step == 0)
        def _():
            _TwoLevelGather(shard_refs, gathered_refs, send_sems, recv_sems, loc_sems).start()

        @pl.when(step == forward_step)
        def _():
            _TwoLevelGather(shard_refs, gathered_refs, send_sems, recv_sems, loc_sems).forward()

        xv = x_ref[...]
        sh1, sc1 = mod_ref[0:1, :], mod_ref[1:2, :]
        r1 = lax.rsqrt(jnp.mean(xv * xv, axis=-1, keepdims=True) + RMS_EPS)
        h1 = (xv * r1) * (1.0 + sc1) + sh1
        proj = _nt_dot(h1.astype(BF16), win_ref[...].reshape(d_in, d_model))
        proj_ref[...] = proj
        val, gate = proj[:, 0:c_half], proj[:, c_half:2 * c_half]
        s_b, s_c, s_h = proj[:, 2 * c_half:3 * c_half], proj[:, 3 * c_half:4 * c_half], proj[:, 4 * c_half:5 * c_half]

        @pl.when(t == 0)
        def _():
            aext[0:HALO, :] = jnp.zeros((HALO, c_half), F32)
            qext[0:SHORT_HALO, :] = jnp.zeros((SHORT_HALO, c_half), F32)

        @pl.when(t > 0)
        def _():
            aext[0:HALO, :] = aext[tm:tm + HALO, :]
            qext[0:SHORT_HALO, :] = qext[tm:tm + SHORT_HALO, :]
        aext[HALO:HALO + tm, :] = val * _sigmoid(gate)
        qext[SHORT_HALO:SHORT_HALO + tm, :] = s_c * s_h

        base = HALO - (n_taps - 1)
        _fill_shifted(aext, ashift, tm)
        for r0 in range(0, tm, ROW_CHUNK):
            acc = jnp.zeros((ROW_CHUNK, c_half), F32)
            for k in range(n_taps):
                acc = acc + cw_ref[k:k + 1, :] * _shifted_rows(aext, ashift, r0 + base + k, ROW_CHUNK)
            a1_ref[r0:r0 + ROW_CHUNK, :] = acc + cp_ref[0:1, :]
        sbase = SHORT_HALO - 2
        conv3 = cp_ref[3:4, :] * qext[sbase:sbase + tm, :]
        conv3 = conv3 + cp_ref[4:5, :] * qext[sbase + 1:sbase + 1 + tm, :]
        conv3 = conv3 + cp_ref[5:6, :] * qext[sbase + 2:sbase + 2 + tm, :]
        cv_ref[...] = conv3

        norm, _ = _layer_norm_parts(a1_ref[...])
        a2 = norm * cp_ref[1:2, :] + cp_ref[2:3, :]
        mixed = jnp.concatenate([a2 * _sigmoid(a2), s_b * conv3], axis=-1).astype(BF16)
        mixed_ref[...] = mixed
        y1 = _nn_dot(mixed, wout_ref[...].reshape(d_model, d_model))
        y1_ref[...] = y1

        @pl.when(step == bsz * nt - 1)
        def _():
            _TwoLevelGather(shard_refs, gathered_refs, send_sems, recv_sems, loc_sems).finish()

    hbm = pl.BlockSpec(memory_space=ANY)

    def tok(width):
        return pl.BlockSpec((None, tm, width), lambda b, t: (b, t, 0))

    def const(shape):
        return pl.BlockSpec(shape, lambda b, t: (0,) * len(shape))

    def resident(shape):
        return pl.BlockSpec(shape, lambda b, t: (0,) * len(shape), pipeline_mode=pl.Buffered(1))

    out_shape = (
        jax.ShapeDtypeStruct((bsz, seq, d_in), F32),
        jax.ShapeDtypeStruct((bsz, seq, c_half), F32),
        jax.ShapeDtypeStruct((bsz, seq, c_half), F32),
        jax.ShapeDtypeStruct((bsz, seq, d_model), BF16),
        jax.ShapeDtypeStruct((bsz, seq, d_model), F32),
    ) + tuple(jax.ShapeDtypeStruct((N_DEV,) + s.shape, s.dtype) for s in later_shards)
    outs = pl.pallas_call(
        body, name="mixer_fwd", out_shape=out_shape, grid=(bsz, nt),
        in_specs=[tok(d_model), pl.BlockSpec((None, 6, d_model), lambda b, t: (b, 0, 0)),
                  resident(win_t.shape), resident(wout.shape), const(cw.shape), const(cp.shape)] + [hbm] * n_later,
        out_specs=(tok(d_in), tok(c_half), tok(c_half), tok(d_model), tok(d_model)) + (hbm,) * n_later,
        scratch_shapes=[pltpu.VMEM((tm + HALO, c_half), F32), pltpu.VMEM((tm + SHORT_HALO, c_half), F32),
                        pltpu.VMEM((SUBLANES - 1, _shifted_rows_count(tm), c_half), F32)]
        + _exchange_sems(n_later),
        compiler_params=pltpu.CompilerParams(
            dimension_semantics=("arbitrary", "arbitrary"), vmem_limit_bytes=V7X_VMEM_LIMIT),
    )(x, mod, win_t, wout, cw, cp, *later_shards)
    return outs[:5], outs[5:]


def _mlp_call(x, y1, target, mod, w1, w2, g_final):
    bsz, seq, d_model = x.shape
    n_blk, _, f_blk = w1.shape
    d_ff = n_blk * f_blk
    tm = MLP_TOKEN_TILE if seq % MLP_TOKEN_TILE == 0 else _token_tile(seq)
    nt = seq // tm

    def body(x_ref, y1_ref, tgt_ref, mod_ref, w1_ref, w2_ref, gf_ref,
             dx1_ref, h2_ref, dy2_ref, u_ref, dz_ref, dmod_ref, head_ref, dy1_ref, relu_scr):
        b, t = pl.program_id(0), pl.program_id(1)
        x1v = x_ref[...] + mod_ref[2:3, :] * y1_ref[...]
        sh2, sc2, g2 = mod_ref[3:4, :], mod_ref[4:5, :], mod_ref[5:6, :]
        gf = gf_ref[...]
        r2 = lax.rsqrt(jnp.mean(x1v * x1v, axis=-1, keepdims=True) + RMS_EPS)
        xn2 = x1v * r2
        h2 = (xn2 * (1.0 + sc2) + sh2).astype(BF16)
        h2_ref[...] = h2
        y2 = jnp.zeros((tm, d_model), F32)
        for j in range(n_blk):
            cols = slice(j * f_blk, (j + 1) * f_blk)
            rz = jnp.maximum(_nn_dot(h2, w1_ref[j]), 0.0)
            relu_scr[:, cols] = rz
            ub = (rz * rz).astype(BF16)
            u_ref[:, cols] = ub
            y2 = y2 + _nn_dot(ub, w2_ref[j])
        x2 = x1v + g2 * y2
        r3 = lax.rsqrt(jnp.mean(x2 * x2, axis=-1, keepdims=True) + RMS_EPS)
        xn3 = x2 * r3
        diff = xn3 * gf - tgt_ref[...]
        dout = diff * (1.0 / d_model)

        @pl.when(jnp.logical_and(b == 0, t == 0))
        def _():
            head_ref[...] = jnp.zeros(head_ref.shape, F32)

        @pl.when(t == 0)
        def _():
            dmod_ref[...] = jnp.zeros(dmod_ref.shape, F32)

        head_ref[0:1, :] += jnp.sum(dout * xn3, axis=0, keepdims=True)
        head_ref[1:2, :] += jnp.sum(diff * diff, axis=0, keepdims=True)
        dxn3 = dout * gf
        dx2 = r3 * (dxn3 - xn3 * jnp.mean(dxn3 * xn3, axis=-1, keepdims=True))
        dmod_ref[2:3, :] += jnp.sum(dx2 * y2, axis=0, keepdims=True)
        dy2 = (g2 * dx2).astype(BF16)
        dy2_ref[...] = dy2
        dh2 = jnp.zeros((tm, d_model), F32)
        for j in range(n_blk):
            cols = slice(j * f_blk, (j + 1) * f_blk)
            dz = (_nt_dot(dy2, w2_ref[j]) * (2.0 * relu_scr[:, cols])).astype(BF16)
            dz_ref[:, cols] = dz
            dh2 = dh2 + _nt_dot(dz, w1_ref[j])
        dmod_ref[0:1, :] += jnp.sum(dh2, axis=0, keepdims=True)
        dmod_ref[1:2, :] += jnp.sum(dh2 * xn2, axis=0, keepdims=True)
        dxn2 = dh2 * (1.0 + sc2)
        dx1 = dx2 + r2 * (dxn2 - xn2 * jnp.mean(dxn2 * xn2, axis=-1, keepdims=True))
        dx1_ref[...] = dx1
        dy1_ref[...] = (mod_ref[2:3, :] * dx1).astype(BF16)
        dmod_ref[3:4, :] += jnp.sum(dx1 * y1_ref[...], axis=0, keepdims=True)

    def tok(width):
        return pl.BlockSpec((None, tm, width), lambda b, t: (b, t, 0))

    def const(shape):
        return pl.BlockSpec(shape, lambda b, t: (0,) * len(shape))

    def resident(shape):
        return pl.BlockSpec(shape, lambda b, t: (0,) * len(shape), pipeline_mode=pl.Buffered(1))

    out_shape = (
        jax.ShapeDtypeStruct((bsz, seq, d_model), F32),
        jax.ShapeDtypeStruct((bsz, seq, d_model), BF16),
        jax.ShapeDtypeStruct((bsz, seq, d_model), BF16),
        jax.ShapeDtypeStruct((bsz, seq, d_ff), BF16),
        jax.ShapeDtypeStruct((bsz, seq, d_ff), BF16),
        jax.ShapeDtypeStruct((bsz, SUBLANES, d_model), F32),
        jax.ShapeDtypeStruct((SUBLANES, d_model), F32),
        jax.ShapeDtypeStruct((bsz, seq, d_model), BF16),
    )
    return pl.pallas_call(
        body, name="mlp_fwd_bwd", out_shape=out_shape, grid=(bsz, nt),
        in_specs=[tok(d_model), tok(d_model), tok(d_model),
                  pl.BlockSpec((None, 6, d_model), lambda b, t: (b, 0, 0)),
                  resident(w1.shape), resident(w2.shape), const(g_final.shape)],
        out_specs=(tok(d_model), tok(d_model), tok(d_model), tok(d_ff), tok(d_ff),
                   pl.BlockSpec((None, SUBLANES, d_model), lambda b, t: (b, 0, 0)),
                   const((SUBLANES, d_model)), tok(d_model)),
        scratch_shapes=[pltpu.VMEM((tm, d_ff), F32)],
        compiler_params=pltpu.CompilerParams(
            dimension_semantics=("arbitrary", "arbitrary"), vmem_limit_bytes=MLP_VMEM_LIMIT),
    )(x, y1, target, mod, w1, w2, g_final)


def _mixer_bwd_call(dx1, x, proj, a1, cv, mod, win_t, wout, cw, cp, partials):
    n_part = len(partials)
    bsz, seq, d_model = x.shape
    c_half = cw.shape[1]
    n_taps = 31
    d_in = win_t.shape[0] * win_t.shape[1]
    tm = _token_tile(seq)
    nt = seq // tm

    def body(dx1_ref, x_ref, proj_ref, a1_ref, cv_ref, mod_ref, win_ref, wout_ref, cw_ref, cp_ref, *rest):
        part_refs, rest = rest[:n_part], rest[n_part:]
        gx_ref, dproj_ref, h1_ref, dmod_ref, cgrad_ref = rest[:5]
        slab_refs, rest = rest[5:5 + n_part], rest[5 + n_part:]
        dext, cext, a0_scr, da0_scr, tap_acc, row_acc, dshift, send_sems, recv_sems, loc_sems = rest
        b, step = pl.program_id(0), pl.program_id(1)
        first = jnp.logical_and(b == 0, step == 0)
        last = jnp.logical_and(b == bsz - 1, step == nt - 1)

        @pl.when(first)
        def _():
            for copy in _scatter_copies(part_refs, slab_refs, send_sems, recv_sems, loc_sems):
                copy.start()

        dx1v = dx1_ref[...]
        xv = x_ref[...]
        sh1, sc1, g1 = mod_ref[0:1, :], mod_ref[1:2, :], mod_ref[2:3, :]

        @pl.when(first)
        def _():
            tap_acc[...] = jnp.zeros(tap_acc.shape, F32)
            row_acc[...] = jnp.zeros(row_acc.shape, F32)

        @pl.when(step == 0)
        def _():
            dmod_ref[...] = jnp.zeros(dmod_ref.shape, F32)
            dext[tm:tm + HALO, :] = jnp.zeros((HALO, c_half), F32)
            cext[tm:tm + SHORT_HALO, :] = jnp.zeros((SHORT_HALO, c_half), F32)

        @pl.when(step > 0)
        def _():
            dext[tm:tm + HALO, :] = dext[0:HALO, :]
            cext[tm:tm + SHORT_HALO, :] = cext[0:SHORT_HALO, :]

        dy1 = (g1 * dx1v).astype(BF16)
        dmixed = _nt_dot(dy1, wout_ref[...].reshape(d_model, d_model))
        d_a, d_s = dmixed[:, 0:c_half], dmixed[:, c_half:2 * c_half]

        val, gate = proj_ref[:, 0:c_half], proj_ref[:, c_half:2 * c_half]
        s_b = proj_ref[:, 2 * c_half:3 * c_half]
        s_c, s_h = proj_ref[:, 3 * c_half:4 * c_half], proj_ref[:, 4 * c_half:5 * c_half]

        d_sb = d_s * cv_ref[...]
        cext[0:tm, :] = d_s * s_b
        q = s_c * s_h
        dq = jnp.zeros((tm, c_half), F32)
        for k in range(3):
            shifted = cext[2 - k:2 - k + tm, :]
            dq = dq + cp_ref[3 + k:4 + k, :] * shifted
            row_acc[k:k + 1, :] += jnp.sum(q * shifted, axis=0, keepdims=True)
        d_sc, d_sh = dq * s_h, dq * s_c

        norm, rstd = _layer_norm_parts(a1_ref[...])
        ln_g = cp_ref[1:2, :]
        a2 = norm * ln_g + cp_ref[2:3, :]
        sg = _sigmoid(a2)
        d_a2 = d_a * (sg * (1.0 + a2 * (1.0 - sg)))
        row_acc[4:5, :] += jnp.sum(d_a2 * norm, axis=0, keepdims=True)
        row_acc[5:6, :] += jnp.sum(d_a2, axis=0, keepdims=True)
        d_n = d_a2 * ln_g
        d_a1 = rstd * (d_n - jnp.mean(d_n, axis=-1, keepdims=True)
                       - norm * jnp.mean(d_n * norm, axis=-1, keepdims=True))
        row_acc[3:4, :] += jnp.sum(d_a1, axis=0, keepdims=True)
        dext[0:tm, :] = d_a1
        sig_g = _sigmoid(gate)
        a0_scr[...] = val * sig_g

        _fill_shifted(dext, dshift, tm)
        for r0 in range(0, tm, ROW_CHUNK):
            a0c = a0_scr[r0:r0 + ROW_CHUNK, :]
            acc = jnp.zeros((ROW_CHUNK, c_half), F32)
            for k in range(n_taps):
                shifted = _shifted_rows(dext, dshift, r0 + (n_taps - 1) - k, ROW_CHUNK)
                acc = acc + cw_ref[k:k + 1, :] * shifted
                prod = a0c * shifted
                part = prod[0:SUBLANES, :]
                for g in range(1, ROW_CHUNK // SUBLANES):
                    part = part + prod[g * SUBLANES:(g + 1) * SUBLANES, :]
                tap_acc[k * SUBLANES:(k + 1) * SUBLANES, :] += part
            da0_scr[r0:r0 + ROW_CHUNK, :] = acc
        d_a0 = da0_scr[...]
        d_val = d_a0 * sig_g
        d_gate = d_a0 * val * sig_g * (1.0 - sig_g)

        dproj = jnp.concatenate([d_val, d_gate, d_sb, d_sc, d_sh], axis=-1).astype(BF16)
        dproj_ref[...] = dproj
        dh1 = _nn_dot(dproj, win_ref[...].reshape(d_in, d_model))
        r1 = lax.rsqrt(jnp.mean(xv * xv, axis=-1, keepdims=True) + RMS_EPS)
        xn1 = xv * r1
        h1_ref[...] = (xn1 * (1.0 + sc1) + sh1).astype(BF16)
        dmod_ref[0:1, :] += jnp.sum(dh1, axis=0, keepdims=True)
        dmod_ref[1:2, :] += jnp.sum(dh1 * xn1, axis=0, keepdims=True)
        dxn1 = dh1 * (1.0 + sc1)
        gx_ref[...] = dx1v + r1 * (dxn1 - xn1 * jnp.mean(dxn1 * xn1, axis=-1, keepdims=True))

        @pl.when(last)
        def _():
            taps = jnp.sum(tap_acc[...].reshape(HALO, SUBLANES, c_half), axis=1)
            cgrad_ref[0:HALO, :] = taps
            cgrad_ref[HALO:HALO + SUBLANES, :] = row_acc[...]
            for copy in _scatter_copies(part_refs, slab_refs, send_sems, recv_sems, loc_sems):
                copy.wait()

    hbm = pl.BlockSpec(memory_space=ANY)

    def tok(width):
        return pl.BlockSpec((None, tm, width), lambda b, s: (b, nt - 1 - s, 0))

    def const(shape):
        return pl.BlockSpec(shape, lambda b, s: (0,) * len(shape))

    mod_spec = pl.BlockSpec((None, 6, d_model), lambda b, s: (b, 0, 0))
    out_shape = (
        jax.ShapeDtypeStruct((bsz, seq, d_model), F32),
        jax.ShapeDtypeStruct((bsz, seq, d_in), BF16),
        jax.ShapeDtypeStruct((bsz, seq, d_model), BF16),
        jax.ShapeDtypeStruct((bsz, SUBLANES, d_model), F32),
        jax.ShapeDtypeStruct((HALO + SUBLANES, c_half), F32),
    ) + tuple(jax.ShapeDtypeStruct(p.shape, p.dtype) for p in partials)
    outs = pl.pallas_call(
        body, name="mixer_bwd", out_shape=out_shape, grid=(bsz, nt),
        in_specs=[tok(d_model), tok(d_model), tok(d_in), tok(c_half), tok(c_half), mod_spec,
                  const(win_t.shape), const(wout.shape), const(cw.shape), const(cp.shape)] + [hbm] * n_part,
        out_specs=(tok(d_model), tok(d_in), tok(d_model),
                   pl.BlockSpec((None, SUBLANES, d_model), lambda b, s: (b, 0, 0)),
                   const((HALO + SUBLANES, c_half))) + (hbm,) * n_part,
        scratch_shapes=[
            pltpu.VMEM((tm + HALO, c_half), F32), pltpu.VMEM((tm + SHORT_HALO, c_half), F32),
            pltpu.VMEM((tm, c_half), F32), pltpu.VMEM((tm, c_half), F32),
            pltpu.VMEM((HALO * SUBLANES, c_half), F32), pltpu.VMEM((SUBLANES, c_half), F32),
            pltpu.VMEM((SUBLANES - 1, _shifted_rows_count(tm), c_half), F32),
        ] + _exchange_sems(n_part),
        compiler_params=pltpu.CompilerParams(
            dimension_semantics=("arbitrary", "arbitrary"), vmem_limit_bytes=V7X_VMEM_LIMIT),
    )(dx1, x, proj, a1, cv, mod, win_t, wout, cw, cp, *partials)
    return outs[:5], outs[5:]


def _largest_divisor(n, cap, multiple):
    best = None
    for cand in range(multiple, min(n, cap) + 1, multiple):
        if n % cand == 0:
            best = cand
    return best if best is not None else n


WGRAD_TOKENS_PER_STEP = 2048
WGRAD_COLS_PER_STEP = 1024


def _wgrad_call(a, b, name, owner_cols=None):
    tokens, m_dim = a.shape
    n_dim = b.shape[1]
    bk = _largest_divisor(tokens, WGRAD_TOKENS_PER_STEP, 128)
    n_k = tokens // bk
    if owner_cols is None:
        bm = _largest_divisor(m_dim, 1024, m_dim // N_DEV)
        bn = n_dim
        owners = 1
        out_shape = jax.ShapeDtypeStruct((m_dim, n_dim), BF16)
        out_spec = pl.BlockSpec((bm, bn), lambda i, j, k: (i, j))
    else:
        bm = m_dim
        bn = _largest_divisor(n_dim, WGRAD_COLS_PER_STEP, owner_cols)
        owners = bn // owner_cols
        out_shape = jax.ShapeDtypeStruct((n_dim // owner_cols, m_dim, owner_cols), BF16)
        out_spec = pl.BlockSpec((owners, bm, owner_cols), lambda i, j, k: (j, i, 0))

    def body(a_ref, b_ref, o_ref, acc):
        k = pl.program_id(2)

        @pl.when(k == 0)
        def _():
            acc[...] = jnp.zeros(acc.shape, F32)

        acc[...] += _tn_dot(a_ref[...], b_ref[...])

        @pl.when(k == n_k - 1)
        def _():
            if owner_cols is None:
                o_ref[...] = acc[...].astype(BF16)
            else:
                for q in range(owners):
                    o_ref[q] = acc[:, q * owner_cols:(q + 1) * owner_cols].astype(BF16)

    out = pl.pallas_call(
        body, name=name, out_shape=out_shape, grid=(m_dim // bm, n_dim // bn, n_k),
        in_specs=[pl.BlockSpec((bk, bm), lambda i, j, k: (k, i)), pl.BlockSpec((bk, bn), lambda i, j, k: (k, j))],
        out_specs=out_spec, scratch_shapes=[pltpu.VMEM((bm, bn), F32)],
        compiler_params=pltpu.CompilerParams(
            dimension_semantics=("arbitrary", "arbitrary", "arbitrary"), vmem_limit_bytes=V7X_VMEM_LIMIT),
    )(a, b)
    if owner_cols is None:
        out = out.reshape(N_DEV, m_dim // N_DEV, n_dim)
    return out


def _tail_scatter_call(partial, small):
    _, rows, cols = partial.shape
    n_chips = N_DEV // 2

    def body(g_ref, small_ref, out_ref, small_all, from_sibling, pair,
             p1_send, p1_recv, p2_send, p2_recv, s_send, s_recv, s_loc):
        x, y, c = _coords()
        sibling = (x, y, 1 - c)
        my_chip = 2 * x + y
        chips = [(1 - x, y), (x, 1 - y), (1 - x, 1 - y)]

        def remote(src, dst, send_sem, recv_sem, to):
            return pltpu.make_async_remote_copy(src_ref=src, dst_ref=dst, send_sem=send_sem, recv_sem=recv_sem,
                                                device_id=to, device_id_type=MESH)

        gather = _TwoLevelGather([small_ref], [small_all], s_send, s_recv, s_loc)
        gather.start()
        first = [remote(g_ref.at[2 * q + (1 - c)], from_sibling.at[q], p1_send.at[q], p1_recv.at[q], sibling)
                 for q in range(n_chips)]
        for cp in first:
            cp.start()
        for cp in first:
            cp.wait()
        for q in range(n_chips):
            pair[q] = (g_ref[2 * q + c].astype(F32) + from_sibling[q].astype(F32)).astype(BF16)
        second = [remote(pair.at[2 * cx + cy], out_ref.at[my_chip], p2_send.at[j], p2_recv.at[j], (cx, cy, c))
                  for j, (cx, cy) in enumerate(chips)]
        for cp in second:
            cp.start()
        out_ref[my_chip] = pair[my_chip]
        gather.forward()
        for cp in second:
            cp.wait()
        gather.finish()

    vm = pl.BlockSpec(memory_space=VMEM)
    slabs = pltpu.VMEM((n_chips, rows, cols), BF16)
    return pl.pallas_call(
        body, name="scatter_tail",
        out_shape=(jax.ShapeDtypeStruct((n_chips, rows, cols), BF16), jax.ShapeDtypeStruct((N_DEV,) + small.shape, F32)),
        in_specs=[vm, vm], out_specs=(vm, vm),
        scratch_shapes=[slabs, slabs,
                        pltpu.SemaphoreType.DMA((n_chips,)), pltpu.SemaphoreType.DMA((n_chips,)),
                        pltpu.SemaphoreType.DMA((n_chips - 1,)), pltpu.SemaphoreType.DMA((n_chips - 1,))]
        + _exchange_sems(1),
        compiler_params=pltpu.CompilerParams(vmem_limit_bytes=V7X_VMEM_LIMIT),
    )(partial, small)


def _adamw(w, g, m, v):
    m2 = ADAM_B1 * m + (1.0 - ADAM_B1) * g
    v2 = ADAM_B2 * v + (1.0 - ADAM_B2) * (g * g)
    m_hat = m2 / (1.0 - ADAM_B1 ** ADAM_STEP)
    v_hat = v2 / (1.0 - ADAM_B2 ** ADAM_STEP)
    delta = -ADAM_LR * (m_hat / (jnp.sqrt(v_hat) + ADAM_EPS) + ADAM_WD * w)
    return delta, m2, v2


def _adam_slabs_call(slabs, w, m, v, name):
    rows, cols = w.shape
    n_slabs = slabs.shape[0]
    tr = _largest_divisor(rows, 256, 2 * SUBLANES)

    def body(s_ref, w_ref, m_ref, v_ref, g_ref, d_ref, m2_ref, v2_ref):
        g = s_ref[0].astype(F32)
        for k in range(1, n_slabs):
            g = g + s_ref[k].astype(F32)
        delta, m2, v2 = _adamw(w_ref[...], g, m_ref[...], v_ref[...])
        g_ref[...] = g
        d_ref[...] = delta
        m2_ref[...] = m2
        v2_ref[...] = v2

    tile = pl.BlockSpec((tr, cols), lambda i: (i, 0))
    shp = jax.ShapeDtypeStruct((rows, cols), F32)
    return pl.pallas_call(
        body, name=name, out_shape=(shp, shp, shp, shp), grid=(rows // tr,),
        in_specs=[pl.BlockSpec((n_slabs, tr, cols), lambda i: (0, i, 0)), tile, tile, tile],
        out_specs=(tile, tile, tile, tile),
        compiler_params=pltpu.CompilerParams(dimension_semantics=("arbitrary",), vmem_limit_bytes=V7X_VMEM_LIMIT),
    )(slabs, w, m, v)


def _adam_ada_call(c_rows, dmod_cols, w, m, v):
    rows, cols = w.shape
    n_rows = c_rows.shape[0]
    tr = _largest_divisor(rows, 256, 128)

    def body(c_ref, dm_ref, w_ref, m_ref, v_ref, g_ref, d_ref, m2_ref, v2_ref):
        cv = c_ref[...]
        c_act = (cv * _sigmoid(cv)).astype(BF16)
        g = _tn_dot(c_act, dm_ref[...].astype(BF16))
        delta, m2, v2 = _adamw(w_ref[...], g, m_ref[...], v_ref[...])
        g_ref[...] = g
        d_ref[...] = delta
        m2_ref[...] = m2
        v2_ref[...] = v2

    tile = pl.BlockSpec((tr, cols), lambda i: (i, 0))
    shp = jax.ShapeDtypeStruct((rows, cols), F32)
    return pl.pallas_call(
        body, name="adam_w_ada", out_shape=(shp, shp, shp, shp), grid=(rows // tr,),
        in_specs=[pl.BlockSpec((n_rows, tr), lambda i: (0, i)), pl.BlockSpec((n_rows, cols), lambda i: (0, 0)),
                  tile, tile, tile],
        out_specs=(tile, tile, tile, tile),
        compiler_params=pltpu.CompilerParams(dimension_semantics=("arbitrary",), vmem_limit_bytes=V7X_VMEM_LIMIT),
    )(c_rows, dmod_cols, w, m, v)


def _small_sum_call(small_all, n_grad_rows, loss_rows, bias_rows, loss_scale):
    _, rows, width = small_all.shape
    lo, hi = loss_rows
    b0, b1, b2 = bias_rows
    nb = b1 - b0

    def body(s_ref, sum_ref, extra_ref):
        tot = s_ref[0]
        for k in range(1, N_DEV):
            tot = tot + s_ref[k]
        sum_ref[...] = tot[0:n_grad_rows, :]
        extra_ref[0:nb, :] = tot[b0:b1, :] + tot[b1:b2, :]
        head = tot[n_grad_rows - 2 * SUBLANES:n_grad_rows, :]
        rows_id = lax.broadcasted_iota(jnp.int32, head.shape, 0) + (n_grad_rows - 2 * SUBLANES)
        sq = jnp.where(jnp.logical_and(rows_id >= lo, rows_id < hi), head, 0.0)
        extra_ref[nb:nb + SUBLANES, :] = jnp.zeros((SUBLANES, width), F32) + jnp.sum(sq) * loss_scale

    vm = pl.BlockSpec(memory_space=VMEM)
    return pl.pallas_call(
        body, name="small_sum",
        out_shape=(jax.ShapeDtypeStruct((n_grad_rows, width), F32), jax.ShapeDtypeStruct((nb + SUBLANES, width), F32)),
        in_specs=[vm], out_specs=(vm, vm),
    )(small_all)


def _adam_small_call(ws, gs, ms, vs):
    n = len(ws)

    def body(*refs):
        w_refs, g_refs, m_refs, v_refs = refs[:n], refs[n:2 * n], refs[2 * n:3 * n], refs[3 * n:4 * n]
        d_refs, m2_refs, v2_refs = refs[4 * n:5 * n], refs[5 * n:6 * n], refs[6 * n:7 * n]
        for i in range(n):
            delta, m2, v2 = _adamw(w_refs[i][...], g_refs[i][...], m_refs[i][...], v_refs[i][...])
            d_refs[i][...] = delta
            m2_refs[i][...] = m2
            v2_refs[i][...] = v2

    vm = pl.BlockSpec(memory_space=VMEM)
    shapes = tuple(jax.ShapeDtypeStruct(w.shape, F32) for w in ws)
    outs = pl.pallas_call(body, name="adam_small", out_shape=shapes * 3,
                          in_specs=[vm] * (4 * n), out_specs=(vm,) * (3 * n))(*ws, *gs, *ms, *vs)
    return outs[:n], outs[n:2 * n], outs[2 * n:]


def kernel(x, c, w_ada, b_ada, w_in, conf_dw_w, conf_dw_b, conf_ln_g, conf_ln_b, sc_conv_w, w_out, w_mlp1, w_mlp2, g_final, loss_target, m_w_ada, m_b_ada, m_w_in, m_conf_dw_w, m_conf_dw_b, m_conf_ln_g, m_conf_ln_b, m_sc_conv_w, m_w_out, m_w_mlp1, m_w_mlp2, m_g_final, v_w_ada, v_b_ada, v_w_in, v_conf_dw_w, v_conf_dw_b, v_conf_ln_g, v_conf_ln_b, v_sc_conv_w, v_w_out, v_w_mlp1, v_w_mlp2, v_g_final):
    bsz, seq, d_model = x.shape
    c_half = conf_dw_b.shape[-1]
    n_taps = conf_dw_w.shape[1]
    cc = conf_dw_w.shape[-1]
    a_cols = w_ada.shape[-1]
    tokens = bsz * seq
    me = _dev_index()

    c_pad = jnp.pad(c, ((0, SUBLANES - bsz), (0, 0)))
    b_ada_loc = lax.dynamic_slice(b_ada, (0, me * a_cols), (1, a_cols))
    small_loc = jnp.zeros((HALO, 128), F32)
    small_loc = small_loc.at[:n_taps, :cc].set(conf_dw_w[0]).at[:3, cc:2 * cc].set(sc_conv_w[0])
    (win_t, wout_all), small_all, c_all, mod_rows = _gather_call(
        c_pad, w_ada[0], b_ada_loc, small_loc, [w_in[0].T.astype(BF16), w_out[0].astype(BF16)])
    cw = small_all[:, :, :cc].transpose(1, 0, 2).reshape(HALO, c_half)
    scw = small_all[:, :3, cc:2 * cc].transpose(1, 0, 2).reshape(3, c_half)
    cp = jnp.concatenate([conf_dw_b, conf_ln_g, conf_ln_b, scw, jnp.zeros((2, c_half), F32)], axis=0)
    mod = mod_rows[:, :bsz, :].transpose(1, 0, 2).reshape(bsz, 6, d_model)

    flat = lambda t: t.reshape(tokens, t.shape[-1])
    (proj, a1, cv, mixed, y1), (w1_all, w2_all) = _mixer_fwd_call(
        x, mod, win_t, wout_all, cw, cp, [w_mlp1[0].astype(BF16), w_mlp2[0].astype(BF16)])
    dx1, h2, dy2, u, dz, dmod2, head, dy1 = _mlp_call(
        x, y1, loss_target, mod, w1_all, w2_all, g_final.reshape(1, d_model))
    g_w1 = _wgrad_call(flat(h2), flat(dz), "wgrad_mlp1", owner_cols=w_mlp1.shape[-1])
    g_w2 = _wgrad_call(flat(u), flat(dy2), "wgrad_mlp2")
    g_out = _wgrad_call(flat(mixed), flat(dy1), "wgrad_out")
    (grad_x, dproj, h1, dmod1, cgrad), (s_w1, s_w2, s_out) = _mixer_bwd_call(
        dx1, x, proj, a1, cv, mod, win_t, wout_all, cw, cp, [g_w1, g_w2, g_out])
    g_in_t = _wgrad_call(flat(dproj), flat(h1), "wgrad_in")

    dmod = jnp.concatenate([dmod1[:, :2, :], dmod2[:, 3:4, :], dmod2[:, :3, :]], axis=1)
    n_cg = cgrad.shape[0]
    per_b = 6 * d_model // c_half
    per_b_pad = -(-per_b // SUBLANES) * SUBLANES
    dmod_rows = jnp.pad(dmod.reshape(bsz, per_b, c_half), ((0, 0), (0, per_b_pad - per_b), (0, 0)))
    small = jnp.concatenate([
        cgrad,
        head.reshape(2 * SUBLANES, c_half),
        dmod_rows.reshape(bsz * per_b_pad, c_half),
    ], axis=0)
    s_in, gathered = _tail_scatter_call(g_in_t, small)

    n_head = n_cg + 2 * SUBLANES
    sums, extra = _small_sum_call(
        gathered, n_head, (n_cg + 2, n_cg + 4), (n_head, n_head + per_b_pad, n_head + 2 * per_b_pad), 0.5 / d_model)
    loss = extra[per_b_pad, 0]
    g_b_ada = extra[:per_b].reshape(1, 6 * d_model)
    g_dw_w = lax.dynamic_slice(sums[:n_taps], (0, me * cc), (n_taps, cc))
    g_sc_w = lax.dynamic_slice(sums[HALO:HALO + 3], (0, me * cc), (3, cc))
    g_dw_b, g_ln_g, g_ln_b = sums[HALO + 3:HALO + 4], sums[HALO + 4:HALO + 5], sums[HALO + 5:HALO + 6]
    g_gf = sums[n_cg:n_cg + 2].reshape(1, d_model)

    dmod_all = gathered[:, n_head:, :].reshape(N_DEV, bsz, per_b_pad, c_half)[:, :, :per_b, :]
    dmod_all = dmod_all.reshape(N_DEV, bsz, 6 * d_model)
    dmod_cols = lax.dynamic_slice(dmod_all, (0, 0, me * a_cols), (N_DEV, bsz, a_cols))
    dmod_cols = jnp.pad(dmod_cols, ((0, 0), (0, SUBLANES - bsz), (0, 0))).reshape(N_DEV * SUBLANES, a_cols)
    c_rows = c_all.reshape(N_DEV * SUBLANES, d_model)
    g_ada, d_ada, m_ada, v_ada = _adam_ada_call(c_rows, dmod_cols, w_ada[0], m_w_ada[0], v_w_ada[0])

    gi, di, mi, vi = _adam_slabs_call(s_in, w_in[0].T, m_w_in[0].T, v_w_in[0].T, "adam_w_in")
    gi, di, mi, vi = gi.T, di.T, mi.T, vi.T
    go, do, mo, vo = _adam_slabs_call(s_out, w_out[0], m_w_out[0], v_w_out[0], "adam_w_out")
    g1, d1, m1, v1 = _adam_slabs_call(s_w1, w_mlp1[0], m_w_mlp1[0], v_w_mlp1[0], "adam_w_mlp1")
    g2, d2, m2, v2 = _adam_slabs_call(s_w2, w_mlp2[0], m_w_mlp2[0], v_w_mlp2[0], "adam_w_mlp2")

    small_like = [b_ada, conf_dw_w, conf_dw_b, conf_ln_g, conf_ln_b, sc_conv_w, g_final]
    two_d = lambda t: t.reshape(-1, t.shape[-1])
    small_g = [g_b_ada, g_dw_w, g_dw_b, g_ln_g, g_ln_b, g_sc_w, g_gf]
    sd, sm, sv = _adam_small_call(
        [two_d(t) for t in small_like], small_g,
        [two_d(t) for t in (m_b_ada, m_conf_dw_w, m_conf_dw_b, m_conf_ln_g, m_conf_ln_b, m_sc_conv_w, m_g_final)],
        [two_d(t) for t in (v_b_ada, v_conf_dw_w, v_conf_dw_b, v_conf_ln_g, v_conf_ln_b, v_sc_conv_w, v_g_final)])
    like = lambda parts: [p.reshape(w.shape) for p, w in zip(parts, small_like)]
    sg, sd, sm, sv = like(small_g), like(sd), like(sm), like(sv)

    def ordered(ada, small_list, w_in_, w_out_, w1_, w2_):
        b_ada_, dw_w_, dw_b_, ln_g_, ln_b_, sc_w_, gf_ = small_list
        return [ada[None], b_ada_, w_in_[None], dw_w_, dw_b_, ln_g_, ln_b_, sc_w_, w_out_[None], w1_[None], w2_[None], gf_]

    grads = ordered(g_ada, sg, gi, go, g1, g2)
    deltas = ordered(d_ada, sd, di, do, d1, d2)
    new_m = ordered(m_ada, sm, mi, mo, m1, m2)
    new_v = ordered(v_ada, sv, vi, vo, v1, v2)
    return (loss, grad_x, *grads, *deltas, *new_m, *new_v)
```

```python
import functools

import jax
import jax.numpy as jnp
from jax import lax
from jax.experimental import pallas as pl
from jax.experimental.pallas import tpu as pltpu

N_DEV = 8
RMS_EPS = 1e-6
ADAM_LR = 0.001
ADAM_B1 = 0.9
ADAM_B2 = 0.999
ADAM_EPS = 1e-08
ADAM_WD = 0.01
ADAM_STEP = 10

F32 = jnp.float32
BF16 = jnp.bfloat16
MESH = pl.DeviceIdType.MESH
VMEM = pltpu.VMEM
ANY = pl.ANY

HALO = 32
SHORT_HALO = 8
ROW_CHUNK = 32
SUBLANES = 8
V7X_VMEM_LIMIT = 56 * 1024 * 1024
MLP_VMEM_LIMIT = 48 * 1024 * 1024
MLP_TOKEN_TILE = 256
MIXER_FWD_TOKEN_TILE = 512


def _coords():
    return lax.axis_index("x"), lax.axis_index("y"), lax.axis_index("c")


def _dev_index():
    x, y, c = _coords()
    return 4 * x + 2 * y + c


def _peer(k):
    x, y, c = _coords()
    px = 1 - x if (k >> 2) & 1 else x
    py = 1 - y if (k >> 1) & 1 else y
    pc = 1 - c if k & 1 else c
    return (px, py, pc), 4 * px + 2 * py + pc


def _sigmoid(v):
    return jax.nn.sigmoid(v)


def _nt_dot(a, b):
    return lax.dot_general(a, b, (((1,), (1,)), ((), ())), preferred_element_type=F32)


def _nn_dot(a, b):
    return jnp.dot(a, b, preferred_element_type=F32)


def _tn_dot(a, b):
    return lax.dot_general(a, b, (((0,), (0,)), ((), ())), preferred_element_type=F32)


def _token_tile(seq):
    return 256 if seq % 256 == 0 else 64


GATHER_SEM_COLUMNS = 9
HALF_ROW_ALIGN = 16


class _TwoLevelGather:
    def __init__(self, srcs, dsts, send_sems, recv_sems, loc_sems):
        x, y, c = _coords()
        me = 4 * x + 2 * y + c
        sibling, along_x, along_y = (x, y, 1 - c), (1 - x, y, c), (x, 1 - y, c)
        from_x, from_y, diagonal = 4 * (1 - x) + 2 * y + c, 4 * x + 2 * (1 - y) + c, 4 * (1 - x) + 2 * (1 - y) + c

        def remote(src, dst, a, col, to):
            return pltpu.make_async_remote_copy(
                src_ref=src, dst_ref=dst, send_sem=send_sems.at[a, col], recv_sem=recv_sems.at[a, col],
                device_id=to, device_id_type=MESH)

        def onward(block, a, col, to):
            return remote(block, block, a, col, to)

        self.first, self.arrivals, self.second, self.halves, self.third = [], [], [], [], []
        for a, (src, dst) in enumerate(zip(srcs, dsts)):
            half = src.shape[0] // (2 * HALF_ROW_ALIGN) * HALF_ROW_ALIGN
            lower, upper = pl.ds(0, half), pl.ds(half, src.shape[0] - half)
            neighbours = [remote(src, dst.at[me], a, 1, along_x), remote(src, dst.at[me], a, 2, along_y)]
            self.first += [pltpu.make_async_copy(src, dst.at[me], loc_sems.at[a]),
                           remote(src, dst.at[me], a, 0, sibling)]
            self.arrivals += neighbours
            halves = [onward(dst.at[from_x, lower], a, 3, along_y), onward(dst.at[from_y, upper], a, 4, along_x)]
            self.halves += halves
            self.second += [[halves[0], onward(dst.at[from_x], a, 5, sibling)],
                            [halves[1], onward(dst.at[from_y], a, 6, sibling)]]
            self.third += [onward(dst.at[diagonal, lower], a, 7, sibling),
                           onward(dst.at[diagonal, upper], a, 8, sibling)]

    def start(self):
        for cp in self.first + self.arrivals:
            cp.start()

    def forward_arrivals(self):
        for arrival, sends in zip(self.arrivals, self.second):
            arrival.wait_recv()
            for cp in sends:
                cp.start()

    def forward_diagonal(self):
        for half, to_sibling in zip(self.halves, self.third):
            half.wait_recv()
            to_sibling.start()

    def finish(self):
        for cp in self.arrivals + self.halves:
            cp.wait_send()
        for cp in self.first + [sends[1] for sends in self.second] + self.third:
            cp.wait()


def _scatter_copies(srcs, dsts, send_sems, recv_sems, loc_sems):
    me = _dev_index()
    copies = []
    for a, (src, dst) in enumerate(zip(srcs, dsts)):
        copies.append(pltpu.make_async_copy(src.at[me], dst.at[me], loc_sems.at[a]))
        for k in range(1, N_DEV):
            peer, pidx = _peer(k)
            copies.append(pltpu.make_async_remote_copy(
                src_ref=src.at[pidx], dst_ref=dst.at[me], send_sem=send_sems.at[a, k - 1],
                recv_sem=recv_sems.at[a, k - 1], device_id=peer, device_id_type=MESH))
    return copies


def _exchange_sems(n_arrays, columns=GATHER_SEM_COLUMNS):
    return [pltpu.SemaphoreType.DMA((n_arrays, columns)), pltpu.SemaphoreType.DMA((n_arrays, columns)),
            pltpu.SemaphoreType.DMA((n_arrays,))]


def _gather_call(c_pad, w_ada, b_ada_loc, small_loc, big_shards):
    n_big = len(big_shards)
    d_model = c_pad.shape[1]
    a_cols = w_ada.shape[1]

    def body(c_ref, wada_ref, bada_ref, small_ref, *rest):
        big_in = rest[:n_big]
        big_out = rest[n_big:2 * n_big]
        small_all, c_all, mod_rows = rest[2 * n_big:2 * n_big + 3]
        modcols, big_send, big_recv, loc_sem, s_send, s_recv = rest[2 * n_big + 3:]
        me = _dev_index()
        big = _TwoLevelGather(big_in, big_out, big_send, big_recv, loc_sem)
        big.start()

        small_all[me] = small_ref[...]
        c_all[me] = c_ref[...]
        first = []
        for k in range(1, N_DEV):
            peer, _ = _peer(k)
            for i, (src, dst) in enumerate(((small_ref, small_all), (c_ref, c_all))):
                cp = pltpu.make_async_remote_copy(
                    src_ref=src, dst_ref=dst.at[me],
                    send_sem=s_send.at[i, k - 1], recv_sem=s_recv.at[i, k - 1],
                    device_id=peer, device_id_type=MESH)
                cp.start()
                first.append(cp)
        for cp in first:
            cp.wait()

        c_rows = c_all[...].reshape(N_DEV * SUBLANES, d_model)
        c_act = c_rows * _sigmoid(c_rows)
        modcols[...] = _nn_dot(c_act.astype(BF16), wada_ref[...].astype(BF16)) + bada_ref[...]
        mod_rows[me] = modcols[pl.ds(pl.multiple_of(me * SUBLANES, SUBLANES), SUBLANES), :]
        second = []
        for k in range(1, N_DEV):
            peer, pidx = _peer(k)
            cp = pltpu.make_async_remote_copy(
                src_ref=modcols.at[pl.ds(pl.multiple_of(pidx * SUBLANES, SUBLANES), SUBLANES), :],
                dst_ref=mod_rows.at[me],
                send_sem=s_send.at[2, k - 1], recv_sem=s_recv.at[2, k - 1],
                device_id=peer, device_id_type=MESH)
            cp.start()
            second.append(cp)
        big.forward_arrivals()
        for cp in second:
            cp.wait()
        big.forward_diagonal()
        big.finish()

    out_shape = tuple(jax.ShapeDtypeStruct((N_DEV,) + s.shape, s.dtype) for s in big_shards) + (
        jax.ShapeDtypeStruct((N_DEV,) + small_loc.shape, F32),
        jax.ShapeDtypeStruct((N_DEV, SUBLANES, d_model), F32),
        jax.ShapeDtypeStruct((N_DEV, SUBLANES, a_cols), F32),
    )
    vm = pl.BlockSpec(memory_space=VMEM)
    hbm = pl.BlockSpec(memory_space=ANY)
    outs = pl.pallas_call(
        body, name="gather_weights_mod", out_shape=out_shape,
        in_specs=[vm, vm, vm, vm] + [hbm] * n_big,
        out_specs=tuple([hbm] * n_big + [vm, vm, vm]),
        scratch_shapes=[
            pltpu.VMEM((N_DEV * SUBLANES, a_cols), F32),
            *_exchange_sems(n_big),
            pltpu.SemaphoreType.DMA((3, N_DEV - 1)),
            pltpu.SemaphoreType.DMA((3, N_DEV - 1)),
        ],
        compiler_params=pltpu.CompilerParams(vmem_limit_bytes=V7X_VMEM_LIMIT),
    )(c_pad, w_ada, b_ada_loc, small_loc, *big_shards)
    return outs[:n_big], outs[n_big], outs[n_big + 1], outs[n_big + 2]


def _shifted_rows_count(tm):
    return tm + HALO - SUBLANES


def _fill_shifted(ext, shifted, tm):
    for s in range(1, SUBLANES):
        shifted[s - 1] = ext[s:s + _shifted_rows_count(tm), :]


def _shifted_rows(ext, shifted, start, rows):
    phase = start % SUBLANES
    aligned = start - phase
    if phase == 0:
        return ext[aligned:aligned + rows, :]
    return shifted[phase - 1, aligned:aligned + rows, :]


def _layer_norm_parts(a1):
    mu = jnp.mean(a1, axis=-1, keepdims=True)
    xc = a1 - mu
    rstd = lax.rsqrt(jnp.mean(xc * xc, axis=-1, keepdims=True) + RMS_EPS)
    return xc * rstd, rstd


def _mixer_fwd_call(x, mod, win_t, wout, cw, cp, later_shards):
    n_later = len(later_shards)
    bsz, seq, d_model = x.shape
    c_half = cw.shape[1]
    n_taps = 31
    d_in = win_t.shape[0] * win_t.shape[1]
    tm = MIXER_FWD_TOKEN_TILE if seq % MIXER_FWD_TOKEN_TILE == 0 else _token_tile(seq)
    nt = seq // tm
    arrivals_step, diagonal_step = (7 * bsz * nt) // 16, (11 * bsz * nt) // 16

    def body(x_ref, mod_ref, win_ref, wout_ref, cw_ref, cp_ref, *rest):
        shard_refs, rest = rest[:n_later], rest[n_later:]
        proj_ref, a1_ref, cv_ref, mixed_ref, y1_ref = rest[:5]
        gathered_refs, rest = rest[5:5 + n_later], rest[5 + n_later:]
        aext, qext, ashift, send_sems, recv_sems, loc_sems = rest
        b, t = pl.program_id(0), pl.program_id(1)

        step = b * nt + t

        @pl.when(step == 0)
        def _():
            _TwoLevelGather(shard_refs, gathered_refs, send_sems, recv_sems, loc_sems).start()

        @pl.when(step == arrivals_step)
        def _():
            _TwoLevelGather(shard_refs, gathered_refs, send_sems, recv_sems, loc_sems).forward_arrivals()

        @pl.when(step == diagonal_step)
        def _():
            _TwoLevelGather(shard_refs, gathered_refs, send_sems, recv_sems, loc_sems).forward_diagonal()

        xv = x_ref[...]
        sh1, sc1 = mod_ref[0:1, :], mod_ref[1:2, :]
        r1 = lax.rsqrt(jnp.mean(xv * xv, axis=-1, keepdims=True) + RMS_EPS)
        h1 = (xv * r1) * (1.0 + sc1) + sh1
        proj = _nt_dot(h1.astype(BF16), win_ref[...].reshape(d_in, d_model))
        proj_ref[...] = proj
        val, gate = proj[:, 0:c_half], proj[:, c_half:2 * c_half]
        s_b, s_c, s_h = proj[:, 2 * c_half:3 * c_half], proj[:, 3 * c_half:4 * c_half], proj[:, 4 * c_half:5 * c_half]

        @pl.when(t == 0)
        def _():
            aext[0:HALO, :] = jnp.zeros((HALO, c_half), F32)
            qext[0:SHORT_HALO, :] = jnp.zeros((SHORT_HALO, c_half), F32)

        @pl.when(t > 0)
        def _():
            aext[0:HALO, :] = aext[tm:tm + HALO, :]
            qext[0:SHORT_HALO, :] = qext[tm:tm + SHORT_HALO, :]
        aext[HALO:HALO + tm, :] = val * _sigmoid(gate)
        qext[SHORT_HALO:SHORT_HALO + tm, :] = s_c * s_h

        base = HALO - (n_taps - 1)
        _fill_shifted(aext, ashift, tm)
        for r0 in range(0, tm, ROW_CHUNK):
            acc = jnp.zeros((ROW_CHUNK, c_half), F32)
            for k in range(n_taps):
                acc = acc + cw_ref[k:k + 1, :] * _shifted_rows(aext, ashift, r0 + base + k, ROW_CHUNK)
            a1_ref[r0:r0 + ROW_CHUNK, :] = acc + cp_ref[0:1, :]
        sbase = SHORT_HALO - 2
        conv3 = cp_ref[3:4, :] * qext[sbase:sbase + tm, :]
        conv3 = conv3 + cp_ref[4:5, :] * qext[sbase + 1:sbase + 1 + tm, :]
        conv3 = conv3 + cp_ref[5:6, :] * qext[sbase + 2:sbase + 2 + tm, :]
        cv_ref[...] = conv3

        norm, _ = _layer_norm_parts(a1_ref[...])
        a2 = norm * cp_ref[1:2, :] + cp_ref[2:3, :]
        mixed = jnp.concatenate([a2 * _sigmoid(a2), s_b * conv3], axis=-1).astype(BF16)
        mixed_ref[...] = mixed
        y1 = _nn_dot(mixed, wout_ref[...].reshape(d_model, d_model))
        y1_ref[...] = y1

        @pl.when(step == bsz * nt - 1)
        def _():
            _TwoLevelGather(shard_refs, gathered_refs, send_sems, recv_sems, loc_sems).finish()

    hbm = pl.BlockSpec(memory_space=ANY)

    def tok(width):
        return pl.BlockSpec((None, tm, width), lambda b, t: (b, t, 0))

    def const(shape):
        return pl.BlockSpec(shape, lambda b, t: (0,) * len(shape))

    def resident(shape):
        return pl.BlockSpec(shape, lambda b, t: (0,) * len(shape), pipeline_mode=pl.Buffered(1))

    out_shape = (
        jax.ShapeDtypeStruct((bsz, seq, d_in), F32),
        jax.ShapeDtypeStruct((bsz, seq, c_half), F32),
        jax.ShapeDtypeStruct((bsz, seq, c_half), F32),
        jax.ShapeDtypeStruct((bsz, seq, d_model), BF16),
        jax.ShapeDtypeStruct((bsz, seq, d_model), F32),
    ) + tuple(jax.ShapeDtypeStruct((N_DEV,) + s.shape, s.dtype) for s in later_shards)
    outs = pl.pallas_call(
        body, name="mixer_fwd", out_shape=out_shape, grid=(bsz, nt),
        in_specs=[tok(d_model), pl.BlockSpec((None, 6, d_model), lambda b, t: (b, 0, 0)),
                  resident(win_t.shape), resident(wout.shape), const(cw.shape), const(cp.shape)] + [hbm] * n_later,
        out_specs=(tok(d_in), tok(c_half), tok(c_half), tok(d_model), tok(d_model)) + (hbm,) * n_later,
        scratch_shapes=[pltpu.VMEM((tm + HALO, c_half), F32), pltpu.VMEM((tm + SHORT_HALO, c_half), F32),
                        pltpu.VMEM((SUBLANES - 1, _shifted_rows_count(tm), c_half), F32)]
        + _exchange_sems(n_later),
        compiler_params=pltpu.CompilerParams(
            dimension_semantics=("arbitrary", "arbitrary"), vmem_limit_bytes=V7X_VMEM_LIMIT),
    )(x, mod, win_t, wout, cw, cp, *later_shards)
    return outs[:5], outs[5:]


def _mlp_call(x, y1, target, mod, w1, w2, g_final):
    bsz, seq, d_model = x.shape
    n_blk, _, f_blk = w1.shape
    d_ff = n_blk * f_blk
    tm = MLP_TOKEN_TILE if seq % MLP_TOKEN_TILE == 0 else _token_tile(seq)
    nt = seq // tm

    def body(x_ref, y1_ref, tgt_ref, mod_ref, w1_ref, w2_ref, gf_ref,
             dx1_ref, h2_ref, dy2_ref, u_ref, dz_ref, dmod_ref, head_ref, dy1_ref, relu_scr):
        b, t = pl.program_id(0), pl.program_id(1)
        x1v = x_ref[...] + mod_ref[2:3, :] * y1_ref[...]
        sh2, sc2, g2 = mod_ref[3:4, :], mod_ref[4:5, :], mod_ref[5:6, :]
        gf = gf_ref[...]
        r2 = lax.rsqrt(jnp.mean(x1v * x1v, axis=-1, keepdims=True) + RMS_EPS)
        xn2 = x1v * r2
        h2 = (xn2 * (1.0 + sc2) + sh2).astype(BF16)
        h2_ref[...] = h2
        y2 = jnp.zeros((tm, d_model), F32)
        for j in range(n_blk):
            cols = slice(j * f_blk, (j + 1) * f_blk)
            rz = jnp.maximum(_nn_dot(h2, w1_ref[j]), 0.0)
            relu_scr[:, cols] = rz
            ub = (rz * rz).astype(BF16)
            u_ref[:, cols] = ub
            y2 = y2 + _nn_dot(ub, w2_ref[j])
        x2 = x1v + g2 * y2
        r3 = lax.rsqrt(jnp.mean(x2 * x2, axis=-1, keepdims=True) + RMS_EPS)
        xn3 = x2 * r3
        diff = xn3 * gf - tgt_ref[...]
        dout = diff * (1.0 / d_model)

        @pl.when(jnp.logical_and(b == 0, t == 0))
        def _():
            head_ref[...] = jnp.zeros(head_ref.shape, F32)

        @pl.when(t == 0)
        def _():
            dmod_ref[...] = jnp.zeros(dmod_ref.shape, F32)

        head_ref[0:1, :] += jnp.sum(dout * xn3, axis=0, keepdims=True)
        head_ref[1:2, :] += jnp.sum(diff * diff, axis=0, keepdims=True)
        dxn3 = dout * gf
        dx2 = r3 * (dxn3 - xn3 * jnp.mean(dxn3 * xn3, axis=-1, keepdims=True))
        dmod_ref[2:3, :] += jnp.sum(dx2 * y2, axis=0, keepdims=True)
        dy2 = (g2 * dx2).astype(BF16)
        dy2_ref[...] = dy2
        dh2 = jnp.zeros((tm, d_model), F32)
        for j in range(n_blk):
            cols = slice(j * f_blk, (j + 1) * f_blk)
            dz = (_nt_dot(dy2, w2_ref[j]) * (2.0 * relu_scr[:, cols])).astype(BF16)
            dz_ref[:, cols] = dz
            dh2 = dh2 + _nt_dot(dz, w1_ref[j])
        dmod_ref[0:1, :] += jnp.sum(dh2, axis=0, keepdims=True)
        dmod_ref[1:2, :] += jnp.sum(dh2 * xn2, axis=0, keepdims=True)
        dxn2 = dh2 * (1.0 + sc2)
        dx1 = dx2 + r2 * (dxn2 - xn2 * jnp.mean(dxn2 * xn2, axis=-1, keepdims=True))
        dx1_ref[...] = dx1
        dy1_ref[...] = (mod_ref[2:3, :] * dx1).astype(BF16)
        dmod_ref[3:4, :] += jnp.sum(dx1 * y1_ref[...], axis=0, keepdims=True)

    def tok(width):
        return pl.BlockSpec((None, tm, width), lambda b, t: (b, t, 0))

    def const(shape):
        return pl.BlockSpec(shape, lambda b, t: (0,) * len(shape))

    def resident(shape):
        return pl.BlockSpec(shape, lambda b, t: (0,) * len(shape), pipeline_mode=pl.Buffered(1))

    out_shape = (
        jax.ShapeDtypeStruct((bsz, seq, d_model), F32),
        jax.ShapeDtypeStruct((bsz, seq, d_model), BF16),
        jax.ShapeDtypeStruct((bsz, seq, d_model), BF16),
        jax.ShapeDtypeStruct((bsz, seq, d_ff), BF16),
        jax.ShapeDtypeStruct((bsz, seq, d_ff), BF16),
        jax.ShapeDtypeStruct((bsz, SUBLANES, d_model), F32),
        jax.ShapeDtypeStruct((SUBLANES, d_model), F32),
        jax.ShapeDtypeStruct((bsz, seq, d_model), BF16),
    )
    return pl.pallas_call(
        body, name="mlp_fwd_bwd", out_shape=out_shape, grid=(bsz, nt),
        in_specs=[tok(d_model), tok(d_model), tok(d_model),
                  pl.BlockSpec((None, 6, d_model), lambda b, t: (b, 0, 0)),
                  resident(w1.shape), resident(w2.shape), const(g_final.shape)],
        out_specs=(tok(d_model), tok(d_model), tok(d_model), tok(d_ff), tok(d_ff),
                   pl.BlockSpec((None, SUBLANES, d_model), lambda b, t: (b, 0, 0)),
                   const((SUBLANES, d_model)), tok(d_model)),
        scratch_shapes=[pltpu.VMEM((tm, d_ff), F32)],
        compiler_params=pltpu.CompilerParams(
            dimension_semantics=("arbitrary", "arbitrary"), vmem_limit_bytes=MLP_VMEM_LIMIT),
    )(x, y1, target, mod, w1, w2, g_final)


def _mixer_bwd_call(dx1, x, proj, a1, cv, mod, win_t, wout, cw, cp, partials):
    n_part = len(partials)
    bsz, seq, d_model = x.shape
    c_half = cw.shape[1]
    n_taps = 31
    d_in = win_t.shape[0] * win_t.shape[1]
    tm = _token_tile(seq)
    nt = seq // tm

    def body(dx1_ref, x_ref, proj_ref, a1_ref, cv_ref, mod_ref, win_ref, wout_ref, cw_ref, cp_ref, *rest):
        part_refs, rest = rest[:n_part], rest[n_part:]
        gx_ref, dproj_ref, h1_ref, dmod_ref, cgrad_ref = rest[:5]
        slab_refs, rest = rest[5:5 + n_part], rest[5 + n_part:]
        dext, cext, a0_scr, da0_scr, tap_acc, row_acc, dshift, send_sems, recv_sems, loc_sems = rest
        b, step = pl.program_id(0), pl.program_id(1)
        first = jnp.logical_and(b == 0, step == 0)
        last = jnp.logical_and(b == bsz - 1, step == nt - 1)

        @pl.when(first)
        def _():
            for copy in _scatter_copies(part_refs, slab_refs, send_sems, recv_sems, loc_sems):
                copy.start()

        dx1v = dx1_ref[...]
        xv = x_ref[...]
        sh1, sc1, g1 = mod_ref[0:1, :], mod_ref[1:2, :], mod_ref[2:3, :]

        @pl.when(first)
        def _():
            tap_acc[...] = jnp.zeros(tap_acc.shape, F32)
            row_acc[...] = jnp.zeros(row_acc.shape, F32)

        @pl.when(step == 0)
        def _():
            dmod_ref[...] = jnp.zeros(dmod_ref.shape, F32)
            dext[tm:tm + HALO, :] = jnp.zeros((HALO, c_half), F32)
            cext[tm:tm + SHORT_HALO, :] = jnp.zeros((SHORT_HALO, c_half), F32)

        @pl.when(step > 0)
        def _():
            dext[tm:tm + HALO, :] = dext[0:HALO, :]
            cext[tm:tm + SHORT_HALO, :] = cext[0:SHORT_HALO, :]

        dy1 = (g1 * dx1v).astype(BF16)
        dmixed = _nt_dot(dy1, wout_ref[...].reshape(d_model, d_model))
        d_a, d_s = dmixed[:, 0:c_half], dmixed[:, c_half:2 * c_half]

        val, gate = proj_ref[:, 0:c_half], proj_ref[:, c_half:2 * c_half]
        s_b = proj_ref[:, 2 * c_half:3 * c_half]
        s_c, s_h = proj_ref[:, 3 * c_half:4 * c_half], proj_ref[:, 4 * c_half:5 * c_half]

        d_sb = d_s * cv_ref[...]
        cext[0:tm, :] = d_s * s_b
        q = s_c * s_h
        dq = jnp.zeros((tm, c_half), F32)
        for k in range(3):
            shifted = cext[2 - k:2 - k + tm, :]
            dq = dq + cp_ref[3 + k:4 + k, :] * shifted
            row_acc[k:k + 1, :] += jnp.sum(q * shifted, axis=0, keepdims=True)
        d_sc, d_sh = dq * s_h, dq * s_c

        norm, rstd = _layer_norm_parts(a1_ref[...])
        ln_g = cp_ref[1:2, :]
        a2 = norm * ln_g + cp_ref[2:3, :]
        sg = _sigmoid(a2)
        d_a2 = d_a * (sg * (1.0 + a2 * (1.0 - sg)))
        row_acc[4:5, :] += jnp.sum(d_a2 * norm, axis=0, keepdims=True)
        row_acc[5:6, :] += jnp.sum(d_a2, axis=0, keepdims=True)
        d_n = d_a2 * ln_g
        d_a1 = rstd * (d_n - jnp.mean(d_n, axis=-1, keepdims=True)
                       - norm * jnp.mean(d_n * norm, axis=-1, keepdims=True))
        row_acc[3:4, :] += jnp.sum(d_a1, axis=0, keepdims=True)
        dext[0:tm, :] = d_a1
        sig_g = _sigmoid(gate)
        a0_scr[...] = val * sig_g

        _fill_shifted(dext, dshift, tm)
        for r0 in range(0, tm, ROW_CHUNK):
            a0c = a0_scr[r0:r0 + ROW_CHUNK, :]
            acc = jnp.zeros((ROW_CHUNK, c_half), F32)
            for k in range(n_taps):
                shifted = _shifted_rows(dext, dshift, r0 + (n_taps - 1) - k, ROW_CHUNK)
                acc = acc + cw_ref[k:k + 1, :] * shifted
                prod = a0c * shifted
                part = prod[0:SUBLANES, :]
                for g in range(1, ROW_CHUNK // SUBLANES):
                    part = part + prod[g * SUBLANES:(g + 1) * SUBLANES, :]
                tap_acc[k * SUBLANES:(k + 1) * SUBLANES, :] += part
            da0_scr[r0:r0 + ROW_CHUNK, :] = acc
        d_a0 = da0_scr[...]
        d_val = d_a0 * sig_g
        d_gate = d_a0 * val * sig_g * (1.0 - sig_g)

        dproj = jnp.concatenate([d_val, d_gate, d_sb, d_sc, d_sh], axis=-1).astype(BF16)
        dproj_ref[...] = dproj
        dh1 = _nn_dot(dproj, win_ref[...].reshape(d_in, d_model))
        r1 = lax.rsqrt(jnp.mean(xv * xv, axis=-1, keepdims=True) + RMS_EPS)
        xn1 = xv * r1
        h1_ref[...] = (xn1 * (1.0 + sc1) + sh1).astype(BF16)
        dmod_ref[0:1, :] += jnp.sum(dh1, axis=0, keepdims=True)
        dmod_ref[1:2, :] += jnp.sum(dh1 * xn1, axis=0, keepdims=True)
        dxn1 = dh1 * (1.0 + sc1)
        gx_ref[...] = dx1v + r1 * (dxn1 - xn1 * jnp.mean(dxn1 * xn1, axis=-1, keepdims=True))

        @pl.when(last)
        def _():
            taps = jnp.sum(tap_acc[...].reshape(HALO, SUBLANES, c_half), axis=1)
            cgrad_ref[0:HALO, :] = taps
            cgrad_ref[HALO:HALO + SUBLANES, :] = row_acc[...]
            for copy in _scatter_copies(part_refs, slab_refs, send_sems, recv_sems, loc_sems):
                copy.wait()

    hbm = pl.BlockSpec(memory_space=ANY)

    def tok(width):
        return pl.BlockSpec((None, tm, width), lambda b, s: (b, nt - 1 - s, 0))

    def const(shape):
        return pl.BlockSpec(shape, lambda b, s: (0,) * len(shape))

    mod_spec = pl.BlockSpec((None, 6, d_model), lambda b, s: (b, 0, 0))
    out_shape = (
        jax.ShapeDtypeStruct((bsz, seq, d_model), F32),
        jax.ShapeDtypeStruct((bsz, seq, d_in), BF16),
        jax.ShapeDtypeStruct((bsz, seq, d_model), BF16),
        jax.ShapeDtypeStruct((bsz, SUBLANES, d_model), F32),
        jax.ShapeDtypeStruct((HALO + SUBLANES, c_half), F32),
    ) + tuple(jax.ShapeDtypeStruct(p.shape, p.dtype) for p in partials)
    outs = pl.pallas_call(
        body, name="mixer_bwd", out_shape=out_shape, grid=(bsz, nt),
        in_specs=[tok(d_model), tok(d_model), tok(d_in), tok(c_half), tok(c_half), mod_spec,
                  const(win_t.shape), const(wout.shape), const(cw.shape), const(cp.shape)] + [hbm] * n_part,
        out_specs=(tok(d_model), tok(d_in), tok(d_model),
                   pl.BlockSpec((None, SUBLANES, d_model), lambda b, s: (b, 0, 0)),
                   const((HALO + SUBLANES, c_half))) + (hbm,) * n_part,
        scratch_shapes=[
            pltpu.VMEM((tm + HALO, c_half), F32), pltpu.VMEM((tm + SHORT_HALO, c_half), F32),
            pltpu.VMEM((tm, c_half), F32), pltpu.VMEM((tm, c_half), F32),
            pltpu.VMEM((HALO * SUBLANES, c_half), F32), pltpu.VMEM((SUBLANES, c_half), F32),
            pltpu.VMEM((SUBLANES - 1, _shifted_rows_count(tm), c_half), F32),
        ] + _exchange_sems(n_part),
        compiler_params=pltpu.CompilerParams(
            dimension_semantics=("arbitrary", "arbitrary"), vmem_limit_bytes=V7X_VMEM_LIMIT),
    )(dx1, x, proj, a1, cv, mod, win_t, wout, cw, cp, *partials)
    return outs[:5], outs[5:]


def _largest_divisor(n, cap, multiple):
    best = None
    for cand in range(multiple, min(n, cap) + 1, multiple):
        if n % cand == 0:
            best = cand
    return best if best is not None else n


WGRAD_TOKENS_PER_STEP = 2048
WGRAD_COLS_PER_STEP = 1024


def _wgrad_call(a, b, name, owner_cols=None):
    tokens, m_dim = a.shape
    n_dim = b.shape[1]
    bk = _largest_divisor(tokens, WGRAD_TOKENS_PER_STEP, 128)
    n_k = tokens // bk
    if owner_cols is None:
        bm = _largest_divisor(m_dim, 1024, m_dim // N_DEV)
        bn = n_dim
        owners = 1
        out_shape = jax.ShapeDtypeStruct((m_dim, n_dim), BF16)
        out_spec = pl.BlockSpec((bm, bn), lambda i, j, k: (i, j))
    else:
        bm = m_dim
        bn = _largest_divisor(n_dim, WGRAD_COLS_PER_STEP, owner_cols)
        owners = bn // owner_cols
        out_shape = jax.ShapeDtypeStruct((n_dim // owner_cols, m_dim, owner_cols), BF16)
        out_spec = pl.BlockSpec((owners, bm, owner_cols), lambda i, j, k: (j, i, 0))

    def body(a_ref, b_ref, o_ref, acc):
        k = pl.program_id(2)

        @pl.when(k == 0)
        def _():
            acc[...] = jnp.zeros(acc.shape, F32)

        acc[...] += _tn_dot(a_ref[...], b_ref[...])

        @pl.when(k == n_k - 1)
        def _():
            if owner_cols is None:
                o_ref[...] = acc[...].astype(BF16)
            else:
                for q in range(owners):
                    o_ref[q] = acc[:, q * owner_cols:(q + 1) * owner_cols].astype(BF16)

    out = pl.pallas_call(
        body, name=name, out_shape=out_shape, grid=(m_dim // bm, n_dim // bn, n_k),
        in_specs=[pl.BlockSpec((bk, bm), lambda i, j, k: (k, i)), pl.BlockSpec((bk, bn), lambda i, j, k: (k, j))],
        out_specs=out_spec, scratch_shapes=[pltpu.VMEM((bm, bn), F32)],
        compiler_params=pltpu.CompilerParams(
            dimension_semantics=("arbitrary", "arbitrary", "arbitrary"), vmem_limit_bytes=V7X_VMEM_LIMIT),
    )(a, b)
    if owner_cols is None:
        out = out.reshape(N_DEV, m_dim // N_DEV, n_dim)
    return out


def _tail_scatter_call(partial, small):
    _, rows, cols = partial.shape
    n_chips = N_DEV // 2

    def body(g_ref, small_ref, out_ref, small_all, from_sibling, pair,
             p1_send, p1_recv, p2_send, p2_recv, s_send, s_recv, s_loc):
        x, y, c = _coords()
        sibling = (x, y, 1 - c)
        my_chip = 2 * x + y
        chips = [(1 - x, y), (x, 1 - y), (1 - x, 1 - y)]

        def remote(src, dst, send_sem, recv_sem, to):
            return pltpu.make_async_remote_copy(src_ref=src, dst_ref=dst, send_sem=send_sem, recv_sem=recv_sem,
                                                device_id=to, device_id_type=MESH)

        gather = _TwoLevelGather([small_ref], [small_all], s_send, s_recv, s_loc)
        gather.start()
        first = [remote(g_ref.at[2 * q + (1 - c)], from_sibling.at[q], p1_send.at[q], p1_recv.at[q], sibling)
                 for q in range(n_chips)]
        for cp in first:
            cp.start()
        for cp in first:
            cp.wait()
        for q in range(n_chips):
            pair[q] = (g_ref[2 * q + c].astype(F32) + from_sibling[q].astype(F32)).astype(BF16)
        second = [remote(pair.at[2 * cx + cy], out_ref.at[my_chip], p2_send.at[j], p2_recv.at[j], (cx, cy, c))
                  for j, (cx, cy) in enumerate(chips)]
        for cp in second:
            cp.start()
        out_ref[my_chip] = pair[my_chip]
        gather.forward_arrivals()
        for cp in second:
            cp.wait()
        gather.forward_diagonal()
        gather.finish()

    vm = pl.BlockSpec(memory_space=VMEM)
    slabs = pltpu.VMEM((n_chips, rows, cols), BF16)
    return pl.pallas_call(
        body, name="scatter_tail",
        out_shape=(jax.ShapeDtypeStruct((n_chips, rows, cols), BF16), jax.ShapeDtypeStruct((N_DEV,) + small.shape, F32)),
        in_specs=[vm, vm], out_specs=(vm, vm),
        scratch_shapes=[slabs, slabs,
                        pltpu.SemaphoreType.DMA((n_chips,)), pltpu.SemaphoreType.DMA((n_chips,)),
                        pltpu.SemaphoreType.DMA((n_chips - 1,)), pltpu.SemaphoreType.DMA((n_chips - 1,))]
        + _exchange_sems(1),
        compiler_params=pltpu.CompilerParams(vmem_limit_bytes=V7X_VMEM_LIMIT),
    )(partial, small)


def _adamw(w, g, m, v):
    m2 = ADAM_B1 * m + (1.0 - ADAM_B1) * g
    v2 = ADAM_B2 * v + (1.0 - ADAM_B2) * (g * g)
    m_hat = m2 / (1.0 - ADAM_B1 ** ADAM_STEP)
    v_hat = v2 / (1.0 - ADAM_B2 ** ADAM_STEP)
    delta = -ADAM_LR * (m_hat / (jnp.sqrt(v_hat) + ADAM_EPS) + ADAM_WD * w)
    return delta, m2, v2


def _adam_slabs_call(slabs, w, m, v, name):
    rows, cols = w.shape
    n_slabs = slabs.shape[0]
    tr = _largest_divisor(rows, 256, 2 * SUBLANES)

    def body(s_ref, w_ref, m_ref, v_ref, g_ref, d_ref, m2_ref, v2_ref):
        g = s_ref[0].astype(F32)
        for k in range(1, n_slabs):
            g = g + s_ref[k].astype(F32)
        delta, m2, v2 = _adamw(w_ref[...], g, m_ref[...], v_ref[...])
        g_ref[...] = g
        d_ref[...] = delta
        m2_ref[...] = m2
        v2_ref[...] = v2

    tile = pl.BlockSpec((tr, cols), lambda i: (i, 0))
    shp = jax.ShapeDtypeStruct((rows, cols), F32)
    return pl.pallas_call(
        body, name=name, out_shape=(shp, shp, shp, shp), grid=(rows // tr,),
        in_specs=[pl.BlockSpec((n_slabs, tr, cols), lambda i: (0, i, 0)), tile, tile, tile],
        out_specs=(tile, tile, tile, tile),
        compiler_params=pltpu.CompilerParams(dimension_semantics=("arbitrary",), vmem_limit_bytes=V7X_VMEM_LIMIT),
    )(slabs, w, m, v)


def _adam_ada_call(c_rows, dmod_cols, w, m, v):
    rows, cols = w.shape
    n_rows = c_rows.shape[0]
    tr = _largest_divisor(rows, 256, 128)

    def body(c_ref, dm_ref, w_ref, m_ref, v_ref, g_ref, d_ref, m2_ref, v2_ref):
        cv = c_ref[...]
        c_act = (cv * _sigmoid(cv)).astype(BF16)
        g = _tn_dot(c_act, dm_ref[...].astype(BF16))
        delta, m2, v2 = _adamw(w_ref[...], g, m_ref[...], v_ref[...])
        g_ref[...] = g
        d_ref[...] = delta
        m2_ref[...] = m2
        v2_ref[...] = v2

    tile = pl.BlockSpec((tr, cols), lambda i: (i, 0))
    shp = jax.ShapeDtypeStruct((rows, cols), F32)
    return pl.pallas_call(
        body, name="adam_w_ada", out_shape=(shp, shp, shp, shp), grid=(rows // tr,),
        in_specs=[pl.BlockSpec((n_rows, tr), lambda i: (0, i)), pl.BlockSpec((n_rows, cols), lambda i: (0, 0)),
                  tile, tile, tile],
        out_specs=(tile, tile, tile, tile),
        compiler_params=pltpu.CompilerParams(dimension_semantics=("arbitrary",), vmem_limit_bytes=V7X_VMEM_LIMIT),
    )(c_rows, dmod_cols, w, m, v)


def _small_sum_call(small_all, n_grad_rows, loss_rows, bias_rows, loss_scale):
    _, rows, width = small_all.shape
    lo, hi = loss_rows
    b0, b1, b2 = bias_rows
    nb = b1 - b0

    def body(s_ref, sum_ref, extra_ref):
        tot = s_ref[0]
        for k in range(1, N_DEV):
            tot = tot + s_ref[k]
        sum_ref[...] = tot[0:n_grad_rows, :]
        extra_ref[0:nb, :] = tot[b0:b1, :] + tot[b1:b2, :]
        head = tot[n_grad_rows - 2 * SUBLANES:n_grad_rows, :]
        rows_id = lax.broadcasted_iota(jnp.int32, head.shape, 0) + (n_grad_rows - 2 * SUBLANES)
        sq = jnp.where(jnp.logical_and(rows_id >= lo, rows_id < hi), head, 0.0)
        extra_ref[nb:nb + SUBLANES, :] = jnp.zeros((SUBLANES, width), F32) + jnp.sum(sq) * loss_scale

    vm = pl.BlockSpec(memory_space=VMEM)
    return pl.pallas_call(
        body, name="small_sum",
        out_shape=(jax.ShapeDtypeStruct((n_grad_rows, width), F32), jax.ShapeDtypeStruct((nb + SUBLANES, width), F32)),
        in_specs=[vm], out_specs=(vm, vm),
    )(small_all)


def _adam_small_call(ws, gs, ms, vs):
    n = len(ws)

    def body(*refs):
        w_refs, g_refs, m_refs, v_refs = refs[:n], refs[n:2 * n], refs[2 * n:3 * n], refs[3 * n:4 * n]
        d_refs, m2_refs, v2_refs = refs[4 * n:5 * n], refs[5 * n:6 * n], refs[6 * n:7 * n]
        for i in range(n):
            delta, m2, v2 = _adamw(w_refs[i][...], g_refs[i][...], m_refs[i][...], v_refs[i][...])
            d_refs[i][...] = delta
            m2_refs[i][...] = m2
            v2_refs[i][...] = v2

    vm = pl.BlockSpec(memory_space=VMEM)
    shapes = tuple(jax.ShapeDtypeStruct(w.shape, F32) for w in ws)
    outs = pl.pallas_call(body, name="adam_small", out_shape=shapes * 3,
                          in_specs=[vm] * (4 * n), out_specs=(vm,) * (3 * n))(*ws, *gs, *ms, *vs)
    return outs[:n], outs[n:2 * n], outs[2 * n:]


def kernel(x, c, w_ada, b_ada, w_in, conf_dw_w, conf_dw_b, conf_ln_g, conf_ln_b, sc_conv_w, w_out, w_mlp1, w_mlp2, g_final, loss_target, m_w_ada, m_b_ada, m_w_in, m_conf_dw_w, m_conf_dw_b, m_conf_ln_g, m_conf_ln_b, m_sc_conv_w, m_w_out, m_w_mlp1, m_w_mlp2, m_g_final, v_w_ada, v_b_ada, v_w_in, v_conf_dw_w, v_conf_dw_b, v_conf_ln_g, v_conf_ln_b, v_sc_conv_w, v_w_out, v_w_mlp1, v_w_mlp2, v_g_final):
    bsz, seq, d_model = x.shape
    c_half = conf_dw_b.shape[-1]
    n_taps = conf_dw_w.shape[1]
    cc = conf_dw_w.shape[-1]
    a_cols = w_ada.shape[-1]
    tokens = bsz * seq
    me = _dev_index()

    c_pad = jnp.pad(c, ((0, SUBLANES - bsz), (0, 0)))
    b_ada_loc = lax.dynamic_slice(b_ada, (0, me * a_cols), (1, a_cols))
    small_loc = jnp.zeros((HALO, 128), F32)
    small_loc = small_loc.at[:n_taps, :cc].set(conf_dw_w[0]).at[:3, cc:2 * cc].set(sc_conv_w[0])
    (win_t, wout_all), small_all, c_all, mod_rows = _gather_call(
        c_pad, w_ada[0], b_ada_loc, small_loc, [w_in[0].T.astype(BF16), w_out[0].astype(BF16)])
    cw = small_all[:, :, :cc].transpose(1, 0, 2).reshape(HALO, c_half)
    scw = small_all[:, :3, cc:2 * cc].transpose(1, 0, 2).reshape(3, c_half)
    cp = jnp.concatenate([conf_dw_b, conf_ln_g, conf_ln_b, scw, jnp.zeros((2, c_half), F32)], axis=0)
    mod = mod_rows[:, :bsz, :].transpose(1, 0, 2).reshape(bsz, 6, d_model)

    flat = lambda t: t.reshape(tokens, t.shape[-1])
    (proj, a1, cv, mixed, y1), (w1_all, w2_all) = _mixer_fwd_call(
        x, mod, win_t, wout_all, cw, cp, [w_mlp1[0].astype(BF16), w_mlp2[0].astype(BF16)])
    dx1, h2, dy2, u, dz, dmod2, head, dy1 = _mlp_call(
        x, y1, loss_target, mod, w1_all, w2_all, g_final.reshape(1, d_model))
    g_w1 = _wgrad_call(flat(h2), flat(dz), "wgrad_mlp1", owner_cols=w_mlp1.shape[-1])
    g_w2 = _wgrad_call(flat(u), flat(dy2), "wgrad_mlp2")
    g_out = _wgrad_call(flat(mixed), flat(dy1), "wgrad_out")
    (grad_x, dproj, h1, dmod1, cgrad), (s_w1, s_w2, s_out) = _mixer_bwd_call(
        dx1, x, proj, a1, cv, mod, win_t, wout_all, cw, cp, [g_w1, g_w2, g_out])
    g_in_t = _wgrad_call(flat(dproj), flat(h1), "wgrad_in")

    dmod = jnp.concatenate([dmod1[:, :2, :], dmod2[:, 3:4, :], dmod2[:, :3, :]], axis=1)
    n_cg = cgrad.shape[0]
    per_b = 6 * d_model // c_half
    per_b_pad = -(-per_b // SUBLANES) * SUBLANES
    dmod_rows = jnp.pad(dmod.reshape(bsz, per_b, c_half), ((0, 0), (0, per_b_pad - per_b), (0, 0)))
    small = jnp.concatenate([
        cgrad,
        head.reshape(2 * SUBLANES, c_half),
        dmod_rows.reshape(bsz * per_b_pad, c_half),
    ], axis=0)
    s_in, gathered = _tail_scatter_call(g_in_t, small)

    n_head = n_cg + 2 * SUBLANES
    sums, extra = _small_sum_call(
        gathered, n_head, (n_cg + 2, n_cg + 4), (n_head, n_head + per_b_pad, n_head + 2 * per_b_pad), 0.5 / d_model)
    loss = extra[per_b_pad, 0]
    g_b_ada = extra[:per_b].reshape(1, 6 * d_model)
    g_dw_w = lax.dynamic_slice(sums[:n_taps], (0, me * cc), (n_taps, cc))
    g_sc_w = lax.dynamic_slice(sums[HALO:HALO + 3], (0, me * cc), (3, cc))
    g_dw_b, g_ln_g, g_ln_b = sums[HALO + 3:HALO + 4], sums[HALO + 4:HALO + 5], sums[HALO + 5:HALO + 6]
    g_gf = sums[n_cg:n_cg + 2].reshape(1, d_model)

    dmod_all = gathered[:, n_head:, :].reshape(N_DEV, bsz, per_b_pad, c_half)[:, :, :per_b, :]
    dmod_all = dmod_all.reshape(N_DEV, bsz, 6 * d_model)
    dmod_cols = lax.dynamic_slice(dmod_all, (0, 0, me * a_cols), (N_DEV, bsz, a_cols))
    dmod_cols = jnp.pad(dmod_cols, ((0, 0), (0, SUBLANES - bsz), (0, 0))).reshape(N_DEV * SUBLANES, a_cols)
    c_rows = c_all.reshape(N_DEV * SUBLANES, d_model)
    g_ada, d_ada, m_ada, v_ada = _adam_ada_call(c_rows, dmod_cols, w_ada[0], m_w_ada[0], v_w_ada[0])

    gi, di, mi, vi = _adam_slabs_call(s_in, w_in[0].T, m_w_in[0].T, v_w_in[0].T, "adam_w_in")
    gi, di, mi, vi = gi.T, di.T, mi.T, vi.T
    go, do, mo, vo = _adam_slabs_call(s_out, w_out[0], m_w_out[0], v_w_out[0], "adam_w_out")
    g1, d1, m1, v1 = _adam_slabs_call(s_w1, w_mlp1[0], m_w_mlp1[0], v_w_mlp1[0], "adam_w_mlp1")
    g2, d2, m2, v2 = _adam_slabs_call(s_w2, w_mlp2[0], m_w_mlp2[0], v_w_mlp2[0], "adam_w_mlp2")

    small_like = [b_ada, conf_dw_w, conf_dw_b, conf_ln_g, conf_ln_b, sc_conv_w, g_final]
    two_d = lambda t: t.reshape(-1, t.shape[-1])
    small_g = [g_b_ada, g_dw_w, g_dw_b, g_ln_g, g_ln_b, g_sc_w, g_gf]
    sd, sm, sv = _adam_small_call(
        [two_d(t) for t in small_like], small_g,
        [two_d(t) for t in (m_b_ada, m_conf_dw_w, m_conf_dw_b, m_conf_ln_g, m_conf_ln_b, m_sc_conv_w, m_g_final)],
        [two_d(t) for t in (v_b_ada, v_conf_dw_w, v_conf_dw_b, v_conf_ln_g, v_conf_ln_b, v_sc_conv_w, v_g_final)])
    like = lambda parts: [p.reshape(w.shape) for p, w in zip(parts, small_like)]
    sg, sd, sm, sv = like(small_g), like(sd), like(sm), like(sv)

    def ordered(ada, small_list, w_in_, w_out_, w1_, w2_):
        b_ada_, dw_w_, dw_b_, ln_g_, ln_b_, sc_w_, gf_ = small_list
        return [ada[None], b_ada_, w_in_[None], dw_w_, dw_b_, ln_g_, ln_b_, sc_w_, w_out_[None], w1_[None], w2_[None], gf_]

    grads = ordered(g_ada, sg, gi, go, g1, g2)
    deltas = ordered(d_ada, sd, di, do, d1, d2)
    new_m = ordered(m_ada, sm, mi, mo, m1, m2)
    new_v = ordered(v_ada, sv, vi, vo, v1, v2)
    return (loss, grad_x, *grads, *deltas, *new_m, *new_v)
```

```python
import jax
import jax.numpy as jnp
from jax import lax
from jax.experimental import pallas as pl
from jax.experimental.pallas import tpu as pltpu

N_DEV = 8
RMS_EPS = 1e-6
ADAM_LR = 0.001
ADAM_B1 = 0.9
ADAM_B2 = 0.999
ADAM_EPS = 1e-08
ADAM_WD = 0.01
ADAM_STEP = 10

F32 = jnp.float32
BF16 = jnp.bfloat16
MESH = pl.DeviceIdType.MESH
VMEM = pltpu.VMEM
ANY = pl.ANY

HALO = 32
SHORT_HALO = 8
ROW_CHUNK = 32
SUBLANES = 8
V7X_VMEM_LIMIT = 56 * 1024 * 1024
MLP_VMEM_LIMIT = 48 * 1024 * 1024
MLP_TOKEN_TILE = 256
MIXER_FWD_TOKEN_TILE = 512


def _coords():
    return lax.axis_index("x"), lax.axis_index("y"), lax.axis_index("c")


def _dev_index():
    x, y, c = _coords()
    return 4 * x + 2 * y + c


def _peer(k):
    x, y, c = _coords()
    px = 1 - x if (k >> 2) & 1 else x
    py = 1 - y if (k >> 1) & 1 else y
    pc = 1 - c if k & 1 else c
    return (px, py, pc), 4 * px + 2 * py + pc


def _sigmoid(v):
    return jax.nn.sigmoid(v)


def _nt_dot(a, b):
    return lax.dot_general(a, b, (((1,), (1,)), ((), ())), preferred_element_type=F32)


def _nn_dot(a, b):
    return jnp.dot(a, b, preferred_element_type=F32)


def _tn_dot(a, b):
    return lax.dot_general(a, b, (((0,), (0,)), ((), ())), preferred_element_type=F32)


def _token_tile(seq):
    return 256 if seq % 256 == 0 else 64


GATHER_SEM_COLUMNS = 9
HALF_ROW_ALIGN = 16


class _TwoLevelGather:
    def __init__(self, srcs, dsts, send_sems, recv_sems, loc_sems):
        x, y, c = _coords()
        me = 4 * x + 2 * y + c
        sibling, along_x, along_y = (x, y, 1 - c), (1 - x, y, c), (x, 1 - y, c)
        from_x, from_y, diagonal = 4 * (1 - x) + 2 * y + c, 4 * x + 2 * (1 - y) + c, 4 * (1 - x) + 2 * (1 - y) + c

        def remote(src, dst, a, col, to):
            return pltpu.make_async_remote_copy(
                src_ref=src, dst_ref=dst, send_sem=send_sems.at[a, col], recv_sem=recv_sems.at[a, col],
                device_id=to, device_id_type=MESH)

        def onward(block, a, col, to):
            return remote(block, block, a, col, to)

        self.first, self.arrivals, self.second, self.halves, self.third = [], [], [], [], []
        for a, (src, dst) in enumerate(zip(srcs, dsts)):
            half = src.shape[0] // (2 * HALF_ROW_ALIGN) * HALF_ROW_ALIGN
            lower, upper = pl.ds(0, half), pl.ds(half, src.shape[0] - half)
            neighbours = [remote(src, dst.at[me], a, 1, along_x), remote(src, dst.at[me], a, 2, along_y)]
            self.first += [pltpu.make_async_copy(src, dst.at[me], loc_sems.at[a]),
                           remote(src, dst.at[me], a, 0, sibling)]
            self.arrivals += neighbours
            halves = [onward(dst.at[from_x, lower], a, 3, along_y), onward(dst.at[from_y, upper], a, 4, along_x)]
            self.halves += halves
            self.second += [[halves[0], onward(dst.at[from_x], a, 5, sibling)],
                            [halves[1], onward(dst.at[from_y], a, 6, sibling)]]
            self.third += [onward(dst.at[diagonal, lower], a, 7, sibling),
                           onward(dst.at[diagonal, upper], a, 8, sibling)]

    def start(self):
        for cp in self.first + self.arrivals:
            cp.start()

    def forward_arrivals(self):
        for arrival, sends in zip(self.arrivals, self.second):
            arrival.wait_recv()
            for cp in sends:
                cp.start()

    def forward_diagonal(self):
        for half, to_sibling in zip(self.halves, self.third):
            half.wait_recv()
            to_sibling.start()

    def finish(self):
        for cp in self.arrivals + self.halves:
            cp.wait_send()
        for cp in self.first + [sends[1] for sends in self.second] + self.third:
            cp.wait()


N_CHIPS = N_DEV // 2
PAIR_SUM_ROWS = 256


class _TwoLevelScatter:
    def __init__(self, partials, sums, from_sibling, pair, first_send, first_recv, second_send, second_recv,
                 local_sems):
        x, y, c = _coords()
        sibling = (x, y, 1 - c)
        chips = [(1 - x, y), (x, 1 - y), (1 - x, 1 - y)]
        self.c, self.my_chip = c, 2 * x + y
        self.partials, self.sums, self.from_sibling, self.pair = partials, sums, from_sibling, pair

        def remote(src, dst, send_sem, recv_sem, to):
            return pltpu.make_async_remote_copy(src_ref=src, dst_ref=dst, send_sem=send_sem, recv_sem=recv_sem,
                                                device_id=to, device_id_type=MESH)

        self.first, self.second, self.local = [], [], []
        for a in range(len(partials)):
            self.local.append(pltpu.make_async_copy(pair[a].at[self.my_chip], sums[a].at[self.my_chip],
                                                    local_sems.at[a]))
            self.first += [remote(partials[a].at[2 * q + (1 - c)], from_sibling[a].at[q],
                                  first_send.at[a, q], first_recv.at[a, q], sibling) for q in range(N_CHIPS)]
            self.second += [remote(pair[a].at[2 * cx + cy], sums[a].at[self.my_chip],
                                   second_send.at[a, j], second_recv.at[a, j], (cx, cy, c))
                            for j, (cx, cy) in enumerate(chips)]

    def start(self):
        for cp in self.first:
            cp.start()

    def pair_sums(self):
        for cp in self.first:
            cp.wait()
        for mine, theirs, both in zip(self.partials, self.from_sibling, self.pair):
            rows = mine.shape[1]
            for q in range(N_CHIPS):
                for r0 in range(0, rows, PAIR_SUM_ROWS):
                    part = pl.ds(r0, min(PAIR_SUM_ROWS, rows - r0))
                    both[q, part, :] = (mine[2 * q + self.c, part, :].astype(F32)
                                        + theirs[q, part, :].astype(F32)).astype(BF16)
        for cp in self.second + self.local:
            cp.start()

    def finish(self):
        for cp in self.second + self.local:
            cp.wait()


def _scatter_scratch(partials):
    n = len(partials)
    zones = [pltpu.VMEM((N_CHIPS,) + p.shape[1:], BF16) for p in partials]
    return zones + zones + [pltpu.SemaphoreType.DMA((n, N_CHIPS)), pltpu.SemaphoreType.DMA((n, N_CHIPS)),
                            pltpu.SemaphoreType.DMA((n, N_CHIPS - 1)), pltpu.SemaphoreType.DMA((n, N_CHIPS - 1)),
                            pltpu.SemaphoreType.DMA((n,))]


def _exchange_sems(n_arrays, columns=GATHER_SEM_COLUMNS):
    return [pltpu.SemaphoreType.DMA((n_arrays, columns)), pltpu.SemaphoreType.DMA((n_arrays, columns)),
            pltpu.SemaphoreType.DMA((n_arrays,))]


def _gather_call(c_pad, w_ada, b_ada_loc, small_loc, big_shards):
    n_big = len(big_shards)
    d_model = c_pad.shape[1]
    a_cols = w_ada.shape[1]

    def body(c_ref, wada_ref, bada_ref, small_ref, *rest):
        big_in = rest[:n_big]
        big_out = rest[n_big:2 * n_big]
        small_all, c_all, mod_rows = rest[2 * n_big:2 * n_big + 3]
        modcols, big_send, big_recv, loc_sem, s_send, s_recv = rest[2 * n_big + 3:]
        me = _dev_index()
        big = _TwoLevelGather(big_in, big_out, big_send, big_recv, loc_sem)
        big.start()

        small_all[me] = small_ref[...]
        c_all[me] = c_ref[...]
        first = []
        for k in range(1, N_DEV):
            peer, _ = _peer(k)
            for i, (src, dst) in enumerate(((small_ref, small_all), (c_ref, c_all))):
                cp = pltpu.make_async_remote_copy(
                    src_ref=src, dst_ref=dst.at[me],
                    send_sem=s_send.at[i, k - 1], recv_sem=s_recv.at[i, k - 1],
                    device_id=peer, device_id_type=MESH)
                cp.start()
                first.append(cp)
        for cp in first:
            cp.wait()

        c_rows = c_all[...].reshape(N_DEV * SUBLANES, d_model)
        c_act = c_rows * _sigmoid(c_rows)
        modcols[...] = _nn_dot(c_act.astype(BF16), wada_ref[...].astype(BF16)) + bada_ref[...]
        mod_rows[me] = modcols[pl.ds(pl.multiple_of(me * SUBLANES, SUBLANES), SUBLANES), :]
        second = []
        for k in range(1, N_DEV):
            peer, pidx = _peer(k)
            cp = pltpu.make_async_remote_copy(
                src_ref=modcols.at[pl.ds(pl.multiple_of(pidx * SUBLANES, SUBLANES), SUBLANES), :],
                dst_ref=mod_rows.at[me],
                send_sem=s_send.at[2, k - 1], recv_sem=s_recv.at[2, k - 1],
                device_id=peer, device_id_type=MESH)
            cp.start()
            second.append(cp)
        big.forward_arrivals()
        for cp in second:
            cp.wait()
        big.forward_diagonal()
        big.finish()

    out_shape = tuple(jax.ShapeDtypeStruct((N_DEV,) + s.shape, s.dtype) for s in big_shards) + (
        jax.ShapeDtypeStruct((N_DEV,) + small_loc.shape, F32),
        jax.ShapeDtypeStruct((N_DEV, SUBLANES, d_model), F32),
        jax.ShapeDtypeStruct((N_DEV, SUBLANES, a_cols), F32),
    )
    vm = pl.BlockSpec(memory_space=VMEM)
    hbm = pl.BlockSpec(memory_space=ANY)
    outs = pl.pallas_call(
        body, name="gather_weights_mod", out_shape=out_shape,
        in_specs=[vm, vm, vm, vm] + [hbm] * n_big,
        out_specs=tuple([hbm] * n_big + [vm, vm, vm]),
        scratch_shapes=[
            pltpu.VMEM((N_DEV * SUBLANES, a_cols), F32),
            *_exchange_sems(n_big),
            pltpu.SemaphoreType.DMA((3, N_DEV - 1)),
            pltpu.SemaphoreType.DMA((3, N_DEV - 1)),
        ],
        compiler_params=pltpu.CompilerParams(vmem_limit_bytes=V7X_VMEM_LIMIT),
    )(c_pad, w_ada, b_ada_loc, small_loc, *big_shards)
    return outs[:n_big], outs[n_big], outs[n_big + 1], outs[n_big + 2]


def _shifted_rows_count(tm):
    return tm + HALO - SUBLANES


def _fill_shifted(ext, shifted, tm):
    for s in range(1, SUBLANES):
        shifted[s - 1] = ext[s:s + _shifted_rows_count(tm), :]


def _shifted_rows(ext, shifted, start, rows):
    phase = start % SUBLANES
    aligned = start - phase
    if phase == 0:
        return ext[aligned:aligned + rows, :]
    return shifted[phase - 1, aligned:aligned + rows, :]


def _layer_norm_parts(a1):
    mu = jnp.mean(a1, axis=-1, keepdims=True)
    xc = a1 - mu
    rstd = lax.rsqrt(jnp.mean(xc * xc, axis=-1, keepdims=True) + RMS_EPS)
    return xc * rstd, rstd


def _mixer_fwd_call(x, mod, win_t, wout, cw, cp, later_shards):
    n_later = len(later_shards)
    bsz, seq, d_model = x.shape
    c_half = cw.shape[1]
    n_taps = 31
    d_in = win_t.shape[0] * win_t.shape[1]
    tm = MIXER_FWD_TOKEN_TILE if seq % MIXER_FWD_TOKEN_TILE == 0 else _token_tile(seq)
    nt = seq // tm
    arrivals_step, diagonal_step = (7 * bsz * nt) // 16, (11 * bsz * nt) // 16

    def body(x_ref, mod_ref, win_ref, wout_ref, cw_ref, cp_ref, *rest):
        shard_refs, rest = rest[:n_later], rest[n_later:]
        proj_ref, a1_ref, cv_ref, mixed_ref, y1_ref = rest[:5]
        gathered_refs, rest = rest[5:5 + n_later], rest[5 + n_later:]
        aext, qext, ashift, send_sems, recv_sems, loc_sems = rest
        b, t = pl.program_id(0), pl.program_id(1)

        step = b * nt + t

        @pl.when(step == 0)
        def _():
            _TwoLevelGather(shard_refs, gathered_refs, send_sems, recv_sems, loc_sems).start()

        @pl.when(step == arrivals_step)
        def _():
            _TwoLevelGather(shard_refs, gathered_refs, send_sems, recv_sems, loc_sems).forward_arrivals()

        @pl.when(step == diagonal_step)
        def _():
            _TwoLevelGather(shard_refs, gathered_refs, send_sems, recv_sems, loc_sems).forward_diagonal()

        xv = x_ref[...]
        sh1, sc1 = mod_ref[0:1, :], mod_ref[1:2, :]
        r1 = lax.rsqrt(jnp.mean(xv * xv, axis=-1, keepdims=True) + RMS_EPS)
        h1 = (xv * r1) * (1.0 + sc1) + sh1
        proj = _nt_dot(h1.astype(BF16), win_ref[...].reshape(d_in, d_model))
        proj_ref[...] = proj
        val, gate = proj[:, 0:c_half], proj[:, c_half:2 * c_half]
        s_b, s_c, s_h = proj[:, 2 * c_half:3 * c_half], proj[:, 3 * c_half:4 * c_half], proj[:, 4 * c_half:5 * c_half]

        @pl.when(t == 0)
        def _():
            aext[0:HALO, :] = jnp.zeros((HALO, c_half), F32)
            qext[0:SHORT_HALO, :] = jnp.zeros((SHORT_HALO, c_half), F32)

        @pl.when(t > 0)
        def _():
            aext[0:HALO, :] = aext[tm:tm + HALO, :]
            qext[0:SHORT_HALO, :] = qext[tm:tm + SHORT_HALO, :]
        aext[HALO:HALO + tm, :] = val * _sigmoid(gate)
        qext[SHORT_HALO:SHORT_HALO + tm, :] = s_c * s_h

        base = HALO - (n_taps - 1)
        _fill_shifted(aext, ashift, tm)
        for r0 in range(0, tm, ROW_CHUNK):
            acc = jnp.zeros((ROW_CHUNK, c_half), F32)
            for k in range(n_taps):
                acc = acc + cw_ref[k:k + 1, :] * _shifted_rows(aext, ashift, r0 + base + k, ROW_CHUNK)
            a1_ref[r0:r0 + ROW_CHUNK, :] = acc + cp_ref[0:1, :]
        sbase = SHORT_HALO - 2
        conv3 = cp_ref[3:4, :] * qext[sbase:sbase + tm, :]
        conv3 = conv3 + cp_ref[4:5, :] * qext[sbase + 1:sbase + 1 + tm, :]
        conv3 = conv3 + cp_ref[5:6, :] * qext[sbase + 2:sbase + 2 + tm, :]
        cv_ref[...] = conv3

        norm, _ = _layer_norm_parts(a1_ref[...])
        a2 = norm * cp_ref[1:2, :] + cp_ref[2:3, :]
        mixed = jnp.concatenate([a2 * _sigmoid(a2), s_b * conv3], axis=-1).astype(BF16)
        mixed_ref[...] = mixed
        y1 = _nn_dot(mixed, wout_ref[...].reshape(d_model, d_model))
        y1_ref[...] = y1

        @pl.when(step == bsz * nt - 1)
        def _():
            _TwoLevelGather(shard_refs, gathered_refs, send_sems, recv_sems, loc_sems).finish()

    hbm = pl.BlockSpec(memory_space=ANY)

    def tok(width):
        return pl.BlockSpec((None, tm, width), lambda b, t: (b, t, 0))

    def const(shape):
        return pl.BlockSpec(shape, lambda b, t: (0,) * len(shape))

    def resident(shape):
        return pl.BlockSpec(shape, lambda b, t: (0,) * len(shape), pipeline_mode=pl.Buffered(1))

    out_shape = (
        jax.ShapeDtypeStruct((bsz, seq, d_in), F32),
        jax.ShapeDtypeStruct((bsz, seq, c_half), F32),
        jax.ShapeDtypeStruct((bsz, seq, c_half), F32),
        jax.ShapeDtypeStruct((bsz, seq, d_model), BF16),
        jax.ShapeDtypeStruct((bsz, seq, d_model), F32),
    ) + tuple(jax.ShapeDtypeStruct((N_DEV,) + s.shape, s.dtype) for s in later_shards)
    outs = pl.pallas_call(
        body, name="mixer_fwd", out_shape=out_shape, grid=(bsz, nt),
        in_specs=[tok(d_model), pl.BlockSpec((None, 6, d_model), lambda b, t: (b, 0, 0)),
                  resident(win_t.shape), resident(wout.shape), const(cw.shape), const(cp.shape)] + [hbm] * n_later,
        out_specs=(tok(d_in), tok(c_half), tok(c_half), tok(d_model), tok(d_model)) + (hbm,) * n_later,
        scratch_shapes=[pltpu.VMEM((tm + HALO, c_half), F32), pltpu.VMEM((tm + SHORT_HALO, c_half), F32),
                        pltpu.VMEM((SUBLANES - 1, _shifted_rows_count(tm), c_half), F32)]
        + _exchange_sems(n_later),
        compiler_params=pltpu.CompilerParams(
            dimension_semantics=("arbitrary", "arbitrary"), vmem_limit_bytes=V7X_VMEM_LIMIT),
    )(x, mod, win_t, wout, cw, cp, *later_shards)
    return outs[:5], outs[5:]


def _mlp_call(x, y1, target, mod, w1, w2, g_final):
    bsz, seq, d_model = x.shape
    n_blk, _, f_blk = w1.shape
    d_ff = n_blk * f_blk
    tm = MLP_TOKEN_TILE if seq % MLP_TOKEN_TILE == 0 else _token_tile(seq)
    nt = seq // tm

    def body(x_ref, y1_ref, tgt_ref, mod_ref, w1_ref, w2_ref, gf_ref,
             dx1_ref, h2_ref, dy2_ref, u_ref, dz_ref, dmod_ref, head_ref, dy1_ref, relu_scr):
        b, t = pl.program_id(0), pl.program_id(1)
        x1v = x_ref[...] + mod_ref[2:3, :] * y1_ref[...]
        sh2, sc2, g2 = mod_ref[3:4, :], mod_ref[4:5, :], mod_ref[5:6, :]
        gf = gf_ref[...]
        r2 = lax.rsqrt(jnp.mean(x1v * x1v, axis=-1, keepdims=True) + RMS_EPS)
        xn2 = x1v * r2
        h2 = (xn2 * (1.0 + sc2) + sh2).astype(BF16)
        h2_ref[...] = h2
        y2 = jnp.zeros((tm, d_model), F32)
        for j in range(n_blk):
            cols = slice(j * f_blk, (j + 1) * f_blk)
            rz = jnp.maximum(_nn_dot(h2, w1_ref[j]), 0.0)
            relu_scr[:, cols] = rz
            ub = (rz * rz).astype(BF16)
            u_ref[:, cols] = ub
            y2 = y2 + _nn_dot(ub, w2_ref[j])
        x2 = x1v + g2 * y2
        r3 = lax.rsqrt(jnp.mean(x2 * x2, axis=-1, keepdims=True) + RMS_EPS)
        xn3 = x2 * r3
        diff = xn3 * gf - tgt_ref[...]
        dout = diff * (1.0 / d_model)

        @pl.when(jnp.logical_and(b == 0, t == 0))
        def _():
            head_ref[...] = jnp.zeros(head_ref.shape, F32)

        @pl.when(t == 0)
        def _():
            dmod_ref[...] = jnp.zeros(dmod_ref.shape, F32)

        head_ref[0:1, :] += jnp.sum(dout * xn3, axis=0, keepdims=True)
        head_ref[1:2, :] += jnp.sum(diff * diff, axis=0, keepdims=True)
        dxn3 = dout * gf
        dx2 = r3 * (dxn3 - xn3 * jnp.mean(dxn3 * xn3, axis=-1, keepdims=True))
        dmod_ref[2:3, :] += jnp.sum(dx2 * y2, axis=0, keepdims=True)
        dy2 = (g2 * dx2).astype(BF16)
        dy2_ref[...] = dy2
        dh2 = jnp.zeros((tm, d_model), F32)
        for j in range(n_blk):
            cols = slice(j * f_blk, (j + 1) * f_blk)
            dz = (_nt_dot(dy2, w2_ref[j]) * (2.0 * relu_scr[:, cols])).astype(BF16)
            dz_ref[:, cols] = dz
            dh2 = dh2 + _nt_dot(dz, w1_ref[j])
        dmod_ref[0:1, :] += jnp.sum(dh2, axis=0, keepdims=True)
        dmod_ref[1:2, :] += jnp.sum(dh2 * xn2, axis=0, keepdims=True)
        dxn2 = dh2 * (1.0 + sc2)
        dx1 = dx2 + r2 * (dxn2 - xn2 * jnp.mean(dxn2 * xn2, axis=-1, keepdims=True))
        dx1_ref[...] = dx1
        dy1_ref[...] = (mod_ref[2:3, :] * dx1).astype(BF16)
        dmod_ref[3:4, :] += jnp.sum(dx1 * y1_ref[...], axis=0, keepdims=True)

    def tok(width):
        return pl.BlockSpec((None, tm, width), lambda b, t: (b, t, 0))

    def const(shape):
        return pl.BlockSpec(shape, lambda b, t: (0,) * len(shape))

    def resident(shape):
        return pl.BlockSpec(shape, lambda b, t: (0,) * len(shape), pipeline_mode=pl.Buffered(1))

    out_shape = (
        jax.ShapeDtypeStruct((bsz, seq, d_model), F32),
        jax.ShapeDtypeStruct((bsz, seq, d_model), BF16),
        jax.ShapeDtypeStruct((bsz, seq, d_model), BF16),
        jax.ShapeDtypeStruct((bsz, seq, d_ff), BF16),
        jax.ShapeDtypeStruct((bsz, seq, d_ff), BF16),
        jax.ShapeDtypeStruct((bsz, SUBLANES, d_model), F32),
        jax.ShapeDtypeStruct((SUBLANES, d_model), F32),
        jax.ShapeDtypeStruct((bsz, seq, d_model), BF16),
    )
    return pl.pallas_call(
        body, name="mlp_fwd_bwd", out_shape=out_shape, grid=(bsz, nt),
        in_specs=[tok(d_model), tok(d_model), tok(d_model),
                  pl.BlockSpec((None, 6, d_model), lambda b, t: (b, 0, 0)),
                  resident(w1.shape), resident(w2.shape), const(g_final.shape)],
        out_specs=(tok(d_model), tok(d_model), tok(d_model), tok(d_ff), tok(d_ff),
                   pl.BlockSpec((None, SUBLANES, d_model), lambda b, t: (b, 0, 0)),
                   const((SUBLANES, d_model)), tok(d_model)),
        scratch_shapes=[pltpu.VMEM((tm, d_ff), F32)],
        compiler_params=pltpu.CompilerParams(
            dimension_semantics=("arbitrary", "arbitrary"), vmem_limit_bytes=MLP_VMEM_LIMIT),
    )(x, y1, target, mod, w1, w2, g_final)


def _mixer_bwd_call(dx1, x, proj, a1, cv, mod, win_t, wout, cw, cp):
    bsz, seq, d_model = x.shape
    c_half = cw.shape[1]
    n_taps = 31
    d_in = win_t.shape[0] * win_t.shape[1]
    tm = _token_tile(seq)
    nt = seq // tm

    def body(dx1_ref, x_ref, proj_ref, a1_ref, cv_ref, mod_ref, win_ref, wout_ref, cw_ref, cp_ref,
             gx_ref, dproj_ref, h1_ref, dmod_ref, cgrad_ref,
             dext, cext, a0_scr, da0_scr, tap_acc, row_acc, dshift):
        b, step = pl.program_id(0), pl.program_id(1)
        first = jnp.logical_and(b == 0, step == 0)
        last = jnp.logical_and(b == bsz - 1, step == nt - 1)
        dx1v = dx1_ref[...]
        xv = x_ref[...]
        sh1, sc1, g1 = mod_ref[0:1, :], mod_ref[1:2, :], mod_ref[2:3, :]

        @pl.when(first)
        def _():
            tap_acc[...] = jnp.zeros(tap_acc.shape, F32)
            row_acc[...] = jnp.zeros(row_acc.shape, F32)

        @pl.when(step == 0)
        def _():
            dmod_ref[...] = jnp.zeros(dmod_ref.shape, F32)
            dext[tm:tm + HALO, :] = jnp.zeros((HALO, c_half), F32)
            cext[tm:tm + SHORT_HALO, :] = jnp.zeros((SHORT_HALO, c_half), F32)

        @pl.when(step > 0)
        def _():
            dext[tm:tm + HALO, :] = dext[0:HALO, :]
            cext[tm:tm + SHORT_HALO, :] = cext[0:SHORT_HALO, :]

        dy1 = (g1 * dx1v).astype(BF16)
        dmixed = _nt_dot(dy1, wout_ref[...].reshape(d_model, d_model))
        d_a, d_s = dmixed[:, 0:c_half], dmixed[:, c_half:2 * c_half]

        val, gate = proj_ref[:, 0:c_half], proj_ref[:, c_half:2 * c_half]
        s_b = proj_ref[:, 2 * c_half:3 * c_half]
        s_c, s_h = proj_ref[:, 3 * c_half:4 * c_half], proj_ref[:, 4 * c_half:5 * c_half]

        d_sb = d_s * cv_ref[...]
        cext[0:tm, :] = d_s * s_b
        q = s_c * s_h
        dq = jnp.zeros((tm, c_half), F32)
        for k in range(3):
            shifted = cext[2 - k:2 - k + tm, :]
            dq = dq + cp_ref[3 + k:4 + k, :] * shifted
            row_acc[k:k + 1, :] += jnp.sum(q * shifted, axis=0, keepdims=True)
        d_sc, d_sh = dq * s_h, dq * s_c

        norm, rstd = _layer_norm_parts(a1_ref[...])
        ln_g = cp_ref[1:2, :]
        a2 = norm * ln_g + cp_ref[2:3, :]
        sg = _sigmoid(a2)
        d_a2 = d_a * (sg * (1.0 + a2 * (1.0 - sg)))
        row_acc[4:5, :] += jnp.sum(d_a2 * norm, axis=0, keepdims=True)
        row_acc[5:6, :] += jnp.sum(d_a2, axis=0, keepdims=True)
        d_n = d_a2 * ln_g
        d_a1 = rstd * (d_n - jnp.mean(d_n, axis=-1, keepdims=True)
                       - norm * jnp.mean(d_n * norm, axis=-1, keepdims=True))
        row_acc[3:4, :] += jnp.sum(d_a1, axis=0, keepdims=True)
        dext[0:tm, :] = d_a1
        sig_g = _sigmoid(gate)
        a0_scr[...] = val * sig_g

        _fill_shifted(dext, dshift, tm)
        for r0 in range(0, tm, ROW_CHUNK):
            a0c = a0_scr[r0:r0 + ROW_CHUNK, :]
            acc = jnp.zeros((ROW_CHUNK, c_half), F32)
            for k in range(n_taps):
                shifted = _shifted_rows(dext, dshift, r0 + (n_taps - 1) - k, ROW_CHUNK)
                acc = acc + cw_ref[k:k + 1, :] * shifted
                prod = a0c * shifted
                part = prod[0:SUBLANES, :]
                for g in range(1, ROW_CHUNK // SUBLANES):
                    part = part + prod[g * SUBLANES:(g + 1) * SUBLANES, :]
                tap_acc[k * SUBLANES:(k + 1) * SUBLANES, :] += part
            da0_scr[r0:r0 + ROW_CHUNK, :] = acc
        d_a0 = da0_scr[...]
        d_val = d_a0 * sig_g
        d_gate = d_a0 * val * sig_g * (1.0 - sig_g)

        dproj = jnp.concatenate([d_val, d_gate, d_sb, d_sc, d_sh], axis=-1).astype(BF16)
        dproj_ref[...] = dproj
        dh1 = _nn_dot(dproj, win_ref[...].reshape(d_in, d_model))
        r1 = lax.rsqrt(jnp.mean(xv * xv, axis=-1, keepdims=True) + RMS_EPS)
        xn1 = xv * r1
        h1_ref[...] = (xn1 * (1.0 + sc1) + sh1).astype(BF16)
        dmod_ref[0:1, :] += jnp.sum(dh1, axis=0, keepdims=True)
        dmod_ref[1:2, :] += jnp.sum(dh1 * xn1, axis=0, keepdims=True)
        dxn1 = dh1 * (1.0 + sc1)
        gx_ref[...] = dx1v + r1 * (dxn1 - xn1 * jnp.mean(dxn1 * xn1, axis=-1, keepdims=True))

        @pl.when(last)
        def _():
            taps = jnp.sum(tap_acc[...].reshape(HALO, SUBLANES, c_half), axis=1)
            cgrad_ref[0:HALO, :] = taps
            cgrad_ref[HALO:HALO + SUBLANES, :] = row_acc[...]

    def tok(width):
        return pl.BlockSpec((None, tm, width), lambda b, s: (b, nt - 1 - s, 0))

    def const(shape):
        return pl.BlockSpec(shape, lambda b, s: (0,) * len(shape))

    mod_spec = pl.BlockSpec((None, 6, d_model), lambda b, s: (b, 0, 0))
    out_shape = (
        jax.ShapeDtypeStruct((bsz, seq, d_model), F32),
        jax.ShapeDtypeStruct((bsz, seq, d_in), BF16),
        jax.ShapeDtypeStruct((bsz, seq, d_model), BF16),
        jax.ShapeDtypeStruct((bsz, SUBLANES, d_model), F32),
        jax.ShapeDtypeStruct((HALO + SUBLANES, c_half), F32),
    )
    return pl.pallas_call(
        body, name="mixer_bwd", out_shape=out_shape, grid=(bsz, nt),
        in_specs=[tok(d_model), tok(d_model), tok(d_in), tok(c_half), tok(c_half), mod_spec,
                  const(win_t.shape), const(wout.shape), const(cw.shape), const(cp.shape)],
        out_specs=(tok(d_model), tok(d_in), tok(d_model),
                   pl.BlockSpec((None, SUBLANES, d_model), lambda b, s: (b, 0, 0)),
                   const((HALO + SUBLANES, c_half))),
        scratch_shapes=[
            pltpu.VMEM((tm + HALO, c_half), F32), pltpu.VMEM((tm + SHORT_HALO, c_half), F32),
            pltpu.VMEM((tm, c_half), F32), pltpu.VMEM((tm, c_half), F32),
            pltpu.VMEM((HALO * SUBLANES, c_half), F32), pltpu.VMEM((SUBLANES, c_half), F32),
            pltpu.VMEM((SUBLANES - 1, _shifted_rows_count(tm), c_half), F32),
        ],
        compiler_params=pltpu.CompilerParams(
            dimension_semantics=("arbitrary", "arbitrary"), vmem_limit_bytes=V7X_VMEM_LIMIT),
    )(dx1, x, proj, a1, cv, mod, win_t, wout, cw, cp)


def _largest_divisor(n, cap, multiple):
    best = None
    for cand in range(multiple, min(n, cap) + 1, multiple):
        if n % cand == 0:
            best = cand
    return best if best is not None else n


WGRAD_TOKENS_PER_STEP = 2048
WGRAD_COLS_PER_STEP = 1024


def _wgrad_call(a, b, name, owner_cols=None, scatter=()):
    n_sc = len(scatter)
    tokens, m_dim = a.shape
    n_dim = b.shape[1]
    bk = _largest_divisor(tokens, WGRAD_TOKENS_PER_STEP, 128)
    n_k = tokens // bk
    if owner_cols is None:
        bm = _largest_divisor(m_dim, 1024, m_dim // N_DEV)
        bn = n_dim
        owners = 1
        out_shape = jax.ShapeDtypeStruct((m_dim, n_dim), BF16)
        out_spec = pl.BlockSpec((bm, bn), lambda i, j, k: (i, j))
    else:
        bm = m_dim
        bn = _largest_divisor(n_dim, WGRAD_COLS_PER_STEP, owner_cols)
        owners = bn // owner_cols
        out_shape = jax.ShapeDtypeStruct((n_dim // owner_cols, m_dim, owner_cols), BF16)
        out_spec = pl.BlockSpec((owners, bm, owner_cols), lambda i, j, k: (j, i, 0))

    grid = (m_dim // bm, n_dim // bn, n_k)
    n_steps = grid[0] * grid[1] * grid[2]
    pair_step = min(max(1, n_steps // 5), n_steps - 1)

    def body(a_ref, b_ref, *rest):
        part_refs, o_ref, sum_refs, acc = rest[:n_sc], rest[n_sc], rest[n_sc + 1:2 * n_sc + 1], rest[2 * n_sc + 1]
        k = pl.program_id(2)
        step = (pl.program_id(0) * grid[1] + pl.program_id(1)) * n_k + k

        def exchange():
            extra = rest[2 * n_sc + 2:]
            return _TwoLevelScatter(part_refs, sum_refs, extra[:n_sc], extra[n_sc:2 * n_sc], *extra[2 * n_sc:])

        if n_sc:
            @pl.when(step == 0)
            def _():
                exchange().start()

            @pl.when(step == pair_step)
            def _():
                exchange().pair_sums()

        @pl.when(k == 0)
        def _():
            acc[...] = jnp.zeros(acc.shape, F32)

        acc[...] += _tn_dot(a_ref[...], b_ref[...])

        @pl.when(k == n_k - 1)
        def _():
            if owner_cols is None:
                o_ref[...] = acc[...].astype(BF16)
            else:
                for q in range(owners):
                    o_ref[q] = acc[:, q * owner_cols:(q + 1) * owner_cols].astype(BF16)

        if n_sc:
            @pl.when(step == n_steps - 1)
            def _():
                exchange().finish()

    whole = pl.BlockSpec(memory_space=VMEM)
    sum_shapes = tuple(jax.ShapeDtypeStruct((N_CHIPS,) + p.shape[1:], BF16) for p in scatter)
    outs = pl.pallas_call(
        body, name=name, out_shape=(out_shape,) + sum_shapes, grid=grid,
        in_specs=[pl.BlockSpec((bk, bm), lambda i, j, k: (k, i)), pl.BlockSpec((bk, bn), lambda i, j, k: (k, j))]
        + [whole] * n_sc,
        out_specs=(out_spec,) + (pl.BlockSpec(memory_space=ANY),) * n_sc,
        scratch_shapes=[pltpu.VMEM((bm, bn), F32)] + (_scatter_scratch(scatter) if n_sc else []),
        compiler_params=pltpu.CompilerParams(
            dimension_semantics=("arbitrary", "arbitrary", "arbitrary"), vmem_limit_bytes=V7X_VMEM_LIMIT),
    )(a, b, *scatter)
    out = outs[0]
    if owner_cols is None:
        out = out.reshape(N_DEV, m_dim // N_DEV, n_dim)
    return (out, list(outs[1:])) if n_sc else out


def _tail_scatter_call(partial, small):
    def body(g_ref, small_ref, out_ref, small_all, from_sibling, pair,
             p1_send, p1_recv, p2_send, p2_recv, p_loc, s_send, s_recv, s_loc):
        gather = _TwoLevelGather([small_ref], [small_all], s_send, s_recv, s_loc)
        scatter = _TwoLevelScatter([g_ref], [out_ref], [from_sibling], [pair],
                                   p1_send, p1_recv, p2_send, p2_recv, p_loc)
        gather.start()
        scatter.start()
        scatter.pair_sums()
        gather.forward_arrivals()
        scatter.finish()
        gather.forward_diagonal()
        gather.finish()

    vm = pl.BlockSpec(memory_space=VMEM)
    return pl.pallas_call(
        body, name="scatter_tail",
        out_shape=(jax.ShapeDtypeStruct((N_CHIPS,) + partial.shape[1:], BF16),
                   jax.ShapeDtypeStruct((N_DEV,) + small.shape, F32)),
        in_specs=[vm, vm], out_specs=(vm, vm),
        scratch_shapes=_scatter_scratch([partial]) + _exchange_sems(1),
        compiler_params=pltpu.CompilerParams(vmem_limit_bytes=V7X_VMEM_LIMIT),
    )(partial, small)


def _adamw(w, g, m, v):
    m2 = ADAM_B1 * m + (1.0 - ADAM_B1) * g
    v2 = ADAM_B2 * v + (1.0 - ADAM_B2) * (g * g)
    m_hat = m2 / (1.0 - ADAM_B1 ** ADAM_STEP)
    v_hat = v2 / (1.0 - ADAM_B2 ** ADAM_STEP)
    delta = -ADAM_LR * (m_hat / (jnp.sqrt(v_hat) + ADAM_EPS) + ADAM_WD * w)
    return delta, m2, v2


def _adam_slabs_call(slabs, w, m, v, name):
    rows, cols = w.shape
    n_slabs = slabs.shape[0]
    tr = _largest_divisor(rows, 256, 2 * SUBLANES)

    def body(s_ref, w_ref, m_ref, v_ref, g_ref, d_ref, m2_ref, v2_ref):
        g = s_ref[0].astype(F32)
        for k in range(1, n_slabs):
            g = g + s_ref[k].astype(F32)
        delta, m2, v2 = _adamw(w_ref[...], g, m_ref[...], v_ref[...])
        g_ref[...] = g
        d_ref[...] = delta
        m2_ref[...] = m2
        v2_ref[...] = v2

    tile = pl.BlockSpec((tr, cols), lambda i: (i, 0))
    shp = jax.ShapeDtypeStruct((rows, cols), F32)
    return pl.pallas_call(
        body, name=name, out_shape=(shp, shp, shp, shp), grid=(rows // tr,),
        in_specs=[pl.BlockSpec((n_slabs, tr, cols), lambda i: (0, i, 0)), tile, tile, tile],
        out_specs=(tile, tile, tile, tile),
        compiler_params=pltpu.CompilerParams(dimension_semantics=("arbitrary",), vmem_limit_bytes=V7X_VMEM_LIMIT),
    )(slabs, w, m, v)


def _adam_ada_call(c_rows, dmod_cols, w, m, v):
    rows, cols = w.shape
    n_rows = c_rows.shape[0]
    tr = _largest_divisor(rows, 256, 128)

    def body(c_ref, dm_ref, w_ref, m_ref, v_ref, g_ref, d_ref, m2_ref, v2_ref):
        cv = c_ref[...]
        c_act = (cv * _sigmoid(cv)).astype(BF16)
        g = _tn_dot(c_act, dm_ref[...].astype(BF16))
        delta, m2, v2 = _adamw(w_ref[...], g, m_ref[...], v_ref[...])
        g_ref[...] = g
        d_ref[...] = delta
        m2_ref[...] = m2
        v2_ref[...] = v2

    tile = pl.BlockSpec((tr, cols), lambda i: (i, 0))
    shp = jax.ShapeDtypeStruct((rows, cols), F32)
    return pl.pallas_call(
        body, name="adam_w_ada", out_shape=(shp, shp, shp, shp), grid=(rows // tr,),
        in_specs=[pl.BlockSpec((n_rows, tr), lambda i: (0, i)), pl.BlockSpec((n_rows, cols), lambda i: (0, 0)),
                  tile, tile, tile],
        out_specs=(tile, tile, tile, tile),
        compiler_params=pltpu.CompilerParams(dimension_semantics=("arbitrary",), vmem_limit_bytes=V7X_VMEM_LIMIT),
    )(c_rows, dmod_cols, w, m, v)


def _small_sum_call(small_all, n_grad_rows, loss_rows, bias_rows, loss_scale):
    _, rows, width = small_all.shape
    lo, hi = loss_rows
    b0, b1, b2 = bias_rows
    nb = b1 - b0

    def body(s_ref, sum_ref, extra_ref):
        tot = s_ref[0]
        for k in range(1, N_DEV):
            tot = tot + s_ref[k]
        sum_ref[...] = tot[0:n_grad_rows, :]
        extra_ref[0:nb, :] = tot[b0:b1, :] + tot[b1:b2, :]
        head = tot[n_grad_rows - 2 * SUBLANES:n_grad_rows, :]
        rows_id = lax.broadcasted_iota(jnp.int32, head.shape, 0) + (n_grad_rows - 2 * SUBLANES)
        sq = jnp.where(jnp.logical_and(rows_id >= lo, rows_id < hi), head, 0.0)
        extra_ref[nb:nb + SUBLANES, :] = jnp.zeros((SUBLANES, width), F32) + jnp.sum(sq) * loss_scale

    vm = pl.BlockSpec(memory_space=VMEM)
    return pl.pallas_call(
        body, name="small_sum",
        out_shape=(jax.ShapeDtypeStruct((n_grad_rows, width), F32), jax.ShapeDtypeStruct((nb + SUBLANES, width), F32)),
        in_specs=[vm], out_specs=(vm, vm),
    )(small_all)


def _adam_small_call(ws, gs, ms, vs):
    n = len(ws)

    def body(*refs):
        w_refs, g_refs, m_refs, v_refs = refs[:n], refs[n:2 * n], refs[2 * n:3 * n], refs[3 * n:4 * n]
        d_refs, m2_refs, v2_refs = refs[4 * n:5 * n], refs[5 * n:6 * n], refs[6 * n:7 * n]
        for i in range(n):
            delta, m2, v2 = _adamw(w_refs[i][...], g_refs[i][...], m_refs[i][...], v_refs[i][...])
            d_refs[i][...] = delta
            m2_refs[i][...] = m2
            v2_refs[i][...] = v2

    vm = pl.BlockSpec(memory_space=VMEM)
    shapes = tuple(jax.ShapeDtypeStruct(w.shape, F32) for w in ws)
    outs = pl.pallas_call(body, name="adam_small", out_shape=shapes * 3,
                          in_specs=[vm] * (4 * n), out_specs=(vm,) * (3 * n))(*ws, *gs, *ms, *vs)
    return outs[:n], outs[n:2 * n], outs[2 * n:]


def kernel(x, c, w_ada, b_ada, w_in, conf_dw_w, conf_dw_b, conf_ln_g, conf_ln_b, sc_conv_w, w_out, w_mlp1, w_mlp2, g_final, loss_target, m_w_ada, m_b_ada, m_w_in, m_conf_dw_w, m_conf_dw_b, m_conf_ln_g, m_conf_ln_b, m_sc_conv_w, m_w_out, m_w_mlp1, m_w_mlp2, m_g_final, v_w_ada, v_b_ada, v_w_in, v_conf_dw_w, v_conf_dw_b, v_conf_ln_g, v_conf_ln_b, v_sc_conv_w, v_w_out, v_w_mlp1, v_w_mlp2, v_g_final):
    bsz, seq, d_model = x.shape
    c_half = conf_dw_b.shape[-1]
    n_taps = conf_dw_w.shape[1]
    cc = conf_dw_w.shape[-1]
    a_cols = w_ada.shape[-1]
    tokens = bsz * seq
    me = _dev_index()

    c_pad = jnp.pad(c, ((0, SUBLANES - bsz), (0, 0)))
    b_ada_loc = lax.dynamic_slice(b_ada, (0, me * a_cols), (1, a_cols))
    small_loc = jnp.zeros((HALO, 128), F32)
    small_loc = small_loc.at[:n_taps, :cc].set(conf_dw_w[0]).at[:3, cc:2 * cc].set(sc_conv_w[0])
    (win_t, wout_all), small_all, c_all, mod_rows = _gather_call(
        c_pad, w_ada[0], b_ada_loc, small_loc, [w_in[0].T.astype(BF16), w_out[0].astype(BF16)])
    cw = small_all[:, :, :cc].transpose(1, 0, 2).reshape(HALO, c_half)
    scw = small_all[:, :3, cc:2 * cc].transpose(1, 0, 2).reshape(3, c_half)
    cp = jnp.concatenate([conf_dw_b, conf_ln_g, conf_ln_b, scw, jnp.zeros((2, c_half), F32)], axis=0)
    mod = mod_rows[:, :bsz, :].transpose(1, 0, 2).reshape(bsz, 6, d_model)

    flat = lambda t: t.reshape(tokens, t.shape[-1])
    (proj, a1, cv, mixed, y1), (w1_all, w2_all) = _mixer_fwd_call(
        x, mod, win_t, wout_all, cw, cp, [w_mlp1[0].astype(BF16), w_mlp2[0].astype(BF16)])
    dx1, h2, dy2, u, dz, dmod2, head, dy1 = _mlp_call(
        x, y1, loss_target, mod, w1_all, w2_all, g_final.reshape(1, d_model))
    g_out = _wgrad_call(flat(mixed), flat(dy1), "wgrad_out")
    g_w1 = _wgrad_call(flat(h2), flat(dz), "wgrad_mlp1", owner_cols=w_mlp1.shape[-1])
    g_w2, (s_w1, s_out) = _wgrad_call(flat(u), flat(dy2), "wgrad_mlp2", scatter=[g_w1, g_out])
    grad_x, dproj, h1, dmod1, cgrad = _mixer_bwd_call(dx1, x, proj, a1, cv, mod, win_t, wout_all, cw, cp)
    g_in_t, (s_w2,) = _wgrad_call(flat(dproj), flat(h1), "wgrad_in", scatter=[g_w2])

    dmod = jnp.concatenate([dmod1[:, :2, :], dmod2[:, 3:4, :], dmod2[:, :3, :]], axis=1)
    n_cg = cgrad.shape[0]
    per_b = 6 * d_model // c_half
    per_b_pad = -(-per_b // SUBLANES) * SUBLANES
    dmod_rows = jnp.pad(dmod.reshape(bsz, per_b, c_half), ((0, 0), (0, per_b_pad - per_b), (0, 0)))
    small = jnp.concatenate([
        cgrad,
        head.reshape(2 * SUBLANES, c_half),
        dmod_rows.reshape(bsz * per_b_pad, c_half),
    ], axis=0)
    s_in, gathered = _tail_scatter_call(g_in_t, small)

    n_head = n_cg + 2 * SUBLANES
    sums, extra = _small_sum_call(
        gathered, n_head, (n_cg + 2, n_cg + 4), (n_head, n_head + per_b_pad, n_head + 2 * per_b_pad), 0.5 / d_model)
    loss = extra[per_b_pad, 0]
    g_b_ada = extra[:per_b].reshape(1, 6 * d_model)
    g_dw_w = lax.dynamic_slice(sums[:n_taps], (0, me * cc), (n_taps, cc))
    g_sc_w = lax.dynamic_slice(sums[HALO:HALO + 3], (0, me * cc), (3, cc))
    g_dw_b, g_ln_g, g_ln_b = sums[HALO + 3:HALO + 4], sums[HALO + 4:HALO + 5], sums[HALO + 5:HALO + 6]
    g_gf = sums[n_cg:n_cg + 2].reshape(1, d_model)

    dmod_all = gathered[:, n_head:, :].reshape(N_DEV, bsz, per_b_pad, c_half)[:, :, :per_b, :]
    dmod_all = dmod_all.reshape(N_DEV, bsz, 6 * d_model)
    dmod_cols = lax.dynamic_slice(dmod_all, (0, 0, me * a_cols), (N_DEV, bsz, a_cols))
    dmod_cols = jnp.pad(dmod_cols, ((0, 0), (0, SUBLANES - bsz), (0, 0))).reshape(N_DEV * SUBLANES, a_cols)
    c_rows = c_all.reshape(N_DEV * SUBLANES, d_model)
    g_ada, d_ada, m_ada, v_ada = _adam_ada_call(c_rows, dmod_cols, w_ada[0], m_w_ada[0], v_w_ada[0])

    gi, di, mi, vi = _adam_slabs_call(s_in, w_in[0].T, m_w_in[0].T, v_w_in[0].T, "adam_w_in")
    gi, di, mi, vi = gi.T, di.T, mi.T, vi.T
    go, do, mo, vo = _adam_slabs_call(s_out, w_out[0], m_w_out[0], v_w_out[0], "adam_w_out")
    g1, d1, m1, v1 = _adam_slabs_call(s_w1, w_mlp1[0], m_w_mlp1[0], v_w_mlp1[0], "adam_w_mlp1")
    g2, d2, m2, v2 = _adam_slabs_call(s_w2, w_mlp2[0], m_w_mlp2[0], v_w_mlp2[0], "adam_w_mlp2")

    small_like = [b_ada, conf_dw_w, conf_dw_b, conf_ln_g, conf_ln_b, sc_conv_w, g_final]
    two_d = lambda t: t.reshape(-1, t.shape[-1])
    small_g = [g_b_ada, g_dw_w, g_dw_b, g_ln_g, g_ln_b, g_sc_w, g_gf]
    sd, sm, sv = _adam_small_call(
        [two_d(t) for t in small_like], small_g,
        [two_d(t) for t in (m_b_ada, m_conf_dw_w, m_conf_dw_b, m_conf_ln_g, m_conf_ln_b, m_sc_conv_w, m_g_final)],
        [two_d(t) for t in (v_b_ada, v_conf_dw_w, v_conf_dw_b, v_conf_ln_g, v_conf_ln_b, v_sc_conv_w, v_g_final)])
    like = lambda parts: [p.reshape(w.shape) for p, w in zip(parts, small_like)]
    sg, sd, sm, sv = like(small_g), like(sd), like(sm), like(sv)

    def ordered(ada, small_list, w_in_, w_out_, w1_, w2_):
        b_ada_, dw_w_, dw_b_, ln_g_, ln_b_, sc_w_, gf_ = small_list
        return [ada[None], b_ada_, w_in_[None], dw_w_, dw_b_, ln_g_, ln_b_, sc_w_, w_out_[None], w1_[None], w2_[None], gf_]

    grads = ordered(g_ada, sg, gi, go, g1, g2)
    deltas = ordered(d_ada, sd, di, do, d1, d2)
    new_m = ordered(m_ada, sm, mi, mo, m1, m2)
    new_v = ordered(v_ada, sv, vi, vo, v1, v2)
    return (loss, grad_x, *grads, *deltas, *new_m, *new_v)
```

```python
import jax
import jax.numpy as jnp
from jax import lax
from jax.experimental import pallas as pl
from jax.experimental.pallas import tpu as pltpu

N_DEV = 8
RMS_EPS = 1e-6
ADAM_LR = 0.001
ADAM_B1 = 0.9
ADAM_B2 = 0.999
ADAM_EPS = 1e-08
ADAM_WD = 0.01
ADAM_STEP = 10

F32 = jnp.float32
BF16 = jnp.bfloat16
MESH = pl.DeviceIdType.MESH
VMEM = pltpu.VMEM
ANY = pl.ANY

HALO = 32
SHORT_HALO = 8
ROW_CHUNK = 32
SUBLANES = 8
V7X_VMEM_LIMIT = 56 * 1024 * 1024
MLP_VMEM_LIMIT = 48 * 1024 * 1024
MLP_TOKEN_TILE = 256
MIXER_FWD_TOKEN_TILE = 512


def _coords():
    return lax.axis_index("x"), lax.axis_index("y"), lax.axis_index("c")


def _dev_index():
    x, y, c = _coords()
    return 4 * x + 2 * y + c


def _peer(k):
    x, y, c = _coords()
    px = 1 - x if (k >> 2) & 1 else x
    py = 1 - y if (k >> 1) & 1 else y
    pc = 1 - c if k & 1 else c
    return (px, py, pc), 4 * px + 2 * py + pc


def _sigmoid(v):
    return jax.nn.sigmoid(v)


def _nt_dot(a, b):
    return lax.dot_general(a, b, (((1,), (1,)), ((), ())), preferred_element_type=F32)


def _nn_dot(a, b):
    return jnp.dot(a, b, preferred_element_type=F32)


def _tn_dot(a, b):
    return lax.dot_general(a, b, (((0,), (0,)), ((), ())), preferred_element_type=F32)


def _token_tile(seq):
    return 256 if seq % 256 == 0 else 64


GATHER_SEM_COLUMNS = 9
HALF_ROW_ALIGN = 16


class _TwoLevelGather:
    def __init__(self, srcs, dsts, send_sems, recv_sems, loc_sems):
        x, y, c = _coords()
        me = 4 * x + 2 * y + c
        sibling, along_x, along_y = (x, y, 1 - c), (1 - x, y, c), (x, 1 - y, c)
        from_x, from_y, diagonal = 4 * (1 - x) + 2 * y + c, 4 * x + 2 * (1 - y) + c, 4 * (1 - x) + 2 * (1 - y) + c

        def remote(src, dst, a, col, to):
            return pltpu.make_async_remote_copy(
                src_ref=src, dst_ref=dst, send_sem=send_sems.at[a, col], recv_sem=recv_sems.at[a, col],
                device_id=to, device_id_type=MESH)

        def onward(block, a, col, to):
            return remote(block, block, a, col, to)

        self.first, self.arrivals, self.second, self.halves, self.third = [], [], [], [], []
        for a, (src, dst) in enumerate(zip(srcs, dsts)):
            half = src.shape[0] // (2 * HALF_ROW_ALIGN) * HALF_ROW_ALIGN
            lower, upper = pl.ds(0, half), pl.ds(half, src.shape[0] - half)
            neighbours = [remote(src, dst.at[me], a, 1, along_x), remote(src, dst.at[me], a, 2, along_y)]
            self.first += [pltpu.make_async_copy(src, dst.at[me], loc_sems.at[a]),
                           remote(src, dst.at[me], a, 0, sibling)]
            self.arrivals += neighbours
            halves = [onward(dst.at[from_x, lower], a, 3, along_y), onward(dst.at[from_y, upper], a, 4, along_x)]
            self.halves += halves
            self.second += [[halves[0], onward(dst.at[from_x], a, 5, sibling)],
                            [halves[1], onward(dst.at[from_y], a, 6, sibling)]]
            self.third += [onward(dst.at[diagonal, lower], a, 7, sibling),
                           onward(dst.at[diagonal, upper], a, 8, sibling)]

    def start(self):
        for cp in self.first + self.arrivals:
            cp.start()

    def forward_arrivals(self):
        for arrival, sends in zip(self.arrivals, self.second):
            arrival.wait_recv()
            for cp in sends:
                cp.start()

    def forward_diagonal(self):
        for half, to_sibling in zip(self.halves, self.third):
            half.wait_recv()
            to_sibling.start()

    def finish(self):
        for cp in self.arrivals + self.halves:
            cp.wait_send()
        for cp in self.first + [sends[1] for sends in self.second] + self.third:
            cp.wait()


N_CHIPS = N_DEV // 2
PAIR_SUM_ROWS = 256


class _TwoLevelScatter:
    def __init__(self, partials, sums, from_sibling, pair, first_send, first_recv, second_send, second_recv,
                 local_sems):
        x, y, c = _coords()
        sibling = (x, y, 1 - c)
        chips = [(1 - x, y), (x, 1 - y), (1 - x, 1 - y)]
        self.c, self.my_chip = c, 2 * x + y
        self.partials, self.sums, self.from_sibling, self.pair = partials, sums, from_sibling, pair

        def remote(src, dst, send_sem, recv_sem, to):
            return pltpu.make_async_remote_copy(src_ref=src, dst_ref=dst, send_sem=send_sem, recv_sem=recv_sem,
                                                device_id=to, device_id_type=MESH)

        self.first, self.second, self.local = [], [], []
        for a in range(len(partials)):
            self.local.append(pltpu.make_async_copy(pair[a].at[self.my_chip], sums[a].at[self.my_chip],
                                                    local_sems.at[a]))
            self.first += [remote(partials[a].at[2 * q + (1 - c)], from_sibling[a].at[q],
                                  first_send.at[a, q], first_recv.at[a, q], sibling) for q in range(N_CHIPS)]
            self.second += [remote(pair[a].at[2 * cx + cy], sums[a].at[self.my_chip],
                                   second_send.at[a, j], second_recv.at[a, j], (cx, cy, c))
                            for j, (cx, cy) in enumerate(chips)]

    def start(self):
        for cp in self.first:
            cp.start()

    def pair_sums(self):
        for cp in self.first:
            cp.wait()
        for mine, theirs, both in zip(self.partials, self.from_sibling, self.pair):
            rows = mine.shape[1]
            for q in range(N_CHIPS):
                for r0 in range(0, rows, PAIR_SUM_ROWS):
                    part = pl.ds(r0, min(PAIR_SUM_ROWS, rows - r0))
                    both[q, part, :] = (mine[2 * q + self.c, part, :].astype(F32)
                                        + theirs[q, part, :].astype(F32)).astype(BF16)
        for cp in self.second + self.local:
            cp.start()

    def finish(self):
        for cp in self.second + self.local:
            cp.wait()


def _scatter_scratch(partials):
    n = len(partials)
    zones = [pltpu.VMEM((N_CHIPS,) + p.shape[1:], BF16) for p in partials]
    return zones + zones + [pltpu.SemaphoreType.DMA((n, N_CHIPS)), pltpu.SemaphoreType.DMA((n, N_CHIPS)),
                            pltpu.SemaphoreType.DMA((n, N_CHIPS - 1)), pltpu.SemaphoreType.DMA((n, N_CHIPS - 1)),
                            pltpu.SemaphoreType.DMA((n,))]


def _exchange_sems(n_arrays, columns=GATHER_SEM_COLUMNS):
    return [pltpu.SemaphoreType.DMA((n_arrays, columns)), pltpu.SemaphoreType.DMA((n_arrays, columns)),
            pltpu.SemaphoreType.DMA((n_arrays,))]


def _gather_call(c_pad, w_ada, b_ada_loc, small_loc, big_shards):
    n_big = len(big_shards)
    d_model = c_pad.shape[1]
    a_cols = w_ada.shape[1]

    def body(c_ref, wada_ref, bada_ref, small_ref, *rest):
        big_in = rest[:n_big]
        big_out = rest[n_big:2 * n_big]
        small_all, c_all, mod_rows = rest[2 * n_big:2 * n_big + 3]
        modcols, big_send, big_recv, loc_sem, s_send, s_recv = rest[2 * n_big + 3:]
        me = _dev_index()
        big = _TwoLevelGather(big_in, big_out, big_send, big_recv, loc_sem)
        big.start()

        small_all[me] = small_ref[...]
        c_all[me] = c_ref[...]
        first = []
        for k in range(1, N_DEV):
            peer, _ = _peer(k)
            for i, (src, dst) in enumerate(((small_ref, small_all), (c_ref, c_all))):
                cp = pltpu.make_async_remote_copy(
                    src_ref=src, dst_ref=dst.at[me],
                    send_sem=s_send.at[i, k - 1], recv_sem=s_recv.at[i, k - 1],
                    device_id=peer, device_id_type=MESH)
                cp.start()
                first.append(cp)
        for cp in first:
            cp.wait()

        c_rows = c_all[...].reshape(N_DEV * SUBLANES, d_model)
        c_act = c_rows * _sigmoid(c_rows)
        modcols[...] = _nn_dot(c_act.astype(BF16), wada_ref[...].astype(BF16)) + bada_ref[...]
        mod_rows[me] = modcols[pl.ds(pl.multiple_of(me * SUBLANES, SUBLANES), SUBLANES), :]
        second = []
        for k in range(1, N_DEV):
            peer, pidx = _peer(k)
            cp = pltpu.make_async_remote_copy(
                src_ref=modcols.at[pl.ds(pl.multiple_of(pidx * SUBLANES, SUBLANES), SUBLANES), :],
                dst_ref=mod_rows.at[me],
                send_sem=s_send.at[2, k - 1], recv_sem=s_recv.at[2, k - 1],
                device_id=peer, device_id_type=MESH)
            cp.start()
            second.append(cp)
        big.forward_arrivals()
        for cp in second:
            cp.wait()
        big.forward_diagonal()
        big.finish()

    out_shape = tuple(jax.ShapeDtypeStruct((N_DEV,) + s.shape, s.dtype) for s in big_shards) + (
        jax.ShapeDtypeStruct((N_DEV,) + small_loc.shape, F32),
        jax.ShapeDtypeStruct((N_DEV, SUBLANES, d_model), F32),
        jax.ShapeDtypeStruct((N_DEV, SUBLANES, a_cols), F32),
    )
    vm = pl.BlockSpec(memory_space=VMEM)
    hbm = pl.BlockSpec(memory_space=ANY)
    outs = pl.pallas_call(
        body, name="gather_weights_mod", out_shape=out_shape,
        in_specs=[vm, vm, vm, vm] + [hbm] * n_big,
        out_specs=tuple([hbm] * n_big + [vm, vm, vm]),
        scratch_shapes=[
            pltpu.VMEM((N_DEV * SUBLANES, a_cols), F32),
            *_exchange_sems(n_big),
            pltpu.SemaphoreType.DMA((3, N_DEV - 1)),
            pltpu.SemaphoreType.DMA((3, N_DEV - 1)),
        ],
        compiler_params=pltpu.CompilerParams(vmem_limit_bytes=V7X_VMEM_LIMIT),
    )(c_pad, w_ada, b_ada_loc, small_loc, *big_shards)
    return outs[:n_big], outs[n_big], outs[n_big + 1], outs[n_big + 2]


def _shifted_rows_count(tm):
    return tm + HALO - SUBLANES


def _fill_shifted(ext, shifted, tm):
    for s in range(1, SUBLANES):
        shifted[s - 1] = ext[s:s + _shifted_rows_count(tm), :]


def _shifted_rows(ext, shifted, start, rows):
    phase = start % SUBLANES
    aligned = start - phase
    if phase == 0:
        return ext[aligned:aligned + rows, :]
    return shifted[phase - 1, aligned:aligned + rows, :]


def _layer_norm_parts(a1):
    mu = jnp.mean(a1, axis=-1, keepdims=True)
    xc = a1 - mu
    rstd = lax.rsqrt(jnp.mean(xc * xc, axis=-1, keepdims=True) + RMS_EPS)
    return xc * rstd, rstd


def _mixer_fwd_call(x, mod, win_t, wout, cw, cp, later_shards):
    n_later = len(later_shards)
    bsz, seq, d_model = x.shape
    c_half = cw.shape[1]
    n_taps = 31
    d_in = win_t.shape[0] * win_t.shape[1]
    tm = MIXER_FWD_TOKEN_TILE if seq % MIXER_FWD_TOKEN_TILE == 0 else _token_tile(seq)
    nt = seq // tm
    arrivals_step, diagonal_step = (7 * bsz * nt) // 16, (11 * bsz * nt) // 16

    def body(x_ref, mod_ref, win_ref, wout_ref, cw_ref, cp_ref, *rest):
        shard_refs, rest = rest[:n_later], rest[n_later:]
        proj_ref, a1_ref, cv_ref, mixed_ref, y1_ref = rest[:5]
        gathered_refs, rest = rest[5:5 + n_later], rest[5 + n_later:]
        aext, qext, ashift, send_sems, recv_sems, loc_sems = rest
        b, t = pl.program_id(0), pl.program_id(1)

        step = b * nt + t

        @pl.when(step == 0)
        def _():
            _TwoLevelGather(shard_refs, gathered_refs, send_sems, recv_sems, loc_sems).start()

        @pl.when(step == arrivals_step)
        def _():
            _TwoLevelGather(shard_refs, gathered_refs, send_sems, recv_sems, loc_sems).forward_arrivals()

        @pl.when(step == diagonal_step)
        def _():
            _TwoLevelGather(shard_refs, gathered_refs, send_sems, recv_sems, loc_sems).forward_diagonal()

        xv = x_ref[...]
        sh1, sc1 = mod_ref[0:1, :], mod_ref[1:2, :]
        r1 = lax.rsqrt(jnp.mean(xv * xv, axis=-1, keepdims=True) + RMS_EPS)
        h1 = (xv * r1) * (1.0 + sc1) + sh1
        proj = _nt_dot(h1.astype(BF16), win_ref[...].reshape(d_in, d_model))
        proj_ref[...] = proj
        val, gate = proj[:, 0:c_half], proj[:, c_half:2 * c_half]
        s_b, s_c, s_h = proj[:, 2 * c_half:3 * c_half], proj[:, 3 * c_half:4 * c_half], proj[:, 4 * c_half:5 * c_half]

        @pl.when(t == 0)
        def _():
            aext[0:HALO, :] = jnp.zeros((HALO, c_half), F32)
            qext[0:SHORT_HALO, :] = jnp.zeros((SHORT_HALO, c_half), F32)

        @pl.when(t > 0)
        def _():
            aext[0:HALO, :] = aext[tm:tm + HALO, :]
            qext[0:SHORT_HALO, :] = qext[tm:tm + SHORT_HALO, :]
        aext[HALO:HALO + tm, :] = val * _sigmoid(gate)
        qext[SHORT_HALO:SHORT_HALO + tm, :] = s_c * s_h

        base = HALO - (n_taps - 1)
        _fill_shifted(aext, ashift, tm)
        for r0 in range(0, tm, ROW_CHUNK):
            acc = jnp.zeros((ROW_CHUNK, c_half), F32)
            for k in range(n_taps):
                acc = acc + cw_ref[k:k + 1, :] * _shifted_rows(aext, ashift, r0 + base + k, ROW_CHUNK)
            a1_ref[r0:r0 + ROW_CHUNK, :] = acc + cp_ref[0:1, :]
        sbase = SHORT_HALO - 2
        conv3 = cp_ref[3:4, :] * qext[sbase:sbase + tm, :]
        conv3 = conv3 + cp_ref[4:5, :] * qext[sbase + 1:sbase + 1 + tm, :]
        conv3 = conv3 + cp_ref[5:6, :] * qext[sbase + 2:sbase + 2 + tm, :]
        cv_ref[...] = conv3

        norm, _ = _layer_norm_parts(a1_ref[...])
        a2 = norm * cp_ref[1:2, :] + cp_ref[2:3, :]
        mixed = jnp.concatenate([a2 * _sigmoid(a2), s_b * conv3], axis=-1).astype(BF16)
        mixed_ref[...] = mixed
        y1 = _nn_dot(mixed, wout_ref[...].reshape(d_model, d_model))
        y1_ref[...] = y1

        @pl.when(step == bsz * nt - 1)
        def _():
            _TwoLevelGather(shard_refs, gathered_refs, send_sems, recv_sems, loc_sems).finish()

    hbm = pl.BlockSpec(memory_space=ANY)

    def tok(width):
        return pl.BlockSpec((None, tm, width), lambda b, t: (b, t, 0))

    def const(shape):
        return pl.BlockSpec(shape, lambda b, t: (0,) * len(shape))

    def resident(shape):
        return pl.BlockSpec(shape, lambda b, t: (0,) * len(shape), pipeline_mode=pl.Buffered(1))

    out_shape = (
        jax.ShapeDtypeStruct((bsz, seq, d_in), F32),
        jax.ShapeDtypeStruct((bsz, seq, c_half), F32),
        jax.ShapeDtypeStruct((bsz, seq, c_half), F32),
        jax.ShapeDtypeStruct((bsz, seq, d_model), BF16),
        jax.ShapeDtypeStruct((bsz, seq, d_model), F32),
    ) + tuple(jax.ShapeDtypeStruct((N_DEV,) + s.shape, s.dtype) for s in later_shards)
    outs = pl.pallas_call(
        body, name="mixer_fwd", out_shape=out_shape, grid=(bsz, nt),
        in_specs=[tok(d_model), pl.BlockSpec((None, 6, d_model), lambda b, t: (b, 0, 0)),
                  resident(win_t.shape), resident(wout.shape), const(cw.shape), const(cp.shape)] + [hbm] * n_later,
        out_specs=(tok(d_in), tok(c_half), tok(c_half), tok(d_model), tok(d_model)) + (hbm,) * n_later,
        scratch_shapes=[pltpu.VMEM((tm + HALO, c_half), F32), pltpu.VMEM((tm + SHORT_HALO, c_half), F32),
                        pltpu.VMEM((SUBLANES - 1, _shifted_rows_count(tm), c_half), F32)]
        + _exchange_sems(n_later),
        compiler_params=pltpu.CompilerParams(
            dimension_semantics=("arbitrary", "arbitrary"), vmem_limit_bytes=V7X_VMEM_LIMIT),
    )(x, mod, win_t, wout, cw, cp, *later_shards)
    return outs[:5], outs[5:]


def _mlp_call(x, y1, target, mod, w1, w2, g_final):
    bsz, seq, d_model = x.shape
    n_blk, _, f_blk = w1.shape
    d_ff = n_blk * f_blk
    tm = MLP_TOKEN_TILE if seq % MLP_TOKEN_TILE == 0 else _token_tile(seq)
    nt = seq // tm

    def body(x_ref, y1_ref, tgt_ref, mod_ref, w1_ref, w2_ref, gf_ref,
             dx1_ref, h2_ref, dy2_ref, u_ref, dz_ref, dmod_ref, head_ref, dy1_ref, relu_scr):
        b, t = pl.program_id(0), pl.program_id(1)
        x1v = x_ref[...] + mod_ref[2:3, :] * y1_ref[...]
        sh2, sc2, g2 = mod_ref[3:4, :], mod_ref[4:5, :], mod_ref[5:6, :]
        gf = gf_ref[...]
        r2 = lax.rsqrt(jnp.mean(x1v * x1v, axis=-1, keepdims=True) + RMS_EPS)
        xn2 = x1v * r2
        h2 = (xn2 * (1.0 + sc2) + sh2).astype(BF16)
        h2_ref[...] = h2
        y2 = jnp.zeros((tm, d_model), F32)
        for j in range(n_blk):
            cols = slice(j * f_blk, (j + 1) * f_blk)
            rz = jnp.maximum(_nn_dot(h2, w1_ref[j]), 0.0)
            relu_scr[:, cols] = rz
            ub = (rz * rz).astype(BF16)
            u_ref[:, cols] = ub
            y2 = y2 + _nn_dot(ub, w2_ref[j])
        x2 = x1v + g2 * y2
        r3 = lax.rsqrt(jnp.mean(x2 * x2, axis=-1, keepdims=True) + RMS_EPS)
        xn3 = x2 * r3
        diff = xn3 * gf - tgt_ref[...]
        dout = diff * (1.0 / d_model)

        @pl.when(jnp.logical_and(b == 0, t == 0))
        def _():
            head_ref[...] = jnp.zeros(head_ref.shape, F32)

        @pl.when(t == 0)
        def _():
            dmod_ref[...] = jnp.zeros(dmod_ref.shape, F32)

        head_ref[0:1, :] += jnp.sum(dout * xn3, axis=0, keepdims=True)
        head_ref[1:2, :] += jnp.sum(diff * diff, axis=0, keepdims=True)
        dxn3 = dout * gf
        dx2 = r3 * (dxn3 - xn3 * jnp.mean(dxn3 * xn3, axis=-1, keepdims=True))
        dmod_ref[2:3, :] += jnp.sum(dx2 * y2, axis=0, keepdims=True)
        dy2 = (g2 * dx2).astype(BF16)
        dy2_ref[...] = dy2
        dh2 = jnp.zeros((tm, d_model), F32)
        for j in range(n_blk):
            cols = slice(j * f_blk, (j + 1) * f_blk)
            dz = (_nt_dot(dy2, w2_ref[j]) * (2.0 * relu_scr[:, cols])).astype(BF16)
            dz_ref[:, cols] = dz
            dh2 = dh2 + _nt_dot(dz, w1_ref[j])
        dmod_ref[0:1, :] += jnp.sum(dh2, axis=0, keepdims=True)
        dmod_ref[1:2, :] += jnp.sum(dh2 * xn2, axis=0, keepdims=True)
        dxn2 = dh2 * (1.0 + sc2)
        dx1 = dx2 + r2 * (dxn2 - xn2 * jnp.mean(dxn2 * xn2, axis=-1, keepdims=True))
        dx1_ref[...] = dx1
        dy1_ref[...] = (mod_ref[2:3, :] * dx1).astype(BF16)
        dmod_ref[3:4, :] += jnp.sum(dx1 * y1_ref[...], axis=0, keepdims=True)

    def tok(width):
        return pl.BlockSpec((None, tm, width), lambda b, t: (b, t, 0))

    def const(shape):
        return pl.BlockSpec(shape, lambda b, t: (0,) * len(shape))

    def resident(shape):
        return pl.BlockSpec(shape, lambda b, t: (0,) * len(shape), pipeline_mode=pl.Buffered(1))

    out_shape = (
        jax.ShapeDtypeStruct((bsz, seq, d_model), F32),
        jax.ShapeDtypeStruct((bsz, seq, d_model), BF16),
        jax.ShapeDtypeStruct((bsz, seq, d_model), BF16),
        jax.ShapeDtypeStruct((bsz, seq, d_ff), BF16),
        jax.ShapeDtypeStruct((bsz, seq, d_ff), BF16),
        jax.ShapeDtypeStruct((bsz, SUBLANES, d_model), F32),
        jax.ShapeDtypeStruct((SUBLANES, d_model), F32),
        jax.ShapeDtypeStruct((bsz, seq, d_model), BF16),
    )
    return pl.pallas_call(
        body, name="mlp_fwd_bwd", out_shape=out_shape, grid=(bsz, nt),
        in_specs=[tok(d_model), tok(d_model), tok(d_model),
                  pl.BlockSpec((None, 6, d_model), lambda b, t: (b, 0, 0)),
                  resident(w1.shape), resident(w2.shape), const(g_final.shape)],
        out_specs=(tok(d_model), tok(d_model), tok(d_model), tok(d_ff), tok(d_ff),
                   pl.BlockSpec((None, SUBLANES, d_model), lambda b, t: (b, 0, 0)),
                   const((SUBLANES, d_model)), tok(d_model)),
        scratch_shapes=[pltpu.VMEM((tm, d_ff), F32)],
        compiler_params=pltpu.CompilerParams(
            dimension_semantics=("arbitrary", "arbitrary"), vmem_limit_bytes=MLP_VMEM_LIMIT),
    )(x, y1, target, mod, w1, w2, g_final)


def _mixer_bwd_call(dx1, x, proj, a1, cv, mod, win_t, wout, cw, cp):
    bsz, seq, d_model = x.shape
    c_half = cw.shape[1]
    n_taps = 31
    d_in = win_t.shape[0] * win_t.shape[1]
    tm = _token_tile(seq)
    nt = seq // tm

    def body(dx1_ref, x_ref, proj_ref, a1_ref, cv_ref, mod_ref, win_ref, wout_ref, cw_ref, cp_ref,
             gx_ref, dproj_ref, h1_ref, dmod_ref, cgrad_ref,
             dext, cext, a0_scr, da0_scr, tap_acc, row_acc, dshift):
        b, step = pl.program_id(0), pl.program_id(1)
        first = jnp.logical_and(b == 0, step == 0)
        last = jnp.logical_and(b == bsz - 1, step == nt - 1)
        dx1v = dx1_ref[...]
        xv = x_ref[...]
        sh1, sc1, g1 = mod_ref[0:1, :], mod_ref[1:2, :], mod_ref[2:3, :]

        @pl.when(first)
        def _():
            tap_acc[...] = jnp.zeros(tap_acc.shape, F32)
            row_acc[...] = jnp.zeros(row_acc.shape, F32)

        @pl.when(step == 0)
        def _():
            dmod_ref[...] = jnp.zeros(dmod_ref.shape, F32)
            dext[tm:tm + HALO, :] = jnp.zeros((HALO, c_half), F32)
            cext[tm:tm + SHORT_HALO, :] = jnp.zeros((SHORT_HALO, c_half), F32)

        @pl.when(step > 0)
        def _():
            dext[tm:tm + HALO, :] = dext[0:HALO, :]
            cext[tm:tm + SHORT_HALO, :] = cext[0:SHORT_HALO, :]

        dy1 = (g1 * dx1v).astype(BF16)
        dmixed = _nt_dot(dy1, wout_ref[...].reshape(d_model, d_model))
        d_a, d_s = dmixed[:, 0:c_half], dmixed[:, c_half:2 * c_half]

        val, gate = proj_ref[:, 0:c_half], proj_ref[:, c_half:2 * c_half]
        s_b = proj_ref[:, 2 * c_half:3 * c_half]
        s_c, s_h = proj_ref[:, 3 * c_half:4 * c_half], proj_ref[:, 4 * c_half:5 * c_half]

        d_sb = d_s * cv_ref[...]
        cext[0:tm, :] = d_s * s_b
        q = s_c * s_h
        dq = jnp.zeros((tm, c_half), F32)
        for k in range(3):
            shifted = cext[2 - k:2 - k + tm, :]
            dq = dq + cp_ref[3 + k:4 + k, :] * shifted
            row_acc[k:k + 1, :] += jnp.sum(q * shifted, axis=0, keepdims=True)
        d_sc, d_sh = dq * s_h, dq * s_c

        norm, rstd = _layer_norm_parts(a1_ref[...])
        ln_g = cp_ref[1:2, :]
        a2 = norm * ln_g + cp_ref[2:3, :]
        sg = _sigmoid(a2)
        d_a2 = d_a * (sg * (1.0 + a2 * (1.0 - sg)))
        row_acc[4:5, :] += jnp.sum(d_a2 * norm, axis=0, keepdims=True)
        row_acc[5:6, :] += jnp.sum(d_a2, axis=0, keepdims=True)
        d_n = d_a2 * ln_g
        d_a1 = rstd * (d_n - jnp.mean(d_n, axis=-1, keepdims=True)
                       - norm * jnp.mean(d_n * norm, axis=-1, keepdims=True))
        row_acc[3:4, :] += jnp.sum(d_a1, axis=0, keepdims=True)
        dext[0:tm, :] = d_a1
        sig_g = _sigmoid(gate)
        a0_scr[...] = val * sig_g

        _fill_shifted(dext, dshift, tm)
        for r0 in range(0, tm, ROW_CHUNK):
            a0c = a0_scr[r0:r0 + ROW_CHUNK, :]
            acc = jnp.zeros((ROW_CHUNK, c_half), F32)
            for k in range(n_taps):
                shifted = _shifted_rows(dext, dshift, r0 + (n_taps - 1) - k, ROW_CHUNK)
                acc = acc + cw_ref[k:k + 1, :] * shifted
                prod = a0c * shifted
                part = prod[0:SUBLANES, :]
                for g in range(1, ROW_CHUNK // SUBLANES):
                    part = part + prod[g * SUBLANES:(g + 1) * SUBLANES, :]
                tap_acc[k * SUBLANES:(k + 1) * SUBLANES, :] += part
            da0_scr[r0:r0 + ROW_CHUNK, :] = acc
        d_a0 = da0_scr[...]
        d_val = d_a0 * sig_g
        d_gate = d_a0 * val * sig_g * (1.0 - sig_g)

        dproj = jnp.concatenate([d_val, d_gate, d_sb, d_sc, d_sh], axis=-1).astype(BF16)
        dproj_ref[...] = dproj
        dh1 = _nn_dot(dproj, win_ref[...].reshape(d_in, d_model))
        r1 = lax.rsqrt(jnp.mean(xv * xv, axis=-1, keepdims=True) + RMS_EPS)
        xn1 = xv * r1
        h1_ref[...] = (xn1 * (1.0 + sc1) + sh1).astype(BF16)
        dmod_ref[0:1, :] += jnp.sum(dh1, axis=0, keepdims=True)
        dmod_ref[1:2, :] += jnp.sum(dh1 * xn1, axis=0, keepdims=True)
        dxn1 = dh1 * (1.0 + sc1)
        gx_ref[...] = dx1v + r1 * (dxn1 - xn1 * jnp.mean(dxn1 * xn1, axis=-1, keepdims=True))

        @pl.when(last)
        def _():
            taps = jnp.sum(tap_acc[...].reshape(HALO, SUBLANES, c_half), axis=1)
            cgrad_ref[0:HALO, :] = taps
            cgrad_ref[HALO:HALO + SUBLANES, :] = row_acc[...]

    def tok(width):
        return pl.BlockSpec((None, tm, width), lambda b, s: (b, nt - 1 - s, 0))

    def const(shape):
        return pl.BlockSpec(shape, lambda b, s: (0,) * len(shape))

    mod_spec = pl.BlockSpec((None, 6, d_model), lambda b, s: (b, 0, 0))
    out_shape = (
        jax.ShapeDtypeStruct((bsz, seq, d_model), F32),
        jax.ShapeDtypeStruct((bsz, seq, d_in), BF16),
        jax.ShapeDtypeStruct((bsz, seq, d_model), BF16),
        jax.ShapeDtypeStruct((bsz, SUBLANES, d_model), F32),
        jax.ShapeDtypeStruct((HALO + SUBLANES, c_half), F32),
    )
    return pl.pallas_call(
        body, name="mixer_bwd", out_shape=out_shape, grid=(bsz, nt),
        in_specs=[tok(d_model), tok(d_model), tok(d_in), tok(c_half), tok(c_half), mod_spec,
                  const(win_t.shape), const(wout.shape), const(cw.shape), const(cp.shape)],
        out_specs=(tok(d_model), tok(d_in), tok(d_model),
                   pl.BlockSpec((None, SUBLANES, d_model), lambda b, s: (b, 0, 0)),
                   const((HALO + SUBLANES, c_half))),
        scratch_shapes=[
            pltpu.VMEM((tm + HALO, c_half), F32), pltpu.VMEM((tm + SHORT_HALO, c_half), F32),
            pltpu.VMEM((tm, c_half), F32), pltpu.VMEM((tm, c_half), F32),
            pltpu.VMEM((HALO * SUBLANES, c_half), F32), pltpu.VMEM((SUBLANES, c_half), F32),
            pltpu.VMEM((SUBLANES - 1, _shifted_rows_count(tm), c_half), F32),
        ],
        compiler_params=pltpu.CompilerParams(
            dimension_semantics=("arbitrary", "arbitrary"), vmem_limit_bytes=V7X_VMEM_LIMIT),
    )(dx1, x, proj, a1, cv, mod, win_t, wout, cw, cp)


def _largest_divisor(n, cap, multiple):
    best = None
    for cand in range(multiple, min(n, cap) + 1, multiple):
        if n % cand == 0:
            best = cand
    return best if best is not None else n


WGRAD_TOKENS_PER_STEP = 2048
WGRAD_COLS_PER_STEP = 1024


def _wgrad_call(a, b, name, owner_cols=None, scatter=()):
    n_sc = len(scatter)
    tokens, m_dim = a.shape
    n_dim = b.shape[1]
    bk = _largest_divisor(tokens, WGRAD_TOKENS_PER_STEP, 128)
    n_k = tokens // bk
    if owner_cols is None:
        bm = _largest_divisor(m_dim, 1024, m_dim // N_DEV)
        bn = n_dim
        owners = 1
        out_shape = jax.ShapeDtypeStruct((m_dim, n_dim), BF16)
        out_spec = pl.BlockSpec((bm, bn), lambda i, j, k: (i, j))
    else:
        bm = m_dim
        bn = _largest_divisor(n_dim, WGRAD_COLS_PER_STEP, owner_cols)
        owners = bn // owner_cols
        out_shape = jax.ShapeDtypeStruct((n_dim // owner_cols, m_dim, owner_cols), BF16)
        out_spec = pl.BlockSpec((owners, bm, owner_cols), lambda i, j, k: (j, i, 0))

    grid = (m_dim // bm, n_dim // bn, n_k)
    n_steps = grid[0] * grid[1] * grid[2]
    pair_step = min(max(1, n_steps // 5), n_steps - 1)

    def body(a_ref, b_ref, *rest):
        part_refs, o_ref, sum_refs, acc = rest[:n_sc], rest[n_sc], rest[n_sc + 1:2 * n_sc + 1], rest[2 * n_sc + 1]
        k = pl.program_id(2)
        step = (pl.program_id(0) * grid[1] + pl.program_id(1)) * n_k + k

        def exchange():
            extra = rest[2 * n_sc + 2:]
            return _TwoLevelScatter(part_refs, sum_refs, extra[:n_sc], extra[n_sc:2 * n_sc], *extra[2 * n_sc:])

        if n_sc:
            @pl.when(step == 0)
            def _():
                exchange().start()

            @pl.when(step == pair_step)
            def _():
                exchange().pair_sums()

        @pl.when(k == 0)
        def _():
            acc[...] = jnp.zeros(acc.shape, F32)

        acc[...] += _tn_dot(a_ref[...], b_ref[...])

        @pl.when(k == n_k - 1)
        def _():
            if owner_cols is None:
                o_ref[...] = acc[...].astype(BF16)
            else:
                for q in range(owners):
                    o_ref[q] = acc[:, q * owner_cols:(q + 1) * owner_cols].astype(BF16)

        if n_sc:
            @pl.when(step == n_steps - 1)
            def _():
                exchange().finish()

    whole = pl.BlockSpec(memory_space=VMEM)
    sum_shapes = tuple(jax.ShapeDtypeStruct((N_CHIPS,) + p.shape[1:], BF16) for p in scatter)
    outs = pl.pallas_call(
        body, name=name, out_shape=(out_shape,) + sum_shapes, grid=grid,
        in_specs=[pl.BlockSpec((bk, bm), lambda i, j, k: (k, i)), pl.BlockSpec((bk, bn), lambda i, j, k: (k, j))]
        + [whole] * n_sc,
        out_specs=(out_spec,) + (pl.BlockSpec(memory_space=ANY),) * n_sc,
        scratch_shapes=[pltpu.VMEM((bm, bn), F32)] + (_scatter_scratch(scatter) if n_sc else []),
        compiler_params=pltpu.CompilerParams(
            dimension_semantics=("arbitrary", "arbitrary", "arbitrary"), vmem_limit_bytes=V7X_VMEM_LIMIT),
    )(a, b, *scatter)
    out = outs[0]
    if owner_cols is None:
        out = out.reshape(N_DEV, m_dim // N_DEV, n_dim)
    return (out, list(outs[1:])) if n_sc else out


def _tail_scatter_call(partial, small):
    def body(g_ref, small_ref, out_ref, small_all, from_sibling, pair,
             p1_send, p1_recv, p2_send, p2_recv, p_loc, s_send, s_recv, s_loc):
        gather = _TwoLevelGather([small_ref], [small_all], s_send, s_recv, s_loc)
        scatter = _TwoLevelScatter([g_ref], [out_ref], [from_sibling], [pair],
                                   p1_send, p1_recv, p2_send, p2_recv, p_loc)
        gather.start()
        scatter.start()
        scatter.pair_sums()
        gather.forward_arrivals()
        scatter.finish()
        gather.forward_diagonal()
        gather.finish()

    vm = pl.BlockSpec(memory_space=VMEM)
    return pl.pallas_call(
        body, name="scatter_tail",
        out_shape=(jax.ShapeDtypeStruct((N_CHIPS,) + partial.shape[1:], BF16),
                   jax.ShapeDtypeStruct((N_DEV,) + small.shape, F32)),
        in_specs=[vm, vm], out_specs=(vm, vm),
        scratch_shapes=_scatter_scratch([partial]) + _exchange_sems(1),
        compiler_params=pltpu.CompilerParams(vmem_limit_bytes=V7X_VMEM_LIMIT),
    )(partial, small)


def _adamw(w, g, m, v):
    m2 = ADAM_B1 * m + (1.0 - ADAM_B1) * g
    v2 = ADAM_B2 * v + (1.0 - ADAM_B2) * (g * g)
    m_hat = m2 / (1.0 - ADAM_B1 ** ADAM_STEP)
    v_hat = v2 / (1.0 - ADAM_B2 ** ADAM_STEP)
    delta = -ADAM_LR * (m_hat / (jnp.sqrt(v_hat) + ADAM_EPS) + ADAM_WD * w)
    return delta, m2, v2


def _adam_slabs_call(slabs, w, m, v, name):
    rows, cols = w.shape
    n_slabs = slabs.shape[0]
    tr = _largest_divisor(rows, 256, 2 * SUBLANES)

    def body(s_ref, w_ref, m_ref, v_ref, g_ref, d_ref, m2_ref, v2_ref):
        g = s_ref[0].astype(F32)
        for k in range(1, n_slabs):
            g = g + s_ref[k].astype(F32)
        delta, m2, v2 = _adamw(w_ref[...], g, m_ref[...], v_ref[...])
        g_ref[...] = g
        d_ref[...] = delta
        m2_ref[...] = m2
        v2_ref[...] = v2

    tile = pl.BlockSpec((tr, cols), lambda i: (i, 0))
    shp = jax.ShapeDtypeStruct((rows, cols), F32)
    return pl.pallas_call(
        body, name=name, out_shape=(shp, shp, shp, shp), grid=(rows // tr,),
        in_specs=[pl.BlockSpec((n_slabs, tr, cols), lambda i: (0, i, 0)), tile, tile, tile],
        out_specs=(tile, tile, tile, tile),
        compiler_params=pltpu.CompilerParams(dimension_semantics=("arbitrary",), vmem_limit_bytes=V7X_VMEM_LIMIT),
    )(slabs, w, m, v)


def _adam_ada_call(c_rows, dmod_cols, w, m, v):
    rows, cols = w.shape
    n_rows = c_rows.shape[0]
    tr = _largest_divisor(rows, 256, 128)

    def body(c_ref, dm_ref, w_ref, m_ref, v_ref, g_ref, d_ref, m2_ref, v2_ref):
        cv = c_ref[...]
        c_act = (cv * _sigmoid(cv)).astype(BF16)
        g = _tn_dot(c_act, dm_ref[...].astype(BF16))
        delta, m2, v2 = _adamw(w_ref[...], g, m_ref[...], v_ref[...])
        g_ref[...] = g
        d_ref[...] = delta
        m2_ref[...] = m2
        v2_ref[...] = v2

    tile = pl.BlockSpec((tr, cols), lambda i: (i, 0))
    shp = jax.ShapeDtypeStruct((rows, cols), F32)
    return pl.pallas_call(
        body, name="adam_w_ada", out_shape=(shp, shp, shp, shp), grid=(rows // tr,),
        in_specs=[pl.BlockSpec((n_rows, tr), lambda i: (0, i)), pl.BlockSpec((n_rows, cols), lambda i: (0, 0)),
                  tile, tile, tile],
        out_specs=(tile, tile, tile, tile),
        compiler_params=pltpu.CompilerParams(dimension_semantics=("arbitrary",), vmem_limit_bytes=V7X_VMEM_LIMIT),
    )(c_rows, dmod_cols, w, m, v)


def _small_sum_call(small_all, n_grad_rows, loss_rows, bias_rows, loss_scale):
    _, rows, width = small_all.shape
    lo, hi = loss_rows
    b0, b1, b2 = bias_rows
    nb = b1 - b0

    def body(s_ref, sum_ref, extra_ref):
        tot = s_ref[0]
        for k in range(1, N_DEV):
            tot = tot + s_ref[k]
        sum_ref[...] = tot[0:n_grad_rows, :]
        extra_ref[0:nb, :] = tot[b0:b1, :] + tot[b1:b2, :]
        head = tot[n_grad_rows - 2 * SUBLANES:n_grad_rows, :]
        rows_id = lax.broadcasted_iota(jnp.int32, head.shape, 0) + (n_grad_rows - 2 * SUBLANES)
        sq = jnp.where(jnp.logical_and(rows_id >= lo, rows_id < hi), head, 0.0)
        extra_ref[nb:nb + SUBLANES, :] = jnp.zeros((SUBLANES, width), F32) + jnp.sum(sq) * loss_scale

    vm = pl.BlockSpec(memory_space=VMEM)
    return pl.pallas_call(
        body, name="small_sum",
        out_shape=(jax.ShapeDtypeStruct((n_grad_rows, width), F32), jax.ShapeDtypeStruct((nb + SUBLANES, width), F32)),
        in_specs=[vm], out_specs=(vm, vm),
    )(small_all)


def _adam_small_call(ws, gs, ms, vs):
    n = len(ws)

    def body(*refs):
        w_refs, g_refs, m_refs, v_refs = refs[:n], refs[n:2 * n], refs[2 * n:3 * n], refs[3 * n:4 * n]
        d_refs, m2_refs, v2_refs = refs[4 * n:5 * n], refs[5 * n:6 * n], refs[6 * n:7 * n]
        for i in range(n):
            delta, m2, v2 = _adamw(w_refs[i][...], g_refs[i][...], m_refs[i][...], v_refs[i][...])
            d_refs[i][...] = delta
            m2_refs[i][...] = m2
            v2_refs[i][...] = v2

    vm = pl.BlockSpec(memory_space=VMEM)
    shapes = tuple(jax.ShapeDtypeStruct(w.shape, F32) for w in ws)
    outs = pl.pallas_call(body, name="adam_small", out_shape=shapes * 3,
                          in_specs=[vm] * (4 * n), out_specs=(vm,) * (3 * n))(*ws, *gs, *ms, *vs)
    return outs[:n], outs[n:2 * n], outs[2 * n:]


def kernel(x, c, w_ada, b_ada, w_in, conf_dw_w, conf_dw_b, conf_ln_g, conf_ln_b, sc_conv_w, w_out, w_mlp1, w_mlp2, g_final, loss_target, m_w_ada, m_b_ada, m_w_in, m_conf_dw_w, m_conf_dw_b, m_conf_ln_g, m_conf_ln_b, m_sc_conv_w, m_w_out, m_w_mlp1, m_w_mlp2, m_g_final, v_w_ada, v_b_ada, v_w_in, v_conf_dw_w, v_conf_dw_b, v_conf_ln_g, v_conf_ln_b, v_sc_conv_w, v_w_out, v_w_mlp1, v_w_mlp2, v_g_final):
    bsz, seq, d_model = x.shape
    c_half = conf_dw_b.shape[-1]
    n_taps = conf_dw_w.shape[1]
    cc = conf_dw_w.shape[-1]
    a_cols = w_ada.shape[-1]
    tokens = bsz * seq
    me = _dev_index()

    c_pad = jnp.pad(c, ((0, SUBLANES - bsz), (0, 0)))
    b_ada_loc = lax.dynamic_slice(b_ada, (0, me * a_cols), (1, a_cols))
    small_loc = jnp.zeros((HALO, 128), F32)
    small_loc = small_loc.at[:n_taps, :cc].set(conf_dw_w[0]).at[:3, cc:2 * cc].set(sc_conv_w[0])
    (win_t, wout_all), small_all, c_all, mod_rows = _gather_call(
        c_pad, w_ada[0], b_ada_loc, small_loc, [w_in[0].T.astype(BF16), w_out[0].astype(BF16)])
    cw = small_all[:, :, :cc].transpose(1, 0, 2).reshape(HALO, c_half)
    scw = small_all[:, :3, cc:2 * cc].transpose(1, 0, 2).reshape(3, c_half)
    cp = jnp.concatenate([conf_dw_b, conf_ln_g, conf_ln_b, scw, jnp.zeros((2, c_half), F32)], axis=0)
    mod = mod_rows[:, :bsz, :].transpose(1, 0, 2).reshape(bsz, 6, d_model)

    flat = lambda t: t.reshape(tokens, t.shape[-1])
    (proj, a1, cv, mixed, y1), (w1_all, w2_all) = _mixer_fwd_call(
        x, mod, win_t, wout_all, cw, cp, [w_mlp1[0].astype(BF16), w_mlp2[0].astype(BF16)])
    dx1, h2, dy2, u, dz, dmod2, head, dy1 = _mlp_call(
        x, y1, loss_target, mod, w1_all, w2_all, g_final.reshape(1, d_model))
    g_out = _wgrad_call(flat(mixed), flat(dy1), "wgrad_out")
    g_w1, (s_out,) = _wgrad_call(flat(h2), flat(dz), "wgrad_mlp1", owner_cols=w_mlp1.shape[-1],
                                 scatter=[g_out])
    g_w2, (s_w1,) = _wgrad_call(flat(u), flat(dy2), "wgrad_mlp2", scatter=[g_w1])
    grad_x, dproj, h1, dmod1, cgrad = _mixer_bwd_call(dx1, x, proj, a1, cv, mod, win_t, wout_all, cw, cp)
    g_in_t, (s_w2,) = _wgrad_call(flat(dproj), flat(h1), "wgrad_in", scatter=[g_w2])

    dmod = jnp.concatenate([dmod1[:, :2, :], dmod2[:, 3:4, :], dmod2[:, :3, :]], axis=1)
    n_cg = cgrad.shape[0]
    per_b = 6 * d_model // c_half
    per_b_pad = -(-per_b // SUBLANES) * SUBLANES
    dmod_rows = jnp.pad(dmod.reshape(bsz, per_b, c_half), ((0, 0), (0, per_b_pad - per_b), (0, 0)))
    small = jnp.concatenate([
        cgrad,
        head.reshape(2 * SUBLANES, c_half),
        dmod_rows.reshape(bsz * per_b_pad, c_half),
    ], axis=0)
    s_in, gathered = _tail_scatter_call(g_in_t, small)

    n_head = n_cg + 2 * SUBLANES
    sums, extra = _small_sum_call(
        gathered, n_head, (n_cg + 2, n_cg + 4), (n_head, n_head + per_b_pad, n_head + 2 * per_b_pad), 0.5 / d_model)
    loss = extra[per_b_pad, 0]
    g_b_ada = extra[:per_b].reshape(1, 6 * d_model)
    g_dw_w = lax.dynamic_slice(sums[:n_taps], (0, me * cc), (n_taps, cc))
    g_sc_w = lax.dynamic_slice(sums[HALO:HALO + 3], (0, me * cc), (3, cc))
    g_dw_b, g_ln_g, g_ln_b = sums[HALO + 3:HALO + 4], sums[HALO + 4:HALO + 5], sums[HALO + 5:HALO + 6]
    g_gf = sums[n_cg:n_cg + 2].reshape(1, d_model)

    dmod_all = gathered[:, n_head:, :].reshape(N_DEV, bsz, per_b_pad, c_half)[:, :, :per_b, :]
    dmod_all = dmod_all.reshape(N_DEV, bsz, 6 * d_model)
    dmod_cols = lax.dynamic_slice(dmod_all, (0, 0, me * a_cols), (N_DEV, bsz, a_cols))
    dmod_cols = jnp.pad(dmod_cols, ((0, 0), (0, SUBLANES - bsz), (0, 0))).reshape(N_DEV * SUBLANES, a_cols)
    c_rows = c_all.reshape(N_DEV * SUBLANES, d_model)
    g_ada, d_ada, m_ada, v_ada = _adam_ada_call(c_rows, dmod_cols, w_ada[0], m_w_ada[0], v_w_ada[0])

    gi, di, mi, vi = _adam_slabs_call(s_in, w_in[0].T, m_w_in[0].T, v_w_in[0].T, "adam_w_in")
    gi, di, mi, vi = gi.T, di.T, mi.T, vi.T
    go, do, mo, vo = _adam_slabs_call(s_out, w_out[0], m_w_out[0], v_w_out[0], "adam_w_out")
    g1, d1, m1, v1 = _adam_slabs_call(s_w1, w_mlp1[0], m_w_mlp1[0], v_w_mlp1[0], "adam_w_mlp1")
    g2, d2, m2, v2 = _adam_slabs_call(s_w2, w_mlp2[0], m_w_mlp2[0], v_w_mlp2[0], "adam_w_mlp2")

    small_like = [b_ada, conf_dw_w, conf_dw_b, conf_ln_g, conf_ln_b, sc_conv_w, g_final]
    two_d = lambda t: t.reshape(-1, t.shape[-1])
    small_g = [g_b_ada, g_dw_w, g_dw_b, g_ln_g, g_ln_b, g_sc_w, g_gf]
    sd, sm, sv = _adam_small_call(
        [two_d(t) for t in small_like], small_g,
        [two_d(t) for t in (m_b_ada, m_conf_dw_w, m_conf_dw_b, m_conf_ln_g, m_conf_ln_b, m_sc_conv_w, m_g_final)],
        [two_d(t) for t in (v_b_ada, v_conf_dw_w, v_conf_dw_b, v_conf_ln_g, v_conf_ln_b, v_sc_conv_w, v_g_final)])
    like = lambda parts: [p.reshape(w.shape) for p, w in zip(parts, small_like)]
    sg, sd, sm, sv = like(small_g), like(sd), like(sm), like(sv)

    def ordered(ada, small_list, w_in_, w_out_, w1_, w2_):
        b_ada_, dw_w_, dw_b_, ln_g_, ln_b_, sc_w_, gf_ = small_list
        return [ada[None], b_ada_, w_in_[None], dw_w_, dw_b_, ln_g_, ln_b_, sc_w_, w_out_[None], w1_[None], w2_[None], gf_]

    grads = ordered(g_ada, sg, gi, go, g1, g2)
    deltas = ordered(d_ada, sd, di, do, d1, d2)
    new_m = ordered(m_ada, sm, mi, mo, m1, m2)
    new_v = ordered(v_ada, sv, vi, vo, v1, v2)
    return (loss, grad_x, *grads, *deltas, *new_m, *new_v)
```

```python
import jax
import jax.numpy as jnp
from jax import lax
from jax.experimental import pallas as pl
from jax.experimental.pallas import tpu as pltpu

N_DEV = 8
RMS_EPS = 1e-6
ADAM_LR = 0.001
ADAM_B1 = 0.9
ADAM_B2 = 0.999
ADAM_EPS = 1e-08
ADAM_WD = 0.01
ADAM_STEP = 10

F32 = jnp.float32
BF16 = jnp.bfloat16
MESH = pl.DeviceIdType.MESH
VMEM = pltpu.VMEM
ANY = pl.ANY

HALO = 32
SHORT_HALO = 8
ROW_CHUNK = 32
SUBLANES = 8
V7X_VMEM_LIMIT = 56 * 1024 * 1024
MLP_VMEM_LIMIT = 48 * 1024 * 1024
MLP_TOKEN_TILE = 256
MIXER_FWD_TOKEN_TILE = 512


def _coords():
    return lax.axis_index("x"), lax.axis_index("y"), lax.axis_index("c")


def _dev_index():
    x, y, c = _coords()
    return 4 * x + 2 * y + c


def _peer(k):
    x, y, c = _coords()
    px = 1 - x if (k >> 2) & 1 else x
    py = 1 - y if (k >> 1) & 1 else y
    pc = 1 - c if k & 1 else c
    return (px, py, pc), 4 * px + 2 * py + pc


def _sigmoid(v):
    return jax.nn.sigmoid(v)


def _nt_dot(a, b):
    return lax.dot_general(a, b, (((1,), (1,)), ((), ())), preferred_element_type=F32)


def _nn_dot(a, b):
    return jnp.dot(a, b, preferred_element_type=F32)


def _tn_dot(a, b):
    return lax.dot_general(a, b, (((0,), (0,)), ((), ())), preferred_element_type=F32)


def _token_tile(seq):
    return 256 if seq % 256 == 0 else 64


GATHER_SEM_COLUMNS = 9
HALF_ROW_ALIGN = 16


class _TwoLevelGather:
    def __init__(self, srcs, dsts, send_sems, recv_sems, loc_sems):
        x, y, c = _coords()
        me = 4 * x + 2 * y + c
        sibling, along_x, along_y = (x, y, 1 - c), (1 - x, y, c), (x, 1 - y, c)
        from_x, from_y, diagonal = 4 * (1 - x) + 2 * y + c, 4 * x + 2 * (1 - y) + c, 4 * (1 - x) + 2 * (1 - y) + c

        def remote(src, dst, a, col, to):
            return pltpu.make_async_remote_copy(
                src_ref=src, dst_ref=dst, send_sem=send_sems.at[a, col], recv_sem=recv_sems.at[a, col],
                device_id=to, device_id_type=MESH)

        def onward(block, a, col, to):
            return remote(block, block, a, col, to)

        self.first, self.arrivals, self.second, self.halves, self.third = [], [], [], [], []
        for a, (src, dst) in enumerate(zip(srcs, dsts)):
            half = src.shape[0] // (2 * HALF_ROW_ALIGN) * HALF_ROW_ALIGN
            lower, upper = pl.ds(0, half), pl.ds(half, src.shape[0] - half)
            neighbours = [remote(src, dst.at[me], a, 1, along_x), remote(src, dst.at[me], a, 2, along_y)]
            self.first += [pltpu.make_async_copy(src, dst.at[me], loc_sems.at[a]),
                           remote(src, dst.at[me], a, 0, sibling)]
            self.arrivals += neighbours
            halves = [onward(dst.at[from_x, lower], a, 3, along_y), onward(dst.at[from_y, upper], a, 4, along_x)]
            self.halves += halves
            self.second += [[halves[0], onward(dst.at[from_x], a, 5, sibling)],
                            [halves[1], onward(dst.at[from_y], a, 6, sibling)]]
            self.third += [onward(dst.at[diagonal, lower], a, 7, sibling),
                           onward(dst.at[diagonal, upper], a, 8, sibling)]

    def start(self):
        for cp in self.first + self.arrivals:
            cp.start()

    def forward_arrivals(self):
        for arrival, sends in zip(self.arrivals, self.second):
            arrival.wait_recv()
            for cp in sends:
                cp.start()

    def forward_diagonal(self):
        for half, to_sibling in zip(self.halves, self.third):
            half.wait_recv()
            to_sibling.start()

    def finish(self):
        for cp in self.arrivals + self.halves:
            cp.wait_send()
        for cp in self.first + [sends[1] for sends in self.second] + self.third:
            cp.wait()


N_CHIPS = N_DEV // 2
PAIR_SUM_ROWS = 256


class _TwoLevelScatter:
    def __init__(self, partials, sums, mine, from_sibling, pair, first_send, first_recv, second_send, second_recv,
                 fetch_sems, local_sems):
        x, y, c = _coords()
        sibling = (x, y, 1 - c)
        chips = [(1 - x, y), (x, 1 - y), (1 - x, 1 - y)]
        self.my_chip = 2 * x + y
        self.mine, self.from_sibling, self.pair = mine, from_sibling, pair

        def remote(src, dst, send_sem, recv_sem, to):
            return pltpu.make_async_remote_copy(src_ref=src, dst_ref=dst, send_sem=send_sem, recv_sem=recv_sem,
                                                device_id=to, device_id_type=MESH)

        self.first, self.fetch, self.second, self.local = [], [], [], []
        for a in range(len(partials)):
            self.local.append(pltpu.make_async_copy(pair[a].at[self.my_chip], sums[a].at[self.my_chip],
                                                    local_sems.at[a]))
            self.first += [remote(partials[a].at[2 * q + (1 - c)], from_sibling[a].at[q],
                                  first_send.at[a, q], first_recv.at[a, q], sibling) for q in range(N_CHIPS)]
            self.fetch += [pltpu.make_async_copy(partials[a].at[2 * q + c], mine[a].at[q], fetch_sems.at[a, q])
                           for q in range(N_CHIPS)]
            self.second += [remote(pair[a].at[2 * cx + cy], sums[a].at[self.my_chip],
                                   second_send.at[a, j], second_recv.at[a, j], (cx, cy, c))
                            for j, (cx, cy) in enumerate(chips)]

    def start(self):
        for cp in self.first + self.fetch:
            cp.start()

    def pair_sums(self):
        for cp in self.first + self.fetch:
            cp.wait()
        for mine, theirs, both in zip(self.mine, self.from_sibling, self.pair):
            rows = mine.shape[1]
            for q in range(N_CHIPS):
                for r0 in range(0, rows, PAIR_SUM_ROWS):
                    part = pl.ds(r0, min(PAIR_SUM_ROWS, rows - r0))
                    both[q, part, :] = (mine[q, part, :].astype(F32) + theirs[q, part, :].astype(F32)).astype(BF16)
        for cp in self.second + self.local:
            cp.start()

    def finish(self):
        for cp in self.second + self.local:
            cp.wait()


def _scatter_scratch(partials):
    n = len(partials)
    zones = [pltpu.VMEM((N_CHIPS,) + p.shape[1:], BF16) for p in partials]
    return 3 * zones + [pltpu.SemaphoreType.DMA((n, N_CHIPS)), pltpu.SemaphoreType.DMA((n, N_CHIPS)),
                        pltpu.SemaphoreType.DMA((n, N_CHIPS - 1)), pltpu.SemaphoreType.DMA((n, N_CHIPS - 1)),
                        pltpu.SemaphoreType.DMA((n, N_CHIPS)), pltpu.SemaphoreType.DMA((n,))]


def _exchange_sems(n_arrays, columns=GATHER_SEM_COLUMNS):
    return [pltpu.SemaphoreType.DMA((n_arrays, columns)), pltpu.SemaphoreType.DMA((n_arrays, columns)),
            pltpu.SemaphoreType.DMA((n_arrays,))]


def _gather_call(c_pad, w_ada, b_ada_loc, small_loc, big_shards):
    n_big = len(big_shards)
    d_model = c_pad.shape[1]
    a_cols = w_ada.shape[1]

    def body(c_ref, wada_ref, bada_ref, small_ref, *rest):
        big_in = rest[:n_big]
        big_out = rest[n_big:2 * n_big]
        small_all, c_all, mod_rows = rest[2 * n_big:2 * n_big + 3]
        modcols, big_send, big_recv, loc_sem, s_send, s_recv = rest[2 * n_big + 3:]
        me = _dev_index()
        big = _TwoLevelGather(big_in, big_out, big_send, big_recv, loc_sem)
        big.start()

        small_all[me] = small_ref[...]
        c_all[me] = c_ref[...]
        first = []
        for k in range(1, N_DEV):
            peer, _ = _peer(k)
            for i, (src, dst) in enumerate(((small_ref, small_all), (c_ref, c_all))):
                cp = pltpu.make_async_remote_copy(
                    src_ref=src, dst_ref=dst.at[me],
                    send_sem=s_send.at[i, k - 1], recv_sem=s_recv.at[i, k - 1],
                    device_id=peer, device_id_type=MESH)
                cp.start()
                first.append(cp)
        for cp in first:
            cp.wait()

        c_rows = c_all[...].reshape(N_DEV * SUBLANES, d_model)
        c_act = c_rows * _sigmoid(c_rows)
        modcols[...] = _nn_dot(c_act.astype(BF16), wada_ref[...].astype(BF16)) + bada_ref[...]
        mod_rows[me] = modcols[pl.ds(pl.multiple_of(me * SUBLANES, SUBLANES), SUBLANES), :]
        second = []
        for k in range(1, N_DEV):
            peer, pidx = _peer(k)
            cp = pltpu.make_async_remote_copy(
                src_ref=modcols.at[pl.ds(pl.multiple_of(pidx * SUBLANES, SUBLANES), SUBLANES), :],
                dst_ref=mod_rows.at[me],
                send_sem=s_send.at[2, k - 1], recv_sem=s_recv.at[2, k - 1],
                device_id=peer, device_id_type=MESH)
            cp.start()
            second.append(cp)
        big.forward_arrivals()
        for cp in second:
            cp.wait()
        big.forward_diagonal()
        big.finish()

    out_shape = tuple(jax.ShapeDtypeStruct((N_DEV,) + s.shape, s.dtype) for s in big_shards) + (
        jax.ShapeDtypeStruct((N_DEV,) + small_loc.shape, F32),
        jax.ShapeDtypeStruct((N_DEV, SUBLANES, d_model), F32),
        jax.ShapeDtypeStruct((N_DEV, SUBLANES, a_cols), F32),
    )
    vm = pl.BlockSpec(memory_space=VMEM)
    hbm = pl.BlockSpec(memory_space=ANY)
    outs = pl.pallas_call(
        body, name="gather_weights_mod", out_shape=out_shape,
        in_specs=[vm, vm, vm, vm] + [hbm] * n_big,
        out_specs=tuple([hbm] * n_big + [vm, vm, vm]),
        scratch_shapes=[
            pltpu.VMEM((N_DEV * SUBLANES, a_cols), F32),
            *_exchange_sems(n_big),
            pltpu.SemaphoreType.DMA((3, N_DEV - 1)),
            pltpu.SemaphoreType.DMA((3, N_DEV - 1)),
        ],
        compiler_params=pltpu.CompilerParams(vmem_limit_bytes=V7X_VMEM_LIMIT),
    )(c_pad, w_ada, b_ada_loc, small_loc, *big_shards)
    return outs[:n_big], outs[n_big], outs[n_big + 1], outs[n_big + 2]


def _shifted_rows_count(tm):
    return tm + HALO - SUBLANES


def _fill_shifted(ext, shifted, tm):
    for s in range(1, SUBLANES):
        shifted[s - 1] = ext[s:s + _shifted_rows_count(tm), :]


def _shifted_rows(ext, shifted, start, rows):
    phase = start % SUBLANES
    aligned = start - phase
    if phase == 0:
        return ext[aligned:aligned + rows, :]
    return shifted[phase - 1, aligned:aligned + rows, :]


def _layer_norm_parts(a1):
    mu = jnp.mean(a1, axis=-1, keepdims=True)
    xc = a1 - mu
    rstd = lax.rsqrt(jnp.mean(xc * xc, axis=-1, keepdims=True) + RMS_EPS)
    return xc * rstd, rstd


def _mixer_fwd_call(x, mod, win_t, wout, cw, cp, later_shards):
    n_later = len(later_shards)
    bsz, seq, d_model = x.shape
    c_half = cw.shape[1]
    n_taps = 31
    d_in = win_t.shape[0] * win_t.shape[1]
    tm = MIXER_FWD_TOKEN_TILE if seq % MIXER_FWD_TOKEN_TILE == 0 else _token_tile(seq)
    nt = seq // tm
    arrivals_step, diagonal_step = (7 * bsz * nt) // 16, (11 * bsz * nt) // 16

    def body(x_ref, mod_ref, win_ref, wout_ref, cw_ref, cp_ref, *rest):
        shard_refs, rest = rest[:n_later], rest[n_later:]
        proj_ref, a1_ref, cv_ref, mixed_ref, y1_ref = rest[:5]
        gathered_refs, rest = rest[5:5 + n_later], rest[5 + n_later:]
        aext, qext, ashift, send_sems, recv_sems, loc_sems = rest
        b, t = pl.program_id(0), pl.program_id(1)

        step = b * nt + t

        @pl.when(step == 0)
        def _():
            _TwoLevelGather(shard_refs, gathered_refs, send_sems, recv_sems, loc_sems).start()

        @pl.when(step == arrivals_step)
        def _():
            _TwoLevelGather(shard_refs, gathered_refs, send_sems, recv_sems, loc_sems).forward_arrivals()

        @pl.when(step == diagonal_step)
        def _():
            _TwoLevelGather(shard_refs, gathered_refs, send_sems, recv_sems, loc_sems).forward_diagonal()

        xv = x_ref[...]
        sh1, sc1 = mod_ref[0:1, :], mod_ref[1:2, :]
        r1 = lax.rsqrt(jnp.mean(xv * xv, axis=-1, keepdims=True) + RMS_EPS)
        h1 = (xv * r1) * (1.0 + sc1) + sh1
        proj = _nt_dot(h1.astype(BF16), win_ref[...].reshape(d_in, d_model))
        proj_ref[...] = proj
        val, gate = proj[:, 0:c_half], proj[:, c_half:2 * c_half]
        s_b, s_c, s_h = proj[:, 2 * c_half:3 * c_half], proj[:, 3 * c_half:4 * c_half], proj[:, 4 * c_half:5 * c_half]

        @pl.when(t == 0)
        def _():
            aext[0:HALO, :] = jnp.zeros((HALO, c_half), F32)
            qext[0:SHORT_HALO, :] = jnp.zeros((SHORT_HALO, c_half), F32)

        @pl.when(t > 0)
        def _():
            aext[0:HALO, :] = aext[tm:tm + HALO, :]
            qext[0:SHORT_HALO, :] = qext[tm:tm + SHORT_HALO, :]
        aext[HALO:HALO + tm, :] = val * _sigmoid(gate)
        qext[SHORT_HALO:SHORT_HALO + tm, :] = s_c * s_h

        base = HALO - (n_taps - 1)
        _fill_shifted(aext, ashift, tm)
        for r0 in range(0, tm, ROW_CHUNK):
            acc = jnp.zeros((ROW_CHUNK, c_half), F32)
            for k in range(n_taps):
                acc = acc + cw_ref[k:k + 1, :] * _shifted_rows(aext, ashift, r0 + base + k, ROW_CHUNK)
            a1_ref[r0:r0 + ROW_CHUNK, :] = acc + cp_ref[0:1, :]
        sbase = SHORT_HALO - 2
        conv3 = cp_ref[3:4, :] * qext[sbase:sbase + tm, :]
        conv3 = conv3 + cp_ref[4:5, :] * qext[sbase + 1:sbase + 1 + tm, :]
        conv3 = conv3 + cp_ref[5:6, :] * qext[sbase + 2:sbase + 2 + tm, :]
        cv_ref[...] = conv3

        norm, _ = _layer_norm_parts(a1_ref[...])
        a2 = norm * cp_ref[1:2, :] + cp_ref[2:3, :]
        mixed = jnp.concatenate([a2 * _sigmoid(a2), s_b * conv3], axis=-1).astype(BF16)
        mixed_ref[...] = mixed
        y1 = _nn_dot(mixed, wout_ref[...].reshape(d_model, d_model))
        y1_ref[...] = y1

        @pl.when(step == bsz * nt - 1)
        def _():
            _TwoLevelGather(shard_refs, gathered_refs, send_sems, recv_sems, loc_sems).finish()

    hbm = pl.BlockSpec(memory_space=ANY)

    def tok(width):
        return pl.BlockSpec((None, tm, width), lambda b, t: (b, t, 0))

    def const(shape):
        return pl.BlockSpec(shape, lambda b, t: (0,) * len(shape))

    def resident(shape):
        return pl.BlockSpec(shape, lambda b, t: (0,) * len(shape), pipeline_mode=pl.Buffered(1))

    out_shape = (
        jax.ShapeDtypeStruct((bsz, seq, d_in), F32),
        jax.ShapeDtypeStruct((bsz, seq, c_half), F32),
        jax.ShapeDtypeStruct((bsz, seq, c_half), F32),
        jax.ShapeDtypeStruct((bsz, seq, d_model), BF16),
        jax.ShapeDtypeStruct((bsz, seq, d_model), F32),
    ) + tuple(jax.ShapeDtypeStruct((N_DEV,) + s.shape, s.dtype) for s in later_shards)
    outs = pl.pallas_call(
        body, name="mixer_fwd", out_shape=out_shape, grid=(bsz, nt),
        in_specs=[tok(d_model), pl.BlockSpec((None, 6, d_model), lambda b, t: (b, 0, 0)),
                  resident(win_t.shape), resident(wout.shape), const(cw.shape), const(cp.shape)] + [hbm] * n_later,
        out_specs=(tok(d_in), tok(c_half), tok(c_half), tok(d_model), tok(d_model)) + (hbm,) * n_later,
        scratch_shapes=[pltpu.VMEM((tm + HALO, c_half), F32), pltpu.VMEM((tm + SHORT_HALO, c_half), F32),
                        pltpu.VMEM((SUBLANES - 1, _shifted_rows_count(tm), c_half), F32)]
        + _exchange_sems(n_later),
        compiler_params=pltpu.CompilerParams(
            dimension_semantics=("arbitrary", "arbitrary"), vmem_limit_bytes=V7X_VMEM_LIMIT),
    )(x, mod, win_t, wout, cw, cp, *later_shards)
    return outs[:5], outs[5:]


def _mlp_call(x, y1, target, mod, w1, w2, g_final):
    bsz, seq, d_model = x.shape
    n_blk, _, f_blk = w1.shape
    d_ff = n_blk * f_blk
    tm = MLP_TOKEN_TILE if seq % MLP_TOKEN_TILE == 0 else _token_tile(seq)
    nt = seq // tm

    def body(x_ref, y1_ref, tgt_ref, mod_ref, w1_ref, w2_ref, gf_ref,
             dx1_ref, h2_ref, dy2_ref, u_ref, dz_ref, dmod_ref, head_ref, dy1_ref, relu_scr):
        b, t = pl.program_id(0), pl.program_id(1)
        x1v = x_ref[...] + mod_ref[2:3, :] * y1_ref[...]
        sh2, sc2, g2 = mod_ref[3:4, :], mod_ref[4:5, :], mod_ref[5:6, :]
        gf = gf_ref[...]
        r2 = lax.rsqrt(jnp.mean(x1v * x1v, axis=-1, keepdims=True) + RMS_EPS)
        xn2 = x1v * r2
        h2 = (xn2 * (1.0 + sc2) + sh2).astype(BF16)
        h2_ref[...] = h2
        y2 = jnp.zeros((tm, d_model), F32)
        for j in range(n_blk):
            cols = slice(j * f_blk, (j + 1) * f_blk)
            rz = jnp.maximum(_nn_dot(h2, w1_ref[j]), 0.0)
            relu_scr[:, cols] = rz
            ub = (rz * rz).astype(BF16)
            u_ref[:, cols] = ub
            y2 = y2 + _nn_dot(ub, w2_ref[j])
        x2 = x1v + g2 * y2
        r3 = lax.rsqrt(jnp.mean(x2 * x2, axis=-1, keepdims=True) + RMS_EPS)
        xn3 = x2 * r3
        diff = xn3 * gf - tgt_ref[...]
        dout = diff * (1.0 / d_model)

        @pl.when(jnp.logical_and(b == 0, t == 0))
        def _():
            head_ref[...] = jnp.zeros(head_ref.shape, F32)

        @pl.when(t == 0)
        def _():
            dmod_ref[...] = jnp.zeros(dmod_ref.shape, F32)

        head_ref[0:1, :] += jnp.sum(dout * xn3, axis=0, keepdims=True)
        head_ref[1:2, :] += jnp.sum(diff * diff, axis=0, keepdims=True)
        dxn3 = dout * gf
        dx2 = r3 * (dxn3 - xn3 * jnp.mean(dxn3 * xn3, axis=-1, keepdims=True))
        dmod_ref[2:3, :] += jnp.sum(dx2 * y2, axis=0, keepdims=True)
        dy2 = (g2 * dx2).astype(BF16)
        dy2_ref[...] = dy2
        dh2 = jnp.zeros((tm, d_model), F32)
        for j in range(n_blk):
            cols = slice(j * f_blk, (j + 1) * f_blk)
            dz = (_nt_dot(dy2, w2_ref[j]) * (2.0 * relu_scr[:, cols])).astype(BF16)
            dz_ref[:, cols] = dz
            dh2 = dh2 + _nt_dot(dz, w1_ref[j])
        dmod_ref[0:1, :] += jnp.sum(dh2, axis=0, keepdims=True)
        dmod_ref[1:2, :] += jnp.sum(dh2 * xn2, axis=0, keepdims=True)
        dxn2 = dh2 * (1.0 + sc2)
        dx1 = dx2 + r2 * (dxn2 - xn2 * jnp.mean(dxn2 * xn2, axis=-1, keepdims=True))
        dx1_ref[...] = dx1
        dy1_ref[...] = (mod_ref[2:3, :] * dx1).astype(BF16)
        dmod_ref[3:4, :] += jnp.sum(dx1 * y1_ref[...], axis=0, keepdims=True)

    def tok(width):
        return pl.BlockSpec((None, tm, width), lambda b, t: (b, t, 0))

    def const(shape):
        return pl.BlockSpec(shape, lambda b, t: (0,) * len(shape))

    def resident(shape):
        return pl.BlockSpec(shape, lambda b, t: (0,) * len(shape), pipeline_mode=pl.Buffered(1))

    out_shape = (
        jax.ShapeDtypeStruct((bsz, seq, d_model), F32),
        jax.ShapeDtypeStruct((bsz, seq, d_model), BF16),
        jax.ShapeDtypeStruct((bsz, seq, d_model), BF16),
        jax.ShapeDtypeStruct((bsz, seq, d_ff), BF16),
        jax.ShapeDtypeStruct((bsz, seq, d_ff), BF16),
        jax.ShapeDtypeStruct((bsz, SUBLANES, d_model), F32),
        jax.ShapeDtypeStruct((SUBLANES, d_model), F32),
        jax.ShapeDtypeStruct((bsz, seq, d_model), BF16),
    )
    return pl.pallas_call(
        body, name="mlp_fwd_bwd", out_shape=out_shape, grid=(bsz, nt),
        in_specs=[tok(d_model), tok(d_model), tok(d_model),
                  pl.BlockSpec((None, 6, d_model), lambda b, t: (b, 0, 0)),
                  resident(w1.shape), resident(w2.shape), const(g_final.shape)],
        out_specs=(tok(d_model), tok(d_model), tok(d_model), tok(d_ff), tok(d_ff),
                   pl.BlockSpec((None, SUBLANES, d_model), lambda b, t: (b, 0, 0)),
                   const((SUBLANES, d_model)), tok(d_model)),
        scratch_shapes=[pltpu.VMEM((tm, d_ff), F32)],
        compiler_params=pltpu.CompilerParams(
            dimension_semantics=("arbitrary", "arbitrary"), vmem_limit_bytes=MLP_VMEM_LIMIT),
    )(x, y1, target, mod, w1, w2, g_final)


def _mixer_bwd_call(dx1, x, proj, a1, cv, mod, win_t, wout, cw, cp):
    bsz, seq, d_model = x.shape
    c_half = cw.shape[1]
    n_taps = 31
    d_in = win_t.shape[0] * win_t.shape[1]
    tm = _token_tile(seq)
    nt = seq // tm

    def body(dx1_ref, x_ref, proj_ref, a1_ref, cv_ref, mod_ref, win_ref, wout_ref, cw_ref, cp_ref,
             gx_ref, dproj_ref, h1_ref, dmod_ref, cgrad_ref,
             dext, cext, a0_scr, da0_scr, tap_acc, row_acc, dshift):
        b, step = pl.program_id(0), pl.program_id(1)
        first = jnp.logical_and(b == 0, step == 0)
        last = jnp.logical_and(b == bsz - 1, step == nt - 1)
        dx1v = dx1_ref[...]
        xv = x_ref[...]
        sh1, sc1, g1 = mod_ref[0:1, :], mod_ref[1:2, :], mod_ref[2:3, :]

        @pl.when(first)
        def _():
            tap_acc[...] = jnp.zeros(tap_acc.shape, F32)
            row_acc[...] = jnp.zeros(row_acc.shape, F32)

        @pl.when(step == 0)
        def _():
            dmod_ref[...] = jnp.zeros(dmod_ref.shape, F32)
            dext[tm:tm + HALO, :] = jnp.zeros((HALO, c_half), F32)
            cext[tm:tm + SHORT_HALO, :] = jnp.zeros((SHORT_HALO, c_half), F32)

        @pl.when(step > 0)
        def _():
            dext[tm:tm + HALO, :] = dext[0:HALO, :]
            cext[tm:tm + SHORT_HALO, :] = cext[0:SHORT_HALO, :]

        dy1 = (g1 * dx1v).astype(BF16)
        dmixed = _nt_dot(dy1, wout_ref[...].reshape(d_model, d_model))
        d_a, d_s = dmixed[:, 0:c_half], dmixed[:, c_half:2 * c_half]

        val, gate = proj_ref[:, 0:c_half], proj_ref[:, c_half:2 * c_half]
        s_b = proj_ref[:, 2 * c_half:3 * c_half]
        s_c, s_h = proj_ref[:, 3 * c_half:4 * c_half], proj_ref[:, 4 * c_half:5 * c_half]

        d_sb = d_s * cv_ref[...]
        cext[0:tm, :] = d_s * s_b
        q = s_c * s_h
        dq = jnp.zeros((tm, c_half), F32)
        for k in range(3):
            shifted = cext[2 - k:2 - k + tm, :]
            dq = dq + cp_ref[3 + k:4 + k, :] * shifted
            row_acc[k:k + 1, :] += jnp.sum(q * shifted, axis=0, keepdims=True)
        d_sc, d_sh = dq * s_h, dq * s_c

        norm, rstd = _layer_norm_parts(a1_ref[...])
        ln_g = cp_ref[1:2, :]
        a2 = norm * ln_g + cp_ref[2:3, :]
        sg = _sigmoid(a2)
        d_a2 = d_a * (sg * (1.0 + a2 * (1.0 - sg)))
        row_acc[4:5, :] += jnp.sum(d_a2 * norm, axis=0, keepdims=True)
        row_acc[5:6, :] += jnp.sum(d_a2, axis=0, keepdims=True)
        d_n = d_a2 * ln_g
        d_a1 = rstd * (d_n - jnp.mean(d_n, axis=-1, keepdims=True)
                       - norm * jnp.mean(d_n * norm, axis=-1, keepdims=True))
        row_acc[3:4, :] += jnp.sum(d_a1, axis=0, keepdims=True)
        dext[0:tm, :] = d_a1
        sig_g = _sigmoid(gate)
        a0_scr[...] = val * sig_g

        _fill_shifted(dext, dshift, tm)
        for r0 in range(0, tm, ROW_CHUNK):
            a0c = a0_scr[r0:r0 + ROW_CHUNK, :]
            acc = jnp.zeros((ROW_CHUNK, c_half), F32)
            for k in range(n_taps):
                shifted = _shifted_rows(dext, dshift, r0 + (n_taps - 1) - k, ROW_CHUNK)
                acc = acc + cw_ref[k:k + 1, :] * shifted
                prod = a0c * shifted
                part = prod[0:SUBLANES, :]
                for g in range(1, ROW_CHUNK // SUBLANES):
                    part = part + prod[g * SUBLANES:(g + 1) * SUBLANES, :]
                tap_acc[k * SUBLANES:(k + 1) * SUBLANES, :] += part
            da0_scr[r0:r0 + ROW_CHUNK, :] = acc
        d_a0 = da0_scr[...]
        d_val = d_a0 * sig_g
        d_gate = d_a0 * val * sig_g * (1.0 - sig_g)

        dproj = jnp.concatenate([d_val, d_gate, d_sb, d_sc, d_sh], axis=-1).astype(BF16)
        dproj_ref[...] = dproj
        dh1 = _nn_dot(dproj, win_ref[...].reshape(d_in, d_model))
        r1 = lax.rsqrt(jnp.mean(xv * xv, axis=-1, keepdims=True) + RMS_EPS)
        xn1 = xv * r1
        h1_ref[...] = (xn1 * (1.0 + sc1) + sh1).astype(BF16)
        dmod_ref[0:1, :] += jnp.sum(dh1, axis=0, keepdims=True)
        dmod_ref[1:2, :] += jnp.sum(dh1 * xn1, axis=0, keepdims=True)
        dxn1 = dh1 * (1.0 + sc1)
        gx_ref[...] = dx1v + r1 * (dxn1 - xn1 * jnp.mean(dxn1 * xn1, axis=-1, keepdims=True))

        @pl.when(last)
        def _():
            taps = jnp.sum(tap_acc[...].reshape(HALO, SUBLANES, c_half), axis=1)
            cgrad_ref[0:HALO, :] = taps
            cgrad_ref[HALO:HALO + SUBLANES, :] = row_acc[...]

    def tok(width):
        return pl.BlockSpec((None, tm, width), lambda b, s: (b, nt - 1 - s, 0))

    def const(shape):
        return pl.BlockSpec(shape, lambda b, s: (0,) * len(shape))

    mod_spec = pl.BlockSpec((None, 6, d_model), lambda b, s: (b, 0, 0))
    out_shape = (
        jax.ShapeDtypeStruct((bsz, seq, d_model), F32),
        jax.ShapeDtypeStruct((bsz, seq, d_in), BF16),
        jax.ShapeDtypeStruct((bsz, seq, d_model), BF16),
        jax.ShapeDtypeStruct((bsz, SUBLANES, d_model), F32),
        jax.ShapeDtypeStruct((HALO + SUBLANES, c_half), F32),
    )
    return pl.pallas_call(
        body, name="mixer_bwd", out_shape=out_shape, grid=(bsz, nt),
        in_specs=[tok(d_model), tok(d_model), tok(d_in), tok(c_half), tok(c_half), mod_spec,
                  const(win_t.shape), const(wout.shape), const(cw.shape), const(cp.shape)],
        out_specs=(tok(d_model), tok(d_in), tok(d_model),
                   pl.BlockSpec((None, SUBLANES, d_model), lambda b, s: (b, 0, 0)),
                   const((HALO + SUBLANES, c_half))),
        scratch_shapes=[
            pltpu.VMEM((tm + HALO, c_half), F32), pltpu.VMEM((tm + SHORT_HALO, c_half), F32),
            pltpu.VMEM((tm, c_half), F32), pltpu.VMEM((tm, c_half), F32),
            pltpu.VMEM((HALO * SUBLANES, c_half), F32), pltpu.VMEM((SUBLANES, c_half), F32),
            pltpu.VMEM((SUBLANES - 1, _shifted_rows_count(tm), c_half), F32),
        ],
        compiler_params=pltpu.CompilerParams(
            dimension_semantics=("arbitrary", "arbitrary"), vmem_limit_bytes=V7X_VMEM_LIMIT),
    )(dx1, x, proj, a1, cv, mod, win_t, wout, cw, cp)


def _largest_divisor(n, cap, multiple):
    best = None
    for cand in range(multiple, min(n, cap) + 1, multiple):
        if n % cand == 0:
            best = cand
    return best if best is not None else n


WGRAD_TOKENS_PER_STEP = 2048
WGRAD_COLS_PER_STEP = 1024


def _wgrad_call(a, b, name, owner_cols=None, scatter=()):
    n_sc = len(scatter)
    tokens, m_dim = a.shape
    n_dim = b.shape[1]
    bk = _largest_divisor(tokens, WGRAD_TOKENS_PER_STEP, 128)
    n_k = tokens // bk
    if owner_cols is None:
        bm = _largest_divisor(m_dim, 1024, m_dim // N_DEV)
        bn = n_dim
        owners = 1
        out_shape = jax.ShapeDtypeStruct((m_dim, n_dim), BF16)
        out_spec = pl.BlockSpec((bm, bn), lambda i, j, k: (i, j))
    else:
        bm = m_dim
        bn = _largest_divisor(n_dim, WGRAD_COLS_PER_STEP, owner_cols)
        owners = bn // owner_cols
        out_shape = jax.ShapeDtypeStruct((n_dim // owner_cols, m_dim, owner_cols), BF16)
        out_spec = pl.BlockSpec((owners, bm, owner_cols), lambda i, j, k: (j, i, 0))

    grid = (m_dim // bm, n_dim // bn, n_k)
    n_steps = grid[0] * grid[1] * grid[2]
    pair_step = min(max(1, n_steps // 5), n_steps - 1)

    def body(a_ref, b_ref, *rest):
        part_refs, o_ref, sum_refs, acc = rest[:n_sc], rest[n_sc], rest[n_sc + 1:2 * n_sc + 1], rest[2 * n_sc + 1]
        k = pl.program_id(2)
        step = (pl.program_id(0) * grid[1] + pl.program_id(1)) * n_k + k

        def exchange():
            extra = rest[2 * n_sc + 2:]
            return _TwoLevelScatter(part_refs, sum_refs, extra[:n_sc], extra[n_sc:2 * n_sc], extra[2 * n_sc:3 * n_sc],
                                    *extra[3 * n_sc:])

        if n_sc:
            @pl.when(step == 0)
            def _():
                exchange().start()

            @pl.when(step == pair_step)
            def _():
                exchange().pair_sums()

        @pl.when(k == 0)
        def _():
            acc[...] = jnp.zeros(acc.shape, F32)

        acc[...] += _tn_dot(a_ref[...], b_ref[...])

        @pl.when(k == n_k - 1)
        def _():
            if owner_cols is None:
                o_ref[...] = acc[...].astype(BF16)
            else:
                for q in range(owners):
                    o_ref[q] = acc[:, q * owner_cols:(q + 1) * owner_cols].astype(BF16)

        if n_sc:
            @pl.when(step == n_steps - 1)
            def _():
                exchange().finish()

    hbm = pl.BlockSpec(memory_space=ANY)
    sum_shapes = tuple(jax.ShapeDtypeStruct((N_CHIPS,) + p.shape[1:], BF16) for p in scatter)
    outs = pl.pallas_call(
        body, name=name, out_shape=(out_shape,) + sum_shapes, grid=grid,
        in_specs=[pl.BlockSpec((bk, bm), lambda i, j, k: (k, i)), pl.BlockSpec((bk, bn), lambda i, j, k: (k, j))]
        + [hbm] * n_sc,
        out_specs=(out_spec,) + (hbm,) * n_sc,
        scratch_shapes=[pltpu.VMEM((bm, bn), F32)] + (_scatter_scratch(scatter) if n_sc else []),
        compiler_params=pltpu.CompilerParams(
            dimension_semantics=("arbitrary", "arbitrary", "arbitrary"), vmem_limit_bytes=V7X_VMEM_LIMIT),
    )(a, b, *scatter)
    out = outs[0]
    if owner_cols is None:
        out = out.reshape(N_DEV, m_dim // N_DEV, n_dim)
    return (out, list(outs[1:])) if n_sc else out


def _tail_scatter_call(partial, small):
    def body(g_ref, small_ref, out_ref, small_all, mine, from_sibling, pair,
             p1_send, p1_recv, p2_send, p2_recv, p_fetch, p_loc, s_send, s_recv, s_loc):
        gather = _TwoLevelGather([small_ref], [small_all], s_send, s_recv, s_loc)
        scatter = _TwoLevelScatter([g_ref], [out_ref], [mine], [from_sibling], [pair],
                                   p1_send, p1_recv, p2_send, p2_recv, p_fetch, p_loc)
        gather.start()
        scatter.start()
        scatter.pair_sums()
        gather.forward_arrivals()
        scatter.finish()
        gather.forward_diagonal()
        gather.finish()

    vm = pl.BlockSpec(memory_space=VMEM)
    return pl.pallas_call(
        body, name="scatter_tail",
        out_shape=(jax.ShapeDtypeStruct((N_CHIPS,) + partial.shape[1:], BF16),
                   jax.ShapeDtypeStruct((N_DEV,) + small.shape, F32)),
        in_specs=[pl.BlockSpec(memory_space=ANY), vm], out_specs=(vm, vm),
        scratch_shapes=_scatter_scratch([partial]) + _exchange_sems(1),
        compiler_params=pltpu.CompilerParams(vmem_limit_bytes=V7X_VMEM_LIMIT),
    )(partial, small)


def _adamw(w, g, m, v):
    m2 = ADAM_B1 * m + (1.0 - ADAM_B1) * g
    v2 = ADAM_B2 * v + (1.0 - ADAM_B2) * (g * g)
    m_hat = m2 / (1.0 - ADAM_B1 ** ADAM_STEP)
    v_hat = v2 / (1.0 - ADAM_B2 ** ADAM_STEP)
    delta = -ADAM_LR * (m_hat / (jnp.sqrt(v_hat) + ADAM_EPS) + ADAM_WD * w)
    return delta, m2, v2


def _adam_slabs_call(slabs, w, m, v, name):
    rows, cols = w.shape
    n_slabs = slabs.shape[0]
    tr = _largest_divisor(rows, 256, 2 * SUBLANES)

    def body(s_ref, w_ref, m_ref, v_ref, g_ref, d_ref, m2_ref, v2_ref):
        g = s_ref[0].astype(F32)
        for k in range(1, n_slabs):
            g = g + s_ref[k].astype(F32)
        delta, m2, v2 = _adamw(w_ref[...], g, m_ref[...], v_ref[...])
        g_ref[...] = g
        d_ref[...] = delta
        m2_ref[...] = m2
        v2_ref[...] = v2

    tile = pl.BlockSpec((tr, cols), lambda i: (i, 0))
    shp = jax.ShapeDtypeStruct((rows, cols), F32)
    return pl.pallas_call(
        body, name=name, out_shape=(shp, shp, shp, shp), grid=(rows // tr,),
        in_specs=[pl.BlockSpec((n_slabs, tr, cols), lambda i: (0, i, 0)), tile, tile, tile],
        out_specs=(tile, tile, tile, tile),
        compiler_params=pltpu.CompilerParams(dimension_semantics=("arbitrary",), vmem_limit_bytes=V7X_VMEM_LIMIT),
    )(slabs, w, m, v)


def _adam_ada_call(c_rows, dmod_cols, w, m, v):
    rows, cols = w.shape
    n_rows = c_rows.shape[0]
    tr = _largest_divisor(rows, 256, 128)

    def body(c_ref, dm_ref, w_ref, m_ref, v_ref, g_ref, d_ref, m2_ref, v2_ref):
        cv = c_ref[...]
        c_act = (cv * _sigmoid(cv)).astype(BF16)
        g = _tn_dot(c_act, dm_ref[...].astype(BF16))
        delta, m2, v2 = _adamw(w_ref[...], g, m_ref[...], v_ref[...])
        g_ref[...] = g
        d_ref[...] = delta
        m2_ref[...] = m2
        v2_ref[...] = v2

    tile = pl.BlockSpec((tr, cols), lambda i: (i, 0))
    shp = jax.ShapeDtypeStruct((rows, cols), F32)
    return pl.pallas_call(
        body, name="adam_w_ada", out_shape=(shp, shp, shp, shp), grid=(rows // tr,),
        in_specs=[pl.BlockSpec((n_rows, tr), lambda i: (0, i)), pl.BlockSpec((n_rows, cols), lambda i: (0, 0)),
                  tile, tile, tile],
        out_specs=(tile, tile, tile, tile),
        compiler_params=pltpu.CompilerParams(dimension_semantics=("arbitrary",), vmem_limit_bytes=V7X_VMEM_LIMIT),
    )(c_rows, dmod_cols, w, m, v)


def _small_sum_call(small_all, n_grad_rows, loss_rows, bias_rows, loss_scale):
    _, rows, width = small_all.shape
    lo, hi = loss_rows
    b0, b1, b2 = bias_rows
    nb = b1 - b0

    def body(s_ref, sum_ref, extra_ref):
        tot = s_ref[0]
        for k in range(1, N_DEV):
            tot = tot + s_ref[k]
        sum_ref[...] = tot[0:n_grad_rows, :]
        extra_ref[0:nb, :] = tot[b0:b1, :] + tot[b1:b2, :]
        head = tot[n_grad_rows - 2 * SUBLANES:n_grad_rows, :]
        rows_id = lax.broadcasted_iota(jnp.int32, head.shape, 0) + (n_grad_rows - 2 * SUBLANES)
        sq = jnp.where(jnp.logical_and(rows_id >= lo, rows_id < hi), head, 0.0)
        extra_ref[nb:nb + SUBLANES, :] = jnp.zeros((SUBLANES, width), F32) + jnp.sum(sq) * loss_scale

    vm = pl.BlockSpec(memory_space=VMEM)
    return pl.pallas_call(
        body, name="small_sum",
        out_shape=(jax.ShapeDtypeStruct((n_grad_rows, width), F32), jax.ShapeDtypeStruct((nb + SUBLANES, width), F32)),
        in_specs=[vm], out_specs=(vm, vm),
    )(small_all)


def _adam_small_call(ws, gs, ms, vs):
    n = len(ws)

    def body(*refs):
        w_refs, g_refs, m_refs, v_refs = refs[:n], refs[n:2 * n], refs[2 * n:3 * n], refs[3 * n:4 * n]
        d_refs, m2_refs, v2_refs = refs[4 * n:5 * n], refs[5 * n:6 * n], refs[6 * n:7 * n]
        for i in range(n):
            delta, m2, v2 = _adamw(w_refs[i][...], g_refs[i][...], m_refs[i][...], v_refs[i][...])
            d_refs[i][...] = delta
            m2_refs[i][...] = m2
            v2_refs[i][...] = v2

    vm = pl.BlockSpec(memory_space=VMEM)
    shapes = tuple(jax.ShapeDtypeStruct(w.shape, F32) for w in ws)
    outs = pl.pallas_call(body, name="adam_small", out_shape=shapes * 3,
                          in_specs=[vm] * (4 * n), out_specs=(vm,) * (3 * n))(*ws, *gs, *ms, *vs)
    return outs[:n], outs[n:2 * n], outs[2 * n:]


def kernel(x, c, w_ada, b_ada, w_in, conf_dw_w, conf_dw_b, conf_ln_g, conf_ln_b, sc_conv_w, w_out, w_mlp1, w_mlp2, g_final, loss_target, m_w_ada, m_b_ada, m_w_in, m_conf_dw_w, m_conf_dw_b, m_conf_ln_g, m_conf_ln_b, m_sc_conv_w, m_w_out, m_w_mlp1, m_w_mlp2, m_g_final, v_w_ada, v_b_ada, v_w_in, v_conf_dw_w, v_conf_dw_b, v_conf_ln_g, v_conf_ln_b, v_sc_conv_w, v_w_out, v_w_mlp1, v_w_mlp2, v_g_final):
    bsz, seq, d_model = x.shape
    c_half = conf_dw_b.shape[-1]
    n_taps = conf_dw_w.shape[1]
    cc = conf_dw_w.shape[-1]
    a_cols = w_ada.shape[-1]
    tokens = bsz * seq
    me = _dev_index()

    c_pad = jnp.pad(c, ((0, SUBLANES - bsz), (0, 0)))
    b_ada_loc = lax.dynamic_slice(b_ada, (0, me * a_cols), (1, a_cols))
    small_loc = jnp.zeros((HALO, 128), F32)
    small_loc = small_loc.at[:n_taps, :cc].set(conf_dw_w[0]).at[:3, cc:2 * cc].set(sc_conv_w[0])
    (win_t, wout_all), small_all, c_all, mod_rows = _gather_call(
        c_pad, w_ada[0], b_ada_loc, small_loc, [w_in[0].T.astype(BF16), w_out[0].astype(BF16)])
    cw = small_all[:, :, :cc].transpose(1, 0, 2).reshape(HALO, c_half)
    scw = small_all[:, :3, cc:2 * cc].transpose(1, 0, 2).reshape(3, c_half)
    cp = jnp.concatenate([conf_dw_b, conf_ln_g, conf_ln_b, scw, jnp.zeros((2, c_half), F32)], axis=0)
    mod = mod_rows[:, :bsz, :].transpose(1, 0, 2).reshape(bsz, 6, d_model)

    flat = lambda t: t.reshape(tokens, t.shape[-1])
    (proj, a1, cv, mixed, y1), (w1_all, w2_all) = _mixer_fwd_call(
        x, mod, win_t, wout_all, cw, cp, [w_mlp1[0].astype(BF16), w_mlp2[0].astype(BF16)])
    dx1, h2, dy2, u, dz, dmod2, head, dy1 = _mlp_call(
        x, y1, loss_target, mod, w1_all, w2_all, g_final.reshape(1, d_model))
    g_out = _wgrad_call(flat(mixed), flat(dy1), "wgrad_out")
    g_w1 = _wgrad_call(flat(h2), flat(dz), "wgrad_mlp1", owner_cols=w_mlp1.shape[-1])
    g_w2, (s_w1, s_out) = _wgrad_call(flat(u), flat(dy2), "wgrad_mlp2", scatter=[g_w1, g_out])
    grad_x, dproj, h1, dmod1, cgrad = _mixer_bwd_call(dx1, x, proj, a1, cv, mod, win_t, wout_all, cw, cp)
    g_in_t, (s_w2,) = _wgrad_call(flat(dproj), flat(h1), "wgrad_in", scatter=[g_w2])

    dmod = jnp.concatenate([dmod1[:, :2, :], dmod2[:, 3:4, :], dmod2[:, :3, :]], axis=1)
    n_cg = cgrad.shape[0]
    per_b = 6 * d_model // c_half
    per_b_pad = -(-per_b // SUBLANES) * SUBLANES
    dmod_rows = jnp.pad(dmod.reshape(bsz, per_b, c_half), ((0, 0), (0, per_b_pad - per_b), (0, 0)))
    small = jnp.concatenate([
        cgrad,
        head.reshape(2 * SUBLANES, c_half),
        dmod_rows.reshape(bsz * per_b_pad, c_half),
    ], axis=0)
    s_in, gathered = _tail_scatter_call(g_in_t, small)

    n_head = n_cg + 2 * SUBLANES
    sums, extra = _small_sum_call(
        gathered, n_head, (n_cg + 2, n_cg + 4), (n_head, n_head + per_b_pad, n_head + 2 * per_b_pad), 0.5 / d_model)
    loss = extra[per_b_pad, 0]
    g_b_ada = extra[:per_b].reshape(1, 6 * d_model)
    g_dw_w = lax.dynamic_slice(sums[:n_taps], (0, me * cc), (n_taps, cc))
    g_sc_w = lax.dynamic_slice(sums[HALO:HALO + 3], (0, me * cc), (3, cc))
    g_dw_b, g_ln_g, g_ln_b = sums[HALO + 3:HALO + 4], sums[HALO + 4:HALO + 5], sums[HALO + 5:HALO + 6]
    g_gf = sums[n_cg:n_cg + 2].reshape(1, d_model)

    dmod_all = gathered[:, n_head:, :].reshape(N_DEV, bsz, per_b_pad, c_half)[:, :, :per_b, :]
    dmod_all = dmod_all.reshape(N_DEV, bsz, 6 * d_model)
    dmod_cols = lax.dynamic_slice(dmod_all, (0, 0, me * a_cols), (N_DEV, bsz, a_cols))
    dmod_cols = jnp.pad(dmod_cols, ((0, 0), (0, SUBLANES - bsz), (0, 0))).reshape(N_DEV * SUBLANES, a_cols)
    c_rows = c_all.reshape(N_DEV * SUBLANES, d_model)
    g_ada, d_ada, m_ada, v_ada = _adam_ada_call(c_rows, dmod_cols, w_ada[0], m_w_ada[0], v_w_ada[0])

    gi, di, mi, vi = _adam_slabs_call(s_in, w_in[0].T, m_w_in[0].T, v_w_in[0].T, "adam_w_in")
    gi, di, mi, vi = gi.T, di.T, mi.T, vi.T
    go, do, mo, vo = _adam_slabs_call(s_out, w_out[0], m_w_out[0], v_w_out[0], "adam_w_out")
    g1, d1, m1, v1 = _adam_slabs_call(s_w1, w_mlp1[0], m_w_mlp1[0], v_w_mlp1[0], "adam_w_mlp1")
    g2, d2, m2, v2 = _adam_slabs_call(s_w2, w_mlp2[0], m_w_mlp2[0], v_w_mlp2[0], "adam_w_mlp2")

    small_like = [b_ada, conf_dw_w, conf_dw_b, conf_ln_g, conf_ln_b, sc_conv_w, g_final]
    two_d = lambda t: t.reshape(-1, t.shape[-1])
    small_g = [g_b_ada, g_dw_w, g_dw_b, g_ln_g, g_ln_b, g_sc_w, g_gf]
    sd, sm, sv = _adam_small_call(
        [two_d(t) for t in small_like], small_g,
        [two_d(t) for t in (m_b_ada, m_conf_dw_w, m_conf_dw_b, m_conf_ln_g, m_conf_ln_b, m_sc_conv_w, m_g_final)],
        [two_d(t) for t in (v_b_ada, v_conf_dw_w, v_conf_dw_b, v_conf_ln_g, v_conf_ln_b, v_sc_conv_w, v_g_final)])
    like = lambda parts: [p.reshape(w.shape) for p, w in zip(parts, small_like)]
    sg, sd, sm, sv = like(small_g), like(sd), like(sm), like(sv)

    def ordered(ada, small_list, w_in_, w_out_, w1_, w2_):
        b_ada_, dw_w_, dw_b_, ln_g_, ln_b_, sc_w_, gf_ = small_list
        return [ada[None], b_ada_, w_in_[None], dw_w_, dw_b_, ln_g_, ln_b_, sc_w_, w_out_[None], w1_[None], w2_[None], gf_]

    grads = ordered(g_ada, sg, gi, go, g1, g2)
    deltas = ordered(d_ada, sd, di, do, d1, d2)
    new_m = ordered(m_ada, sm, mi, mo, m1, m2)
    new_v = ordered(v_ada, sv, vi, vo, v1, v2)
    return (loss, grad_x, *grads, *deltas, *new_m, *new_v)
```

```python
import jax
import jax.numpy as jnp
from jax import lax
from jax.experimental import pallas as pl
from jax.experimental.pallas import tpu as pltpu

N_DEV = 8
RMS_EPS = 1e-6
ADAM_LR = 0.001
ADAM_B1 = 0.9
ADAM_B2 = 0.999
ADAM_EPS = 1e-08
ADAM_WD = 0.01
ADAM_STEP = 10

F32 = jnp.float32
BF16 = jnp.bfloat16
MESH = pl.DeviceIdType.MESH
VMEM = pltpu.VMEM
ANY = pl.ANY

HALO = 32
SHORT_HALO = 8
ROW_CHUNK = 32
SUBLANES = 8
V7X_VMEM_LIMIT = 56 * 1024 * 1024
MLP_VMEM_LIMIT = 48 * 1024 * 1024
MLP_TOKEN_TILE = 256
MIXER_FWD_TOKEN_TILE = 512


def _coords():
    return lax.axis_index("x"), lax.axis_index("y"), lax.axis_index("c")


def _dev_index():
    x, y, c = _coords()
    return 4 * x + 2 * y + c


def _peer(k):
    x, y, c = _coords()
    px = 1 - x if (k >> 2) & 1 else x
    py = 1 - y if (k >> 1) & 1 else y
    pc = 1 - c if k & 1 else c
    return (px, py, pc), 4 * px + 2 * py + pc


def _sigmoid(v):
    return jax.nn.sigmoid(v)


def _nt_dot(a, b):
    return lax.dot_general(a, b, (((1,), (1,)), ((), ())), preferred_element_type=F32)


def _nn_dot(a, b):
    return jnp.dot(a, b, preferred_element_type=F32)


def _tn_dot(a, b):
    return lax.dot_general(a, b, (((0,), (0,)), ((), ())), preferred_element_type=F32)


def _token_tile(seq):
    return 256 if seq % 256 == 0 else 64


GATHER_SEM_COLUMNS = 9
HALF_ROW_ALIGN = 16


class _TwoLevelGather:
    def __init__(self, srcs, dsts, send_sems, recv_sems, loc_sems):
        x, y, c = _coords()
        me = 4 * x + 2 * y + c
        sibling, along_x, along_y = (x, y, 1 - c), (1 - x, y, c), (x, 1 - y, c)
        from_x, from_y, diagonal = 4 * (1 - x) + 2 * y + c, 4 * x + 2 * (1 - y) + c, 4 * (1 - x) + 2 * (1 - y) + c

        def remote(src, dst, a, col, to):
            return pltpu.make_async_remote_copy(
                src_ref=src, dst_ref=dst, send_sem=send_sems.at[a, col], recv_sem=recv_sems.at[a, col],
                device_id=to, device_id_type=MESH)

        def onward(block, a, col, to):
            return remote(block, block, a, col, to)

        self.first, self.arrivals, self.second, self.halves, self.third = [], [], [], [], []
        for a, (src, dst) in enumerate(zip(srcs, dsts)):
            half = src.shape[0] // (2 * HALF_ROW_ALIGN) * HALF_ROW_ALIGN
            lower, upper = pl.ds(0, half), pl.ds(half, src.shape[0] - half)
            neighbours = [remote(src, dst.at[me], a, 1, along_x), remote(src, dst.at[me], a, 2, along_y)]
            self.first += [pltpu.make_async_copy(src, dst.at[me], loc_sems.at[a]),
                           remote(src, dst.at[me], a, 0, sibling)]
            self.arrivals += neighbours
            halves = [onward(dst.at[from_x, lower], a, 3, along_y), onward(dst.at[from_y, upper], a, 4, along_x)]
            self.halves += halves
            self.second += [[halves[0], onward(dst.at[from_x], a, 5, sibling)],
                            [halves[1], onward(dst.at[from_y], a, 6, sibling)]]
            self.third += [onward(dst.at[diagonal, lower], a, 7, sibling),
                           onward(dst.at[diagonal, upper], a, 8, sibling)]

    def start(self):
        for cp in self.first + self.arrivals:
            cp.start()

    def forward_arrivals(self):
        for arrival, sends in zip(self.arrivals, self.second):
            arrival.wait_recv()
            for cp in sends:
                cp.start()

    def forward_diagonal(self):
        for half, to_sibling in zip(self.halves, self.third):
            half.wait_recv()
            to_sibling.start()

    def finish(self):
        for cp in self.arrivals + self.halves:
            cp.wait_send()
        for cp in self.first + [sends[1] for sends in self.second] + self.third:
            cp.wait()


N_CHIPS = N_DEV // 2
PAIR_SUM_ROWS = 256


class _TwoLevelScatter:
    def __init__(self, partials, sums, mine, from_sibling, pair, first_send, first_recv, second_send, second_recv,
                 fetch_sems, local_sems):
        x, y, c = _coords()
        sibling = (x, y, 1 - c)
        chips = [(1 - x, y), (x, 1 - y), (1 - x, 1 - y)]
        self.my_chip = 2 * x + y
        self.mine, self.from_sibling, self.pair = mine, from_sibling, pair

        def remote(src, dst, send_sem, recv_sem, to):
            return pltpu.make_async_remote_copy(src_ref=src, dst_ref=dst, send_sem=send_sem, recv_sem=recv_sem,
                                                device_id=to, device_id_type=MESH)

        self.first, self.fetch, self.second, self.local = [], [], [], []
        for a in range(len(partials)):
            self.local.append(pltpu.make_async_copy(pair[a].at[self.my_chip], sums[a].at[self.my_chip],
                                                    local_sems.at[a]))
            self.first += [remote(partials[a].at[2 * q + (1 - c)], from_sibling[a].at[q],
                                  first_send.at[a, q], first_recv.at[a, q], sibling) for q in range(N_CHIPS)]
            self.fetch += [pltpu.make_async_copy(partials[a].at[2 * q + c], mine[a].at[q], fetch_sems.at[a, q])
                           for q in range(N_CHIPS)]
            self.second += [remote(pair[a].at[2 * cx + cy], sums[a].at[self.my_chip],
                                   second_send.at[a, j], second_recv.at[a, j], (cx, cy, c))
                            for j, (cx, cy) in enumerate(chips)]

    def start(self):
        for cp in self.first + self.fetch:
            cp.start()

    def pair_sums(self):
        for cp in self.first + self.fetch:
            cp.wait()
        for mine, theirs, both in zip(self.mine, self.from_sibling, self.pair):
            rows = mine.shape[1]
            for q in range(N_CHIPS):
                for r0 in range(0, rows, PAIR_SUM_ROWS):
                    part = pl.ds(r0, min(PAIR_SUM_ROWS, rows - r0))
                    both[q, part, :] = (mine[q, part, :].astype(F32) + theirs[q, part, :].astype(F32)).astype(BF16)
        for cp in self.second + self.local:
            cp.start()

    def finish(self):
        for cp in self.second + self.local:
            cp.wait()


def _scatter_scratch(partials):
    n = len(partials)
    zones = [pltpu.VMEM((N_CHIPS,) + p.shape[1:], BF16) for p in partials]
    return 3 * zones + [pltpu.SemaphoreType.DMA((n, N_CHIPS)), pltpu.SemaphoreType.DMA((n, N_CHIPS)),
                        pltpu.SemaphoreType.DMA((n, N_CHIPS - 1)), pltpu.SemaphoreType.DMA((n, N_CHIPS - 1)),
                        pltpu.SemaphoreType.DMA((n, N_CHIPS)), pltpu.SemaphoreType.DMA((n,))]


def _exchange_sems(n_arrays, columns=GATHER_SEM_COLUMNS):
    return [pltpu.SemaphoreType.DMA((n_arrays, columns)), pltpu.SemaphoreType.DMA((n_arrays, columns)),
            pltpu.SemaphoreType.DMA((n_arrays,))]


def _gather_call(c_pad, w_ada, b_ada_loc, small_loc, big_shards):
    n_big = len(big_shards)
    d_model = c_pad.shape[1]
    a_cols = w_ada.shape[1]

    def body(c_ref, wada_ref, bada_ref, small_ref, *rest):
        big_in = rest[:n_big]
        big_out = rest[n_big:2 * n_big]
        small_all, c_all, mod_rows = rest[2 * n_big:2 * n_big + 3]
        modcols, big_send, big_recv, loc_sem, s_send, s_recv = rest[2 * n_big + 3:]
        me = _dev_index()
        big = _TwoLevelGather(big_in, big_out, big_send, big_recv, loc_sem)
        big.start()

        small_all[me] = small_ref[...]
        c_all[me] = c_ref[...]
        first = []
        for k in range(1, N_DEV):
            peer, _ = _peer(k)
            for i, (src, dst) in enumerate(((small_ref, small_all), (c_ref, c_all))):
                cp = pltpu.make_async_remote_copy(
                    src_ref=src, dst_ref=dst.at[me],
                    send_sem=s_send.at[i, k - 1], recv_sem=s_recv.at[i, k - 1],
                    device_id=peer, device_id_type=MESH)
                cp.start()
                first.append(cp)
        for cp in first:
            cp.wait()

        c_rows = c_all[...].reshape(N_DEV * SUBLANES, d_model)
        c_act = c_rows * _sigmoid(c_rows)
        modcols[...] = _nn_dot(c_act.astype(BF16), wada_ref[...].astype(BF16)) + bada_ref[...]
        mod_rows[me] = modcols[pl.ds(pl.multiple_of(me * SUBLANES, SUBLANES), SUBLANES), :]
        second = []
        for k in range(1, N_DEV):
            peer, pidx = _peer(k)
            cp = pltpu.make_async_remote_copy(
                src_ref=modcols.at[pl.ds(pl.multiple_of(pidx * SUBLANES, SUBLANES), SUBLANES), :],
                dst_ref=mod_rows.at[me],
                send_sem=s_send.at[2, k - 1], recv_sem=s_recv.at[2, k - 1],
                device_id=peer, device_id_type=MESH)
            cp.start()
            second.append(cp)
        big.forward_arrivals()
        for cp in second:
            cp.wait()
        big.forward_diagonal()
        big.finish()

    out_shape = tuple(jax.ShapeDtypeStruct((N_DEV,) + s.shape, s.dtype) for s in big_shards) + (
        jax.ShapeDtypeStruct((N_DEV,) + small_loc.shape, F32),
        jax.ShapeDtypeStruct((N_DEV, SUBLANES, d_model), F32),
        jax.ShapeDtypeStruct((N_DEV, SUBLANES, a_cols), F32),
    )
    vm = pl.BlockSpec(memory_space=VMEM)
    hbm = pl.BlockSpec(memory_space=ANY)
    outs = pl.pallas_call(
        body, name="gather_weights_mod", out_shape=out_shape,
        in_specs=[vm, vm, vm, vm] + [hbm] * n_big,
        out_specs=tuple([hbm] * n_big + [vm, vm, vm]),
        scratch_shapes=[
            pltpu.VMEM((N_DEV * SUBLANES, a_cols), F32),
            *_exchange_sems(n_big),
            pltpu.SemaphoreType.DMA((3, N_DEV - 1)),
            pltpu.SemaphoreType.DMA((3, N_DEV - 1)),
        ],
        compiler_params=pltpu.CompilerParams(vmem_limit_bytes=V7X_VMEM_LIMIT),
    )(c_pad, w_ada, b_ada_loc, small_loc, *big_shards)
    return outs[:n_big], outs[n_big], outs[n_big + 1], outs[n_big + 2]


def _shifted_rows_count(tm):
    return tm + HALO - SUBLANES


def _fill_shifted(ext, shifted, tm):
    for s in range(1, SUBLANES):
        shifted[s - 1] = ext[s:s + _shifted_rows_count(tm), :]


def _shifted_rows(ext, shifted, start, rows):
    phase = start % SUBLANES
    aligned = start - phase
    if phase == 0:
        return ext[aligned:aligned + rows, :]
    return shifted[phase - 1, aligned:aligned + rows, :]


def _layer_norm_parts(a1):
    mu = jnp.mean(a1, axis=-1, keepdims=True)
    xc = a1 - mu
    rstd = lax.rsqrt(jnp.mean(xc * xc, axis=-1, keepdims=True) + RMS_EPS)
    return xc * rstd, rstd


def _mixer_fwd_call(x, mod, win_t, wout, cw, cp, later_shards):
    n_later = len(later_shards)
    bsz, seq, d_model = x.shape
    c_half = cw.shape[1]
    n_taps = 31
    d_in = win_t.shape[0] * win_t.shape[1]
    tm = MIXER_FWD_TOKEN_TILE if seq % MIXER_FWD_TOKEN_TILE == 0 else _token_tile(seq)
    nt = seq // tm
    arrivals_step, diagonal_step = (7 * bsz * nt) // 16, (11 * bsz * nt) // 16

    def body(x_ref, mod_ref, win_ref, wout_ref, cw_ref, cp_ref, *rest):
        shard_refs, rest = rest[:n_later], rest[n_later:]
        proj_ref, a1_ref, cv_ref, mixed_ref, y1_ref = rest[:5]
        gathered_refs, rest = rest[5:5 + n_later], rest[5 + n_later:]
        aext, qext, ashift, send_sems, recv_sems, loc_sems = rest
        b, t = pl.program_id(0), pl.program_id(1)

        step = b * nt + t

        @pl.when(step == 0)
        def _():
            _TwoLevelGather(shard_refs, gathered_refs, send_sems, recv_sems, loc_sems).start()

        @pl.when(step == arrivals_step)
        def _():
            _TwoLevelGather(shard_refs, gathered_refs, send_sems, recv_sems, loc_sems).forward_arrivals()

        @pl.when(step == diagonal_step)
        def _():
            _TwoLevelGather(shard_refs, gathered_refs, send_sems, recv_sems, loc_sems).forward_diagonal()

        xv = x_ref[...]
        sh1, sc1 = mod_ref[0:1, :], mod_ref[1:2, :]
        r1 = lax.rsqrt(jnp.mean(xv * xv, axis=-1, keepdims=True) + RMS_EPS)
        h1 = (xv * r1) * (1.0 + sc1) + sh1
        proj = _nt_dot(h1.astype(BF16), win_ref[...].reshape(d_in, d_model))
        proj_ref[...] = proj
        val, gate = proj[:, 0:c_half], proj[:, c_half:2 * c_half]
        s_b, s_c, s_h = proj[:, 2 * c_half:3 * c_half], proj[:, 3 * c_half:4 * c_half], proj[:, 4 * c_half:5 * c_half]

        @pl.when(t == 0)
        def _():
            aext[0:HALO, :] = jnp.zeros((HALO, c_half), F32)
            qext[0:SHORT_HALO, :] = jnp.zeros((SHORT_HALO, c_half), F32)

        @pl.when(t > 0)
        def _():
            aext[0:HALO, :] = aext[tm:tm + HALO, :]
            qext[0:SHORT_HALO, :] = qext[tm:tm + SHORT_HALO, :]
        aext[HALO:HALO + tm, :] = val * _sigmoid(gate)
        qext[SHORT_HALO:SHORT_HALO + tm, :] = s_c * s_h

        base = HALO - (n_taps - 1)
        _fill_shifted(aext, ashift, tm)
        for r0 in range(0, tm, ROW_CHUNK):
            acc = jnp.zeros((ROW_CHUNK, c_half), F32)
            for k in range(n_taps):
                acc = acc + cw_ref[k:k + 1, :] * _shifted_rows(aext, ashift, r0 + base + k, ROW_CHUNK)
            a1_ref[r0:r0 + ROW_CHUNK, :] = acc + cp_ref[0:1, :]
        sbase = SHORT_HALO - 2
        conv3 = cp_ref[3:4, :] * qext[sbase:sbase + tm, :]
        conv3 = conv3 + cp_ref[4:5, :] * qext[sbase + 1:sbase + 1 + tm, :]
        conv3 = conv3 + cp_ref[5:6, :] * qext[sbase + 2:sbase + 2 + tm, :]
        cv_ref[...] = conv3

        norm, _ = _layer_norm_parts(a1_ref[...])
        a2 = norm * cp_ref[1:2, :] + cp_ref[2:3, :]
        mixed = jnp.concatenate([a2 * _sigmoid(a2), s_b * conv3], axis=-1).astype(BF16)
        mixed_ref[...] = mixed
        y1 = _nn_dot(mixed, wout_ref[...].reshape(d_model, d_model))
        y1_ref[...] = y1

        @pl.when(step == bsz * nt - 1)
        def _():
            _TwoLevelGather(shard_refs, gathered_refs, send_sems, recv_sems, loc_sems).finish()

    hbm = pl.BlockSpec(memory_space=ANY)

    def tok(width):
        return pl.BlockSpec((None, tm, width), lambda b, t: (b, t, 0))

    def const(shape):
        return pl.BlockSpec(shape, lambda b, t: (0,) * len(shape))

    def resident(shape):
        return pl.BlockSpec(shape, lambda b, t: (0,) * len(shape), pipeline_mode=pl.Buffered(1))

    out_shape = (
        jax.ShapeDtypeStruct((bsz, seq, d_in), F32),
        jax.ShapeDtypeStruct((bsz, seq, c_half), F32),
        jax.ShapeDtypeStruct((bsz, seq, c_half), F32),
        jax.ShapeDtypeStruct((bsz, seq, d_model), BF16),
        jax.ShapeDtypeStruct((bsz, seq, d_model), F32),
    ) + tuple(jax.ShapeDtypeStruct((N_DEV,) + s.shape, s.dtype) for s in later_shards)
    outs = pl.pallas_call(
        body, name="mixer_fwd", out_shape=out_shape, grid=(bsz, nt),
        in_specs=[tok(d_model), pl.BlockSpec((None, 6, d_model), lambda b, t: (b, 0, 0)),
                  resident(win_t.shape), resident(wout.shape), const(cw.shape), const(cp.shape)] + [hbm] * n_later,
        out_specs=(tok(d_in), tok(c_half), tok(c_half), tok(d_model), tok(d_model)) + (hbm,) * n_later,
        scratch_shapes=[pltpu.VMEM((tm + HALO, c_half), F32), pltpu.VMEM((tm + SHORT_HALO, c_half), F32),
                        pltpu.VMEM((SUBLANES - 1, _shifted_rows_count(tm), c_half), F32)]
        + _exchange_sems(n_later),
        compiler_params=pltpu.CompilerParams(
            dimension_semantics=("arbitrary", "arbitrary"), vmem_limit_bytes=V7X_VMEM_LIMIT),
    )(x, mod, win_t, wout, cw, cp, *later_shards)
    return outs[:5], outs[5:]


def _mlp_call(x, y1, target, mod, w1, w2, g_final):
    bsz, seq, d_model = x.shape
    n_blk, _, f_blk = w1.shape
    d_ff = n_blk * f_blk
    tm = MLP_TOKEN_TILE if seq % MLP_TOKEN_TILE == 0 else _token_tile(seq)
    nt = seq // tm

    def body(x_ref, y1_ref, tgt_ref, mod_ref, w1_ref, w2_ref, gf_ref,
             dx1_ref, h2_ref, dy2_ref, u_ref, dz_ref, dmod_ref, head_ref, dy1_ref, relu_scr):
        b, t = pl.program_id(0), pl.program_id(1)
        x1v = x_ref[...] + mod_ref[2:3, :] * y1_ref[...]
        sh2, sc2, g2 = mod_ref[3:4, :], mod_ref[4:5, :], mod_ref[5:6, :]
        gf = gf_ref[...]
        r2 = lax.rsqrt(jnp.mean(x1v * x1v, axis=-1, keepdims=True) + RMS_EPS)
        xn2 = x1v * r2
        h2 = (xn2 * (1.0 + sc2) + sh2).astype(BF16)
        h2_ref[...] = h2
        for j in range(n_blk):
            cols = slice(j * f_blk, (j + 1) * f_blk)
            rz = jnp.maximum(_nn_dot(h2, w1_ref[j]), 0.0)
            relu_scr[:, cols] = rz
            u_ref[:, cols] = (rz * rz).astype(BF16)
        y2 = _nn_dot(u_ref[...], w2_ref[...].reshape(d_ff, d_model))
        x2 = x1v + g2 * y2
        r3 = lax.rsqrt(jnp.mean(x2 * x2, axis=-1, keepdims=True) + RMS_EPS)
        xn3 = x2 * r3
        diff = xn3 * gf - tgt_ref[...]
        dout = diff * (1.0 / d_model)

        @pl.when(jnp.logical_and(b == 0, t == 0))
        def _():
            head_ref[...] = jnp.zeros(head_ref.shape, F32)

        @pl.when(t == 0)
        def _():
            dmod_ref[...] = jnp.zeros(dmod_ref.shape, F32)

        head_ref[0:1, :] += jnp.sum(dout * xn3, axis=0, keepdims=True)
        head_ref[1:2, :] += jnp.sum(diff * diff, axis=0, keepdims=True)
        dxn3 = dout * gf
        dx2 = r3 * (dxn3 - xn3 * jnp.mean(dxn3 * xn3, axis=-1, keepdims=True))
        dmod_ref[2:3, :] += jnp.sum(dx2 * y2, axis=0, keepdims=True)
        dy2 = (g2 * dx2).astype(BF16)
        dy2_ref[...] = dy2
        dh2 = jnp.zeros((tm, d_model), F32)
        for j in range(n_blk):
            cols = slice(j * f_blk, (j + 1) * f_blk)
            dz = (_nt_dot(dy2, w2_ref[j]) * (2.0 * relu_scr[:, cols])).astype(BF16)
            dz_ref[:, cols] = dz
            dh2 = dh2 + _nt_dot(dz, w1_ref[j])
        dmod_ref[0:1, :] += jnp.sum(dh2, axis=0, keepdims=True)
        dmod_ref[1:2, :] += jnp.sum(dh2 * xn2, axis=0, keepdims=True)
        dxn2 = dh2 * (1.0 + sc2)
        dx1 = dx2 + r2 * (dxn2 - xn2 * jnp.mean(dxn2 * xn2, axis=-1, keepdims=True))
        dx1_ref[...] = dx1
        dy1_ref[...] = (mod_ref[2:3, :] * dx1).astype(BF16)
        dmod_ref[3:4, :] += jnp.sum(dx1 * y1_ref[...], axis=0, keepdims=True)

    def tok(width):
        return pl.BlockSpec((None, tm, width), lambda b, t: (b, t, 0))

    def const(shape):
        return pl.BlockSpec(shape, lambda b, t: (0,) * len(shape))

    def resident(shape):
        return pl.BlockSpec(shape, lambda b, t: (0,) * len(shape), pipeline_mode=pl.Buffered(1))

    out_shape = (
        jax.ShapeDtypeStruct((bsz, seq, d_model), F32),
        jax.ShapeDtypeStruct((bsz, seq, d_model), BF16),
        jax.ShapeDtypeStruct((bsz, seq, d_model), BF16),
        jax.ShapeDtypeStruct((bsz, seq, d_ff), BF16),
        jax.ShapeDtypeStruct((bsz, seq, d_ff), BF16),
        jax.ShapeDtypeStruct((bsz, SUBLANES, d_model), F32),
        jax.ShapeDtypeStruct((SUBLANES, d_model), F32),
        jax.ShapeDtypeStruct((bsz, seq, d_model), BF16),
    )
    return pl.pallas_call(
        body, name="mlp_fwd_bwd", out_shape=out_shape, grid=(bsz, nt),
        in_specs=[tok(d_model), tok(d_model), tok(d_model),
                  pl.BlockSpec((None, 6, d_model), lambda b, t: (b, 0, 0)),
                  resident(w1.shape), resident(w2.shape), const(g_final.shape)],
        out_specs=(tok(d_model), tok(d_model), tok(d_model), tok(d_ff), tok(d_ff),
                   pl.BlockSpec((None, SUBLANES, d_model), lambda b, t: (b, 0, 0)),
                   const((SUBLANES, d_model)), tok(d_model)),
        scratch_shapes=[pltpu.VMEM((tm, d_ff), F32)],
        compiler_params=pltpu.CompilerParams(
            dimension_semantics=("arbitrary", "arbitrary"), vmem_limit_bytes=MLP_VMEM_LIMIT),
    )(x, y1, target, mod, w1, w2, g_final)


def _mixer_bwd_call(dx1, x, proj, a1, cv, mod, win_t, wout, cw, cp):
    bsz, seq, d_model = x.shape
    c_half = cw.shape[1]
    n_taps = 31
    d_in = win_t.shape[0] * win_t.shape[1]
    tm = _token_tile(seq)
    nt = seq // tm

    def body(dx1_ref, x_ref, proj_ref, a1_ref, cv_ref, mod_ref, win_ref, wout_ref, cw_ref, cp_ref,
             gx_ref, dproj_ref, h1_ref, dmod_ref, cgrad_ref,
             dext, cext, a0_scr, da0_scr, tap_acc, row_acc, dshift):
        b, step = pl.program_id(0), pl.program_id(1)
        first = jnp.logical_and(b == 0, step == 0)
        last = jnp.logical_and(b == bsz - 1, step == nt - 1)
        dx1v = dx1_ref[...]
        xv = x_ref[...]
        sh1, sc1, g1 = mod_ref[0:1, :], mod_ref[1:2, :], mod_ref[2:3, :]

        @pl.when(first)
        def _():
            tap_acc[...] = jnp.zeros(tap_acc.shape, F32)
            row_acc[...] = jnp.zeros(row_acc.shape, F32)

        @pl.when(step == 0)
        def _():
            dmod_ref[...] = jnp.zeros(dmod_ref.shape, F32)
            dext[tm:tm + HALO, :] = jnp.zeros((HALO, c_half), F32)
            cext[tm:tm + SHORT_HALO, :] = jnp.zeros((SHORT_HALO, c_half), F32)

        @pl.when(step > 0)
        def _():
            dext[tm:tm + HALO, :] = dext[0:HALO, :]
            cext[tm:tm + SHORT_HALO, :] = cext[0:SHORT_HALO, :]

        dy1 = (g1 * dx1v).astype(BF16)
        dmixed = _nt_dot(dy1, wout_ref[...].reshape(d_model, d_model))
        d_a, d_s = dmixed[:, 0:c_half], dmixed[:, c_half:2 * c_half]

        val, gate = proj_ref[:, 0:c_half], proj_ref[:, c_half:2 * c_half]
        s_b = proj_ref[:, 2 * c_half:3 * c_half]
        s_c, s_h = proj_ref[:, 3 * c_half:4 * c_half], proj_ref[:, 4 * c_half:5 * c_half]

        d_sb = d_s * cv_ref[...]
        cext[0:tm, :] = d_s * s_b
        q = s_c * s_h
        dq = jnp.zeros((tm, c_half), F32)
        for k in range(3):
            shifted = cext[2 - k:2 - k + tm, :]
            dq = dq + cp_ref[3 + k:4 + k, :] * shifted
            row_acc[k:k + 1, :] += jnp.sum(q * shifted, axis=0, keepdims=True)
        d_sc, d_sh = dq * s_h, dq * s_c

        norm, rstd = _layer_norm_parts(a1_ref[...])
        ln_g = cp_ref[1:2, :]
        a2 = norm * ln_g + cp_ref[2:3, :]
        sg = _sigmoid(a2)
        d_a2 = d_a * (sg * (1.0 + a2 * (1.0 - sg)))
        row_acc[4:5, :] += jnp.sum(d_a2 * norm, axis=0, keepdims=True)
        row_acc[5:6, :] += jnp.sum(d_a2, axis=0, keepdims=True)
        d_n = d_a2 * ln_g
        d_a1 = rstd * (d_n - jnp.mean(d_n, axis=-1, keepdims=True)
                       - norm * jnp.mean(d_n * norm, axis=-1, keepdims=True))
        row_acc[3:4, :] += jnp.sum(d_a1, axis=0, keepdims=True)
        dext[0:tm, :] = d_a1
        sig_g = _sigmoid(gate)
        a0_scr[...] = val * sig_g

        _fill_shifted(dext, dshift, tm)
        for r0 in range(0, tm, ROW_CHUNK):
            a0c = a0_scr[r0:r0 + ROW_CHUNK, :]
            acc = jnp.zeros((ROW_CHUNK, c_half), F32)
            for k in range(n_taps):
                shifted = _shifted_rows(dext, dshift, r0 + (n_taps - 1) - k, ROW_CHUNK)
                acc = acc + cw_ref[k:k + 1, :] * shifted
                prod = a0c * shifted
                part = prod[0:SUBLANES, :]
                for g in range(1, ROW_CHUNK // SUBLANES):
                    part = part + prod[g * SUBLANES:(g + 1) * SUBLANES, :]
                tap_acc[k * SUBLANES:(k + 1) * SUBLANES, :] += part
            da0_scr[r0:r0 + ROW_CHUNK, :] = acc
        d_a0 = da0_scr[...]
        d_val = d_a0 * sig_g
        d_gate = d_a0 * val * sig_g * (1.0 - sig_g)

        dproj = jnp.concatenate([d_val, d_gate, d_sb, d_sc, d_sh], axis=-1).astype(BF16)
        dproj_ref[...] = dproj
        dh1 = _nn_dot(dproj, win_ref[...].reshape(d_in, d_model))
        r1 = lax.rsqrt(jnp.mean(xv * xv, axis=-1, keepdims=True) + RMS_EPS)
        xn1 = xv * r1
        h1_ref[...] = (xn1 * (1.0 + sc1) + sh1).astype(BF16)
        dmod_ref[0:1, :] += jnp.sum(dh1, axis=0, keepdims=True)
        dmod_ref[1:2, :] += jnp.sum(dh1 * xn1, axis=0, keepdims=True)
        dxn1 = dh1 * (1.0 + sc1)
        gx_ref[...] = dx1v + r1 * (dxn1 - xn1 * jnp.mean(dxn1 * xn1, axis=-1, keepdims=True))

        @pl.when(last)
        def _():
            taps = jnp.sum(tap_acc[...].reshape(HALO, SUBLANES, c_half), axis=1)
            cgrad_ref[0:HALO, :] = taps
            cgrad_ref[HALO:HALO + SUBLANES, :] = row_acc[...]

    def tok(width):
        return pl.BlockSpec((None, tm, width), lambda b, s: (b, nt - 1 - s, 0))

    def const(shape):
        return pl.BlockSpec(shape, lambda b, s: (0,) * len(shape))

    mod_spec = pl.BlockSpec((None, 6, d_model), lambda b, s: (b, 0, 0))
    out_shape = (
        jax.ShapeDtypeStruct((bsz, seq, d_model), F32),
        jax.ShapeDtypeStruct((bsz, seq, d_in), BF16),
        jax.ShapeDtypeStruct((bsz, seq, d_model), BF16),
        jax.ShapeDtypeStruct((bsz, SUBLANES, d_model), F32),
        jax.ShapeDtypeStruct((HALO + SUBLANES, c_half), F32),
    )
    return pl.pallas_call(
        body, name="mixer_bwd", out_shape=out_shape, grid=(bsz, nt),
        in_specs=[tok(d_model), tok(d_model), tok(d_in), tok(c_half), tok(c_half), mod_spec,
                  const(win_t.shape), const(wout.shape), const(cw.shape), const(cp.shape)],
        out_specs=(tok(d_model), tok(d_in), tok(d_model),
                   pl.BlockSpec((None, SUBLANES, d_model), lambda b, s: (b, 0, 0)),
                   const((HALO + SUBLANES, c_half))),
        scratch_shapes=[
            pltpu.VMEM((tm + HALO, c_half), F32), pltpu.VMEM((tm + SHORT_HALO, c_half), F32),
            pltpu.VMEM((tm, c_half), F32), pltpu.VMEM((tm, c_half), F32),
            pltpu.VMEM((HALO * SUBLANES, c_half), F32), pltpu.VMEM((SUBLANES, c_half), F32),
            pltpu.VMEM((SUBLANES - 1, _shifted_rows_count(tm), c_half), F32),
        ],
        compiler_params=pltpu.CompilerParams(
            dimension_semantics=("arbitrary", "arbitrary"), vmem_limit_bytes=V7X_VMEM_LIMIT),
    )(dx1, x, proj, a1, cv, mod, win_t, wout, cw, cp)


def _largest_divisor(n, cap, multiple):
    best = None
    for cand in range(multiple, min(n, cap) + 1, multiple):
        if n % cand == 0:
            best = cand
    return best if best is not None else n


WGRAD_TOKENS_PER_STEP = 2048
WGRAD_COLS_PER_STEP = 1024


def _wgrad_call(a, b, name, owner_cols=None, scatter=()):
    n_sc = len(scatter)
    tokens, m_dim = a.shape
    n_dim = b.shape[1]
    bk = _largest_divisor(tokens, WGRAD_TOKENS_PER_STEP, 128)
    n_k = tokens // bk
    if owner_cols is None:
        bm = _largest_divisor(m_dim, 1024, m_dim // N_DEV)
        bn = n_dim
        owners = 1
        out_shape = jax.ShapeDtypeStruct((m_dim, n_dim), BF16)
        out_spec = pl.BlockSpec((bm, bn), lambda i, j, k: (i, j))
    else:
        bm = m_dim
        bn = _largest_divisor(n_dim, WGRAD_COLS_PER_STEP, owner_cols)
        owners = bn // owner_cols
        out_shape = jax.ShapeDtypeStruct((n_dim // owner_cols, m_dim, owner_cols), BF16)
        out_spec = pl.BlockSpec((owners, bm, owner_cols), lambda i, j, k: (j, i, 0))

    grid = (m_dim // bm, n_dim // bn, n_k)
    n_steps = grid[0] * grid[1] * grid[2]
    pair_step = min(max(1, n_steps // 5), n_steps - 1)

    def body(a_ref, b_ref, *rest):
        part_refs, o_ref, sum_refs, acc = rest[:n_sc], rest[n_sc], rest[n_sc + 1:2 * n_sc + 1], rest[2 * n_sc + 1]
        k = pl.program_id(2)
        step = (pl.program_id(0) * grid[1] + pl.program_id(1)) * n_k + k

        def exchange():
            extra = rest[2 * n_sc + 2:]
            return _TwoLevelScatter(part_refs, sum_refs, extra[:n_sc], extra[n_sc:2 * n_sc], extra[2 * n_sc:3 * n_sc],
                                    *extra[3 * n_sc:])

        if n_sc:
            @pl.when(step == 0)
            def _():
                exchange().start()

            @pl.when(step == pair_step)
            def _():
                exchange().pair_sums()

        @pl.when(k == 0)
        def _():
            acc[...] = jnp.zeros(acc.shape, F32)

        acc[...] += _tn_dot(a_ref[...], b_ref[...])

        @pl.when(k == n_k - 1)
        def _():
            if owner_cols is None:
                o_ref[...] = acc[...].astype(BF16)
            else:
                for q in range(owners):
                    o_ref[q] = acc[:, q * owner_cols:(q + 1) * owner_cols].astype(BF16)

        if n_sc:
            @pl.when(step == n_steps - 1)
            def _():
                exchange().finish()

    hbm = pl.BlockSpec(memory_space=ANY)
    sum_shapes = tuple(jax.ShapeDtypeStruct((N_CHIPS,) + p.shape[1:], BF16) for p in scatter)
    outs = pl.pallas_call(
        body, name=name, out_shape=(out_shape,) + sum_shapes, grid=grid,
        in_specs=[pl.BlockSpec((bk, bm), lambda i, j, k: (k, i)), pl.BlockSpec((bk, bn), lambda i, j, k: (k, j))]
        + [hbm] * n_sc,
        out_specs=(out_spec,) + (hbm,) * n_sc,
        scratch_shapes=[pltpu.VMEM((bm, bn), F32)] + (_scatter_scratch(scatter) if n_sc else []),
        compiler_params=pltpu.CompilerParams(
            dimension_semantics=("arbitrary", "arbitrary", "arbitrary"), vmem_limit_bytes=V7X_VMEM_LIMIT),
    )(a, b, *scatter)
    out = outs[0]
    if owner_cols is None:
        out = out.reshape(N_DEV, m_dim // N_DEV, n_dim)
    return (out, list(outs[1:])) if n_sc else out


def _tail_scatter_call(partial, small):
    def body(g_ref, small_ref, out_ref, small_all, mine, from_sibling, pair,
             p1_send, p1_recv, p2_send, p2_recv, p_fetch, p_loc, s_send, s_recv, s_loc):
        gather = _TwoLevelGather([small_ref], [small_all], s_send, s_recv, s_loc)
        scatter = _TwoLevelScatter([g_ref], [out_ref], [mine], [from_sibling], [pair],
                                   p1_send, p1_recv, p2_send, p2_recv, p_fetch, p_loc)
        gather.start()
        scatter.start()
        scatter.pair_sums()
        gather.forward_arrivals()
        scatter.finish()
        gather.forward_diagonal()
        gather.finish()

    vm = pl.BlockSpec(memory_space=VMEM)
    return pl.pallas_call(
        body, name="scatter_tail",
        out_shape=(jax.ShapeDtypeStruct((N_CHIPS,) + partial.shape[1:], BF16),
                   jax.ShapeDtypeStruct((N_DEV,) + small.shape, F32)),
        in_specs=[pl.BlockSpec(memory_space=ANY), vm], out_specs=(vm, vm),
        scratch_shapes=_scatter_scratch([partial]) + _exchange_sems(1),
        compiler_params=pltpu.CompilerParams(vmem_limit_bytes=V7X_VMEM_LIMIT),
    )(partial, small)


def _adamw(w, g, m, v):
    m2 = ADAM_B1 * m + (1.0 - ADAM_B1) * g
    v2 = ADAM_B2 * v + (1.0 - ADAM_B2) * (g * g)
    m_hat = m2 / (1.0 - ADAM_B1 ** ADAM_STEP)
    v_hat = v2 / (1.0 - ADAM_B2 ** ADAM_STEP)
    delta = -ADAM_LR * (m_hat / (jnp.sqrt(v_hat) + ADAM_EPS) + ADAM_WD * w)
    return delta, m2, v2


def _adam_slabs_call(slabs, w, m, v, name):
    rows, cols = w.shape
    n_slabs = slabs.shape[0]
    tr = _largest_divisor(rows, 256, 2 * SUBLANES)

    def body(s_ref, w_ref, m_ref, v_ref, g_ref, d_ref, m2_ref, v2_ref):
        g = s_ref[0].astype(F32)
        for k in range(1, n_slabs):
            g = g + s_ref[k].astype(F32)
        delta, m2, v2 = _adamw(w_ref[...], g, m_ref[...], v_ref[...])
        g_ref[...] = g
        d_ref[...] = delta
        m2_ref[...] = m2
        v2_ref[...] = v2

    tile = pl.BlockSpec((tr, cols), lambda i: (i, 0))
    shp = jax.ShapeDtypeStruct((rows, cols), F32)
    return pl.pallas_call(
        body, name=name, out_shape=(shp, shp, shp, shp), grid=(rows // tr,),
        in_specs=[pl.BlockSpec((n_slabs, tr, cols), lambda i: (0, i, 0)), tile, tile, tile],
        out_specs=(tile, tile, tile, tile),
        compiler_params=pltpu.CompilerParams(dimension_semantics=("arbitrary",), vmem_limit_bytes=V7X_VMEM_LIMIT),
    )(slabs, w, m, v)


def _adam_ada_call(c_rows, dmod_cols, w, m, v):
    rows, cols = w.shape
    n_rows = c_rows.shape[0]
    tr = _largest_divisor(rows, 256, 128)

    def body(c_ref, dm_ref, w_ref, m_ref, v_ref, g_ref, d_ref, m2_ref, v2_ref):
        cv = c_ref[...]
        c_act = (cv * _sigmoid(cv)).astype(BF16)
        g = _tn_dot(c_act, dm_ref[...].astype(BF16))
        delta, m2, v2 = _adamw(w_ref[...], g, m_ref[...], v_ref[...])
        g_ref[...] = g
        d_ref[...] = delta
        m2_ref[...] = m2
        v2_ref[...] = v2

    tile = pl.BlockSpec((tr, cols), lambda i: (i, 0))
    shp = jax.ShapeDtypeStruct((rows, cols), F32)
    return pl.pallas_call(
        body, name="adam_w_ada", out_shape=(shp, shp, shp, shp), grid=(rows // tr,),
        in_specs=[pl.BlockSpec((n_rows, tr), lambda i: (0, i)), pl.BlockSpec((n_rows, cols), lambda i: (0, 0)),
                  tile, tile, tile],
        out_specs=(tile, tile, tile, tile),
        compiler_params=pltpu.CompilerParams(dimension_semantics=("arbitrary",), vmem_limit_bytes=V7X_VMEM_LIMIT),
    )(c_rows, dmod_cols, w, m, v)


def _small_sum_call(small_all, n_grad_rows, loss_rows, bias_rows, loss_scale):
    _, rows, width = small_all.shape
    lo, hi = loss_rows
    b0, b1, b2 = bias_rows
    nb = b1 - b0

    def body(s_ref, sum_ref, extra_ref):
        tot = s_ref[0]
        for k in range(1, N_DEV):
            tot = tot + s_ref[k]
        sum_ref[...] = tot[0:n_grad_rows, :]
        extra_ref[0:nb, :] = tot[b0:b1, :] + tot[b1:b2, :]
        head = tot[n_grad_rows - 2 * SUBLANES:n_grad_rows, :]
        rows_id = lax.broadcasted_iota(jnp.int32, head.shape, 0) + (n_grad_rows - 2 * SUBLANES)
        sq = jnp.where(jnp.logical_and(rows_id >= lo, rows_id < hi), head, 0.0)
        extra_ref[nb:nb + SUBLANES, :] = jnp.zeros((SUBLANES, width), F32) + jnp.sum(sq) * loss_scale

    vm = pl.BlockSpec(memory_space=VMEM)
    return pl.pallas_call(
        body, name="small_sum",
        out_shape=(jax.ShapeDtypeStruct((n_grad_rows, width), F32), jax.ShapeDtypeStruct((nb + SUBLANES, width), F32)),
        in_specs=[vm], out_specs=(vm, vm),
    )(small_all)


def _adam_small_call(ws, gs, ms, vs):
    n = len(ws)

    def body(*refs):
        w_refs, g_refs, m_refs, v_refs = refs[:n], refs[n:2 * n], refs[2 * n:3 * n], refs[3 * n:4 * n]
        d_refs, m2_refs, v2_refs = refs[4 * n:5 * n], refs[5 * n:6 * n], refs[6 * n:7 * n]
        for i in range(n):
            delta, m2, v2 = _adamw(w_refs[i][...], g_refs[i][...], m_refs[i][...], v_refs[i][...])
            d_refs[i][...] = delta
            m2_refs[i][...] = m2
            v2_refs[i][...] = v2

    vm = pl.BlockSpec(memory_space=VMEM)
    shapes = tuple(jax.ShapeDtypeStruct(w.shape, F32) for w in ws)
    outs = pl.pallas_call(body, name="adam_small", out_shape=shapes * 3,
                          in_specs=[vm] * (4 * n), out_specs=(vm,) * (3 * n))(*ws, *gs, *ms, *vs)
    return outs[:n], outs[n:2 * n], outs[2 * n:]


def kernel(x, c, w_ada, b_ada, w_in, conf_dw_w, conf_dw_b, conf_ln_g, conf_ln_b, sc_conv_w, w_out, w_mlp1, w_mlp2, g_final, loss_target, m_w_ada, m_b_ada, m_w_in, m_conf_dw_w, m_conf_dw_b, m_conf_ln_g, m_conf_ln_b, m_sc_conv_w, m_w_out, m_w_mlp1, m_w_mlp2, m_g_final, v_w_ada, v_b_ada, v_w_in, v_conf_dw_w, v_conf_dw_b, v_conf_ln_g, v_conf_ln_b, v_sc_conv_w, v_w_out, v_w_mlp1, v_w_mlp2, v_g_final):
    bsz, seq, d_model = x.shape
    c_half = conf_dw_b.shape[-1]
    n_taps = conf_dw_w.shape[1]
    cc = conf_dw_w.shape[-1]
    a_cols = w_ada.shape[-1]
    tokens = bsz * seq
    me = _dev_index()

    c_pad = jnp.pad(c, ((0, SUBLANES - bsz), (0, 0)))
    b_ada_loc = lax.dynamic_slice(b_ada, (0, me * a_cols), (1, a_cols))
    small_loc = jnp.zeros((HALO, 128), F32)
    small_loc = small_loc.at[:n_taps, :cc].set(conf_dw_w[0]).at[:3, cc:2 * cc].set(sc_conv_w[0])
    (win_t, wout_all), small_all, c_all, mod_rows = _gather_call(
        c_pad, w_ada[0], b_ada_loc, small_loc, [w_in[0].T.astype(BF16), w_out[0].astype(BF16)])
    cw = small_all[:, :, :cc].transpose(1, 0, 2).reshape(HALO, c_half)
    scw = small_all[:, :3, cc:2 * cc].transpose(1, 0, 2).reshape(3, c_half)
    cp = jnp.concatenate([conf_dw_b, conf_ln_g, conf_ln_b, scw, jnp.zeros((2, c_half), F32)], axis=0)
    mod = mod_rows[:, :bsz, :].transpose(1, 0, 2).reshape(bsz, 6, d_model)

    flat = lambda t: t.reshape(tokens, t.shape[-1])
    (proj, a1, cv, mixed, y1), (w1_all, w2_all) = _mixer_fwd_call(
        x, mod, win_t, wout_all, cw, cp, [w_mlp1[0].astype(BF16), w_mlp2[0].astype(BF16)])
    dx1, h2, dy2, u, dz, dmod2, head, dy1 = _mlp_call(
        x, y1, loss_target, mod, w1_all, w2_all, g_final.reshape(1, d_model))
    g_out = _wgrad_call(flat(mixed), flat(dy1), "wgrad_out")
    g_w1 = _wgrad_call(flat(h2), flat(dz), "wgrad_mlp1", owner_cols=w_mlp1.shape[-1])
    g_w2, (s_w1, s_out) = _wgrad_call(flat(u), flat(dy2), "wgrad_mlp2", scatter=[g_w1, g_out])
    grad_x, dproj, h1, dmod1, cgrad = _mixer_bwd_call(dx1, x, proj, a1, cv, mod, win_t, wout_all, cw, cp)
    g_in_t, (s_w2,) = _wgrad_call(flat(dproj), flat(h1), "wgrad_in", scatter=[g_w2])

    dmod = jnp.concatenate([dmod1[:, :2, :], dmod2[:, 3:4, :], dmod2[:, :3, :]], axis=1)
    n_cg = cgrad.shape[0]
    per_b = 6 * d_model // c_half
    per_b_pad = -(-per_b // SUBLANES) * SUBLANES
    dmod_rows = jnp.pad(dmod.reshape(bsz, per_b, c_half), ((0, 0), (0, per_b_pad - per_b), (0, 0)))
    small = jnp.concatenate([
        cgrad,
        head.reshape(2 * SUBLANES, c_half),
        dmod_rows.reshape(bsz * per_b_pad, c_half),
    ], axis=0)
    s_in, gathered = _tail_scatter_call(g_in_t, small)

    n_head = n_cg + 2 * SUBLANES
    sums, extra = _small_sum_call(
        gathered, n_head, (n_cg + 2, n_cg + 4), (n_head, n_head + per_b_pad, n_head + 2 * per_b_pad), 0.5 / d_model)
    loss = extra[per_b_pad, 0]
    g_b_ada = extra[:per_b].reshape(1, 6 * d_model)
    g_dw_w = lax.dynamic_slice(sums[:n_taps], (0, me * cc), (n_taps, cc))
    g_sc_w = lax.dynamic_slice(sums[HALO:HALO + 3], (0, me * cc), (3, cc))
    g_dw_b, g_ln_g, g_ln_b = sums[HALO + 3:HALO + 4], sums[HALO + 4:HALO + 5], sums[HALO + 5:HALO + 6]
    g_gf = sums[n_cg:n_cg + 2].reshape(1, d_model)

    dmod_all = gathered[:, n_head:, :].reshape(N_DEV, bsz, per_b_pad, c_half)[:, :, :per_b, :]
    dmod_all = dmod_all.reshape(N_DEV, bsz, 6 * d_model)
    dmod_cols = lax.dynamic_slice(dmod_all, (0, 0, me * a_cols), (N_DEV, bsz, a_cols))
    dmod_cols = jnp.pad(dmod_cols, ((0, 0), (0, SUBLANES - bsz), (0, 0))).reshape(N_DEV * SUBLANES, a_cols)
    c_rows = c_all.reshape(N_DEV * SUBLANES, d_model)
    g_ada, d_ada, m_ada, v_ada = _adam_ada_call(c_rows, dmod_cols, w_ada[0], m_w_ada[0], v_w_ada[0])

    gi, di, mi, vi = _adam_slabs_call(s_in, w_in[0].T, m_w_in[0].T, v_w_in[0].T, "adam_w_in")
    gi, di, mi, vi = gi.T, di.T, mi.T, vi.T
    go, do, mo, vo = _adam_slabs_call(s_out, w_out[0], m_w_out[0], v_w_out[0], "adam_w_out")
    g1, d1, m1, v1 = _adam_slabs_call(s_w1, w_mlp1[0], m_w_mlp1[0], v_w_mlp1[0], "adam_w_mlp1")
    g2, d2, m2, v2 = _adam_slabs_call(s_w2, w_mlp2[0], m_w_mlp2[0], v_w_mlp2[0], "adam_w_mlp2")

    small_like = [b_ada, conf_dw_w, conf_dw_b, conf_ln_g, conf_ln_b, sc_conv_w, g_final]
    two_d = lambda t: t.reshape(-1, t.shape[-1])
    small_g = [g_b_ada, g_dw_w, g_dw_b, g_ln_g, g_ln_b, g_sc_w, g_gf]
    sd, sm, sv = _adam_small_call(
        [two_d(t) for t in small_like], small_g,
        [two_d(t) for t in (m_b_ada, m_conf_dw_w, m_conf_dw_b, m_conf_ln_g, m_conf_ln_b, m_sc_conv_w, m_g_final)],
        [two_d(t) for t in (v_b_ada, v_conf_dw_w, v_conf_dw_b, v_conf_ln_g, v_conf_ln_b, v_sc_conv_w, v_g_final)])
    like = lambda parts: [p.reshape(w.shape) for p, w in zip(parts, small_like)]
    sg, sd, sm, sv = like(small_g), like(sd), like(sm), like(sv)

    def ordered(ada, small_list, w_in_, w_out_, w1_, w2_):
        b_ada_, dw_w_, dw_b_, ln_g_, ln_b_, sc_w_, gf_ = small_list
        return [ada[None], b_ada_, w_in_[None], dw_w_, dw_b_, ln_g_, ln_b_, sc_w_, w_out_[None], w1_[None], w2_[None], gf_]

    grads = ordered(g_ada, sg, gi, go, g1, g2)
    deltas = ordered(d_ada, sd, di, do, d1, d2)
    new_m = ordered(m_ada, sm, mi, mo, m1, m2)
    new_v = ordered(v_ada, sv, vi, vo, v1, v2)
    return (loss, grad_x, *grads, *deltas, *new_m, *new_v)
```

```python
import jax
import jax.numpy as jnp
from jax import lax
from jax.experimental import pallas as pl
from jax.experimental.pallas import tpu as pltpu

N_DEV = 8
RMS_EPS = 1e-6
ADAM_LR = 0.001
ADAM_B1 = 0.9
ADAM_B2 = 0.999
ADAM_EPS = 1e-08
ADAM_WD = 0.01
ADAM_STEP = 10

F32 = jnp.float32
BF16 = jnp.bfloat16
MESH = pl.DeviceIdType.MESH
VMEM = pltpu.VMEM
ANY = pl.ANY

HALO = 32
SHORT_HALO = 8
ROW_CHUNK = 32
SUBLANES = 8
V7X_VMEM_LIMIT = 56 * 1024 * 1024
MLP_VMEM_LIMIT = 48 * 1024 * 1024
MLP_TOKEN_TILE = 256
MIXER_FWD_TOKEN_TILE = 512


def _coords():
    return lax.axis_index("x"), lax.axis_index("y"), lax.axis_index("c")


def _dev_index():
    x, y, c = _coords()
    return 4 * x + 2 * y + c


def _peer(k):
    x, y, c = _coords()
    px = 1 - x if (k >> 2) & 1 else x
    py = 1 - y if (k >> 1) & 1 else y
    pc = 1 - c if k & 1 else c
    return (px, py, pc), 4 * px + 2 * py + pc


def _sigmoid(v):
    return jax.nn.sigmoid(v)


def _nt_dot(a, b):
    return lax.dot_general(a, b, (((1,), (1,)), ((), ())), preferred_element_type=F32)


def _nn_dot(a, b):
    return jnp.dot(a, b, preferred_element_type=F32)


def _tn_dot(a, b):
    return lax.dot_general(a, b, (((0,), (0,)), ((), ())), preferred_element_type=F32)


def _token_tile(seq):
    return 256 if seq % 256 == 0 else 64


GATHER_SEM_COLUMNS = 9
HALF_ROW_ALIGN = 16


class _TwoLevelGather:
    def __init__(self, srcs, dsts, send_sems, recv_sems, loc_sems):
        x, y, c = _coords()
        me = 4 * x + 2 * y + c
        sibling, along_x, along_y = (x, y, 1 - c), (1 - x, y, c), (x, 1 - y, c)
        from_x, from_y, diagonal = 4 * (1 - x) + 2 * y + c, 4 * x + 2 * (1 - y) + c, 4 * (1 - x) + 2 * (1 - y) + c

        def remote(src, dst, a, col, to):
            return pltpu.make_async_remote_copy(
                src_ref=src, dst_ref=dst, send_sem=send_sems.at[a, col], recv_sem=recv_sems.at[a, col],
                device_id=to, device_id_type=MESH)

        def onward(block, a, col, to):
            return remote(block, block, a, col, to)

        self.first, self.arrivals, self.second, self.halves, self.third = [], [], [], [], []
        for a, (src, dst) in enumerate(zip(srcs, dsts)):
            half = src.shape[0] // (2 * HALF_ROW_ALIGN) * HALF_ROW_ALIGN
            lower, upper = pl.ds(0, half), pl.ds(half, src.shape[0] - half)
            neighbours = [remote(src, dst.at[me], a, 1, along_x), remote(src, dst.at[me], a, 2, along_y)]
            self.first += [pltpu.make_async_copy(src, dst.at[me], loc_sems.at[a]),
                           remote(src, dst.at[me], a, 0, sibling)]
            self.arrivals += neighbours
            halves = [onward(dst.at[from_x, lower], a, 3, along_y), onward(dst.at[from_y, upper], a, 4, along_x)]
            self.halves += halves
            self.second += [[halves[0], onward(dst.at[from_x], a, 5, sibling)],
                            [halves[1], onward(dst.at[from_y], a, 6, sibling)]]
            self.third += [onward(dst.at[diagonal, lower], a, 7, sibling),
                           onward(dst.at[diagonal, upper], a, 8, sibling)]

    def start(self):
        for cp in self.first + self.arrivals:
            cp.start()

    def forward_arrivals(self):
        for arrival, sends in zip(self.arrivals, self.second):
            arrival.wait_recv()
            for cp in sends:
                cp.start()

    def forward_diagonal(self):
        for half, to_sibling in zip(self.halves, self.third):
            half.wait_recv()
            to_sibling.start()

    def finish(self):
        for cp in self.arrivals + self.halves:
            cp.wait_send()
        for cp in self.first + [sends[1] for sends in self.second] + self.third:
            cp.wait()


N_CHIPS = N_DEV // 2
PAIR_SUM_ROWS = 256


class _TwoLevelScatter:
    def __init__(self, partials, sums, mine, from_sibling, pair, first_send, first_recv, second_send, second_recv,
                 fetch_sems, local_sems):
        x, y, c = _coords()
        sibling = (x, y, 1 - c)
        chips = [(1 - x, y), (x, 1 - y), (1 - x, 1 - y)]
        self.my_chip = 2 * x + y
        self.mine, self.from_sibling, self.pair = mine, from_sibling, pair

        def remote(src, dst, send_sem, recv_sem, to):
            return pltpu.make_async_remote_copy(src_ref=src, dst_ref=dst, send_sem=send_sem, recv_sem=recv_sem,
                                                device_id=to, device_id_type=MESH)

        self.first, self.fetch, self.second, self.local = [], [], [], []
        for a in range(len(partials)):
            self.local.append(pltpu.make_async_copy(pair[a].at[self.my_chip], sums[a].at[self.my_chip],
                                                    local_sems.at[a]))
            self.first += [remote(partials[a].at[2 * q + (1 - c)], from_sibling[a].at[q],
                                  first_send.at[a, q], first_recv.at[a, q], sibling) for q in range(N_CHIPS)]
            self.fetch += [pltpu.make_async_copy(partials[a].at[2 * q + c], mine[a].at[q], fetch_sems.at[a, q])
                           for q in range(N_CHIPS)]
            self.second += [remote(pair[a].at[2 * cx + cy], sums[a].at[self.my_chip],
                                   second_send.at[a, j], second_recv.at[a, j], (cx, cy, c))
                            for j, (cx, cy) in enumerate(chips)]

    def start(self):
        for cp in self.first + self.fetch:
            cp.start()

    def pair_sums(self):
        for cp in self.first + self.fetch:
            cp.wait()
        for mine, theirs, both in zip(self.mine, self.from_sibling, self.pair):
            rows = mine.shape[1]
            for q in range(N_CHIPS):
                for r0 in range(0, rows, PAIR_SUM_ROWS):
                    part = pl.ds(r0, min(PAIR_SUM_ROWS, rows - r0))
                    both[q, part, :] = (mine[q, part, :].astype(F32) + theirs[q, part, :].astype(F32)).astype(BF16)
        for cp in self.second + self.local:
            cp.start()

    def finish(self):
        for cp in self.second + self.local:
            cp.wait()


def _scatter_scratch(partials):
    n = len(partials)
    zones = [pltpu.VMEM((N_CHIPS,) + p.shape[1:], BF16) for p in partials]
    return 3 * zones + [pltpu.SemaphoreType.DMA((n, N_CHIPS)), pltpu.SemaphoreType.DMA((n, N_CHIPS)),
                        pltpu.SemaphoreType.DMA((n, N_CHIPS - 1)), pltpu.SemaphoreType.DMA((n, N_CHIPS - 1)),
                        pltpu.SemaphoreType.DMA((n, N_CHIPS)), pltpu.SemaphoreType.DMA((n,))]


def _exchange_sems(n_arrays, columns=GATHER_SEM_COLUMNS):
    return [pltpu.SemaphoreType.DMA((n_arrays, columns)), pltpu.SemaphoreType.DMA((n_arrays, columns)),
            pltpu.SemaphoreType.DMA((n_arrays,))]


def _gather_call(c_pad, w_ada, b_ada_loc, small_loc, big_shards):
    n_big = len(big_shards)
    d_model = c_pad.shape[1]
    a_cols = w_ada.shape[1]

    def body(c_ref, wada_ref, bada_ref, small_ref, *rest):
        big_in = rest[:n_big]
        big_out = rest[n_big:2 * n_big]
        small_all, c_all, mod_rows = rest[2 * n_big:2 * n_big + 3]
        modcols, big_send, big_recv, loc_sem, s_send, s_recv = rest[2 * n_big + 3:]
        me = _dev_index()
        big = _TwoLevelGather(big_in, big_out, big_send, big_recv, loc_sem)
        big.start()

        small_all[me] = small_ref[...]
        c_all[me] = c_ref[...]
        first = []
        for k in range(1, N_DEV):
            peer, _ = _peer(k)
            for i, (src, dst) in enumerate(((small_ref, small_all), (c_ref, c_all))):
                cp = pltpu.make_async_remote_copy(
                    src_ref=src, dst_ref=dst.at[me],
                    send_sem=s_send.at[i, k - 1], recv_sem=s_recv.at[i, k - 1],
                    device_id=peer, device_id_type=MESH)
                cp.start()
                first.append(cp)
        for cp in first:
            cp.wait()

        c_rows = c_all[...].reshape(N_DEV * SUBLANES, d_model)
        c_act = c_rows * _sigmoid(c_rows)
        modcols[...] = _nn_dot(c_act.astype(BF16), wada_ref[...].astype(BF16)) + bada_ref[...]
        mod_rows[me] = modcols[pl.ds(pl.multiple_of(me * SUBLANES, SUBLANES), SUBLANES), :]
        second = []
        for k in range(1, N_DEV):
            peer, pidx = _peer(k)
            cp = pltpu.make_async_remote_copy(
                src_ref=modcols.at[pl.ds(pl.multiple_of(pidx * SUBLANES, SUBLANES), SUBLANES), :],
                dst_ref=mod_rows.at[me],
                send_sem=s_send.at[2, k - 1], recv_sem=s_recv.at[2, k - 1],
                device_id=peer, device_id_type=MESH)
            cp.start()
            second.append(cp)
        big.forward_arrivals()
        for cp in second:
            cp.wait()
        big.forward_diagonal()
        big.finish()

    out_shape = tuple(jax.ShapeDtypeStruct((N_DEV,) + s.shape, s.dtype) for s in big_shards) + (
        jax.ShapeDtypeStruct((N_DEV,) + small_loc.shape, F32),
        jax.ShapeDtypeStruct((N_DEV, SUBLANES, d_model), F32),
        jax.ShapeDtypeStruct((N_DEV, SUBLANES, a_cols), F32),
    )
    vm = pl.BlockSpec(memory_space=VMEM)
    hbm = pl.BlockSpec(memory_space=ANY)
    outs = pl.pallas_call(
        body, name="gather_weights_mod", out_shape=out_shape,
        in_specs=[vm, vm, vm, vm] + [hbm] * n_big,
        out_specs=tuple([hbm] * n_big + [vm, vm, vm]),
        scratch_shapes=[
            pltpu.VMEM((N_DEV * SUBLANES, a_cols), F32),
            *_exchange_sems(n_big),
            pltpu.SemaphoreType.DMA((3, N_DEV - 1)),
            pltpu.SemaphoreType.DMA((3, N_DEV - 1)),
        ],
        compiler_params=pltpu.CompilerParams(vmem_limit_bytes=V7X_VMEM_LIMIT),
    )(c_pad, w_ada, b_ada_loc, small_loc, *big_shards)
    return outs[:n_big], outs[n_big], outs[n_big + 1], outs[n_big + 2]


def _shifted_rows_count(tm):
    return tm + HALO - SUBLANES


def _fill_shifted(ext, shifted, tm):
    for s in range(1, SUBLANES):
        shifted[s - 1] = ext[s:s + _shifted_rows_count(tm), :]


def _shifted_rows(ext, shifted, start, rows):
    phase = start % SUBLANES
    aligned = start - phase
    if phase == 0:
        return ext[aligned:aligned + rows, :]
    return shifted[phase - 1, aligned:aligned + rows, :]


def _layer_norm_parts(a1):
    mu = jnp.mean(a1, axis=-1, keepdims=True)
    xc = a1 - mu
    rstd = lax.rsqrt(jnp.mean(xc * xc, axis=-1, keepdims=True) + RMS_EPS)
    return xc * rstd, rstd


def _mixer_fwd_call(x, mod, win_t, wout, cw, cp, later_shards):
    n_later = len(later_shards)
    bsz, seq, d_model = x.shape
    c_half = cw.shape[1]
    n_taps = 31
    d_in = win_t.shape[0] * win_t.shape[1]
    tm = MIXER_FWD_TOKEN_TILE if seq % MIXER_FWD_TOKEN_TILE == 0 else _token_tile(seq)
    nt = seq // tm
    arrivals_step, diagonal_step = (7 * bsz * nt) // 16, (11 * bsz * nt) // 16

    def body(x_ref, mod_ref, win_ref, wout_ref, cw_ref, cp_ref, *rest):
        shard_refs, rest = rest[:n_later], rest[n_later:]
        proj_ref, a1_ref, cv_ref, mixed_ref, y1_ref = rest[:5]
        gathered_refs, rest = rest[5:5 + n_later], rest[5 + n_later:]
        aext, qext, ashift, send_sems, recv_sems, loc_sems = rest
        b, t = pl.program_id(0), pl.program_id(1)

        step = b * nt + t

        @pl.when(step == 0)
        def _():
            _TwoLevelGather(shard_refs, gathered_refs, send_sems, recv_sems, loc_sems).start()

        @pl.when(step == arrivals_step)
        def _():
            _TwoLevelGather(shard_refs, gathered_refs, send_sems, recv_sems, loc_sems).forward_arrivals()

        @pl.when(step == diagonal_step)
        def _():
            _TwoLevelGather(shard_refs, gathered_refs, send_sems, recv_sems, loc_sems).forward_diagonal()

        xv = x_ref[...]
        sh1, sc1 = mod_ref[0:1, :], mod_ref[1:2, :]
        r1 = lax.rsqrt(jnp.mean(xv * xv, axis=-1, keepdims=True) + RMS_EPS)
        h1 = (xv * r1) * (1.0 + sc1) + sh1
        proj = _nt_dot(h1.astype(BF16), win_ref[...].reshape(d_in, d_model))
        proj_ref[...] = proj
        val, gate = proj[:, 0:c_half], proj[:, c_half:2 * c_half]
        s_b, s_c, s_h = proj[:, 2 * c_half:3 * c_half], proj[:, 3 * c_half:4 * c_half], proj[:, 4 * c_half:5 * c_half]

        @pl.when(t == 0)
        def _():
            aext[0:HALO, :] = jnp.zeros((HALO, c_half), F32)
            qext[0:SHORT_HALO, :] = jnp.zeros((SHORT_HALO, c_half), F32)

        @pl.when(t > 0)
        def _():
            aext[0:HALO, :] = aext[tm:tm + HALO, :]
            qext[0:SHORT_HALO, :] = qext[tm:tm + SHORT_HALO, :]
        aext[HALO:HALO + tm, :] = val * _sigmoid(gate)
        qext[SHORT_HALO:SHORT_HALO + tm, :] = s_c * s_h

        base = HALO - (n_taps - 1)
        _fill_shifted(aext, ashift, tm)
        for r0 in range(0, tm, ROW_CHUNK):
            acc = jnp.zeros((ROW_CHUNK, c_half), F32)
            for k in range(n_taps):
                acc = acc + cw_ref[k:k + 1, :] * _shifted_rows(aext, ashift, r0 + base + k, ROW_CHUNK)
            a1_ref[r0:r0 + ROW_CHUNK, :] = acc + cp_ref[0:1, :]
        sbase = SHORT_HALO - 2
        conv3 = cp_ref[3:4, :] * qext[sbase:sbase + tm, :]
        conv3 = conv3 + cp_ref[4:5, :] * qext[sbase + 1:sbase + 1 + tm, :]
        conv3 = conv3 + cp_ref[5:6, :] * qext[sbase + 2:sbase + 2 + tm, :]
        cv_ref[...] = conv3

        norm, _ = _layer_norm_parts(a1_ref[...])
        a2 = norm * cp_ref[1:2, :] + cp_ref[2:3, :]
        mixed = jnp.concatenate([a2 * _sigmoid(a2), s_b * conv3], axis=-1).astype(BF16)
        mixed_ref[...] = mixed
        y1 = _nn_dot(mixed, wout_ref[...].reshape(d_model, d_model))
        y1_ref[...] = y1

        @pl.when(step == bsz * nt - 1)
        def _():
            _TwoLevelGather(shard_refs, gathered_refs, send_sems, recv_sems, loc_sems).finish()

    hbm = pl.BlockSpec(memory_space=ANY)

    def tok(width):
        return pl.BlockSpec((None, tm, width), lambda b, t: (b, t, 0))

    def const(shape):
        return pl.BlockSpec(shape, lambda b, t: (0,) * len(shape))

    def resident(shape):
        return pl.BlockSpec(shape, lambda b, t: (0,) * len(shape), pipeline_mode=pl.Buffered(1))

    out_shape = (
        jax.ShapeDtypeStruct((bsz, seq, d_in), F32),
        jax.ShapeDtypeStruct((bsz, seq, c_half), F32),
        jax.ShapeDtypeStruct((bsz, seq, c_half), F32),
        jax.ShapeDtypeStruct((bsz, seq, d_model), BF16),
        jax.ShapeDtypeStruct((bsz, seq, d_model), F32),
    ) + tuple(jax.ShapeDtypeStruct((N_DEV,) + s.shape, s.dtype) for s in later_shards)
    outs = pl.pallas_call(
        body, name="mixer_fwd", out_shape=out_shape, grid=(bsz, nt),
        in_specs=[tok(d_model), pl.BlockSpec((None, 6, d_model), lambda b, t: (b, 0, 0)),
                  resident(win_t.shape), resident(wout.shape), const(cw.shape), const(cp.shape)] + [hbm] * n_later,
        out_specs=(tok(d_in), tok(c_half), tok(c_half), tok(d_model), tok(d_model)) + (hbm,) * n_later,
        scratch_shapes=[pltpu.VMEM((tm + HALO, c_half), F32), pltpu.VMEM((tm + SHORT_HALO, c_half), F32),
                        pltpu.VMEM((SUBLANES - 1, _shifted_rows_count(tm), c_half), F32)]
        + _exchange_sems(n_later),
        compiler_params=pltpu.CompilerParams(
            dimension_semantics=("arbitrary", "arbitrary"), vmem_limit_bytes=V7X_VMEM_LIMIT),
    )(x, mod, win_t, wout, cw, cp, *later_shards)
    return outs[:5], outs[5:]


def _mlp_call(x, y1, target, mod, w1, w2, g_final):
    bsz, seq, d_model = x.shape
    n_blk, f_blk, _ = w1.shape
    d_ff = n_blk * f_blk
    tm = MLP_TOKEN_TILE if seq % MLP_TOKEN_TILE == 0 else _token_tile(seq)
    nt = seq // tm

    def body(x_ref, y1_ref, tgt_ref, mod_ref, w1_ref, w2_ref, gf_ref,
             dx1_ref, h2_ref, dy2_ref, u_ref, dz_ref, dmod_ref, head_ref, dy1_ref, relu_scr):
        b, t = pl.program_id(0), pl.program_id(1)
        x1v = x_ref[...] + mod_ref[2:3, :] * y1_ref[...]
        sh2, sc2, g2 = mod_ref[3:4, :], mod_ref[4:5, :], mod_ref[5:6, :]
        gf = gf_ref[...]
        r2 = lax.rsqrt(jnp.mean(x1v * x1v, axis=-1, keepdims=True) + RMS_EPS)
        xn2 = x1v * r2
        h2 = (xn2 * (1.0 + sc2) + sh2).astype(BF16)
        h2_ref[...] = h2
        for j in range(n_blk):
            cols = slice(j * f_blk, (j + 1) * f_blk)
            rz = jnp.maximum(_nt_dot(h2, w1_ref[j]), 0.0)
            relu_scr[:, cols] = rz
            u_ref[:, cols] = (rz * rz).astype(BF16)
        y2 = _nn_dot(u_ref[...], w2_ref[...].reshape(d_ff, d_model))
        x2 = x1v + g2 * y2
        r3 = lax.rsqrt(jnp.mean(x2 * x2, axis=-1, keepdims=True) + RMS_EPS)
        xn3 = x2 * r3
        diff = xn3 * gf - tgt_ref[...]
        dout = diff * (1.0 / d_model)

        @pl.when(jnp.logical_and(b == 0, t == 0))
        def _():
            head_ref[...] = jnp.zeros(head_ref.shape, F32)

        @pl.when(t == 0)
        def _():
            dmod_ref[...] = jnp.zeros(dmod_ref.shape, F32)

        head_ref[0:1, :] += jnp.sum(dout * xn3, axis=0, keepdims=True)
        head_ref[1:2, :] += jnp.sum(diff * diff, axis=0, keepdims=True)
        dxn3 = dout * gf
        dx2 = r3 * (dxn3 - xn3 * jnp.mean(dxn3 * xn3, axis=-1, keepdims=True))
        dmod_ref[2:3, :] += jnp.sum(dx2 * y2, axis=0, keepdims=True)
        dy2 = (g2 * dx2).astype(BF16)
        dy2_ref[...] = dy2
        for j in range(n_blk):
            cols = slice(j * f_blk, (j + 1) * f_blk)
            dz_ref[:, cols] = (_nt_dot(dy2, w2_ref[j]) * (2.0 * relu_scr[:, cols])).astype(BF16)
        dh2 = _nn_dot(dz_ref[...], w1_ref[...].reshape(d_ff, d_model))
        dmod_ref[0:1, :] += jnp.sum(dh2, axis=0, keepdims=True)
        dmod_ref[1:2, :] += jnp.sum(dh2 * xn2, axis=0, keepdims=True)
        dxn2 = dh2 * (1.0 + sc2)
        dx1 = dx2 + r2 * (dxn2 - xn2 * jnp.mean(dxn2 * xn2, axis=-1, keepdims=True))
        dx1_ref[...] = dx1
        dy1_ref[...] = (mod_ref[2:3, :] * dx1).astype(BF16)
        dmod_ref[3:4, :] += jnp.sum(dx1 * y1_ref[...], axis=0, keepdims=True)

    def tok(width):
        return pl.BlockSpec((None, tm, width), lambda b, t: (b, t, 0))

    def const(shape):
        return pl.BlockSpec(shape, lambda b, t: (0,) * len(shape))

    def resident(shape):
        return pl.BlockSpec(shape, lambda b, t: (0,) * len(shape), pipeline_mode=pl.Buffered(1))

    out_shape = (
        jax.ShapeDtypeStruct((bsz, seq, d_model), F32),
        jax.ShapeDtypeStruct((bsz, seq, d_model), BF16),
        jax.ShapeDtypeStruct((bsz, seq, d_model), BF16),
        jax.ShapeDtypeStruct((bsz, seq, d_ff), BF16),
        jax.ShapeDtypeStruct((bsz, seq, d_ff), BF16),
        jax.ShapeDtypeStruct((bsz, SUBLANES, d_model), F32),
        jax.ShapeDtypeStruct((SUBLANES, d_model), F32),
        jax.ShapeDtypeStruct((bsz, seq, d_model), BF16),
    )
    return pl.pallas_call(
        body, name="mlp_fwd_bwd", out_shape=out_shape, grid=(bsz, nt),
        in_specs=[tok(d_model), tok(d_model), tok(d_model),
                  pl.BlockSpec((None, 6, d_model), lambda b, t: (b, 0, 0)),
                  resident(w1.shape), resident(w2.shape), const(g_final.shape)],
        out_specs=(tok(d_model), tok(d_model), tok(d_model), tok(d_ff), tok(d_ff),
                   pl.BlockSpec((None, SUBLANES, d_model), lambda b, t: (b, 0, 0)),
                   const((SUBLANES, d_model)), tok(d_model)),
        scratch_shapes=[pltpu.VMEM((tm, d_ff), F32)],
        compiler_params=pltpu.CompilerParams(
            dimension_semantics=("arbitrary", "arbitrary"), vmem_limit_bytes=MLP_VMEM_LIMIT),
    )(x, y1, target, mod, w1, w2, g_final)


def _mixer_bwd_call(dx1, x, proj, a1, cv, mod, win_t, wout, cw, cp):
    bsz, seq, d_model = x.shape
    c_half = cw.shape[1]
    n_taps = 31
    d_in = win_t.shape[0] * win_t.shape[1]
    tm = _token_tile(seq)
    nt = seq // tm

    def body(dx1_ref, x_ref, proj_ref, a1_ref, cv_ref, mod_ref, win_ref, wout_ref, cw_ref, cp_ref,
             gx_ref, dproj_ref, h1_ref, dmod_ref, cgrad_ref,
             dext, cext, a0_scr, da0_scr, tap_acc, row_acc, dshift):
        b, step = pl.program_id(0), pl.program_id(1)
        first = jnp.logical_and(b == 0, step == 0)
        last = jnp.logical_and(b == bsz - 1, step == nt - 1)
        dx1v = dx1_ref[...]
        xv = x_ref[...]
        sh1, sc1, g1 = mod_ref[0:1, :], mod_ref[1:2, :], mod_ref[2:3, :]

        @pl.when(first)
        def _():
            tap_acc[...] = jnp.zeros(tap_acc.shape, F32)
            row_acc[...] = jnp.zeros(row_acc.shape, F32)

        @pl.when(step == 0)
        def _():
            dmod_ref[...] = jnp.zeros(dmod_ref.shape, F32)
            dext[tm:tm + HALO, :] = jnp.zeros((HALO, c_half), F32)
            cext[tm:tm + SHORT_HALO, :] = jnp.zeros((SHORT_HALO, c_half), F32)

        @pl.when(step > 0)
        def _():
            dext[tm:tm + HALO, :] = dext[0:HALO, :]
            cext[tm:tm + SHORT_HALO, :] = cext[0:SHORT_HALO, :]

        dy1 = (g1 * dx1v).astype(BF16)
        dmixed = _nt_dot(dy1, wout_ref[...].reshape(d_model, d_model))
        d_a, d_s = dmixed[:, 0:c_half], dmixed[:, c_half:2 * c_half]

        val, gate = proj_ref[:, 0:c_half], proj_ref[:, c_half:2 * c_half]
        s_b = proj_ref[:, 2 * c_half:3 * c_half]
        s_c, s_h = proj_ref[:, 3 * c_half:4 * c_half], proj_ref[:, 4 * c_half:5 * c_half]

        d_sb = d_s * cv_ref[...]
        cext[0:tm, :] = d_s * s_b
        q = s_c * s_h
        dq = jnp.zeros((tm, c_half), F32)
        for k in range(3):
            shifted = cext[2 - k:2 - k + tm, :]
            dq = dq + cp_ref[3 + k:4 + k, :] * shifted
            row_acc[k:k + 1, :] += jnp.sum(q * shifted, axis=0, keepdims=True)
        d_sc, d_sh = dq * s_h, dq * s_c

        norm, rstd = _layer_norm_parts(a1_ref[...])
        ln_g = cp_ref[1:2, :]
        a2 = norm * ln_g + cp_ref[2:3, :]
        sg = _sigmoid(a2)
        d_a2 = d_a * (sg * (1.0 + a2 * (1.0 - sg)))
        row_acc[4:5, :] += jnp.sum(d_a2 * norm, axis=0, keepdims=True)
        row_acc[5:6, :] += jnp.sum(d_a2, axis=0, keepdims=True)
        d_n = d_a2 * ln_g
        d_a1 = rstd * (d_n - jnp.mean(d_n, axis=-1, keepdims=True)
                       - norm * jnp.mean(d_n * norm, axis=-1, keepdims=True))
        row_acc[3:4, :] += jnp.sum(d_a1, axis=0, keepdims=True)
        dext[0:tm, :] = d_a1
        sig_g = _sigmoid(gate)
        a0_scr[...] = val * sig_g

        _fill_shifted(dext, dshift, tm)
        for r0 in range(0, tm, ROW_CHUNK):
            a0c = a0_scr[r0:r0 + ROW_CHUNK, :]
            acc = jnp.zeros((ROW_CHUNK, c_half), F32)
            for k in range(n_taps):
                shifted = _shifted_rows(dext, dshift, r0 + (n_taps - 1) - k, ROW_CHUNK)
                acc = acc + cw_ref[k:k + 1, :] * shifted
                prod = a0c * shifted
                part = prod[0:SUBLANES, :]
                for g in range(1, ROW_CHUNK // SUBLANES):
                    part = part + prod[g * SUBLANES:(g + 1) * SUBLANES, :]
                tap_acc[k * SUBLANES:(k + 1) * SUBLANES, :] += part
            da0_scr[r0:r0 + ROW_CHUNK, :] = acc
        d_a0 = da0_scr[...]
        d_val = d_a0 * sig_g
        d_gate = d_a0 * val * sig_g * (1.0 - sig_g)

        dproj = jnp.concatenate([d_val, d_gate, d_sb, d_sc, d_sh], axis=-1).astype(BF16)
        dproj_ref[...] = dproj
        dh1 = _nn_dot(dproj, win_ref[...].reshape(d_in, d_model))
        r1 = lax.rsqrt(jnp.mean(xv * xv, axis=-1, keepdims=True) + RMS_EPS)
        xn1 = xv * r1
        h1_ref[...] = (xn1 * (1.0 + sc1) + sh1).astype(BF16)
        dmod_ref[0:1, :] += jnp.sum(dh1, axis=0, keepdims=True)
        dmod_ref[1:2, :] += jnp.sum(dh1 * xn1, axis=0, keepdims=True)
        dxn1 = dh1 * (1.0 + sc1)
        gx_ref[...] = dx1v + r1 * (dxn1 - xn1 * jnp.mean(dxn1 * xn1, axis=-1, keepdims=True))

        @pl.when(last)
        def _():
            taps = jnp.sum(tap_acc[...].reshape(HALO, SUBLANES, c_half), axis=1)
            cgrad_ref[0:HALO, :] = taps
            cgrad_ref[HALO:HALO + SUBLANES, :] = row_acc[...]

    def tok(width):
        return pl.BlockSpec((None, tm, width), lambda b, s: (b, nt - 1 - s, 0))

    def const(shape):
        return pl.BlockSpec(shape, lambda b, s: (0,) * len(shape))

    mod_spec = pl.BlockSpec((None, 6, d_model), lambda b, s: (b, 0, 0))
    out_shape = (
        jax.ShapeDtypeStruct((bsz, seq, d_model), F32),
        jax.ShapeDtypeStruct((bsz, seq, d_in), BF16),
        jax.ShapeDtypeStruct((bsz, seq, d_model), BF16),
        jax.ShapeDtypeStruct((bsz, SUBLANES, d_model), F32),
        jax.ShapeDtypeStruct((HALO + SUBLANES, c_half), F32),
    )
    return pl.pallas_call(
        body, name="mixer_bwd", out_shape=out_shape, grid=(bsz, nt),
        in_specs=[tok(d_model), tok(d_model), tok(d_in), tok(c_half), tok(c_half), mod_spec,
                  const(win_t.shape), const(wout.shape), const(cw.shape), const(cp.shape)],
        out_specs=(tok(d_model), tok(d_in), tok(d_model),
                   pl.BlockSpec((None, SUBLANES, d_model), lambda b, s: (b, 0, 0)),
                   const((HALO + SUBLANES, c_half))),
        scratch_shapes=[
            pltpu.VMEM((tm + HALO, c_half), F32), pltpu.VMEM((tm + SHORT_HALO, c_half), F32),
            pltpu.VMEM((tm, c_half), F32), pltpu.VMEM((tm, c_half), F32),
            pltpu.VMEM((HALO * SUBLANES, c_half), F32), pltpu.VMEM((SUBLANES, c_half), F32),
            pltpu.VMEM((SUBLANES - 1, _shifted_rows_count(tm), c_half), F32),
        ],
        compiler_params=pltpu.CompilerParams(
            dimension_semantics=("arbitrary", "arbitrary"), vmem_limit_bytes=V7X_VMEM_LIMIT),
    )(dx1, x, proj, a1, cv, mod, win_t, wout, cw, cp)


def _largest_divisor(n, cap, multiple):
    best = None
    for cand in range(multiple, min(n, cap) + 1, multiple):
        if n % cand == 0:
            best = cand
    return best if best is not None else n


WGRAD_TOKENS_PER_STEP = 2048
WGRAD_COLS_PER_STEP = 1024


def _wgrad_call(a, b, name, owner_cols=None, scatter=()):
    n_sc = len(scatter)
    tokens, m_dim = a.shape
    n_dim = b.shape[1]
    bk = _largest_divisor(tokens, WGRAD_TOKENS_PER_STEP, 128)
    n_k = tokens // bk
    if owner_cols is None:
        bm = _largest_divisor(m_dim, 1024, m_dim // N_DEV)
        bn = n_dim
        owners = 1
        out_shape = jax.ShapeDtypeStruct((m_dim, n_dim), BF16)
        out_spec = pl.BlockSpec((bm, bn), lambda i, j, k: (i, j))
    else:
        bm = m_dim
        bn = _largest_divisor(n_dim, WGRAD_COLS_PER_STEP, owner_cols)
        owners = bn // owner_cols
        out_shape = jax.ShapeDtypeStruct((n_dim // owner_cols, m_dim, owner_cols), BF16)
        out_spec = pl.BlockSpec((owners, bm, owner_cols), lambda i, j, k: (j, i, 0))

    grid = (m_dim // bm, n_dim // bn, n_k)
    n_steps = grid[0] * grid[1] * grid[2]
    pair_step = min(max(1, n_steps // 5), n_steps - 1)

    def body(a_ref, b_ref, *rest):
        part_refs, o_ref, sum_refs, acc = rest[:n_sc], rest[n_sc], rest[n_sc + 1:2 * n_sc + 1], rest[2 * n_sc + 1]
        k = pl.program_id(2)
        step = (pl.program_id(0) * grid[1] + pl.program_id(1)) * n_k + k

        def exchange():
            extra = rest[2 * n_sc + 2:]
            return _TwoLevelScatter(part_refs, sum_refs, extra[:n_sc], extra[n_sc:2 * n_sc], extra[2 * n_sc:3 * n_sc],
                                    *extra[3 * n_sc:])

        if n_sc:
            @pl.when(step == 0)
            def _():
                exchange().start()

            @pl.when(step == pair_step)
            def _():
                exchange().pair_sums()

        @pl.when(k == 0)
        def _():
            acc[...] = jnp.zeros(acc.shape, F32)

        acc[...] += _tn_dot(a_ref[...], b_ref[...])

        @pl.when(k == n_k - 1)
        def _():
            if owner_cols is None:
                o_ref[...] = acc[...].astype(BF16)
            else:
                for q in range(owners):
                    o_ref[q] = acc[:, q * owner_cols:(q + 1) * owner_cols].astype(BF16)

        if n_sc:
            @pl.when(step == n_steps - 1)
            def _():
                exchange().finish()

    hbm = pl.BlockSpec(memory_space=ANY)
    sum_shapes = tuple(jax.ShapeDtypeStruct((N_CHIPS,) + p.shape[1:], BF16) for p in scatter)
    outs = pl.pallas_call(
        body, name=name, out_shape=(out_shape,) + sum_shapes, grid=grid,
        in_specs=[pl.BlockSpec((bk, bm), lambda i, j, k: (k, i)), pl.BlockSpec((bk, bn), lambda i, j, k: (k, j))]
        + [hbm] * n_sc,
        out_specs=(out_spec,) + (hbm,) * n_sc,
        scratch_shapes=[pltpu.VMEM((bm, bn), F32)] + (_scatter_scratch(scatter) if n_sc else []),
        compiler_params=pltpu.CompilerParams(
            dimension_semantics=("arbitrary", "arbitrary", "arbitrary"), vmem_limit_bytes=V7X_VMEM_LIMIT),
    )(a, b, *scatter)
    out = outs[0]
    if owner_cols is None:
        out = out.reshape(N_DEV, m_dim // N_DEV, n_dim)
    return (out, list(outs[1:])) if n_sc else out


def _tail_scatter_call(partial, small):
    def body(g_ref, small_ref, out_ref, small_all, mine, from_sibling, pair,
             p1_send, p1_recv, p2_send, p2_recv, p_fetch, p_loc, s_send, s_recv, s_loc):
        gather = _TwoLevelGather([small_ref], [small_all], s_send, s_recv, s_loc)
        scatter = _TwoLevelScatter([g_ref], [out_ref], [mine], [from_sibling], [pair],
                                   p1_send, p1_recv, p2_send, p2_recv, p_fetch, p_loc)
        gather.start()
        scatter.start()
        scatter.pair_sums()
        gather.forward_arrivals()
        scatter.finish()
        gather.forward_diagonal()
        gather.finish()

    vm = pl.BlockSpec(memory_space=VMEM)
    return pl.pallas_call(
        body, name="scatter_tail",
        out_shape=(jax.ShapeDtypeStruct((N_CHIPS,) + partial.shape[1:], BF16),
                   jax.ShapeDtypeStruct((N_DEV,) + small.shape, F32)),
        in_specs=[pl.BlockSpec(memory_space=ANY), vm], out_specs=(vm, vm),
        scratch_shapes=_scatter_scratch([partial]) + _exchange_sems(1),
        compiler_params=pltpu.CompilerParams(vmem_limit_bytes=V7X_VMEM_LIMIT),
    )(partial, small)


def _adamw(w, g, m, v):
    m2 = ADAM_B1 * m + (1.0 - ADAM_B1) * g
    v2 = ADAM_B2 * v + (1.0 - ADAM_B2) * (g * g)
    m_hat = m2 / (1.0 - ADAM_B1 ** ADAM_STEP)
    v_hat = v2 / (1.0 - ADAM_B2 ** ADAM_STEP)
    delta = -ADAM_LR * (m_hat / (jnp.sqrt(v_hat) + ADAM_EPS) + ADAM_WD * w)
    return delta, m2, v2


def _adam_slabs_call(slabs, w, m, v, name, transposed=False):
    rows, cols = w.shape
    n_slabs, slab_rows, slab_cols = slabs.shape
    tr = _largest_divisor(slab_rows, 256, 128 if transposed else 2 * SUBLANES)

    def body(s_ref, w_ref, m_ref, v_ref, g_ref, d_ref, m2_ref, v2_ref):
        g = s_ref[0].astype(F32)
        for k in range(1, n_slabs):
            g = g + s_ref[k].astype(F32)
        if transposed:
            g = g.T
        delta, m2, v2 = _adamw(w_ref[...], g, m_ref[...], v_ref[...])
        g_ref[...] = g
        d_ref[...] = delta
        m2_ref[...] = m2
        v2_ref[...] = v2

    tile = pl.BlockSpec((rows, tr), lambda i: (0, i)) if transposed else pl.BlockSpec((tr, cols), lambda i: (i, 0))
    shp = jax.ShapeDtypeStruct((rows, cols), F32)
    return pl.pallas_call(
        body, name=name, out_shape=(shp, shp, shp, shp), grid=(slab_rows // tr,),
        in_specs=[pl.BlockSpec((n_slabs, tr, slab_cols), lambda i: (0, i, 0)), tile, tile, tile],
        out_specs=(tile, tile, tile, tile),
        compiler_params=pltpu.CompilerParams(dimension_semantics=("arbitrary",), vmem_limit_bytes=V7X_VMEM_LIMIT),
    )(slabs, w, m, v)


def _adam_ada_call(c_rows, dmod_cols, w, m, v):
    rows, cols = w.shape
    n_rows = c_rows.shape[0]
    tr = _largest_divisor(rows, 256, 128)

    def body(c_ref, dm_ref, w_ref, m_ref, v_ref, g_ref, d_ref, m2_ref, v2_ref):
        cv = c_ref[...]
        c_act = (cv * _sigmoid(cv)).astype(BF16)
        g = _tn_dot(c_act, dm_ref[...].astype(BF16))
        delta, m2, v2 = _adamw(w_ref[...], g, m_ref[...], v_ref[...])
        g_ref[...] = g
        d_ref[...] = delta
        m2_ref[...] = m2
        v2_ref[...] = v2

    tile = pl.BlockSpec((tr, cols), lambda i: (i, 0))
    shp = jax.ShapeDtypeStruct((rows, cols), F32)
    return pl.pallas_call(
        body, name="adam_w_ada", out_shape=(shp, shp, shp, shp), grid=(rows // tr,),
        in_specs=[pl.BlockSpec((n_rows, tr), lambda i: (0, i)), pl.BlockSpec((n_rows, cols), lambda i: (0, 0)),
                  tile, tile, tile],
        out_specs=(tile, tile, tile, tile),
        compiler_params=pltpu.CompilerParams(dimension_semantics=("arbitrary",), vmem_limit_bytes=V7X_VMEM_LIMIT),
    )(c_rows, dmod_cols, w, m, v)


def _small_sum_call(small_all, n_grad_rows, loss_rows, bias_rows, loss_scale):
    _, rows, width = small_all.shape
    lo, hi = loss_rows
    b0, b1, b2 = bias_rows
    nb = b1 - b0

    def body(s_ref, sum_ref, extra_ref):
        tot = s_ref[0]
        for k in range(1, N_DEV):
            tot = tot + s_ref[k]
        sum_ref[...] = tot[0:n_grad_rows, :]
        extra_ref[0:nb, :] = tot[b0:b1, :] + tot[b1:b2, :]
        head = tot[n_grad_rows - 2 * SUBLANES:n_grad_rows, :]
        rows_id = lax.broadcasted_iota(jnp.int32, head.shape, 0) + (n_grad_rows - 2 * SUBLANES)
        sq = jnp.where(jnp.logical_and(rows_id >= lo, rows_id < hi), head, 0.0)
        extra_ref[nb:nb + SUBLANES, :] = jnp.zeros((SUBLANES, width), F32) + jnp.sum(sq) * loss_scale

    vm = pl.BlockSpec(memory_space=VMEM)
    return pl.pallas_call(
        body, name="small_sum",
        out_shape=(jax.ShapeDtypeStruct((n_grad_rows, width), F32), jax.ShapeDtypeStruct((nb + SUBLANES, width), F32)),
        in_specs=[vm], out_specs=(vm, vm),
    )(small_all)


def _adam_small_call(ws, gs, ms, vs):
    n = len(ws)

    def body(*refs):
        w_refs, g_refs, m_refs, v_refs = refs[:n], refs[n:2 * n], refs[2 * n:3 * n], refs[3 * n:4 * n]
        d_refs, m2_refs, v2_refs = refs[4 * n:5 * n], refs[5 * n:6 * n], refs[6 * n:7 * n]
        for i in range(n):
            delta, m2, v2 = _adamw(w_refs[i][...], g_refs[i][...], m_refs[i][...], v_refs[i][...])
            d_refs[i][...] = delta
            m2_refs[i][...] = m2
            v2_refs[i][...] = v2

    vm = pl.BlockSpec(memory_space=VMEM)
    shapes = tuple(jax.ShapeDtypeStruct(w.shape, F32) for w in ws)
    outs = pl.pallas_call(body, name="adam_small", out_shape=shapes * 3,
                          in_specs=[vm] * (4 * n), out_specs=(vm,) * (3 * n))(*ws, *gs, *ms, *vs)
    return outs[:n], outs[n:2 * n], outs[2 * n:]


def kernel(x, c, w_ada, b_ada, w_in, conf_dw_w, conf_dw_b, conf_ln_g, conf_ln_b, sc_conv_w, w_out, w_mlp1, w_mlp2, g_final, loss_target, m_w_ada, m_b_ada, m_w_in, m_conf_dw_w, m_conf_dw_b, m_conf_ln_g, m_conf_ln_b, m_sc_conv_w, m_w_out, m_w_mlp1, m_w_mlp2, m_g_final, v_w_ada, v_b_ada, v_w_in, v_conf_dw_w, v_conf_dw_b, v_conf_ln_g, v_conf_ln_b, v_sc_conv_w, v_w_out, v_w_mlp1, v_w_mlp2, v_g_final):
    bsz, seq, d_model = x.shape
    c_half = conf_dw_b.shape[-1]
    n_taps = conf_dw_w.shape[1]
    cc = conf_dw_w.shape[-1]
    a_cols = w_ada.shape[-1]
    tokens = bsz * seq
    me = _dev_index()

    c_pad = jnp.pad(c, ((0, SUBLANES - bsz), (0, 0)))
    b_ada_loc = lax.dynamic_slice(b_ada, (0, me * a_cols), (1, a_cols))
    small_loc = jnp.zeros((HALO, 128), F32)
    small_loc = small_loc.at[:n_taps, :cc].set(conf_dw_w[0]).at[:3, cc:2 * cc].set(sc_conv_w[0])
    (win_t, wout_all), small_all, c_all, mod_rows = _gather_call(
        c_pad, w_ada[0], b_ada_loc, small_loc, [w_in[0].T.astype(BF16), w_out[0].astype(BF16)])
    cw = small_all[:, :, :cc].transpose(1, 0, 2).reshape(HALO, c_half)
    scw = small_all[:, :3, cc:2 * cc].transpose(1, 0, 2).reshape(3, c_half)
    cp = jnp.concatenate([conf_dw_b, conf_ln_g, conf_ln_b, scw, jnp.zeros((2, c_half), F32)], axis=0)
    mod = mod_rows[:, :bsz, :].transpose(1, 0, 2).reshape(bsz, 6, d_model)

    flat = lambda t: t.reshape(tokens, t.shape[-1])
    (proj, a1, cv, mixed, y1), (w1_all, w2_all) = _mixer_fwd_call(
        x, mod, win_t, wout_all, cw, cp, [w_mlp1[0].T.astype(BF16), w_mlp2[0].astype(BF16)])
    dx1, h2, dy2, u, dz, dmod2, head, dy1 = _mlp_call(
        x, y1, loss_target, mod, w1_all, w2_all, g_final.reshape(1, d_model))
    g_out = _wgrad_call(flat(mixed), flat(dy1), "wgrad_out")
    g_w1 = _wgrad_call(flat(dz), flat(h2), "wgrad_mlp1")
    g_w2, (s_w1, s_out) = _wgrad_call(flat(u), flat(dy2), "wgrad_mlp2", scatter=[g_w1, g_out])
    grad_x, dproj, h1, dmod1, cgrad = _mixer_bwd_call(dx1, x, proj, a1, cv, mod, win_t, wout_all, cw, cp)
    g_in_t, (s_w2,) = _wgrad_call(flat(dproj), flat(h1), "wgrad_in", scatter=[g_w2])

    dmod = jnp.concatenate([dmod1[:, :2, :], dmod2[:, 3:4, :], dmod2[:, :3, :]], axis=1)
    n_cg = cgrad.shape[0]
    per_b = 6 * d_model // c_half
    per_b_pad = -(-per_b // SUBLANES) * SUBLANES
    dmod_rows = jnp.pad(dmod.reshape(bsz, per_b, c_half), ((0, 0), (0, per_b_pad - per_b), (0, 0)))
    small = jnp.concatenate([
        cgrad,
        head.reshape(2 * SUBLANES, c_half),
        dmod_rows.reshape(bsz * per_b_pad, c_half),
    ], axis=0)
    s_in, gathered = _tail_scatter_call(g_in_t, small)

    n_head = n_cg + 2 * SUBLANES
    sums, extra = _small_sum_call(
        gathered, n_head, (n_cg + 2, n_cg + 4), (n_head, n_head + per_b_pad, n_head + 2 * per_b_pad), 0.5 / d_model)
    loss = extra[per_b_pad, 0]
    g_b_ada = extra[:per_b].reshape(1, 6 * d_model)
    g_dw_w = lax.dynamic_slice(sums[:n_taps], (0, me * cc), (n_taps, cc))
    g_sc_w = lax.dynamic_slice(sums[HALO:HALO + 3], (0, me * cc), (3, cc))
    g_dw_b, g_ln_g, g_ln_b = sums[HALO + 3:HALO + 4], sums[HALO + 4:HALO + 5], sums[HALO + 5:HALO + 6]
    g_gf = sums[n_cg:n_cg + 2].reshape(1, d_model)

    dmod_all = gathered[:, n_head:, :].reshape(N_DEV, bsz, per_b_pad, c_half)[:, :, :per_b, :]
    dmod_all = dmod_all.reshape(N_DEV, bsz, 6 * d_model)
    dmod_cols = lax.dynamic_slice(dmod_all, (0, 0, me * a_cols), (N_DEV, bsz, a_cols))
    dmod_cols = jnp.pad(dmod_cols, ((0, 0), (0, SUBLANES - bsz), (0, 0))).reshape(N_DEV * SUBLANES, a_cols)
    c_rows = c_all.reshape(N_DEV * SUBLANES, d_model)
    g_ada, d_ada, m_ada, v_ada = _adam_ada_call(c_rows, dmod_cols, w_ada[0], m_w_ada[0], v_w_ada[0])

    gi, di, mi, vi = _adam_slabs_call(s_in, w_in[0].T, m_w_in[0].T, v_w_in[0].T, "adam_w_in")
    gi, di, mi, vi = gi.T, di.T, mi.T, vi.T
    go, do, mo, vo = _adam_slabs_call(s_out, w_out[0], m_w_out[0], v_w_out[0], "adam_w_out")
    g1, d1, m1, v1 = _adam_slabs_call(s_w1, w_mlp1[0], m_w_mlp1[0], v_w_mlp1[0], "adam_w_mlp1", transposed=True)
    g2, d2, m2, v2 = _adam_slabs_call(s_w2, w_mlp2[0], m_w_mlp2[0], v_w_mlp2[0], "adam_w_mlp2")

    small_like = [b_ada, conf_dw_w, conf_dw_b, conf_ln_g, conf_ln_b, sc_conv_w, g_final]
    two_d = lambda t: t.reshape(-1, t.shape[-1])
    small_g = [g_b_ada, g_dw_w, g_dw_b, g_ln_g, g_ln_b, g_sc_w, g_gf]
    sd, sm, sv = _adam_small_call(
        [two_d(t) for t in small_like], small_g,
        [two_d(t) for t in (m_b_ada, m_conf_dw_w, m_conf_dw_b, m_conf_ln_g, m_conf_ln_b, m_sc_conv_w, m_g_final)],
        [two_d(t) for t in (v_b_ada, v_conf_dw_w, v_conf_dw_b, v_conf_ln_g, v_conf_ln_b, v_sc_conv_w, v_g_final)])
    like = lambda parts: [p.reshape(w.shape) for p, w in zip(parts, small_like)]
    sg, sd, sm, sv = like(small_g), like(sd), like(sm), like(sv)

    def ordered(ada, small_list, w_in_, w_out_, w1_, w2_):
        b_ada_, dw_w_, dw_b_, ln_g_, ln_b_, sc_w_, gf_ = small_list
        return [ada[None], b_ada_, w_in_[None], dw_w_, dw_b_, ln_g_, ln_b_, sc_w_, w_out_[None], w1_[None], w2_[None], gf_]

    grads = ordered(g_ada, sg, gi, go, g1, g2)
    deltas = ordered(d_ada, sd, di, do, d1, d2)
    new_m = ordered(m_ada, sm, mi, mo, m1, m2)
    new_v = ordered(v_ada, sv, vi, vo, v1, v2)
    return (loss, grad_x, *grads, *deltas, *new_m, *new_v)
```

```python
import jax
import jax.numpy as jnp
from jax import lax
from jax.experimental import pallas as pl
from jax.experimental.pallas import tpu as pltpu

N_DEV = 8
RMS_EPS = 1e-6
ADAM_LR = 0.001
ADAM_B1 = 0.9
ADAM_B2 = 0.999
ADAM_EPS = 1e-08
ADAM_WD = 0.01
ADAM_STEP = 10

F32 = jnp.float32
BF16 = jnp.bfloat16
MESH = pl.DeviceIdType.MESH
VMEM = pltpu.VMEM
ANY = pl.ANY

HALO = 32
SHORT_HALO = 8
ROW_CHUNK = 32
SUBLANES = 8
V7X_VMEM_LIMIT = 56 * 1024 * 1024
MLP_VMEM_LIMIT = 48 * 1024 * 1024
MLP_TOKEN_TILE = 256
MIXER_FWD_TOKEN_TILE = 512


def _coords():
    return lax.axis_index("x"), lax.axis_index("y"), lax.axis_index("c")


def _dev_index():
    x, y, c = _coords()
    return 4 * x + 2 * y + c


def _peer(k):
    x, y, c = _coords()
    px = 1 - x if (k >> 2) & 1 else x
    py = 1 - y if (k >> 1) & 1 else y
    pc = 1 - c if k & 1 else c
    return (px, py, pc), 4 * px + 2 * py + pc


def _sigmoid(v):
    return jax.nn.sigmoid(v)


def _nt_dot(a, b):
    return lax.dot_general(a, b, (((1,), (1,)), ((), ())), preferred_element_type=F32)


def _nn_dot(a, b):
    return jnp.dot(a, b, preferred_element_type=F32)


def _tn_dot(a, b):
    return lax.dot_general(a, b, (((0,), (0,)), ((), ())), preferred_element_type=F32)


def _token_tile(seq):
    return 256 if seq % 256 == 0 else 64


GATHER_SEM_COLUMNS = 9
HALF_ROW_ALIGN = 16


class _TwoLevelGather:
    def __init__(self, srcs, dsts, send_sems, recv_sems, loc_sems):
        x, y, c = _coords()
        me = 4 * x + 2 * y + c
        sibling, along_x, along_y = (x, y, 1 - c), (1 - x, y, c), (x, 1 - y, c)
        from_x, from_y, diagonal = 4 * (1 - x) + 2 * y + c, 4 * x + 2 * (1 - y) + c, 4 * (1 - x) + 2 * (1 - y) + c

        def remote(src, dst, a, col, to):
            return pltpu.make_async_remote_copy(
                src_ref=src, dst_ref=dst, send_sem=send_sems.at[a, col], recv_sem=recv_sems.at[a, col],
                device_id=to, device_id_type=MESH)

        def onward(block, a, col, to):
            return remote(block, block, a, col, to)

        self.first, self.arrivals, self.second, self.halves, self.third = [], [], [], [], []
        for a, (src, dst) in enumerate(zip(srcs, dsts)):
            half = src.shape[0] // (2 * HALF_ROW_ALIGN) * HALF_ROW_ALIGN
            lower, upper = pl.ds(0, half), pl.ds(half, src.shape[0] - half)
            neighbours = [remote(src, dst.at[me], a, 1, along_x), remote(src, dst.at[me], a, 2, along_y)]
            self.first += [pltpu.make_async_copy(src, dst.at[me], loc_sems.at[a]),
                           remote(src, dst.at[me], a, 0, sibling)]
            self.arrivals += neighbours
            halves = [onward(dst.at[from_x, lower], a, 3, along_y), onward(dst.at[from_y, upper], a, 4, along_x)]
            self.halves += halves
            self.second += [[halves[0], onward(dst.at[from_x], a, 5, sibling)],
                            [halves[1], onward(dst.at[from_y], a, 6, sibling)]]
            self.third += [onward(dst.at[diagonal, lower], a, 7, sibling),
                           onward(dst.at[diagonal, upper], a, 8, sibling)]

    def start(self):
        for cp in self.first + self.arrivals:
            cp.start()

    def forward_arrivals(self):
        for arrival, sends in zip(self.arrivals, self.second):
            arrival.wait_recv()
            for cp in sends:
                cp.start()

    def forward_diagonal(self):
        for half, to_sibling in zip(self.halves, self.third):
            half.wait_recv()
            to_sibling.start()

    def finish(self):
        for cp in self.arrivals + self.halves:
            cp.wait_send()
        for cp in self.first + [sends[1] for sends in self.second] + self.third:
            cp.wait()


N_CHIPS = N_DEV // 2
PAIR_SUM_ROWS = 256


class _TwoLevelScatter:
    def __init__(self, partials, sums, mine, from_sibling, pair, first_send, first_recv, second_send, second_recv,
                 fetch_sems, local_sems):
        x, y, c = _coords()
        sibling = (x, y, 1 - c)
        chips = [(1 - x, y), (x, 1 - y), (1 - x, 1 - y)]
        self.my_chip = 2 * x + y
        self.mine, self.from_sibling, self.pair = mine, from_sibling, pair

        def remote(src, dst, send_sem, recv_sem, to):
            return pltpu.make_async_remote_copy(src_ref=src, dst_ref=dst, send_sem=send_sem, recv_sem=recv_sem,
                                                device_id=to, device_id_type=MESH)

        self.first, self.fetch, self.second, self.local = [], [], [], []
        for a in range(len(partials)):
            self.local.append(pltpu.make_async_copy(pair[a].at[self.my_chip], sums[a].at[self.my_chip],
                                                    local_sems.at[a]))
            self.first += [remote(partials[a].at[2 * q + (1 - c)], from_sibling[a].at[q],
                                  first_send.at[a, q], first_recv.at[a, q], sibling) for q in range(N_CHIPS)]
            self.fetch += [pltpu.make_async_copy(partials[a].at[2 * q + c], mine[a].at[q], fetch_sems.at[a, q])
                           for q in range(N_CHIPS)]
            self.second += [remote(pair[a].at[2 * cx + cy], sums[a].at[self.my_chip],
                                   second_send.at[a, j], second_recv.at[a, j], (cx, cy, c))
                            for j, (cx, cy) in enumerate(chips)]

    def start(self):
        for cp in self.first + self.fetch:
            cp.start()

    def pair_sums(self):
        for cp in self.first + self.fetch:
            cp.wait()
        for mine, theirs, both in zip(self.mine, self.from_sibling, self.pair):
            rows = mine.shape[1]
            for q in range(N_CHIPS):
                for r0 in range(0, rows, PAIR_SUM_ROWS):
                    part = pl.ds(r0, min(PAIR_SUM_ROWS, rows - r0))
                    both[q, part, :] = (mine[q, part, :].astype(F32) + theirs[q, part, :].astype(F32)).astype(BF16)
        for cp in self.second + self.local:
            cp.start()

    def finish(self):
        for cp in self.second + self.local:
            cp.wait()


def _scatter_scratch(partials):
    n = len(partials)
    zones = [pltpu.VMEM((N_CHIPS,) + p.shape[1:], BF16) for p in partials]
    return 3 * zones + [pltpu.SemaphoreType.DMA((n, N_CHIPS)), pltpu.SemaphoreType.DMA((n, N_CHIPS)),
                        pltpu.SemaphoreType.DMA((n, N_CHIPS - 1)), pltpu.SemaphoreType.DMA((n, N_CHIPS - 1)),
                        pltpu.SemaphoreType.DMA((n, N_CHIPS)), pltpu.SemaphoreType.DMA((n,))]


def _exchange_sems(n_arrays, columns=GATHER_SEM_COLUMNS):
    return [pltpu.SemaphoreType.DMA((n_arrays, columns)), pltpu.SemaphoreType.DMA((n_arrays, columns)),
            pltpu.SemaphoreType.DMA((n_arrays,))]


def _gather_call(c_pad, w_ada, b_ada_loc, small_loc, big_shards):
    n_big = len(big_shards)
    d_model = c_pad.shape[1]
    a_cols = w_ada.shape[1]

    def body(c_ref, wada_ref, bada_ref, small_ref, *rest):
        big_in = rest[:n_big]
        big_out = rest[n_big:2 * n_big]
        small_all, c_all, mod_rows = rest[2 * n_big:2 * n_big + 3]
        modcols, big_send, big_recv, loc_sem, s_send, s_recv = rest[2 * n_big + 3:]
        me = _dev_index()
        big = _TwoLevelGather(big_in, big_out, big_send, big_recv, loc_sem)
        big.start()

        small_all[me] = small_ref[...]
        c_all[me] = c_ref[...]
        first = []
        for k in range(1, N_DEV):
            peer, _ = _peer(k)
            for i, (src, dst) in enumerate(((small_ref, small_all), (c_ref, c_all))):
                cp = pltpu.make_async_remote_copy(
                    src_ref=src, dst_ref=dst.at[me],
                    send_sem=s_send.at[i, k - 1], recv_sem=s_recv.at[i, k - 1],
                    device_id=peer, device_id_type=MESH)
                cp.start()
                first.append(cp)
        for cp in first:
            cp.wait()

        c_rows = c_all[...].reshape(N_DEV * SUBLANES, d_model)
        c_act = c_rows * _sigmoid(c_rows)
        modcols[...] = _nn_dot(c_act.astype(BF16), wada_ref[...].astype(BF16)) + bada_ref[...]
        mod_rows[me] = modcols[pl.ds(pl.multiple_of(me * SUBLANES, SUBLANES), SUBLANES), :]
        second = []
        for k in range(1, N_DEV):
            peer, pidx = _peer(k)
            cp = pltpu.make_async_remote_copy(
                src_ref=modcols.at[pl.ds(pl.multiple_of(pidx * SUBLANES, SUBLANES), SUBLANES), :],
                dst_ref=mod_rows.at[me],
                send_sem=s_send.at[2, k - 1], recv_sem=s_recv.at[2, k - 1],
                device_id=peer, device_id_type=MESH)
            cp.start()
            second.append(cp)
        big.forward_arrivals()
        for cp in second:
            cp.wait()
        big.forward_diagonal()
        big.finish()

    out_shape = tuple(jax.ShapeDtypeStruct((N_DEV,) + s.shape, s.dtype) for s in big_shards) + (
        jax.ShapeDtypeStruct((N_DEV,) + small_loc.shape, F32),
        jax.ShapeDtypeStruct((N_DEV, SUBLANES, d_model), F32),
        jax.ShapeDtypeStruct((N_DEV, SUBLANES, a_cols), F32),
    )
    vm = pl.BlockSpec(memory_space=VMEM)
    hbm = pl.BlockSpec(memory_space=ANY)
    outs = pl.pallas_call(
        body, name="gather_weights_mod", out_shape=out_shape,
        in_specs=[vm, vm, vm, vm] + [hbm] * n_big,
        out_specs=tuple([hbm] * n_big + [vm, vm, vm]),
        scratch_shapes=[
            pltpu.VMEM((N_DEV * SUBLANES, a_cols), F32),
            *_exchange_sems(n_big),
            pltpu.SemaphoreType.DMA((3, N_DEV - 1)),
            pltpu.SemaphoreType.DMA((3, N_DEV - 1)),
        ],
        compiler_params=pltpu.CompilerParams(vmem_limit_bytes=V7X_VMEM_LIMIT),
    )(c_pad, w_ada, b_ada_loc, small_loc, *big_shards)
    return outs[:n_big], outs[n_big], outs[n_big + 1], outs[n_big + 2]


def _shifted_rows_count(tm):
    return tm + HALO - SUBLANES


def _fill_shifted(ext, shifted, tm):
    for s in range(1, SUBLANES):
        shifted[s - 1] = ext[s:s + _shifted_rows_count(tm), :]


def _shifted_rows(ext, shifted, start, rows):
    phase = start % SUBLANES
    aligned = start - phase
    if phase == 0:
        return ext[aligned:aligned + rows, :]
    return shifted[phase - 1, aligned:aligned + rows, :]


def _layer_norm_parts(a1):
    mu = jnp.mean(a1, axis=-1, keepdims=True)
    xc = a1 - mu
    rstd = lax.rsqrt(jnp.mean(xc * xc, axis=-1, keepdims=True) + RMS_EPS)
    return xc * rstd, rstd


def _mixer_fwd_call(x, mod, win_t, wout, cw, cp, later_shards):
    n_later = len(later_shards)
    bsz, seq, d_model = x.shape
    c_half = cw.shape[1]
    n_taps = 31
    d_in = win_t.shape[0] * win_t.shape[1]
    tm = MIXER_FWD_TOKEN_TILE if seq % MIXER_FWD_TOKEN_TILE == 0 else _token_tile(seq)
    nt = seq // tm
    arrivals_step, diagonal_step = (7 * bsz * nt) // 16, (11 * bsz * nt) // 16

    def body(x_ref, mod_ref, win_ref, wout_ref, cw_ref, cp_ref, *rest):
        shard_refs, rest = rest[:n_later], rest[n_later:]
        proj_ref, a1_ref, cv_ref, mixed_ref, y1_ref = rest[:5]
        gathered_refs, rest = rest[5:5 + n_later], rest[5 + n_later:]
        aext, qext, ashift, send_sems, recv_sems, loc_sems = rest
        b, t = pl.program_id(0), pl.program_id(1)

        step = b * nt + t

        @pl.when(step == 0)
        def _():
            _TwoLevelGather(shard_refs, gathered_refs, send_sems, recv_sems, loc_sems).start()

        @pl.when(step == arrivals_step)
        def _():
            _TwoLevelGather(shard_refs, gathered_refs, send_sems, recv_sems, loc_sems).forward_arrivals()

        @pl.when(step == diagonal_step)
        def _():
            _TwoLevelGather(shard_refs, gathered_refs, send_sems, recv_sems, loc_sems).forward_diagonal()

        xv = x_ref[...]
        sh1, sc1 = mod_ref[0:1, :], mod_ref[1:2, :]
        r1 = lax.rsqrt(jnp.mean(xv * xv, axis=-1, keepdims=True) + RMS_EPS)
        h1 = (xv * r1) * (1.0 + sc1) + sh1
        proj = _nt_dot(h1.astype(BF16), win_ref[...].reshape(d_in, d_model))
        proj_ref[...] = proj
        val, gate = proj[:, 0:c_half], proj[:, c_half:2 * c_half]
        s_b, s_c, s_h = proj[:, 2 * c_half:3 * c_half], proj[:, 3 * c_half:4 * c_half], proj[:, 4 * c_half:5 * c_half]

        @pl.when(t == 0)
        def _():
            aext[0:HALO, :] = jnp.zeros((HALO, c_half), F32)
            qext[0:SHORT_HALO, :] = jnp.zeros((SHORT_HALO, c_half), F32)

        @pl.when(t > 0)
        def _():
            aext[0:HALO, :] = aext[tm:tm + HALO, :]
            qext[0:SHORT_HALO, :] = qext[tm:tm + SHORT_HALO, :]
        aext[HALO:HALO + tm, :] = val * _sigmoid(gate)
        qext[SHORT_HALO:SHORT_HALO + tm, :] = s_c * s_h

        base = HALO - (n_taps - 1)
        _fill_shifted(aext, ashift, tm)
        for r0 in range(0, tm, ROW_CHUNK):
            acc = jnp.zeros((ROW_CHUNK, c_half), F32)
            for k in range(n_taps):
                acc = acc + cw_ref[k:k + 1, :] * _shifted_rows(aext, ashift, r0 + base + k, ROW_CHUNK)
            a1_ref[r0:r0 + ROW_CHUNK, :] = acc + cp_ref[0:1, :]
        sbase = SHORT_HALO - 2
        conv3 = cp_ref[3:4, :] * qext[sbase:sbase + tm, :]
        conv3 = conv3 + cp_ref[4:5, :] * qext[sbase + 1:sbase + 1 + tm, :]
        conv3 = conv3 + cp_ref[5:6, :] * qext[sbase + 2:sbase + 2 + tm, :]
        cv_ref[...] = conv3

        norm, _ = _layer_norm_parts(a1_ref[...])
        a2 = norm * cp_ref[1:2, :] + cp_ref[2:3, :]
        mixed = jnp.concatenate([a2 * _sigmoid(a2), s_b * conv3], axis=-1).astype(BF16)
        mixed_ref[...] = mixed
        y1 = _nn_dot(mixed, wout_ref[...].reshape(d_model, d_model))
        y1_ref[...] = y1

        @pl.when(step == bsz * nt - 1)
        def _():
            _TwoLevelGather(shard_refs, gathered_refs, send_sems, recv_sems, loc_sems).finish()

    hbm = pl.BlockSpec(memory_space=ANY)

    def tok(width):
        return pl.BlockSpec((None, tm, width), lambda b, t: (b, t, 0))

    def const(shape):
        return pl.BlockSpec(shape, lambda b, t: (0,) * len(shape))

    def resident(shape):
        return pl.BlockSpec(shape, lambda b, t: (0,) * len(shape), pipeline_mode=pl.Buffered(1))

    out_shape = (
        jax.ShapeDtypeStruct((bsz, seq, d_in), F32),
        jax.ShapeDtypeStruct((bsz, seq, c_half), F32),
        jax.ShapeDtypeStruct((bsz, seq, c_half), F32),
        jax.ShapeDtypeStruct((bsz, seq, d_model), BF16),
        jax.ShapeDtypeStruct((bsz, seq, d_model), F32),
    ) + tuple(jax.ShapeDtypeStruct((N_DEV,) + s.shape, s.dtype) for s in later_shards)
    outs = pl.pallas_call(
        body, name="mixer_fwd", out_shape=out_shape, grid=(bsz, nt),
        in_specs=[tok(d_model), pl.BlockSpec((None, 6, d_model), lambda b, t: (b, 0, 0)),
                  resident(win_t.shape), resident(wout.shape), const(cw.shape), const(cp.shape)] + [hbm] * n_later,
        out_specs=(tok(d_in), tok(c_half), tok(c_half), tok(d_model), tok(d_model)) + (hbm,) * n_later,
        scratch_shapes=[pltpu.VMEM((tm + HALO, c_half), F32), pltpu.VMEM((tm + SHORT_HALO, c_half), F32),
                        pltpu.VMEM((SUBLANES - 1, _shifted_rows_count(tm), c_half), F32)]
        + _exchange_sems(n_later),
        compiler_params=pltpu.CompilerParams(
            dimension_semantics=("arbitrary", "arbitrary"), vmem_limit_bytes=V7X_VMEM_LIMIT),
    )(x, mod, win_t, wout, cw, cp, *later_shards)
    return outs[:5], outs[5:]


def _mlp_call(x, y1, target, mod, w1, w2, g_final):
    bsz, seq, d_model = x.shape
    n_blk, f_blk, _ = w1.shape
    d_ff = n_blk * f_blk
    tm = MLP_TOKEN_TILE if seq % MLP_TOKEN_TILE == 0 else _token_tile(seq)
    nt = seq // tm

    def body(x_ref, y1_ref, tgt_ref, mod_ref, w1_ref, w2_ref, gf_ref,
             dx1_ref, h2_ref, dy2_ref, u_ref, dz_ref, dmod_ref, head_ref, dy1_ref, relu_scr):
        b, t = pl.program_id(0), pl.program_id(1)
        x1v = x_ref[...] + mod_ref[2:3, :] * y1_ref[...]
        sh2, sc2, g2 = mod_ref[3:4, :], mod_ref[4:5, :], mod_ref[5:6, :]
        gf = gf_ref[...]
        r2 = lax.rsqrt(jnp.mean(x1v * x1v, axis=-1, keepdims=True) + RMS_EPS)
        xn2 = x1v * r2
        h2 = (xn2 * (1.0 + sc2) + sh2).astype(BF16)
        h2_ref[...] = h2
        for j in range(n_blk):
            cols = slice(j * f_blk, (j + 1) * f_blk)
            rz = jnp.maximum(_nt_dot(h2, w1_ref[j]), 0.0)
            relu_scr[:, cols] = rz
            u_ref[:, cols] = (rz * rz).astype(BF16)
        y2 = _nn_dot(u_ref[...], w2_ref[...].reshape(d_ff, d_model))
        x2 = x1v + g2 * y2
        r3 = lax.rsqrt(jnp.mean(x2 * x2, axis=-1, keepdims=True) + RMS_EPS)
        xn3 = x2 * r3
        diff = xn3 * gf - tgt_ref[...]
        dout = diff * (1.0 / d_model)

        @pl.when(jnp.logical_and(b == 0, t == 0))
        def _():
            head_ref[...] = jnp.zeros(head_ref.shape, F32)

        @pl.when(t == 0)
        def _():
            dmod_ref[...] = jnp.zeros(dmod_ref.shape, F32)

        head_ref[0:1, :] += jnp.sum(dout * xn3, axis=0, keepdims=True)
        head_ref[1:2, :] += jnp.sum(diff * diff, axis=0, keepdims=True)
        dxn3 = dout * gf
        dx2 = r3 * (dxn3 - xn3 * jnp.mean(dxn3 * xn3, axis=-1, keepdims=True))
        dmod_ref[2:3, :] += jnp.sum(dx2 * y2, axis=0, keepdims=True)
        dy2 = (g2 * dx2).astype(BF16)
        dy2_ref[...] = dy2
        for j in range(n_blk):
            cols = slice(j * f_blk, (j + 1) * f_blk)
            dz_ref[:, cols] = (_nt_dot(dy2, w2_ref[j]) * (2.0 * relu_scr[:, cols])).astype(BF16)
        dh2 = _nn_dot(dz_ref[...], w1_ref[...].reshape(d_ff, d_model))
        dmod_ref[0:1, :] += jnp.sum(dh2, axis=0, keepdims=True)
        dmod_ref[1:2, :] += jnp.sum(dh2 * xn2, axis=0, keepdims=True)
        dxn2 = dh2 * (1.0 + sc2)
        dx1 = dx2 + r2 * (dxn2 - xn2 * jnp.mean(dxn2 * xn2, axis=-1, keepdims=True))
        dx1_ref[...] = dx1
        dy1_ref[...] = (mod_ref[2:3, :] * dx1).astype(BF16)
        dmod_ref[3:4, :] += jnp.sum(dx1 * y1_ref[...], axis=0, keepdims=True)

    def tok(width):
        return pl.BlockSpec((None, tm, width), lambda b, t: (b, t, 0))

    def const(shape):
        return pl.BlockSpec(shape, lambda b, t: (0,) * len(shape))

    def resident(shape):
        return pl.BlockSpec(shape, lambda b, t: (0,) * len(shape), pipeline_mode=pl.Buffered(1))

    out_shape = (
        jax.ShapeDtypeStruct((bsz, seq, d_model), F32),
        jax.ShapeDtypeStruct((bsz, seq, d_model), BF16),
        jax.ShapeDtypeStruct((bsz, seq, d_model), BF16),
        jax.ShapeDtypeStruct((bsz, seq, d_ff), BF16),
        jax.ShapeDtypeStruct((bsz, seq, d_ff), BF16),
        jax.ShapeDtypeStruct((bsz, SUBLANES, d_model), F32),
        jax.ShapeDtypeStruct((SUBLANES, d_model), F32),
        jax.ShapeDtypeStruct((bsz, seq, d_model), BF16),
    )
    return pl.pallas_call(
        body, name="mlp_fwd_bwd", out_shape=out_shape, grid=(bsz, nt),
        in_specs=[tok(d_model), tok(d_model), tok(d_model),
                  pl.BlockSpec((None, 6, d_model), lambda b, t: (b, 0, 0)),
                  resident(w1.shape), resident(w2.shape), const(g_final.shape)],
        out_specs=(tok(d_model), tok(d_model), tok(d_model), tok(d_ff), tok(d_ff),
                   pl.BlockSpec((None, SUBLANES, d_model), lambda b, t: (b, 0, 0)),
                   const((SUBLANES, d_model)), tok(d_model)),
        scratch_shapes=[pltpu.VMEM((tm, d_ff), F32)],
        compiler_params=pltpu.CompilerParams(
            dimension_semantics=("arbitrary", "arbitrary"), vmem_limit_bytes=MLP_VMEM_LIMIT),
    )(x, y1, target, mod, w1, w2, g_final)


def _mixer_bwd_call(dx1, x, proj, a1, cv, mod, win_t, wout, cw, cp):
    bsz, seq, d_model = x.shape
    c_half = cw.shape[1]
    n_taps = 31
    d_in = win_t.shape[0] * win_t.shape[1]
    tm = _token_tile(seq)
    nt = seq // tm

    def body(dx1_ref, x_ref, proj_ref, a1_ref, cv_ref, mod_ref, win_ref, wout_ref, cw_ref, cp_ref,
             gx_ref, dproj_ref, h1_ref, dmod_ref, cgrad_ref,
             dext, cext, a0_scr, da0_scr, tap_acc, row_acc, dshift):
        b, step = pl.program_id(0), pl.program_id(1)
        first = jnp.logical_and(b == 0, step == 0)
        last = jnp.logical_and(b == bsz - 1, step == nt - 1)
        dx1v = dx1_ref[...]
        xv = x_ref[...]
        sh1, sc1, g1 = mod_ref[0:1, :], mod_ref[1:2, :], mod_ref[2:3, :]

        @pl.when(first)
        def _():
            tap_acc[...] = jnp.zeros(tap_acc.shape, F32)
            row_acc[...] = jnp.zeros(row_acc.shape, F32)

        @pl.when(step == 0)
        def _():
            dmod_ref[...] = jnp.zeros(dmod_ref.shape, F32)
            dext[tm:tm + HALO, :] = jnp.zeros((HALO, c_half), F32)
            cext[tm:tm + SHORT_HALO, :] = jnp.zeros((SHORT_HALO, c_half), F32)

        @pl.when(step > 0)
        def _():
            dext[tm:tm + HALO, :] = dext[0:HALO, :]
            cext[tm:tm + SHORT_HALO, :] = cext[0:SHORT_HALO, :]

        dy1 = (g1 * dx1v).astype(BF16)
        dmixed = _nt_dot(dy1, wout_ref[...].reshape(d_model, d_model))
        d_a, d_s = dmixed[:, 0:c_half], dmixed[:, c_half:2 * c_half]

        val, gate = proj_ref[:, 0:c_half], proj_ref[:, c_half:2 * c_half]
        s_b = proj_ref[:, 2 * c_half:3 * c_half]
        s_c, s_h = proj_ref[:, 3 * c_half:4 * c_half], proj_ref[:, 4 * c_half:5 * c_half]

        d_sb = d_s * cv_ref[...]
        cext[0:tm, :] = d_s * s_b
        q = s_c * s_h
        dq = jnp.zeros((tm, c_half), F32)
        for k in range(3):
            shifted = cext[2 - k:2 - k + tm, :]
            dq = dq + cp_ref[3 + k:4 + k, :] * shifted
            row_acc[k:k + 1, :] += jnp.sum(q * shifted, axis=0, keepdims=True)
        d_sc, d_sh = dq * s_h, dq * s_c

        norm, rstd = _layer_norm_parts(a1_ref[...])
        ln_g = cp_ref[1:2, :]
        a2 = norm * ln_g + cp_ref[2:3, :]
        sg = _sigmoid(a2)
        d_a2 = d_a * (sg * (1.0 + a2 * (1.0 - sg)))
        row_acc[4:5, :] += jnp.sum(d_a2 * norm, axis=0, keepdims=True)
        row_acc[5:6, :] += jnp.sum(d_a2, axis=0, keepdims=True)
        d_n = d_a2 * ln_g
        d_a1 = rstd * (d_n - jnp.mean(d_n, axis=-1, keepdims=True)
                       - norm * jnp.mean(d_n * norm, axis=-1, keepdims=True))
        row_acc[3:4, :] += jnp.sum(d_a1, axis=0, keepdims=True)
        dext[0:tm, :] = d_a1
        sig_g = _sigmoid(gate)
        a0_scr[...] = val * sig_g

        _fill_shifted(dext, dshift, tm)
        for r0 in range(0, tm, ROW_CHUNK):
            a0c = a0_scr[r0:r0 + ROW_CHUNK, :]
            acc = jnp.zeros((ROW_CHUNK, c_half), F32)
            for k in range(n_taps):
                shifted = _shifted_rows(dext, dshift, r0 + (n_taps - 1) - k, ROW_CHUNK)
                acc = acc + cw_ref[k:k + 1, :] * shifted
                prod = a0c * shifted
                part = prod[0:SUBLANES, :]
                for g in range(1, ROW_CHUNK // SUBLANES):
                    part = part + prod[g * SUBLANES:(g + 1) * SUBLANES, :]
                tap_acc[k * SUBLANES:(k + 1) * SUBLANES, :] += part
            da0_scr[r0:r0 + ROW_CHUNK, :] = acc
        d_a0 = da0_scr[...]
        d_val = d_a0 * sig_g
        d_gate = d_a0 * val * sig_g * (1.0 - sig_g)

        dproj = jnp.concatenate([d_val, d_gate, d_sb, d_sc, d_sh], axis=-1).astype(BF16)
        dproj_ref[...] = dproj
        dh1 = _nn_dot(dproj, win_ref[...].reshape(d_in, d_model))
        r1 = lax.rsqrt(jnp.mean(xv * xv, axis=-1, keepdims=True) + RMS_EPS)
        xn1 = xv * r1
        h1_ref[...] = (xn1 * (1.0 + sc1) + sh1).astype(BF16)
        dmod_ref[0:1, :] += jnp.sum(dh1, axis=0, keepdims=True)
        dmod_ref[1:2, :] += jnp.sum(dh1 * xn1, axis=0, keepdims=True)
        dxn1 = dh1 * (1.0 + sc1)
        gx_ref[...] = dx1v + r1 * (dxn1 - xn1 * jnp.mean(dxn1 * xn1, axis=-1, keepdims=True))

        @pl.when(last)
        def _():
            taps = jnp.sum(tap_acc[...].reshape(HALO, SUBLANES, c_half), axis=1)
            cgrad_ref[0:HALO, :] = taps
            cgrad_ref[HALO:HALO + SUBLANES, :] = row_acc[...]

    def tok(width):
        return pl.BlockSpec((None, tm, width), lambda b, s: (b, nt - 1 - s, 0))

    def const(shape):
        return pl.BlockSpec(shape, lambda b, s: (0,) * len(shape))

    mod_spec = pl.BlockSpec((None, 6, d_model), lambda b, s: (b, 0, 0))
    out_shape = (
        jax.ShapeDtypeStruct((bsz, seq, d_model), F32),
        jax.ShapeDtypeStruct((bsz, seq, d_in), BF16),
        jax.ShapeDtypeStruct((bsz, seq, d_model), BF16),
        jax.ShapeDtypeStruct((bsz, SUBLANES, d_model), F32),
        jax.ShapeDtypeStruct((HALO + SUBLANES, c_half), F32),
    )
    return pl.pallas_call(
        body, name="mixer_bwd", out_shape=out_shape, grid=(bsz, nt),
        in_specs=[tok(d_model), tok(d_model), tok(d_in), tok(c_half), tok(c_half), mod_spec,
                  const(win_t.shape), const(wout.shape), const(cw.shape), const(cp.shape)],
        out_specs=(tok(d_model), tok(d_in), tok(d_model),
                   pl.BlockSpec((None, SUBLANES, d_model), lambda b, s: (b, 0, 0)),
                   const((HALO + SUBLANES, c_half))),
        scratch_shapes=[
            pltpu.VMEM((tm + HALO, c_half), F32), pltpu.VMEM((tm + SHORT_HALO, c_half), F32),
            pltpu.VMEM((tm, c_half), F32), pltpu.VMEM((tm, c_half), F32),
            pltpu.VMEM((HALO * SUBLANES, c_half), F32), pltpu.VMEM((SUBLANES, c_half), F32),
            pltpu.VMEM((SUBLANES - 1, _shifted_rows_count(tm), c_half), F32),
        ],
        compiler_params=pltpu.CompilerParams(
            dimension_semantics=("arbitrary", "arbitrary"), vmem_limit_bytes=V7X_VMEM_LIMIT),
    )(dx1, x, proj, a1, cv, mod, win_t, wout, cw, cp)


def _largest_divisor(n, cap, multiple):
    best = None
    for cand in range(multiple, min(n, cap) + 1, multiple):
        if n % cand == 0:
            best = cand
    return best if best is not None else n


WGRAD_TOKENS_PER_STEP = 2048


def _wgrad_call(a, b, name, scatter=()):
    n_sc = len(scatter)
    tokens, m_dim = a.shape
    n_dim = b.shape[1]
    bk = _largest_divisor(tokens, WGRAD_TOKENS_PER_STEP, 128)
    n_k = tokens // bk
    bm = _largest_divisor(m_dim, 1024, m_dim // N_DEV)
    bn = n_dim
    out_shape = jax.ShapeDtypeStruct((m_dim, n_dim), BF16)
    out_spec = pl.BlockSpec((bm, bn), lambda i, j, k: (i, j))

    grid = (m_dim // bm, n_dim // bn, n_k)
    n_steps = grid[0] * grid[1] * grid[2]
    pair_step = min(max(1, n_steps // 5), n_steps - 1)

    def body(a_ref, b_ref, *rest):
        part_refs, o_ref, sum_refs, acc = rest[:n_sc], rest[n_sc], rest[n_sc + 1:2 * n_sc + 1], rest[2 * n_sc + 1]
        k = pl.program_id(2)
        step = (pl.program_id(0) * grid[1] + pl.program_id(1)) * n_k + k

        def exchange():
            extra = rest[2 * n_sc + 2:]
            return _TwoLevelScatter(part_refs, sum_refs, extra[:n_sc], extra[n_sc:2 * n_sc], extra[2 * n_sc:3 * n_sc],
                                    *extra[3 * n_sc:])

        if n_sc:
            @pl.when(step == 0)
            def _():
                exchange().start()

            @pl.when(step == pair_step)
            def _():
                exchange().pair_sums()

        @pl.when(k == 0)
        def _():
            acc[...] = jnp.zeros(acc.shape, F32)

        acc[...] += _tn_dot(a_ref[...], b_ref[...])

        @pl.when(k == n_k - 1)
        def _():
            o_ref[...] = acc[...].astype(BF16)

        if n_sc:
            @pl.when(step == n_steps - 1)
            def _():
                exchange().finish()

    hbm = pl.BlockSpec(memory_space=ANY)
    sum_shapes = tuple(jax.ShapeDtypeStruct((N_CHIPS,) + p.shape[1:], BF16) for p in scatter)
    outs = pl.pallas_call(
        body, name=name, out_shape=(out_shape,) + sum_shapes, grid=grid,
        in_specs=[pl.BlockSpec((bk, bm), lambda i, j, k: (k, i)), pl.BlockSpec((bk, bn), lambda i, j, k: (k, j))]
        + [hbm] * n_sc,
        out_specs=(out_spec,) + (hbm,) * n_sc,
        scratch_shapes=[pltpu.VMEM((bm, bn), F32)] + (_scatter_scratch(scatter) if n_sc else []),
        compiler_params=pltpu.CompilerParams(
            dimension_semantics=("arbitrary", "arbitrary", "arbitrary"), vmem_limit_bytes=V7X_VMEM_LIMIT),
    )(a, b, *scatter)
    out = outs[0].reshape(N_DEV, m_dim // N_DEV, n_dim)
    return (out, list(outs[1:])) if n_sc else out


def _tail_scatter_call(partial, small, n_grad_rows, loss_rows, bias_rows, loss_scale):
    width = small.shape[1]
    lo, hi = loss_rows
    b0, b1, b2 = bias_rows
    nb = b1 - b0

    def body(g_ref, small_ref, out_ref, small_all, sum_ref, extra_ref, mine, from_sibling, pair,
             p1_send, p1_recv, p2_send, p2_recv, p_fetch, p_loc, s_send, s_recv, s_loc):
        gather = _TwoLevelGather([small_ref], [small_all], s_send, s_recv, s_loc)
        scatter = _TwoLevelScatter([g_ref], [out_ref], [mine], [from_sibling], [pair],
                                   p1_send, p1_recv, p2_send, p2_recv, p_fetch, p_loc)
        gather.start()
        scatter.start()
        scatter.pair_sums()
        gather.forward_arrivals()
        gather.forward_diagonal()
        gather.finish()
        tot = small_all[0]
        for k in range(1, N_DEV):
            tot = tot + small_all[k]
        sum_ref[...] = tot[0:n_grad_rows, :]
        extra_ref[0:nb, :] = tot[b0:b1, :] + tot[b1:b2, :]
        head = tot[n_grad_rows - 2 * SUBLANES:n_grad_rows, :]
        rows_id = lax.broadcasted_iota(jnp.int32, head.shape, 0) + (n_grad_rows - 2 * SUBLANES)
        sq = jnp.where(jnp.logical_and(rows_id >= lo, rows_id < hi), head, 0.0)
        extra_ref[nb:nb + SUBLANES, :] = jnp.zeros((SUBLANES, width), F32) + jnp.sum(sq) * loss_scale
        scatter.finish()

    vm = pl.BlockSpec(memory_space=VMEM)
    return pl.pallas_call(
        body, name="scatter_tail",
        out_shape=(jax.ShapeDtypeStruct((N_CHIPS,) + partial.shape[1:], BF16),
                   jax.ShapeDtypeStruct((N_DEV,) + small.shape, F32),
                   jax.ShapeDtypeStruct((n_grad_rows, width), F32),
                   jax.ShapeDtypeStruct((nb + SUBLANES, width), F32)),
        in_specs=[pl.BlockSpec(memory_space=ANY), vm], out_specs=(vm, vm, vm, vm),
        scratch_shapes=_scatter_scratch([partial]) + _exchange_sems(1),
        compiler_params=pltpu.CompilerParams(vmem_limit_bytes=V7X_VMEM_LIMIT),
    )(partial, small)


def _adamw(w, g, m, v):
    m2 = ADAM_B1 * m + (1.0 - ADAM_B1) * g
    v2 = ADAM_B2 * v + (1.0 - ADAM_B2) * (g * g)
    m_hat = m2 / (1.0 - ADAM_B1 ** ADAM_STEP)
    v_hat = v2 / (1.0 - ADAM_B2 ** ADAM_STEP)
    delta = -ADAM_LR * (m_hat / (jnp.sqrt(v_hat) + ADAM_EPS) + ADAM_WD * w)
    return delta, m2, v2


def _adam_slabs_call(slabs, w, m, v, name, transposed=False):
    rows, cols = w.shape
    n_slabs, slab_rows, slab_cols = slabs.shape
    tr = _largest_divisor(slab_rows, 256, 128 if transposed else 2 * SUBLANES)

    def body(s_ref, w_ref, m_ref, v_ref, g_ref, d_ref, m2_ref, v2_ref):
        g = s_ref[0].astype(F32)
        for k in range(1, n_slabs):
            g = g + s_ref[k].astype(F32)
        if transposed:
            g = g.T
        delta, m2, v2 = _adamw(w_ref[...], g, m_ref[...], v_ref[...])
        g_ref[...] = g
        d_ref[...] = delta
        m2_ref[...] = m2
        v2_ref[...] = v2

    tile = pl.BlockSpec((rows, tr), lambda i: (0, i)) if transposed else pl.BlockSpec((tr, cols), lambda i: (i, 0))
    shp = jax.ShapeDtypeStruct((rows, cols), F32)
    return pl.pallas_call(
        body, name=name, out_shape=(shp, shp, shp, shp), grid=(slab_rows // tr,),
        in_specs=[pl.BlockSpec((n_slabs, tr, slab_cols), lambda i: (0, i, 0)), tile, tile, tile],
        out_specs=(tile, tile, tile, tile),
        compiler_params=pltpu.CompilerParams(dimension_semantics=("arbitrary",), vmem_limit_bytes=V7X_VMEM_LIMIT),
    )(slabs, w, m, v)


def _adam_ada_call(c_rows, dmod_cols, w, m, v):
    rows, cols = w.shape
    n_rows = c_rows.shape[0]
    tr = _largest_divisor(rows, 256, 128)

    def body(c_ref, dm_ref, w_ref, m_ref, v_ref, g_ref, d_ref, m2_ref, v2_ref):
        cv = c_ref[...]
        c_act = (cv * _sigmoid(cv)).astype(BF16)
        g = _tn_dot(c_act, dm_ref[...].astype(BF16))
        delta, m2, v2 = _adamw(w_ref[...], g, m_ref[...], v_ref[...])
        g_ref[...] = g
        d_ref[...] = delta
        m2_ref[...] = m2
        v2_ref[...] = v2

    tile = pl.BlockSpec((tr, cols), lambda i: (i, 0))
    shp = jax.ShapeDtypeStruct((rows, cols), F32)
    return pl.pallas_call(
        body, name="adam_w_ada", out_shape=(shp, shp, shp, shp), grid=(rows // tr,),
        in_specs=[pl.BlockSpec((n_rows, tr), lambda i: (0, i)), pl.BlockSpec((n_rows, cols), lambda i: (0, 0)),
                  tile, tile, tile],
        out_specs=(tile, tile, tile, tile),
        compiler_params=pltpu.CompilerParams(dimension_semantics=("arbitrary",), vmem_limit_bytes=V7X_VMEM_LIMIT),
    )(c_rows, dmod_cols, w, m, v)


def _adam_small_call(ws, gs, ms, vs):
    n = len(ws)

    def body(*refs):
        w_refs, g_refs, m_refs, v_refs = refs[:n], refs[n:2 * n], refs[2 * n:3 * n], refs[3 * n:4 * n]
        d_refs, m2_refs, v2_refs = refs[4 * n:5 * n], refs[5 * n:6 * n], refs[6 * n:7 * n]
        for i in range(n):
            delta, m2, v2 = _adamw(w_refs[i][...], g_refs[i][...], m_refs[i][...], v_refs[i][...])
            d_refs[i][...] = delta
            m2_refs[i][...] = m2
            v2_refs[i][...] = v2

    vm = pl.BlockSpec(memory_space=VMEM)
    shapes = tuple(jax.ShapeDtypeStruct(w.shape, F32) for w in ws)
    outs = pl.pallas_call(body, name="adam_small", out_shape=shapes * 3,
                          in_specs=[vm] * (4 * n), out_specs=(vm,) * (3 * n))(*ws, *gs, *ms, *vs)
    return outs[:n], outs[n:2 * n], outs[2 * n:]


def kernel(x, c, w_ada, b_ada, w_in, conf_dw_w, conf_dw_b, conf_ln_g, conf_ln_b, sc_conv_w, w_out, w_mlp1, w_mlp2, g_final, loss_target, m_w_ada, m_b_ada, m_w_in, m_conf_dw_w, m_conf_dw_b, m_conf_ln_g, m_conf_ln_b, m_sc_conv_w, m_w_out, m_w_mlp1, m_w_mlp2, m_g_final, v_w_ada, v_b_ada, v_w_in, v_conf_dw_w, v_conf_dw_b, v_conf_ln_g, v_conf_ln_b, v_sc_conv_w, v_w_out, v_w_mlp1, v_w_mlp2, v_g_final):
    bsz, seq, d_model = x.shape
    c_half = conf_dw_b.shape[-1]
    n_taps = conf_dw_w.shape[1]
    cc = conf_dw_w.shape[-1]
    a_cols = w_ada.shape[-1]
    tokens = bsz * seq
    me = _dev_index()

    c_pad = jnp.pad(c, ((0, SUBLANES - bsz), (0, 0)))
    b_ada_loc = lax.dynamic_slice(b_ada, (0, me * a_cols), (1, a_cols))
    small_loc = jnp.zeros((HALO, 128), F32)
    small_loc = small_loc.at[:n_taps, :cc].set(conf_dw_w[0]).at[:3, cc:2 * cc].set(sc_conv_w[0])
    (win_t, wout_all), small_all, c_all, mod_rows = _gather_call(
        c_pad, w_ada[0], b_ada_loc, small_loc, [w_in[0].T.astype(BF16), w_out[0].astype(BF16)])
    cw = small_all[:, :, :cc].transpose(1, 0, 2).reshape(HALO, c_half)
    scw = small_all[:, :3, cc:2 * cc].transpose(1, 0, 2).reshape(3, c_half)
    cp = jnp.concatenate([conf_dw_b, conf_ln_g, conf_ln_b, scw, jnp.zeros((2, c_half), F32)], axis=0)
    mod = mod_rows[:, :bsz, :].transpose(1, 0, 2).reshape(bsz, 6, d_model)

    flat = lambda t: t.reshape(tokens, t.shape[-1])
    (proj, a1, cv, mixed, y1), (w1_all, w2_all) = _mixer_fwd_call(
        x, mod, win_t, wout_all, cw, cp, [w_mlp1[0].T.astype(BF16), w_mlp2[0].astype(BF16)])
    dx1, h2, dy2, u, dz, dmod2, head, dy1 = _mlp_call(
        x, y1, loss_target, mod, w1_all, w2_all, g_final.reshape(1, d_model))
    g_out = _wgrad_call(flat(mixed), flat(dy1), "wgrad_out")
    g_w1 = _wgrad_call(flat(dz), flat(h2), "wgrad_mlp1")
    g_w2, (s_w1, s_out) = _wgrad_call(flat(u), flat(dy2), "wgrad_mlp2", scatter=[g_w1, g_out])
    grad_x, dproj, h1, dmod1, cgrad = _mixer_bwd_call(dx1, x, proj, a1, cv, mod, win_t, wout_all, cw, cp)
    g_in_t, (s_w2,) = _wgrad_call(flat(dproj), flat(h1), "wgrad_in", scatter=[g_w2])

    dmod = jnp.concatenate([dmod1[:, :2, :], dmod2[:, 3:4, :], dmod2[:, :3, :]], axis=1)
    n_cg = cgrad.shape[0]
    per_b = 6 * d_model // c_half
    per_b_pad = -(-per_b // SUBLANES) * SUBLANES
    dmod_rows = jnp.pad(dmod.reshape(bsz, per_b, c_half), ((0, 0), (0, per_b_pad - per_b), (0, 0)))
    small = jnp.concatenate([
        cgrad,
        head.reshape(2 * SUBLANES, c_half),
        dmod_rows.reshape(bsz * per_b_pad, c_half),
    ], axis=0)
    n_head = n_cg + 2 * SUBLANES
    s_in, gathered, sums, extra = _tail_scatter_call(
        g_in_t, small, n_head, (n_cg + 2, n_cg + 4), (n_head, n_head + per_b_pad, n_head + 2 * per_b_pad),
        0.5 / d_model)

    loss = extra[per_b_pad, 0]
    g_b_ada = extra[:per_b].reshape(1, 6 * d_model)
    g_dw_w = lax.dynamic_slice(sums[:n_taps], (0, me * cc), (n_taps, cc))
    g_sc_w = lax.dynamic_slice(sums[HALO:HALO + 3], (0, me * cc), (3, cc))
    g_dw_b, g_ln_g, g_ln_b = sums[HALO + 3:HALO + 4], sums[HALO + 4:HALO + 5], sums[HALO + 5:HALO + 6]
    g_gf = sums[n_cg:n_cg + 2].reshape(1, d_model)

    dmod_all = gathered[:, n_head:, :].reshape(N_DEV, bsz, per_b_pad, c_half)[:, :, :per_b, :]
    dmod_all = dmod_all.reshape(N_DEV, bsz, 6 * d_model)
    dmod_cols = lax.dynamic_slice(dmod_all, (0, 0, me * a_cols), (N_DEV, bsz, a_cols))
    dmod_cols = jnp.pad(dmod_cols, ((0, 0), (0, SUBLANES - bsz), (0, 0))).reshape(N_DEV * SUBLANES, a_cols)
    c_rows = c_all.reshape(N_DEV * SUBLANES, d_model)
    g_ada, d_ada, m_ada, v_ada = _adam_ada_call(c_rows, dmod_cols, w_ada[0], m_w_ada[0], v_w_ada[0])

    gi, di, mi, vi = _adam_slabs_call(s_in, w_in[0].T, m_w_in[0].T, v_w_in[0].T, "adam_w_in")
    gi, di, mi, vi = gi.T, di.T, mi.T, vi.T
    go, do, mo, vo = _adam_slabs_call(s_out, w_out[0], m_w_out[0], v_w_out[0], "adam_w_out")
    g1, d1, m1, v1 = _adam_slabs_call(s_w1, w_mlp1[0], m_w_mlp1[0], v_w_mlp1[0], "adam_w_mlp1", transposed=True)
    g2, d2, m2, v2 = _adam_slabs_call(s_w2, w_mlp2[0], m_w_mlp2[0], v_w_mlp2[0], "adam_w_mlp2")

    small_like = [b_ada, conf_dw_w, conf_dw_b, conf_ln_g, conf_ln_b, sc_conv_w, g_final]
    two_d = lambda t: t.reshape(-1, t.shape[-1])
    small_g = [g_b_ada, g_dw_w, g_dw_b, g_ln_g, g_ln_b, g_sc_w, g_gf]
    sd, sm, sv = _adam_small_call(
        [two_d(t) for t in small_like], small_g,
        [two_d(t) for t in (m_b_ada, m_conf_dw_w, m_conf_dw_b, m_conf_ln_g, m_conf_ln_b, m_sc_conv_w, m_g_final)],
        [two_d(t) for t in (v_b_ada, v_conf_dw_w, v_conf_dw_b, v_conf_ln_g, v_conf_ln_b, v_sc_conv_w, v_g_final)])
    like = lambda parts: [p.reshape(w.shape) for p, w in zip(parts, small_like)]
    sg, sd, sm, sv = like(small_g), like(sd), like(sm), like(sv)

    def ordered(ada, small_list, w_in_, w_out_, w1_, w2_):
        b_ada_, dw_w_, dw_b_, ln_g_, ln_b_, sc_w_, gf_ = small_list
        return [ada[None], b_ada_, w_in_[None], dw_w_, dw_b_, ln_g_, ln_b_, sc_w_, w_out_[None], w1_[None], w2_[None], gf_]

    grads = ordered(g_ada, sg, gi, go, g1, g2)
    deltas = ordered(d_ada, sd, di, do, d1, d2)
    new_m = ordered(m_ada, sm, mi, mo, m1, m2)
    new_v = ordered(v_ada, sv, vi, vo, v1, v2)
    return (loss, grad_x, *grads, *deltas, *new_m, *new_v)
```

```python
import jax
import jax.numpy as jnp
from jax import lax
from jax.experimental import pallas as pl
from jax.experimental.pallas import tpu as pltpu

N_DEV = 8
RMS_EPS = 1e-6
ADAM_LR = 0.001
ADAM_B1 = 0.9
ADAM_B2 = 0.999
ADAM_EPS = 1e-08
ADAM_WD = 0.01
ADAM_STEP = 10

F32 = jnp.float32
BF16 = jnp.bfloat16
MESH = pl.DeviceIdType.MESH
VMEM = pltpu.VMEM
ANY = pl.ANY

HALO = 32
SHORT_HALO = 8
ROW_CHUNK = 32
SUBLANES = 8
V7X_VMEM_LIMIT = 56 * 1024 * 1024
MLP_VMEM_LIMIT = 48 * 1024 * 1024
MLP_TOKEN_TILE = 256
MIXER_FWD_TOKEN_TILE = 512


def _coords():
    return lax.axis_index("x"), lax.axis_index("y"), lax.axis_index("c")


def _dev_index():
    x, y, c = _coords()
    return 4 * x + 2 * y + c


def _peer(k):
    x, y, c = _coords()
    px = 1 - x if (k >> 2) & 1 else x
    py = 1 - y if (k >> 1) & 1 else y
    pc = 1 - c if k & 1 else c
    return (px, py, pc), 4 * px + 2 * py + pc


def _sigmoid(v):
    return jax.nn.sigmoid(v)


def _nt_dot(a, b):
    return lax.dot_general(a, b, (((1,), (1,)), ((), ())), preferred_element_type=F32)


def _nn_dot(a, b):
    return jnp.dot(a, b, preferred_element_type=F32)


def _tn_dot(a, b):
    return lax.dot_general(a, b, (((0,), (0,)), ((), ())), preferred_element_type=F32)


def _token_tile(seq):
    return 256 if seq % 256 == 0 else 64


GATHER_SEM_COLUMNS = 9
HALF_ROW_ALIGN = 16


class _TwoLevelGather:
    def __init__(self, srcs, dsts, send_sems, recv_sems, loc_sems):
        x, y, c = _coords()
        me = 4 * x + 2 * y + c
        sibling, along_x, along_y = (x, y, 1 - c), (1 - x, y, c), (x, 1 - y, c)
        from_x, from_y, diagonal = 4 * (1 - x) + 2 * y + c, 4 * x + 2 * (1 - y) + c, 4 * (1 - x) + 2 * (1 - y) + c

        def remote(src, dst, a, col, to):
            return pltpu.make_async_remote_copy(
                src_ref=src, dst_ref=dst, send_sem=send_sems.at[a, col], recv_sem=recv_sems.at[a, col],
                device_id=to, device_id_type=MESH)

        def onward(block, a, col, to):
            return remote(block, block, a, col, to)

        self.first, self.arrivals, self.second, self.halves, self.third = [], [], [], [], []
        for a, (src, dst) in enumerate(zip(srcs, dsts)):
            half = src.shape[0] // (2 * HALF_ROW_ALIGN) * HALF_ROW_ALIGN
            lower, upper = pl.ds(0, half), pl.ds(half, src.shape[0] - half)
            neighbours = [remote(src, dst.at[me], a, 1, along_x), remote(src, dst.at[me], a, 2, along_y)]
            self.first += [pltpu.make_async_copy(src, dst.at[me], loc_sems.at[a]),
                           remote(src, dst.at[me], a, 0, sibling)]
            self.arrivals += neighbours
            halves = [onward(dst.at[from_x, lower], a, 3, along_y), onward(dst.at[from_y, upper], a, 4, along_x)]
            self.halves += halves
            self.second += [[halves[0], onward(dst.at[from_x], a, 5, sibling)],
                            [halves[1], onward(dst.at[from_y], a, 6, sibling)]]
            self.third += [onward(dst.at[diagonal, lower], a, 7, sibling),
                           onward(dst.at[diagonal, upper], a, 8, sibling)]

    def start(self):
        for cp in self.first + self.arrivals:
            cp.start()

    def forward_arrivals(self):
        for arrival, sends in zip(self.arrivals, self.second):
            arrival.wait_recv()
            for cp in sends:
                cp.start()

    def forward_diagonal(self):
        for half, to_sibling in zip(self.halves, self.third):
            half.wait_recv()
            to_sibling.start()

    def finish(self):
        for cp in self.arrivals + self.halves:
            cp.wait_send()
        for cp in self.first + [sends[1] for sends in self.second] + self.third:
            cp.wait()


N_CHIPS = N_DEV // 2
PAIR_SUM_ROWS = 256


class _TwoLevelScatter:
    def __init__(self, partials, sums, mine, from_sibling, pair, first_send, first_recv, second_send, second_recv,
                 fetch_sems, local_sems):
        x, y, c = _coords()
        sibling = (x, y, 1 - c)
        chips = [(1 - x, y), (x, 1 - y), (1 - x, 1 - y)]
        self.my_chip = 2 * x + y
        self.mine, self.from_sibling, self.pair = mine, from_sibling, pair

        def remote(src, dst, send_sem, recv_sem, to):
            return pltpu.make_async_remote_copy(src_ref=src, dst_ref=dst, send_sem=send_sem, recv_sem=recv_sem,
                                                device_id=to, device_id_type=MESH)

        self.first, self.fetch, self.second, self.local = [], [], [], []
        for a in range(len(partials)):
            self.local.append(pltpu.make_async_copy(pair[a].at[self.my_chip], sums[a].at[self.my_chip],
                                                    local_sems.at[a]))
            self.first += [remote(partials[a].at[2 * q + (1 - c)], from_sibling[a].at[q],
                                  first_send.at[a, q], first_recv.at[a, q], sibling) for q in range(N_CHIPS)]
            self.fetch += [pltpu.make_async_copy(partials[a].at[2 * q + c], mine[a].at[q], fetch_sems.at[a, q])
                           for q in range(N_CHIPS)]
            self.second += [remote(pair[a].at[2 * cx + cy], sums[a].at[self.my_chip],
                                   second_send.at[a, j], second_recv.at[a, j], (cx, cy, c))
                            for j, (cx, cy) in enumerate(chips)]

    def start(self):
        for cp in self.first + self.fetch:
            cp.start()

    def pair_sums(self):
        for cp in self.first + self.fetch:
            cp.wait()
        for mine, theirs, both in zip(self.mine, self.from_sibling, self.pair):
            rows = mine.shape[1]
            for q in range(N_CHIPS):
                for r0 in range(0, rows, PAIR_SUM_ROWS):
                    part = pl.ds(r0, min(PAIR_SUM_ROWS, rows - r0))
                    both[q, part, :] = (mine[q, part, :].astype(F32) + theirs[q, part, :].astype(F32)).astype(BF16)
        for cp in self.second + self.local:
            cp.start()

    def finish(self):
        for cp in self.second + self.local:
            cp.wait()


def _scatter_scratch(partials):
    n = len(partials)
    zones = [pltpu.VMEM((N_CHIPS,) + p.shape[1:], BF16) for p in partials]
    return 3 * zones + [pltpu.SemaphoreType.DMA((n, N_CHIPS)), pltpu.SemaphoreType.DMA((n, N_CHIPS)),
                        pltpu.SemaphoreType.DMA((n, N_CHIPS - 1)), pltpu.SemaphoreType.DMA((n, N_CHIPS - 1)),
                        pltpu.SemaphoreType.DMA((n, N_CHIPS)), pltpu.SemaphoreType.DMA((n,))]


def _exchange_sems(n_arrays, columns=GATHER_SEM_COLUMNS):
    return [pltpu.SemaphoreType.DMA((n_arrays, columns)), pltpu.SemaphoreType.DMA((n_arrays, columns)),
            pltpu.SemaphoreType.DMA((n_arrays,))]


def _gather_call(c_pad, w_ada, b_ada_loc, small_loc, big_shards):
    n_big = len(big_shards)
    d_model = c_pad.shape[1]
    a_cols = w_ada.shape[1]

    def body(c_ref, wada_ref, bada_ref, small_ref, *rest):
        big_in = rest[:n_big]
        big_out = rest[n_big:2 * n_big]
        small_all, c_all, mod_rows = rest[2 * n_big:2 * n_big + 3]
        modcols, big_send, big_recv, loc_sem, s_send, s_recv = rest[2 * n_big + 3:]
        me = _dev_index()
        big = _TwoLevelGather(big_in, big_out, big_send, big_recv, loc_sem)
        big.start()

        small_all[me] = small_ref[...]
        c_all[me] = c_ref[...]
        first = []
        for k in range(1, N_DEV):
            peer, _ = _peer(k)
            for i, (src, dst) in enumerate(((small_ref, small_all), (c_ref, c_all))):
                cp = pltpu.make_async_remote_copy(
                    src_ref=src, dst_ref=dst.at[me],
                    send_sem=s_send.at[i, k - 1], recv_sem=s_recv.at[i, k - 1],
                    device_id=peer, device_id_type=MESH)
                cp.start()
                first.append(cp)
        for cp in first:
            cp.wait()

        c_rows = c_all[...].reshape(N_DEV * SUBLANES, d_model)
        c_act = c_rows * _sigmoid(c_rows)
        modcols[...] = _nn_dot(c_act.astype(BF16), wada_ref[...].astype(BF16)) + bada_ref[...]
        mod_rows[me] = modcols[pl.ds(pl.multiple_of(me * SUBLANES, SUBLANES), SUBLANES), :]
        second = []
        for k in range(1, N_DEV):
            peer, pidx = _peer(k)
            cp = pltpu.make_async_remote_copy(
                src_ref=modcols.at[pl.ds(pl.multiple_of(pidx * SUBLANES, SUBLANES), SUBLANES), :],
                dst_ref=mod_rows.at[me],
                send_sem=s_send.at[2, k - 1], recv_sem=s_recv.at[2, k - 1],
                device_id=peer, device_id_type=MESH)
            cp.start()
            second.append(cp)
        big.forward_arrivals()
        for cp in second:
            cp.wait()
        big.forward_diagonal()
        big.finish()

    out_shape = tuple(jax.ShapeDtypeStruct((N_DEV,) + s.shape, s.dtype) for s in big_shards) + (
        jax.ShapeDtypeStruct((N_DEV,) + small_loc.shape, F32),
        jax.ShapeDtypeStruct((N_DEV, SUBLANES, d_model), F32),
        jax.ShapeDtypeStruct((N_DEV, SUBLANES, a_cols), F32),
    )
    vm = pl.BlockSpec(memory_space=VMEM)
    hbm = pl.BlockSpec(memory_space=ANY)
    outs = pl.pallas_call(
        body, name="gather_weights_mod", out_shape=out_shape,
        in_specs=[vm, vm, vm, vm] + [hbm] * n_big,
        out_specs=tuple([hbm] * n_big + [vm, vm, vm]),
        scratch_shapes=[
            pltpu.VMEM((N_DEV * SUBLANES, a_cols), F32),
            *_exchange_sems(n_big),
            pltpu.SemaphoreType.DMA((3, N_DEV - 1)),
            pltpu.SemaphoreType.DMA((3, N_DEV - 1)),
        ],
        compiler_params=pltpu.CompilerParams(vmem_limit_bytes=V7X_VMEM_LIMIT),
    )(c_pad, w_ada, b_ada_loc, small_loc, *big_shards)
    return outs[:n_big], outs[n_big], outs[n_big + 1], outs[n_big + 2]


def _shifted_rows_count(tm):
    return tm + HALO - SUBLANES


def _fill_shifted(ext, shifted, tm):
    for s in range(1, SUBLANES):
        shifted[s - 1] = ext[s:s + _shifted_rows_count(tm), :]


def _shifted_rows(ext, shifted, start, rows):
    phase = start % SUBLANES
    aligned = start - phase
    if phase == 0:
        return ext[aligned:aligned + rows, :]
    return shifted[phase - 1, aligned:aligned + rows, :]


def _layer_norm_parts(a1):
    mu = jnp.mean(a1, axis=-1, keepdims=True)
    xc = a1 - mu
    rstd = lax.rsqrt(jnp.mean(xc * xc, axis=-1, keepdims=True) + RMS_EPS)
    return xc * rstd, rstd


def _mixer_fwd_call(x, mod, win_t, wout, cw, cp, later_shards):
    n_later = len(later_shards)
    bsz, seq, d_model = x.shape
    c_half = cw.shape[1]
    n_taps = 31
    d_in = win_t.shape[0] * win_t.shape[1]
    tm = MIXER_FWD_TOKEN_TILE if seq % MIXER_FWD_TOKEN_TILE == 0 else _token_tile(seq)
    nt = seq // tm
    arrivals_step, diagonal_step = (7 * bsz * nt) // 16, (11 * bsz * nt) // 16

    def body(x_ref, mod_ref, win_ref, wout_ref, cw_ref, cp_ref, *rest):
        shard_refs, rest = rest[:n_later], rest[n_later:]
        proj_ref, a1_ref, cv_ref, mixed_ref, y1_ref = rest[:5]
        gathered_refs, rest = rest[5:5 + n_later], rest[5 + n_later:]
        aext, qext, ashift, send_sems, recv_sems, loc_sems = rest
        b, t = pl.program_id(0), pl.program_id(1)

        step = b * nt + t

        @pl.when(step == 0)
        def _():
            _TwoLevelGather(shard_refs, gathered_refs, send_sems, recv_sems, loc_sems).start()

        @pl.when(step == arrivals_step)
        def _():
            _TwoLevelGather(shard_refs, gathered_refs, send_sems, recv_sems, loc_sems).forward_arrivals()

        @pl.when(step == diagonal_step)
        def _():
            _TwoLevelGather(shard_refs, gathered_refs, send_sems, recv_sems, loc_sems).forward_diagonal()

        xv = x_ref[...]
        sh1, sc1 = mod_ref[0:1, :], mod_ref[1:2, :]
        r1 = lax.rsqrt(jnp.mean(xv * xv, axis=-1, keepdims=True) + RMS_EPS)
        h1 = (xv * r1) * (1.0 + sc1) + sh1
        proj = _nt_dot(h1.astype(BF16), win_ref[...].reshape(d_in, d_model))
        proj_ref[...] = proj
        val, gate = proj[:, 0:c_half], proj[:, c_half:2 * c_half]
        s_b, s_c, s_h = proj[:, 2 * c_half:3 * c_half], proj[:, 3 * c_half:4 * c_half], proj[:, 4 * c_half:5 * c_half]

        @pl.when(t == 0)
        def _():
            aext[0:HALO, :] = jnp.zeros((HALO, c_half), F32)
            qext[0:SHORT_HALO, :] = jnp.zeros((SHORT_HALO, c_half), F32)

        @pl.when(t > 0)
        def _():
            aext[0:HALO, :] = aext[tm:tm + HALO, :]
            qext[0:SHORT_HALO, :] = qext[tm:tm + SHORT_HALO, :]
        aext[HALO:HALO + tm, :] = val * _sigmoid(gate)
        qext[SHORT_HALO:SHORT_HALO + tm, :] = s_c * s_h

        base = HALO - (n_taps - 1)
        _fill_shifted(aext, ashift, tm)
        for r0 in range(0, tm, ROW_CHUNK):
            acc = jnp.zeros((ROW_CHUNK, c_half), F32)
            for k in range(n_taps):
                acc = acc + cw_ref[k:k + 1, :] * _shifted_rows(aext, ashift, r0 + base + k, ROW_CHUNK)
            a1_ref[r0:r0 + ROW_CHUNK, :] = acc + cp_ref[0:1, :]
        sbase = SHORT_HALO - 2
        conv3 = cp_ref[3:4, :] * qext[sbase:sbase + tm, :]
        conv3 = conv3 + cp_ref[4:5, :] * qext[sbase + 1:sbase + 1 + tm, :]
        conv3 = conv3 + cp_ref[5:6, :] * qext[sbase + 2:sbase + 2 + tm, :]
        cv_ref[...] = conv3

        norm, _ = _layer_norm_parts(a1_ref[...])
        a2 = norm * cp_ref[1:2, :] + cp_ref[2:3, :]
        mixed = jnp.concatenate([a2 * _sigmoid(a2), s_b * conv3], axis=-1).astype(BF16)
        mixed_ref[...] = mixed
        y1 = _nn_dot(mixed, wout_ref[...].reshape(d_model, d_model))
        y1_ref[...] = y1

        @pl.when(step == bsz * nt - 1)
        def _():
            _TwoLevelGather(shard_refs, gathered_refs, send_sems, recv_sems, loc_sems).finish()

    hbm = pl.BlockSpec(memory_space=ANY)

    def tok(width):
        return pl.BlockSpec((None, tm, width), lambda b, t: (b, t, 0))

    def const(shape):
        return pl.BlockSpec(shape, lambda b, t: (0,) * len(shape))

    def resident(shape):
        return pl.BlockSpec(shape, lambda b, t: (0,) * len(shape), pipeline_mode=pl.Buffered(1))

    out_shape = (
        jax.ShapeDtypeStruct((bsz, seq, d_in), F32),
        jax.ShapeDtypeStruct((bsz, seq, c_half), F32),
        jax.ShapeDtypeStruct((bsz, seq, c_half), F32),
        jax.ShapeDtypeStruct((bsz, seq, d_model), BF16),
        jax.ShapeDtypeStruct((bsz, seq, d_model), F32),
    ) + tuple(jax.ShapeDtypeStruct((N_DEV,) + s.shape, s.dtype) for s in later_shards)
    outs = pl.pallas_call(
        body, name="mixer_fwd", out_shape=out_shape, grid=(bsz, nt),
        in_specs=[tok(d_model), pl.BlockSpec((None, 6, d_model), lambda b, t: (b, 0, 0)),
                  resident(win_t.shape), resident(wout.shape), const(cw.shape), const(cp.shape)] + [hbm] * n_later,
        out_specs=(tok(d_in), tok(c_half), tok(c_half), tok(d_model), tok(d_model)) + (hbm,) * n_later,
        scratch_shapes=[pltpu.VMEM((tm + HALO, c_half), F32), pltpu.VMEM((tm + SHORT_HALO, c_half), F32),
                        pltpu.VMEM((SUBLANES - 1, _shifted_rows_count(tm), c_half), F32)]
        + _exchange_sems(n_later),
        compiler_params=pltpu.CompilerParams(
            dimension_semantics=("arbitrary", "arbitrary"), vmem_limit_bytes=V7X_VMEM_LIMIT),
    )(x, mod, win_t, wout, cw, cp, *later_shards)
    return outs[:5], outs[5:]


def _mlp_call(x, y1, target, mod, w1, w2, g_final):
    bsz, seq, d_model = x.shape
    n_blk, f_blk, _ = w1.shape
    d_ff = n_blk * f_blk
    tm = MLP_TOKEN_TILE if seq % MLP_TOKEN_TILE == 0 else _token_tile(seq)
    nt = seq // tm

    def body(x_ref, y1_ref, tgt_ref, mod_ref, w1_ref, w2_ref, gf_ref,
             dx1_ref, h2_ref, dy2_ref, u_ref, dz_ref, dmod_ref, head_ref, dy1_ref, relu_scr):
        b, t = pl.program_id(0), pl.program_id(1)
        x1v = x_ref[...] + mod_ref[2:3, :] * y1_ref[...]
        sh2, sc2, g2 = mod_ref[3:4, :], mod_ref[4:5, :], mod_ref[5:6, :]
        gf = gf_ref[...]
        r2 = lax.rsqrt(jnp.mean(x1v * x1v, axis=-1, keepdims=True) + RMS_EPS)
        xn2 = x1v * r2
        h2 = (xn2 * (1.0 + sc2) + sh2).astype(BF16)
        h2_ref[...] = h2
        for j in range(n_blk):
            cols = slice(j * f_blk, (j + 1) * f_blk)
            rz = jnp.maximum(_nt_dot(h2, w1_ref[j]), 0.0)
            relu_scr[:, cols] = rz
            u_ref[:, cols] = (rz * rz).astype(BF16)
        y2 = _nn_dot(u_ref[...], w2_ref[...].reshape(d_ff, d_model))
        x2 = x1v + g2 * y2
        r3 = lax.rsqrt(jnp.mean(x2 * x2, axis=-1, keepdims=True) + RMS_EPS)
        xn3 = x2 * r3
        diff = xn3 * gf - tgt_ref[...]
        dout = diff * (1.0 / d_model)

        @pl.when(jnp.logical_and(b == 0, t == 0))
        def _():
            head_ref[...] = jnp.zeros(head_ref.shape, F32)

        @pl.when(t == 0)
        def _():
            dmod_ref[...] = jnp.zeros(dmod_ref.shape, F32)

        head_ref[0:1, :] += jnp.sum(dout * xn3, axis=0, keepdims=True)
        head_ref[1:2, :] += jnp.sum(diff * diff, axis=0, keepdims=True)
        dxn3 = dout * gf
        dx2 = r3 * (dxn3 - xn3 * jnp.mean(dxn3 * xn3, axis=-1, keepdims=True))
        dmod_ref[2:3, :] += jnp.sum(dx2 * y2, axis=0, keepdims=True)
        dy2 = (g2 * dx2).astype(BF16)
        dy2_ref[...] = dy2
        for j in range(n_blk):
            cols = slice(j * f_blk, (j + 1) * f_blk)
            dz_ref[:, cols] = (_nt_dot(dy2, w2_ref[j]) * (2.0 * relu_scr[:, cols])).astype(BF16)
        dh2 = _nn_dot(dz_ref[...], w1_ref[...].reshape(d_ff, d_model))
        dmod_ref[0:1, :] += jnp.sum(dh2, axis=0, keepdims=True)
        dmod_ref[1:2, :] += jnp.sum(dh2 * xn2, axis=0, keepdims=True)
        dxn2 = dh2 * (1.0 + sc2)
        dx1 = dx2 + r2 * (dxn2 - xn2 * jnp.mean(dxn2 * xn2, axis=-1, keepdims=True))
        dx1_ref[...] = dx1
        dy1_ref[...] = (mod_ref[2:3, :] * dx1).astype(BF16)
        dmod_ref[3:4, :] += jnp.sum(dx1 * y1_ref[...], axis=0, keepdims=True)

    def tok(width):
        return pl.BlockSpec((None, tm, width), lambda b, t: (b, t, 0))

    def const(shape):
        return pl.BlockSpec(shape, lambda b, t: (0,) * len(shape))

    def resident(shape):
        return pl.BlockSpec(shape, lambda b, t: (0,) * len(shape), pipeline_mode=pl.Buffered(1))

    out_shape = (
        jax.ShapeDtypeStruct((bsz, seq, d_model), F32),
        jax.ShapeDtypeStruct((bsz, seq, d_model), BF16),
        jax.ShapeDtypeStruct((bsz, seq, d_model), BF16),
        jax.ShapeDtypeStruct((bsz, seq, d_ff), BF16),
        jax.ShapeDtypeStruct((bsz, seq, d_ff), BF16),
        jax.ShapeDtypeStruct((bsz, SUBLANES, d_model), F32),
        jax.ShapeDtypeStruct((SUBLANES, d_model), F32),
        jax.ShapeDtypeStruct((bsz, seq, d_model), BF16),
    )
    return pl.pallas_call(
        body, name="mlp_fwd_bwd", out_shape=out_shape, grid=(bsz, nt),
        in_specs=[tok(d_model), tok(d_model), tok(d_model),
                  pl.BlockSpec((None, 6, d_model), lambda b, t: (b, 0, 0)),
                  resident(w1.shape), resident(w2.shape), const(g_final.shape)],
        out_specs=(tok(d_model), tok(d_model), tok(d_model), tok(d_ff), tok(d_ff),
                   pl.BlockSpec((None, SUBLANES, d_model), lambda b, t: (b, 0, 0)),
                   const((SUBLANES, d_model)), tok(d_model)),
        scratch_shapes=[pltpu.VMEM((tm, d_ff), F32)],
        compiler_params=pltpu.CompilerParams(
            dimension_semantics=("arbitrary", "arbitrary"), vmem_limit_bytes=MLP_VMEM_LIMIT),
    )(x, y1, target, mod, w1, w2, g_final)


def _mixer_bwd_call(dx1, x, proj, a1, cv, mod, win_t, wout, cw, cp):
    bsz, seq, d_model = x.shape
    c_half = cw.shape[1]
    n_taps = 31
    d_in = win_t.shape[0] * win_t.shape[1]
    tm = _token_tile(seq)
    nt = seq // tm

    def body(dx1_ref, x_ref, proj_ref, a1_ref, cv_ref, mod_ref, win_ref, wout_ref, cw_ref, cp_ref,
             gx_ref, dproj_ref, h1_ref, dmod_ref, cgrad_ref,
             dext, cext, a0_scr, da0_scr, tap_acc, row_acc, dshift):
        b, step = pl.program_id(0), pl.program_id(1)
        first = jnp.logical_and(b == 0, step == 0)
        last = jnp.logical_and(b == bsz - 1, step == nt - 1)
        dx1v = dx1_ref[...]
        xv = x_ref[...]
        sh1, sc1, g1 = mod_ref[0:1, :], mod_ref[1:2, :], mod_ref[2:3, :]

        @pl.when(first)
        def _():
            tap_acc[...] = jnp.zeros(tap_acc.shape, F32)
            row_acc[...] = jnp.zeros(row_acc.shape, F32)

        @pl.when(step == 0)
        def _():
            dmod_ref[...] = jnp.zeros(dmod_ref.shape, F32)
            dext[tm:tm + HALO, :] = jnp.zeros((HALO, c_half), F32)
            cext[tm:tm + SHORT_HALO, :] = jnp.zeros((SHORT_HALO, c_half), F32)

        @pl.when(step > 0)
        def _():
            dext[tm:tm + HALO, :] = dext[0:HALO, :]
            cext[tm:tm + SHORT_HALO, :] = cext[0:SHORT_HALO, :]

        dy1 = (g1 * dx1v).astype(BF16)
        dmixed = _nt_dot(dy1, wout_ref[...].reshape(d_model, d_model))
        d_a, d_s = dmixed[:, 0:c_half], dmixed[:, c_half:2 * c_half]

        val, gate = proj_ref[:, 0:c_half], proj_ref[:, c_half:2 * c_half]
        s_b = proj_ref[:, 2 * c_half:3 * c_half]
        s_c, s_h = proj_ref[:, 3 * c_half:4 * c_half], proj_ref[:, 4 * c_half:5 * c_half]

        d_sb = d_s * cv_ref[...]
        cext[0:tm, :] = d_s * s_b
        q = s_c * s_h
        dq = jnp.zeros((tm, c_half), F32)
        for k in range(3):
            shifted = cext[2 - k:2 - k + tm, :]
            dq = dq + cp_ref[3 + k:4 + k, :] * shifted
            row_acc[k:k + 1, :] += jnp.sum(q * shifted, axis=0, keepdims=True)
        d_sc, d_sh = dq * s_h, dq * s_c

        norm, rstd = _layer_norm_parts(a1_ref[...])
        ln_g = cp_ref[1:2, :]
        a2 = norm * ln_g + cp_ref[2:3, :]
        sg = _sigmoid(a2)
        d_a2 = d_a * (sg * (1.0 + a2 * (1.0 - sg)))
        row_acc[4:5, :] += jnp.sum(d_a2 * norm, axis=0, keepdims=True)
        row_acc[5:6, :] += jnp.sum(d_a2, axis=0, keepdims=True)
        d_n = d_a2 * ln_g
        d_a1 = rstd * (d_n - jnp.mean(d_n, axis=-1, keepdims=True)
                       - norm * jnp.mean(d_n * norm, axis=-1, keepdims=True))
        row_acc[3:4, :] += jnp.sum(d_a1, axis=0, keepdims=True)
        dext[0:tm, :] = d_a1
        sig_g = _sigmoid(gate)
        a0_scr[...] = val * sig_g

        _fill_shifted(dext, dshift, tm)
        for r0 in range(0, tm, ROW_CHUNK):
            a0c = a0_scr[r0:r0 + ROW_CHUNK, :]
            acc = jnp.zeros((ROW_CHUNK, c_half), F32)
            for k in range(n_taps):
                shifted = _shifted_rows(dext, dshift, r0 + (n_taps - 1) - k, ROW_CHUNK)
                acc = acc + cw_ref[k:k + 1, :] * shifted
                prod = a0c * shifted
                part = prod[0:SUBLANES, :]
                for g in range(1, ROW_CHUNK // SUBLANES):
                    part = part + prod[g * SUBLANES:(g + 1) * SUBLANES, :]
                tap_acc[k * SUBLANES:(k + 1) * SUBLANES, :] += part
            da0_scr[r0:r0 + ROW_CHUNK, :] = acc
        d_a0 = da0_scr[...]
        d_val = d_a0 * sig_g
        d_gate = d_a0 * val * sig_g * (1.0 - sig_g)

        dproj = jnp.concatenate([d_val, d_gate, d_sb, d_sc, d_sh], axis=-1).astype(BF16)
        dproj_ref[...] = dproj
        dh1 = _nn_dot(dproj, win_ref[...].reshape(d_in, d_model))
        r1 = lax.rsqrt(jnp.mean(xv * xv, axis=-1, keepdims=True) + RMS_EPS)
        xn1 = xv * r1
        h1_ref[...] = (xn1 * (1.0 + sc1) + sh1).astype(BF16)
        dmod_ref[0:1, :] += jnp.sum(dh1, axis=0, keepdims=True)
        dmod_ref[1:2, :] += jnp.sum(dh1 * xn1, axis=0, keepdims=True)
        dxn1 = dh1 * (1.0 + sc1)
        gx_ref[...] = dx1v + r1 * (dxn1 - xn1 * jnp.mean(dxn1 * xn1, axis=-1, keepdims=True))

        @pl.when(last)
        def _():
            taps = jnp.sum(tap_acc[...].reshape(HALO, SUBLANES, c_half), axis=1)
            cgrad_ref[0:HALO, :] = taps
            cgrad_ref[HALO:HALO + SUBLANES, :] = row_acc[...]

    def tok(width):
        return pl.BlockSpec((None, tm, width), lambda b, s: (b, nt - 1 - s, 0))

    def const(shape):
        return pl.BlockSpec(shape, lambda b, s: (0,) * len(shape))

    mod_spec = pl.BlockSpec((None, 6, d_model), lambda b, s: (b, 0, 0))
    out_shape = (
        jax.ShapeDtypeStruct((bsz, seq, d_model), F32),
        jax.ShapeDtypeStruct((bsz, seq, d_in), BF16),
        jax.ShapeDtypeStruct((bsz, seq, d_model), BF16),
        jax.ShapeDtypeStruct((bsz, SUBLANES, d_model), F32),
        jax.ShapeDtypeStruct((HALO + SUBLANES, c_half), F32),
    )
    return pl.pallas_call(
        body, name="mixer_bwd", out_shape=out_shape, grid=(bsz, nt),
        in_specs=[tok(d_model), tok(d_model), tok(d_in), tok(c_half), tok(c_half), mod_spec,
                  const(win_t.shape), const(wout.shape), const(cw.shape), const(cp.shape)],
        out_specs=(tok(d_model), tok(d_in), tok(d_model),
                   pl.BlockSpec((None, SUBLANES, d_model), lambda b, s: (b, 0, 0)),
                   const((HALO + SUBLANES, c_half))),
        scratch_shapes=[
            pltpu.VMEM((tm + HALO, c_half), F32), pltpu.VMEM((tm + SHORT_HALO, c_half), F32),
            pltpu.VMEM((tm, c_half), F32), pltpu.VMEM((tm, c_half), F32),
            pltpu.VMEM((HALO * SUBLANES, c_half), F32), pltpu.VMEM((SUBLANES, c_half), F32),
            pltpu.VMEM((SUBLANES - 1, _shifted_rows_count(tm), c_half), F32),
        ],
        compiler_params=pltpu.CompilerParams(
            dimension_semantics=("arbitrary", "arbitrary"), vmem_limit_bytes=V7X_VMEM_LIMIT),
    )(dx1, x, proj, a1, cv, mod, win_t, wout, cw, cp)


def _largest_divisor(n, cap, multiple):
    best = None
    for cand in range(multiple, min(n, cap) + 1, multiple):
        if n % cand == 0:
            best = cand
    return best if best is not None else n


WGRAD_TOKENS_PER_STEP = 2048


def _wgrad_call(a, b, name, scatter=()):
    n_sc = len(scatter)
    tokens, m_dim = a.shape
    n_dim = b.shape[1]
    bk = _largest_divisor(tokens, WGRAD_TOKENS_PER_STEP, 128)
    n_k = tokens // bk
    bm = _largest_divisor(m_dim, 1024, m_dim // N_DEV)
    bn = n_dim
    out_shape = jax.ShapeDtypeStruct((m_dim, n_dim), BF16)
    out_spec = pl.BlockSpec((bm, bn), lambda i, j, k: (i, j))

    grid = (m_dim // bm, n_dim // bn, n_k)
    n_steps = grid[0] * grid[1] * grid[2]
    pair_step = min(max(1, n_steps // 8), n_steps - 1)

    def body(a_ref, b_ref, *rest):
        part_refs, o_ref, sum_refs, acc = rest[:n_sc], rest[n_sc], rest[n_sc + 1:2 * n_sc + 1], rest[2 * n_sc + 1]
        k = pl.program_id(2)
        step = (pl.program_id(0) * grid[1] + pl.program_id(1)) * n_k + k

        def exchange():
            extra = rest[2 * n_sc + 2:]
            return _TwoLevelScatter(part_refs, sum_refs, extra[:n_sc], extra[n_sc:2 * n_sc], extra[2 * n_sc:3 * n_sc],
                                    *extra[3 * n_sc:])

        if n_sc:
            @pl.when(step == 0)
            def _():
                exchange().start()

            @pl.when(step == pair_step)
            def _():
                exchange().pair_sums()

        @pl.when(k == 0)
        def _():
            acc[...] = jnp.zeros(acc.shape, F32)

        acc[...] += _tn_dot(a_ref[...], b_ref[...])

        @pl.when(k == n_k - 1)
        def _():
            o_ref[...] = acc[...].astype(BF16)

        if n_sc:
            @pl.when(step == n_steps - 1)
            def _():
                exchange().finish()

    hbm = pl.BlockSpec(memory_space=ANY)
    sum_shapes = tuple(jax.ShapeDtypeStruct((N_CHIPS,) + p.shape[1:], BF16) for p in scatter)
    outs = pl.pallas_call(
        body, name=name, out_shape=(out_shape,) + sum_shapes, grid=grid,
        in_specs=[pl.BlockSpec((bk, bm), lambda i, j, k: (k, i)), pl.BlockSpec((bk, bn), lambda i, j, k: (k, j))]
        + [hbm] * n_sc,
        out_specs=(out_spec,) + (hbm,) * n_sc,
        scratch_shapes=[pltpu.VMEM((bm, bn), F32)] + (_scatter_scratch(scatter) if n_sc else []),
        compiler_params=pltpu.CompilerParams(
            dimension_semantics=("arbitrary", "arbitrary", "arbitrary"), vmem_limit_bytes=V7X_VMEM_LIMIT),
    )(a, b, *scatter)
    out = outs[0].reshape(N_DEV, m_dim // N_DEV, n_dim)
    return (out, list(outs[1:])) if n_sc else out


def _tail_scatter_call(partial, small, n_grad_rows, loss_rows, bias_rows, loss_scale):
    width = small.shape[1]
    lo, hi = loss_rows
    b0, b1, b2 = bias_rows
    nb = b1 - b0

    def body(g_ref, small_ref, out_ref, small_all, sum_ref, extra_ref, mine, from_sibling, pair,
             p1_send, p1_recv, p2_send, p2_recv, p_fetch, p_loc, s_send, s_recv, s_loc):
        gather = _TwoLevelGather([small_ref], [small_all], s_send, s_recv, s_loc)
        scatter = _TwoLevelScatter([g_ref], [out_ref], [mine], [from_sibling], [pair],
                                   p1_send, p1_recv, p2_send, p2_recv, p_fetch, p_loc)
        gather.start()
        scatter.start()
        scatter.pair_sums()
        gather.forward_arrivals()
        gather.forward_diagonal()
        gather.finish()
        tot = small_all[0]
        for k in range(1, N_DEV):
            tot = tot + small_all[k]
        sum_ref[...] = tot[0:n_grad_rows, :]
        extra_ref[0:nb, :] = tot[b0:b1, :] + tot[b1:b2, :]
        head = tot[n_grad_rows - 2 * SUBLANES:n_grad_rows, :]
        rows_id = lax.broadcasted_iota(jnp.int32, head.shape, 0) + (n_grad_rows - 2 * SUBLANES)
        sq = jnp.where(jnp.logical_and(rows_id >= lo, rows_id < hi), head, 0.0)
        extra_ref[nb:nb + SUBLANES, :] = jnp.zeros((SUBLANES, width), F32) + jnp.sum(sq) * loss_scale
        scatter.finish()

    vm = pl.BlockSpec(memory_space=VMEM)
    return pl.pallas_call(
        body, name="scatter_tail",
        out_shape=(jax.ShapeDtypeStruct((N_CHIPS,) + partial.shape[1:], BF16),
                   jax.ShapeDtypeStruct((N_DEV,) + small.shape, F32),
                   jax.ShapeDtypeStruct((n_grad_rows, width), F32),
                   jax.ShapeDtypeStruct((nb + SUBLANES, width), F32)),
        in_specs=[pl.BlockSpec(memory_space=ANY), vm], out_specs=(vm, vm, vm, vm),
        scratch_shapes=_scatter_scratch([partial]) + _exchange_sems(1),
        compiler_params=pltpu.CompilerParams(vmem_limit_bytes=V7X_VMEM_LIMIT),
    )(partial, small)


def _adamw(w, g, m, v):
    m2 = ADAM_B1 * m + (1.0 - ADAM_B1) * g
    v2 = ADAM_B2 * v + (1.0 - ADAM_B2) * (g * g)
    m_hat = m2 / (1.0 - ADAM_B1 ** ADAM_STEP)
    v_hat = v2 / (1.0 - ADAM_B2 ** ADAM_STEP)
    delta = -ADAM_LR * (m_hat / (jnp.sqrt(v_hat) + ADAM_EPS) + ADAM_WD * w)
    return delta, m2, v2


def _adam_slabs_call(slabs, w, m, v, name, transposed=False):
    rows, cols = w.shape
    n_slabs, slab_rows, slab_cols = slabs.shape
    tr = _largest_divisor(slab_rows, 256, 128 if transposed else 2 * SUBLANES)

    def body(s_ref, w_ref, m_ref, v_ref, g_ref, d_ref, m2_ref, v2_ref):
        g = s_ref[0].astype(F32)
        for k in range(1, n_slabs):
            g = g + s_ref[k].astype(F32)
        if transposed:
            g = g.T
        delta, m2, v2 = _adamw(w_ref[...], g, m_ref[...], v_ref[...])
        g_ref[...] = g
        d_ref[...] = delta
        m2_ref[...] = m2
        v2_ref[...] = v2

    tile = pl.BlockSpec((rows, tr), lambda i: (0, i)) if transposed else pl.BlockSpec((tr, cols), lambda i: (i, 0))
    shp = jax.ShapeDtypeStruct((rows, cols), F32)
    return pl.pallas_call(
        body, name=name, out_shape=(shp, shp, shp, shp), grid=(slab_rows // tr,),
        in_specs=[pl.BlockSpec((n_slabs, tr, slab_cols), lambda i: (0, i, 0)), tile, tile, tile],
        out_specs=(tile, tile, tile, tile),
        compiler_params=pltpu.CompilerParams(dimension_semantics=("arbitrary",), vmem_limit_bytes=V7X_VMEM_LIMIT),
    )(slabs, w, m, v)


def _adam_ada_call(c_rows, dmod_cols, w, m, v):
    rows, cols = w.shape
    n_rows = c_rows.shape[0]
    tr = _largest_divisor(rows, 256, 128)

    def body(c_ref, dm_ref, w_ref, m_ref, v_ref, g_ref, d_ref, m2_ref, v2_ref):
        cv = c_ref[...]
        c_act = (cv * _sigmoid(cv)).astype(BF16)
        g = _tn_dot(c_act, dm_ref[...].astype(BF16))
        delta, m2, v2 = _adamw(w_ref[...], g, m_ref[...], v_ref[...])
        g_ref[...] = g
        d_ref[...] = delta
        m2_ref[...] = m2
        v2_ref[...] = v2

    tile = pl.BlockSpec((tr, cols), lambda i: (i, 0))
    shp = jax.ShapeDtypeStruct((rows, cols), F32)
    return pl.pallas_call(
        body, name="adam_w_ada", out_shape=(shp, shp, shp, shp), grid=(rows // tr,),
        in_specs=[pl.BlockSpec((n_rows, tr), lambda i: (0, i)), pl.BlockSpec((n_rows, cols), lambda i: (0, 0)),
                  tile, tile, tile],
        out_specs=(tile, tile, tile, tile),
        compiler_params=pltpu.CompilerParams(dimension_semantics=("arbitrary",), vmem_limit_bytes=V7X_VMEM_LIMIT),
    )(c_rows, dmod_cols, w, m, v)


def _adam_small_call(ws, gs, ms, vs):
    n = len(ws)

    def body(*refs):
        w_refs, g_refs, m_refs, v_refs = refs[:n], refs[n:2 * n], refs[2 * n:3 * n], refs[3 * n:4 * n]
        d_refs, m2_refs, v2_refs = refs[4 * n:5 * n], refs[5 * n:6 * n], refs[6 * n:7 * n]
        for i in range(n):
            delta, m2, v2 = _adamw(w_refs[i][...], g_refs[i][...], m_refs[i][...], v_refs[i][...])
            d_refs[i][...] = delta
            m2_refs[i][...] = m2
            v2_refs[i][...] = v2

    vm = pl.BlockSpec(memory_space=VMEM)
    shapes = tuple(jax.ShapeDtypeStruct(w.shape, F32) for w in ws)
    outs = pl.pallas_call(body, name="adam_small", out_shape=shapes * 3,
                          in_specs=[vm] * (4 * n), out_specs=(vm,) * (3 * n))(*ws, *gs, *ms, *vs)
    return outs[:n], outs[n:2 * n], outs[2 * n:]


def kernel(x, c, w_ada, b_ada, w_in, conf_dw_w, conf_dw_b, conf_ln_g, conf_ln_b, sc_conv_w, w_out, w_mlp1, w_mlp2, g_final, loss_target, m_w_ada, m_b_ada, m_w_in, m_conf_dw_w, m_conf_dw_b, m_conf_ln_g, m_conf_ln_b, m_sc_conv_w, m_w_out, m_w_mlp1, m_w_mlp2, m_g_final, v_w_ada, v_b_ada, v_w_in, v_conf_dw_w, v_conf_dw_b, v_conf_ln_g, v_conf_ln_b, v_sc_conv_w, v_w_out, v_w_mlp1, v_w_mlp2, v_g_final):
    bsz, seq, d_model = x.shape
    c_half = conf_dw_b.shape[-1]
    n_taps = conf_dw_w.shape[1]
    cc = conf_dw_w.shape[-1]
    a_cols = w_ada.shape[-1]
    tokens = bsz * seq
    me = _dev_index()

    c_pad = jnp.pad(c, ((0, SUBLANES - bsz), (0, 0)))
    b_ada_loc = lax.dynamic_slice(b_ada, (0, me * a_cols), (1, a_cols))
    small_loc = jnp.zeros((HALO, 128), F32)
    small_loc = small_loc.at[:n_taps, :cc].set(conf_dw_w[0]).at[:3, cc:2 * cc].set(sc_conv_w[0])
    (win_t, wout_all), small_all, c_all, mod_rows = _gather_call(
        c_pad, w_ada[0], b_ada_loc, small_loc, [w_in[0].T.astype(BF16), w_out[0].astype(BF16)])
    cw = small_all[:, :, :cc].transpose(1, 0, 2).reshape(HALO, c_half)
    scw = small_all[:, :3, cc:2 * cc].transpose(1, 0, 2).reshape(3, c_half)
    cp = jnp.concatenate([conf_dw_b, conf_ln_g, conf_ln_b, scw, jnp.zeros((2, c_half), F32)], axis=0)
    mod = mod_rows[:, :bsz, :].transpose(1, 0, 2).reshape(bsz, 6, d_model)

    flat = lambda t: t.reshape(tokens, t.shape[-1])
    (proj, a1, cv, mixed, y1), (w1_all, w2_all) = _mixer_fwd_call(
        x, mod, win_t, wout_all, cw, cp, [w_mlp1[0].T.astype(BF16), w_mlp2[0].astype(BF16)])
    dx1, h2, dy2, u, dz, dmod2, head, dy1 = _mlp_call(
        x, y1, loss_target, mod, w1_all, w2_all, g_final.reshape(1, d_model))
    g_out = _wgrad_call(flat(mixed), flat(dy1), "wgrad_out")
    g_w1 = _wgrad_call(flat(dz), flat(h2), "wgrad_mlp1")
    g_w2, (s_w1, s_out) = _wgrad_call(flat(u), flat(dy2), "wgrad_mlp2", scatter=[g_w1, g_out])
    grad_x, dproj, h1, dmod1, cgrad = _mixer_bwd_call(dx1, x, proj, a1, cv, mod, win_t, wout_all, cw, cp)
    g_in_t, (s_w2,) = _wgrad_call(flat(dproj), flat(h1), "wgrad_in", scatter=[g_w2])

    dmod = jnp.concatenate([dmod1[:, :2, :], dmod2[:, 3:4, :], dmod2[:, :3, :]], axis=1)
    n_cg = cgrad.shape[0]
    per_b = 6 * d_model // c_half
    per_b_pad = -(-per_b // SUBLANES) * SUBLANES
    dmod_rows = jnp.pad(dmod.reshape(bsz, per_b, c_half), ((0, 0), (0, per_b_pad - per_b), (0, 0)))
    small = jnp.concatenate([
        cgrad,
        head.reshape(2 * SUBLANES, c_half),
        dmod_rows.reshape(bsz * per_b_pad, c_half),
    ], axis=0)
    n_head = n_cg + 2 * SUBLANES
    s_in, gathered, sums, extra = _tail_scatter_call(
        g_in_t, small, n_head, (n_cg + 2, n_cg + 4), (n_head, n_head + per_b_pad, n_head + 2 * per_b_pad),
        0.5 / d_model)

    loss = extra[per_b_pad, 0]
    g_b_ada = extra[:per_b].reshape(1, 6 * d_model)
    g_dw_w = lax.dynamic_slice(sums[:n_taps], (0, me * cc), (n_taps, cc))
    g_sc_w = lax.dynamic_slice(sums[HALO:HALO + 3], (0, me * cc), (3, cc))
    g_dw_b, g_ln_g, g_ln_b = sums[HALO + 3:HALO + 4], sums[HALO + 4:HALO + 5], sums[HALO + 5:HALO + 6]
    g_gf = sums[n_cg:n_cg + 2].reshape(1, d_model)

    dmod_all = gathered[:, n_head:, :].reshape(N_DEV, bsz, per_b_pad, c_half)[:, :, :per_b, :]
    dmod_all = dmod_all.reshape(N_DEV, bsz, 6 * d_model)
    dmod_cols = lax.dynamic_slice(dmod_all, (0, 0, me * a_cols), (N_DEV, bsz, a_cols))
    dmod_cols = jnp.pad(dmod_cols, ((0, 0), (0, SUBLANES - bsz), (0, 0))).reshape(N_DEV * SUBLANES, a_cols)
    c_rows = c_all.reshape(N_DEV * SUBLANES, d_model)
    g_ada, d_ada, m_ada, v_ada = _adam_ada_call(c_rows, dmod_cols, w_ada[0], m_w_ada[0], v_w_ada[0])

    gi, di, mi, vi = _adam_slabs_call(s_in, w_in[0].T, m_w_in[0].T, v_w_in[0].T, "adam_w_in")
    gi, di, mi, vi = gi.T, di.T, mi.T, vi.T
    go, do, mo, vo = _adam_slabs_call(s_out, w_out[0], m_w_out[0], v_w_out[0], "adam_w_out")
    g1, d1, m1, v1 = _adam_slabs_call(s_w1, w_mlp1[0], m_w_mlp1[0], v_w_mlp1[0], "adam_w_mlp1", transposed=True)
    g2, d2, m2, v2 = _adam_slabs_call(s_w2, w_mlp2[0], m_w_mlp2[0], v_w_mlp2[0], "adam_w_mlp2")

    small_like = [b_ada, conf_dw_w, conf_dw_b, conf_ln_g, conf_ln_b, sc_conv_w, g_final]
    two_d = lambda t: t.reshape(-1, t.shape[-1])
    small_g = [g_b_ada, g_dw_w, g_dw_b, g_ln_g, g_ln_b, g_sc_w, g_gf]
    sd, sm, sv = _adam_small_call(
        [two_d(t) for t in small_like], small_g,
        [two_d(t) for t in (m_b_ada, m_conf_dw_w, m_conf_dw_b, m_conf_ln_g, m_conf_ln_b, m_sc_conv_w, m_g_final)],
        [two_d(t) for t in (v_b_ada, v_conf_dw_w, v_conf_dw_b, v_conf_ln_g, v_conf_ln_b, v_sc_conv_w, v_g_final)])
    like = lambda parts: [p.reshape(w.shape) for p, w in zip(parts, small_like)]
    sg, sd, sm, sv = like(small_g), like(sd), like(sm), like(sv)

    def ordered(ada, small_list, w_in_, w_out_, w1_, w2_):
        b_ada_, dw_w_, dw_b_, ln_g_, ln_b_, sc_w_, gf_ = small_list
        return [ada[None], b_ada_, w_in_[None], dw_w_, dw_b_, ln_g_, ln_b_, sc_w_, w_out_[None], w1_[None], w2_[None], gf_]

    grads = ordered(g_ada, sg, gi, go, g1, g2)
    deltas = ordered(d_ada, sd, di, do, d1, d2)
    new_m = ordered(m_ada, sm, mi, mo, m1, m2)
    new_v = ordered(v_ada, sv, vi, vo, v1, v2)
    return (loss, grad_x, *grads, *deltas, *new_m, *new_v)
```
